```python
import math
import jax
import jax.numpy as jnp
from jax import lax
import numpy as np

D_MODEL = 1024
BATCH = 1
SEQ = 16384
DEPTH = 4

GRID_W = 64
CTX_LEN = 256
N_MIXERS = 4
N_LRU = (DEPTH + 3) // N_MIXERS
N_RWKV = (DEPTH + 2) // N_MIXERS
N_RET = (DEPTH + 1) // N_MIXERS
N_HGRN = DEPTH // N_MIXERS
DEEPNORM_ALPHA = (2.0 * DEPTH) ** 0.25
DEEPNORM_BETA = (8.0 * DEPTH) ** -0.25
LN_EPS = 1e-5

LRU_WIDTH = D_MODEL
LRU_BLOCKS = 16
LRU_BLOCK = LRU_WIDTH // LRU_BLOCKS
LRU_CONV = 4
LRU_C = 8.0

RWKV_HEAD = 64
RWKV_HEADS = D_MODEL // RWKV_HEAD
RWKV_W_LORA = 64
RWKV_A_LORA = 64
RWKV_G_LORA = 128
RWKV_DECAY_SCALE = math.exp(-0.5)
RWKV_GN_EPS = 64e-5

RET_HEADS = 4
RET_QK = D_MODEL // RET_HEADS
RET_V = 2 * RET_QK
RET_CHUNK = 128
ROPE_BASE = 10000.0

HGRN_HEADS = 8
HGRN_EXPAND = D_MODEL // HGRN_HEADS
HGRN_HEAD_V = D_MODEL // HGRN_HEADS
HGRN_CHUNK = 64

N_EXPERTS = 64
TOP_K = 8
N_GROUPS = 8
TOPK_GROUPS = 4
EXPERT_DIM = 256
SHARED_DIM = 256
ROUTED_SCALE = 2.5
MOE_BLOCK = 128

kernel_name = 'hybrid_lru_rwkv7_retnet_hgrn2_moe_trunk'


def _layer_norm(h, g, b):
    hf = h.astype(jnp.float32)
    mu = jnp.mean(hf, axis=-1, keepdims=True)
    var = jnp.mean(jnp.square(hf - mu), axis=-1, keepdims=True)
    return ((hf - mu) * lax.rsqrt(var + LN_EPS)).astype(h.dtype) * g + b


def _head_norm(o, eps):
    of = o.astype(jnp.float32)
    mu = jnp.mean(of, axis=-1, keepdims=True)
    var = jnp.mean(jnp.square(of - mu), axis=-1, keepdims=True)
    return ((of - mu) * lax.rsqrt(var + eps)).astype(o.dtype)


def _head_rms(o, eps):
    of = o.astype(jnp.float32)
    return (of * lax.rsqrt(jnp.mean(jnp.square(of), axis=-1, keepdims=True) + eps)).astype(o.dtype)


def _modulate(h, shift, scale):
    return h * (1.0 + scale) + shift


def _heads(z, n):
    return z.reshape(*z.shape[:-1], n, z.shape[-1] // n)


def _merge(z):
    return z.reshape(*z.shape[:-2], z.shape[-2] * z.shape[-1])


def _rev(z, d):
    return z[:, ::-1] if d == 1 else z


def _to_blocks(z, size):
    b, t, h, f = z.shape
    return z.reshape(b, t // size, size, h, f).transpose(1, 0, 3, 2, 4)


def _from_blocks(z):
    n, b, h, size, f = z.shape
    return z.transpose(1, 0, 3, 2, 4).reshape(b, n * size, h, f)


def _dwconv_centred(z, w, b):
    k = w.shape[0]
    y = lax.conv_general_dilated(z, w[:, None, :], window_strides=(1,), padding=[((k - 1) // 2, k // 2)], dimension_numbers=('NWC', 'WIO', 'NWC'), feature_group_count=z.shape[-1])
    return y + b


def _rope(z, cos, sin):
    half = z.shape[-1] // 2
    rot = jnp.concatenate([-z[..., half:], z[..., :half]], axis=-1)
    return z * cos[None, :, None, :] + rot * sin[None, :, None, :]


def _lru_gates(xc, gate_w, gate_b, lam):
    b, t, w = xc.shape
    xb = xc.reshape(b, t, LRU_BLOCKS, LRU_BLOCK)
    gates = jax.nn.sigmoid(jnp.einsum('btni,gnij->gbtnj', xb, gate_w).reshape(2, b, t, w) + gate_b[:, None, None, :])
    log_a = -LRU_C * gates[0] * jax.nn.softplus(-lam)
    return jnp.exp(log_a), jnp.sqrt(1.0 - jnp.exp(2.0 * log_a)) * (gates[1] * xc)


def _linear_scan(a, bx, h0, reverse):
    def combine(e1, e2):
        return e1[0] * e2[0], e2[0] * e1[1] + e2[1]
    a_cum, b_cum = lax.associative_scan(combine, (a, bx), reverse=reverse, axis=1)
    return a_cum * h0[:, None, :] + b_cum


def _lru_mixer(u_ctx, u_lat, w_in, conv_w, conv_b, gate_w, gate_b, lam, w_out):
    def branches(u):
        gelu_in, rnn_in = jnp.split(u @ w_in, 2, axis=-1)
        return jax.nn.gelu(gelu_in), _dwconv_centred(rnn_in, conv_w, conv_b)
    g_ctx, x_ctx = branches(u_ctx)
    g_lat, x_lat = branches(u_lat)
    h0 = jnp.zeros_like(x_ctx[:, 0])
    hs_ctx, hs_lat = [], []
    for d in range(2):
        a_c, b_c = _lru_gates(x_ctx, gate_w[d], gate_b[d], lam[d])
        a_l, b_l = _lru_gates(x_lat, gate_w[d], gate_b[d], lam[d])
        h_c = _linear_scan(a_c, b_c, h0, d == 1)
        h_end = h_c[:, 0] if d == 1 else h_c[:, -1]
        hs_ctx.append(h_c)
        hs_lat.append(_linear_scan(a_l, b_l, h_end, d == 1))
    return (g_ctx * (hs_ctx[0] + hs_ctx[1])) @ w_out, (g_lat * (hs_lat[0] + hs_lat[1])) @ w_out


def _bi_shift(u):
    half = u.shape[-1] // 2
    prev = jnp.pad(u[:, :-1, :half], ((0, 0), (1, 0), (0, 0)))
    nxt = jnp.pad(u[:, 1:, half:], ((0, 0), (0, 1), (0, 0)))
    return jnp.concatenate([prev, nxt], axis=-1)


def _rwkv_prep(u, mu, w_in, w0, w_l1, w_l2, a0, a_l1, a_l2, g_l1, g_l2, k_k, k_a):
    xm = u[None] + (_bi_shift(u) - u)[None] * mu[:, None, None, :]
    rkv = jnp.einsum('cbtd,cde->cbte', xm[:3], w_in)
    r, k, v = rkv[0], rkv[1], rkv[2]
    d_w = w0[:, None, None, :] + jnp.einsum('zbtl,zld->zbtd', jnp.tanh(jnp.einsum('btd,zdl->zbtl', xm[3], w_l1)), w_l2)
    w = jnp.exp(-RWKV_DECAY_SCALE * jax.nn.sigmoid(d_w))
    a = jax.nn.sigmoid(a0[:, None, None, :] + jnp.einsum('zbtl,zld->zbtd', jnp.einsum('btd,zdl->zbtl', xm[4], a_l1), a_l2))
    g = jax.nn.sigmoid(xm[5] @ g_l1) @ g_l2
    kk = _heads(k * k_k, RWKV_HEADS)
    kk = kk * lax.rsqrt(jnp.sum(jnp.square(kk), axis=-1, keepdims=True) + 1e-12)
    k_t = k[None] * (1.0 + (a - 1.0) * k_a)
    h = RWKV_HEADS
    return _heads(r, h), _heads(v, h), kk, g, _heads(w, h), _heads(k_t, h), _heads(a, h)


def _rwkv_scan(r, w, k, v, kk, a, s0, reverse):
    xs = tuple(jnp.moveaxis(z, 1, 0) for z in (r, w, k, v, kk, a))
    def step(s, inp):
        r_t, w_t, k_t, v_t, kk_t, a_t = inp
        sa = jnp.einsum('bhvk,bhk->bhv', s, kk_t)
        s = s * w_t[:, :, None, :] - sa[..., None] * (kk_t * a_t)[:, :, None, :] + v_t[..., None] * k_t[:, :, None, :]
        return s, jnp.einsum('bhvk,bhk->bhv', s, r_t)
    s_end, o = lax.scan(step, s0, xs, reverse=reverse)
    return jnp.moveaxis(o, 0, 1), s_end


def _rwkv_mixer(u_ctx, u_lat, mu, w_in, w0, w_l1, w_l2, a0, a_l1, a_l2, g_l1, g_l2, k_k, k_a, r_k, ln_g, ln_b, w_out):
    def prep(u):
        return _rwkv_prep(u, mu, w_in, w0, w_l1, w_l2, a0, a_l1, a_l2, g_l1, g_l2, k_k, k_a)
    p_ctx = prep(u_ctx)
    p_lat = prep(u_lat)
    s0 = jnp.zeros((u_ctx.shape[0], RWKV_HEADS, RWKV_HEAD, RWKV_HEAD), u_ctx.dtype)
    o_ctx, o_lat = [], []
    for d in range(2):
        r, v, kk, _, w, k_t, a = p_ctx
        oc, s_c = _rwkv_scan(r, w[d], k_t[d], v, kk, a[d], s0, d == 1)
        r, v, kk, _, w, k_t, a = p_lat
        ol, _ = _rwkv_scan(r, w[d], k_t[d], v, kk, a[d], s_c, d == 1)
        o_ctx.append(oc)
        o_lat.append(ol)
    def readout(p, o):
        r, v, _, g, _, k_t, _ = p
        bonus = jnp.sum(r[None] * k_t * r_k, axis=-1, keepdims=True).sum(axis=0) * v
        y = _merge(_head_norm(o, RWKV_GN_EPS)) * ln_g + ln_b + _merge(bonus)
        return (y * g) @ w_out
    return readout(p_ctx, o_ctx[0] + o_ctx[1]), readout(p_lat, o_lat[0] + o_lat[1])


def _retention_chunks(q, k, v, log_gamma, r0):
    c = RET_CHUNK
    qb, kb, vb = _to_blocks(q, c), _to_blocks(k, c), _to_blocks(v, c)
    pos = jnp.arange(c, dtype=jnp.float32)
    rel = pos[:, None] - pos[None, :]
    lg = log_gamma[:, None, None]
    inner = jnp.where(rel >= 0, jnp.exp(jnp.maximum(rel, 0.0) * lg), 0.0).astype(q.dtype)
    q_dec = jnp.exp((pos + 1.0) * log_gamma[:, None]).astype(q.dtype)[:, :, None]
    k_dec = jnp.exp((c - 1.0 - pos) * log_gamma[:, None]).astype(q.dtype)[:, :, None]
    blk_dec = jnp.exp(c * log_gamma).astype(q.dtype)[:, None, None]
    def step(r, inp):
        q_c, k_c, v_c = inp
        scores = jnp.einsum('bhid,bhjd->bhij', q_c, k_c) * inner
        o = jnp.einsum('bhij,bhjv->bhiv', scores, v_c) + jnp.einsum('bhid,bhdv->bhiv', q_c, r) * q_dec
        r = r * blk_dec + jnp.einsum('bhjd,bhjv->bhdv', k_c * k_dec, v_c)
        return r, o
    r_end, o = lax.scan(step, r0, (qb, kb, vb))
    return _from_blocks(o), r_end


def _retention_mixer(u_ctx, u_lat, rope_cos, rope_sin, w_in, decay_logit, gn_g, gn_b, w_out):
    def proj(u):
        q, k, v, g = jnp.split(u @ w_in, [D_MODEL, 2 * D_MODEL, 2 * D_MODEL + RET_HEADS * RET_V], axis=-1)
        return _heads(q, RET_HEADS), _heads(k, RET_HEADS) * (RET_QK ** -0.5), _heads(v, RET_HEADS), g
    q_c, k_c, v_c, g_c = proj(u_ctx)
    q_l, k_l, v_l, g_l = proj(u_lat)
    q_l = _rope(q_l, rope_cos, rope_sin)
    k_l = _rope(k_l, rope_cos, rope_sin)
    log_gamma = jax.nn.log_sigmoid(decay_logit.astype(jnp.float32))
    r0 = jnp.zeros((u_ctx.shape[0], RET_HEADS, RET_QK, RET_V), u_ctx.dtype)
    o_ctx, o_lat = [], []
    for d in range(2):
        oc, r_c = _retention_chunks(_rev(q_c, d), _rev(k_c, d), _rev(v_c, d), log_gamma[d], r0)
        ol, _ = _retention_chunks(_rev(q_l, d), _rev(k_l, d), _rev(v_l, d), log_gamma[d], r_c)
        o_ctx.append(_rev(oc, d))
        o_lat.append(_rev(ol, d))
    def readout(o, g):
        y = _merge(_head_norm(o, LN_EPS)) * gn_g + gn_b
        return (jax.nn.silu(g) * y) @ w_out
    return readout(o_ctx[0] + o_ctx[1], g_c), readout(o_lat[0] + o_lat[1], g_l)


def _gla_chunks(q, k, v, log_f, s0):
    c = HGRN_CHUNK
    qb, kb, vb = _to_blocks(q, c), _to_blocks(k, c), _to_blocks(v, c)
    cum = jnp.cumsum(_to_blocks(log_f, c), axis=3)
    causal = jnp.tril(jnp.ones((c, c), dtype=bool))[:, :, None]
    def step(s, inp):
        q_c, k_c, v_c, b_c = inp
        diff = b_c[:, :, :, None, :] - b_c[:, :, None, :, :]
        decay = jnp.exp(jnp.where(causal, diff, -jnp.inf))
        scores = jnp.einsum('bhtk,bhsk,bhtsk->bhts', q_c, k_c, decay)
        o = jnp.einsum('bhts,bhsv->bhtv', scores, v_c) + jnp.einsum('bhtk,bhkv->bhtv', q_c * jnp.exp(b_c), s)
        b_last = b_c[:, :, -1:, :]
        s = s * jnp.exp(b_last[:, :, 0, :])[..., None] + jnp.einsum('bhsk,bhsv->bhkv', k_c * jnp.exp(b_last - b_c), v_c)
        return s, o
    s_end, o = lax.scan(step, s0, (qb, kb, vb, cum))
    return _from_blocks(o), s_end


def _hgrn_mixer(u_ctx, u_lat, lb, w_in, b_f, norm_g, w_out):
    def proj(u):
        q, f_fwd, f_bwd, i_in, g = jnp.split(u @ w_in, 5, axis=-1)
        f = lb + (1.0 - lb) * jax.nn.sigmoid(jnp.stack([f_fwd, f_bwd]) + b_f[:, None, None, :])
        return _heads(jax.nn.silu(q), HGRN_HEADS), _heads(i_in, HGRN_HEADS), _heads(f, HGRN_HEADS), g
    q_c, v_c, f_c, g_c = proj(u_ctx)
    q_l, v_l, f_l, g_l = proj(u_lat)
    s0 = jnp.zeros((u_ctx.shape[0], HGRN_HEADS, HGRN_EXPAND, HGRN_HEAD_V), u_ctx.dtype)
    o_ctx, o_lat = [], []
    for d in range(2):
        oc, s_c = _gla_chunks(_rev(q_c, d), _rev(1.0 - f_c[d], d), _rev(v_c, d), _rev(jnp.log(f_c[d]), d), s0)
        ol, _ = _gla_chunks(_rev(q_l, d), _rev(1.0 - f_l[d], d), _rev(v_l, d), _rev(jnp.log(f_l[d]), d), s_c)
        o_ctx.append(_rev(oc, d))
        o_lat.append(_rev(ol, d))
    def readout(o, g):
        return (_merge(_head_rms(o, LN_EPS) * norm_g) * jax.nn.silu(g)) @ w_out
    return readout(o_ctx[0] + o_ctx[1], g_c), readout(o_lat[0] + o_lat[1], g_l)


def _swiglu(z, w_gu, w_down):
    gate, up = jnp.split(z @ w_gu, 2, axis=-1)
    return (jax.nn.silu(gate) * up) @ w_down


def _routed_experts(u, top_idx, top_w, w_gu, w_down):
    n = u.shape[0]
    n_pairs = n * TOP_K
    n_blocks = -(-(n_pairs + N_EXPERTS * (MOE_BLOCK - 1)) // MOE_BLOCK)
    flat_e = top_idx.reshape(-1)
    flat_tok = jnp.arange(n_pairs, dtype=jnp.int32) // TOP_K
    order = jnp.argsort(flat_e)
    e_sorted = flat_e[order]
    counts = jnp.bincount(flat_e, length=N_EXPERTS)
    padded = (counts + MOE_BLOCK - 1) // MOE_BLOCK * MOE_BLOCK
    start = jnp.cumsum(counts) - counts
    padded_end = jnp.cumsum(padded)
    dest = (padded_end - padded)[e_sorted] + jnp.arange(n_pairs, dtype=jnp.int32) - start[e_sorted]
    row_tok = jnp.zeros((n_blocks * MOE_BLOCK,), jnp.int32).at[dest].set(flat_tok[order])
    row_w = jnp.zeros((n_blocks * MOE_BLOCK,), u.dtype).at[dest].set(top_w.reshape(-1)[order])
    block_e = jnp.minimum(jnp.searchsorted(padded_end, jnp.arange(n_blocks, dtype=jnp.int32) * MOE_BLOCK, side='right'), N_EXPERTS - 1)
    def body(acc, inp):
        e, toks, wts = inp
        y = _swiglu(u[toks], w_gu[e], w_down[e])
        return acc.at[toks].add(y * wts[:, None]), None
    acc, _ = lax.scan(body, jnp.zeros_like(u), (block_e, row_tok.reshape(n_blocks, MOE_BLOCK), row_w.reshape(n_blocks, MOE_BLOCK)))
    return acc


def _moe_ffn(u, router_w, router_bias, w_gu, w_down, sh_gu, sh_down):
    n = u.shape[0]
    scores = jax.nn.sigmoid((u @ router_w).astype(jnp.float32))
    choice = scores + router_bias.astype(jnp.float32)
    group_score = lax.top_k(choice.reshape(n, N_GROUPS, N_EXPERTS // N_GROUPS), 2)[0].sum(axis=-1)
    _, top_groups = lax.top_k(group_score, TOPK_GROUPS)
    group_mask = jnp.any(top_groups[:, :, None] == jnp.arange(N_GROUPS)[None, None, :], axis=1)
    expert_mask = jnp.repeat(group_mask, N_EXPERTS // N_GROUPS, axis=1)
    _, top_idx = lax.top_k(jnp.where(expert_mask, choice, -jnp.inf), TOP_K)
    top_w = jnp.take_along_axis(scores, top_idx, axis=1)
    top_w = ROUTED_SCALE * top_w / jnp.sum(top_w, axis=-1, keepdims=True)
    return _routed_experts(u, top_idx, top_w.astype(u.dtype), w_gu, w_down) + _swiglu(u, sh_gu, sh_down)


def setup_inputs(seed: int = 0) -> dict:
    key = jax.random.key(seed)
    ks = iter(jax.random.split(key, 64))
    f32 = jnp.float32
    d = D_MODEL

    def nrm(shape, scale=1.0):
        return jax.random.normal(next(ks), shape, f32) * scale

    def gain(shape):
        return 1.0 + nrm(shape, 0.01)

    lam_u = jax.random.uniform(next(ks), (N_LRU, 2, LRU_WIDTH), f32, 0.9, 0.999)
    p = 5.0 + jnp.arange(RET_HEADS, dtype=f32)
    gam = 1.0 - 2.0 ** (-p)
    ret_logit = jnp.log(gam) - jnp.log1p(-gam)
    hv = RET_HEADS * RET_V
    return {
        'x': nrm((BATCH, SEQ, d)),
        'c': nrm((BATCH, d)),
        'ctx': nrm((BATCH, CTX_LEN, d)),
        'c_ctx': nrm((d,)),
        'ada_w': nrm((DEPTH, d, 6 * d), 0.5 * d ** -0.5),
        'ada_b': nrm((DEPTH, 6 * d), 0.01),
        'post_ln_g': gain((DEPTH, 2, d)),
        'post_ln_b': nrm((DEPTH, 2, d), 0.01),
        'lru_w_in': nrm((N_LRU, d, 2 * LRU_WIDTH), d ** -0.5),
        'lru_conv_w': nrm((N_LRU, LRU_CONV, LRU_WIDTH), LRU_CONV ** -0.5),
        'lru_conv_b': nrm((N_LRU, LRU_WIDTH), 0.01),
        'lru_gate_w': nrm((N_LRU, 2, 2, LRU_BLOCKS, LRU_BLOCK, LRU_BLOCK), LRU_BLOCK ** -0.5),
        'lru_gate_b': nrm((N_LRU, 2, 2, LRU_WIDTH), 0.01),
        'lru_lambda': jnp.log(lam_u) - jnp.log1p(-lam_u),
        'lru_w_out': nrm((N_LRU, LRU_WIDTH, d), DEEPNORM_BETA * LRU_WIDTH ** -0.5),
        'rwkv_mu': jax.random.uniform(next(ks), (N_RWKV, 6, d), f32, 0.0, 1.0),
        'rwkv_w_in': nrm((N_RWKV, 3, d, d), d ** -0.5),
        'rwkv_w0': jax.random.uniform(next(ks), (N_RWKV, 2, d), f32, -4.0, 1.0),
        'rwkv_w_l1': nrm((N_RWKV, 2, d, RWKV_W_LORA), d ** -0.5),
        'rwkv_w_l2': nrm((N_RWKV, 2, RWKV_W_LORA, d), 0.1 * RWKV_W_LORA ** -0.5),
        'rwkv_a0': nrm((N_RWKV, 2, d), 0.1),
        'rwkv_a_l1': nrm((N_RWKV, 2, d, RWKV_A_LORA), d ** -0.5),
        'rwkv_a_l2': nrm((N_RWKV, 2, RWKV_A_LORA, d), 0.1 * RWKV_A_LORA ** -0.5),
        'rwkv_g_l1': nrm((N_RWKV, d, RWKV_G_LORA), d ** -0.5),
        'rwkv_g_l2': nrm((N_RWKV, RWKV_G_LORA, d), RWKV_G_LORA ** -0.5),
        'rwkv_k_k': 1.0 + nrm((N_RWKV, d), 0.1),
        'rwkv_k_a': 1.0 + nrm((N_RWKV, d), 0.1),
        'rwkv_r_k': nrm((N_RWKV, RWKV_HEADS, RWKV_HEAD), 0.1),
        'rwkv_ln_g': gain((N_RWKV, d)),
        'rwkv_ln_b': nrm((N_RWKV, d), 0.01),
        'rwkv_w_out': nrm((N_RWKV, d, d), DEEPNORM_BETA * d ** -0.5),
        'ret_w_in': nrm((N_RET, d, 2 * d + 2 * hv), d ** -0.5),
        'ret_decay': ret_logit[None, None, :] + nrm((N_RET, 2, RET_HEADS), 0.1),
        'ret_gn_g': gain((N_RET, hv)),
        'ret_gn_b': nrm((N_RET, hv), 0.01),
        'ret_w_out': nrm((N_RET, hv, d), DEEPNORM_BETA * hv ** -0.5),
        'hgrn_w_in': nrm((N_HGRN, d, 5 * d), d ** -0.5),
        'hgrn_b_f': nrm((N_HGRN, 2, d), 0.5),
        'hgrn_lb': nrm((DEPTH, d), 0.5),
        'hgrn_norm_g': gain((N_HGRN, HGRN_HEAD_V)),
        'hgrn_w_out': nrm((N_HGRN, d, d), DEEPNORM_BETA * d ** -0.5),
        'moe_router': nrm((DEPTH, d, N_EXPERTS), d ** -0.5),
        'moe_bias': nrm((DEPTH, N_EXPERTS), 0.01),
        'moe_w_gu': nrm((DEPTH, N_EXPERTS, d, 2 * EXPERT_DIM), d ** -0.5),
        'moe_w_down': nrm((DEPTH, N_EXPERTS, EXPERT_DIM, d), DEEPNORM_BETA * EXPERT_DIM ** -0.5),
        'moe_sh_gu': nrm((DEPTH, d, 2 * SHARED_DIM), d ** -0.5),
        'moe_sh_down': nrm((DEPTH, SHARED_DIM, d), DEEPNORM_BETA * SHARED_DIM ** -0.5),
    }


def reference(x, c, ctx, c_ctx, ada_w, ada_b, post_ln_g, post_ln_b,
              lru_w_in, lru_conv_w, lru_conv_b, lru_gate_w, lru_gate_b, lru_lambda, lru_w_out,
              rwkv_mu, rwkv_w_in, rwkv_w0, rwkv_w_l1, rwkv_w_l2, rwkv_a0, rwkv_a_l1, rwkv_a_l2,
              rwkv_g_l1, rwkv_g_l2, rwkv_k_k, rwkv_k_a, rwkv_r_k, rwkv_ln_g, rwkv_ln_b, rwkv_w_out,
              ret_w_in, ret_decay, ret_gn_g, ret_gn_b, ret_w_out,
              hgrn_w_in, hgrn_b_f, hgrn_lb, hgrn_norm_g, hgrn_w_out,
              moe_router, moe_bias, moe_w_gu, moe_w_down, moe_sh_gu, moe_sh_down):
    n_ctx = ctx.shape[1]
    n_lat = x.shape[1]
    rows = n_lat // GRID_W
    pos_row = jnp.repeat(jnp.arange(rows, dtype=jnp.float32), GRID_W)
    pos_col = jnp.tile(jnp.arange(GRID_W, dtype=jnp.float32), rows)
    n_freq = RET_QK // 4
    freqs = ROPE_BASE ** (-jnp.arange(n_freq, dtype=jnp.float32) / n_freq)
    ang = jnp.concatenate([pos_row[:, None] * freqs, pos_col[:, None] * freqs], axis=-1)
    ang = jnp.concatenate([ang, ang], axis=-1)
    rope_cos = jnp.cos(ang).astype(x.dtype)
    rope_sin = jnp.sin(ang).astype(x.dtype)
    lb_cum = jnp.cumsum(jax.nn.softmax(hgrn_lb.astype(jnp.float32), axis=0), axis=0).astype(x.dtype)

    s_lat = jax.nn.silu(c)[:, None, :]
    s_ctx = jax.nn.silu(c_ctx)[None, None, :]
    h_ctx, h_lat = ctx, x
    for i in range(DEPTH):
        kind, j = i % N_MIXERS, i // N_MIXERS
        m_lat = jnp.split(s_lat @ ada_w[i] + ada_b[i], 6, axis=-1)
        m_ctx = jnp.split(s_ctx @ ada_w[i] + ada_b[i], 6, axis=-1)
        u_ctx = _modulate(h_ctx, m_ctx[0], m_ctx[1])
        u_lat = _modulate(h_lat, m_lat[0], m_lat[1])
        if kind == 0:
            y_ctx, y_lat = _lru_mixer(u_ctx, u_lat, lru_w_in[j], lru_conv_w[j], lru_conv_b[j], lru_gate_w[j], lru_gate_b[j], lru_lambda[j], lru_w_out[j])
        elif kind == 1:
            y_ctx, y_lat = _rwkv_mixer(u_ctx, u_lat, rwkv_mu[j], rwkv_w_in[j], rwkv_w0[j], rwkv_w_l1[j], rwkv_w_l2[j], rwkv_a0[j], rwkv_a_l1[j], rwkv_a_l2[j], rwkv_g_l1[j], rwkv_g_l2[j], rwkv_k_k[j], rwkv_k_a[j], rwkv_r_k[j], rwkv_ln_g[j], rwkv_ln_b[j], rwkv_w_out[j])
        elif kind == 2:
            y_ctx, y_lat = _retention_mixer(u_ctx, u_lat, rope_cos, rope_sin, ret_w_in[j], ret_decay[j], ret_gn_g[j], ret_gn_b[j], ret_w_out[j])
        else:
            y_ctx, y_lat = _hgrn_mixer(u_ctx, u_lat, lb_cum[i] - lb_cum[0], hgrn_w_in[j], hgrn_b_f[j], hgrn_norm_g[j], hgrn_w_out[j])
        h_lat = _layer_norm(DEEPNORM_ALPHA * h_lat + m_lat[2] * y_lat, post_ln_g[i, 0], post_ln_b[i, 0])
        moe_args = (moe_router[i], moe_bias[i], moe_w_gu[i], moe_w_down[i], moe_sh_gu[i], moe_sh_down[i])
        if i < DEPTH - 1:
            h_ctx = _layer_norm(DEEPNORM_ALPHA * h_ctx + m_ctx[2] * y_ctx, post_ln_g[i, 0], post_ln_b[i, 0])
            u = jnp.concatenate([_modulate(h_ctx, m_ctx[3], m_ctx[4]), _modulate(h_lat, m_lat[3], m_lat[4])], axis=1)
            y = _moe_ffn(u.reshape(-1, D_MODEL), *moe_args).reshape(u.shape)
            h_ctx = _layer_norm(DEEPNORM_ALPHA * h_ctx + m_ctx[5] * y[:, :n_ctx], post_ln_g[i, 1], post_ln_b[i, 1])
            h_lat = _layer_norm(DEEPNORM_ALPHA * h_lat + m_lat[5] * y[:, n_ctx:], post_ln_g[i, 1], post_ln_b[i, 1])
        else:
            u = _modulate(h_lat, m_lat[3], m_lat[4])
            y = _moe_ffn(u.reshape(-1, D_MODEL), *moe_args).reshape(u.shape)
            h_lat = _layer_norm(DEEPNORM_ALPHA * h_lat + m_lat[5] * y, post_ln_g[i, 1], post_ln_b[i, 1])
    return h_lat
```

```python
import math
import functools
import jax
import jax.numpy as jnp
from jax import lax
from jax.experimental import pallas as pl
from jax.experimental.pallas import tpu as pltpu

D_MODEL = 1024
DEPTH = 4
GRID_W = 64
N_MIXERS = 4
DEEPNORM_ALPHA = (2.0 * DEPTH) ** 0.25
LN_EPS = 1e-5
LRU_WIDTH = D_MODEL
LRU_BLOCKS = 16
LRU_BLOCK = LRU_WIDTH // LRU_BLOCKS
LRU_C = 8.0
RWKV_HEAD = 64
RWKV_HEADS = D_MODEL // RWKV_HEAD
RWKV_DECAY_SCALE = math.exp(-0.5)
RWKV_GN_EPS = 64e-5
RET_HEADS = 4
RET_QK = D_MODEL // RET_HEADS
RET_V = 2 * RET_QK
RET_CHUNK = 128
ROPE_BASE = 10000.0
HGRN_HEADS = 8
HGRN_EXPAND = D_MODEL // HGRN_HEADS
HGRN_HEAD_V = D_MODEL // HGRN_HEADS
HGRN_CHUNK = 64
N_EXPERTS = 64
TOP_K = 8
N_GROUPS = 8
TOPK_GROUPS = 4
ROUTED_SCALE = 2.5
MOE_BLOCK = 128


def _ln_kernel(h_ref, y_ref, gate_ref, g_ref, b_ref, o_ref):
    z = DEEPNORM_ALPHA * h_ref[...] + gate_ref[...] * y_ref[...]
    mu = jnp.mean(z, axis=-1, keepdims=True)
    zc = z - mu
    var = jnp.mean(zc * zc, axis=-1, keepdims=True)
    o_ref[...] = zc * lax.rsqrt(var + LN_EPS) * g_ref[...] + b_ref[...]


def _post_ln(h, y, gate, g, b):
    t, d = h.shape
    tm = 256
    row = pl.BlockSpec((tm, d), lambda i: (i, 0))
    vec = pl.BlockSpec((1, d), lambda i: (0, 0))
    return pl.pallas_call(
        _ln_kernel, grid=(t // tm,), in_specs=[row, row, vec, vec, vec], out_specs=row,
        out_shape=jax.ShapeDtypeStruct((t, d), h.dtype))(h, y, gate, g, b)


def _layer_norm(h, g, b):
    hf = h.astype(jnp.float32)
    mu = jnp.mean(hf, axis=-1, keepdims=True)
    var = jnp.mean(jnp.square(hf - mu), axis=-1, keepdims=True)
    return ((hf - mu) * lax.rsqrt(var + LN_EPS)).astype(h.dtype) * g + b


def _head_norm(o, eps):
    of = o.astype(jnp.float32)
    mu = jnp.mean(of, axis=-1, keepdims=True)
    var = jnp.mean(jnp.square(of - mu), axis=-1, keepdims=True)
    return ((of - mu) * lax.rsqrt(var + eps)).astype(o.dtype)


def _head_rms(o, eps):
    of = o.astype(jnp.float32)
    return (of * lax.rsqrt(jnp.mean(jnp.square(of), axis=-1, keepdims=True) + eps)).astype(o.dtype)


def _modulate(h, shift, scale):
    return h * (1.0 + scale) + shift


def _heads(z, n):
    return z.reshape(*z.shape[:-1], n, z.shape[-1] // n)


def _merge(z):
    return z.reshape(*z.shape[:-2], z.shape[-2] * z.shape[-1])


def _rev(z, d):
    return z[:, ::-1] if d == 1 else z


def _to_blocks(z, size):
    b, t, h, f = z.shape
    return z.reshape(b, t // size, size, h, f).transpose(1, 0, 3, 2, 4)


def _from_blocks(z):
    n, b, h, size, f = z.shape
    return z.transpose(1, 0, 3, 2, 4).reshape(b, n * size, h, f)


def _dwconv_centred(z, w, b):
    k = w.shape[0]
    y = lax.conv_general_dilated(z, w[:, None, :], window_strides=(1,), padding=[((k - 1) // 2, k // 2)], dimension_numbers=('NWC', 'WIO', 'NWC'), feature_group_count=z.shape[-1])
    return y + b


def _rope(z, cos, sin):
    half = z.shape[-1] // 2
    rot = jnp.concatenate([-z[..., half:], z[..., :half]], axis=-1)
    return z * cos[None, :, None, :] + rot * sin[None, :, None, :]


def _lru_gates(xc, gate_w, gate_b, lam):
    b, t, w = xc.shape
    xb = xc.reshape(b, t, LRU_BLOCKS, LRU_BLOCK)
    gates = jax.nn.sigmoid(jnp.einsum('btni,gnij->gbtnj', xb, gate_w).reshape(2, b, t, w) + gate_b[:, None, None, :])
    log_a = -LRU_C * gates[0] * jax.nn.softplus(-lam)
    return jnp.exp(log_a), jnp.sqrt(1.0 - jnp.exp(2.0 * log_a)) * (gates[1] * xc)


def _linear_scan(a, bx, h0, reverse):
    def combine(e1, e2):
        return e1[0] * e2[0], e2[0] * e1[1] + e2[1]
    a_cum, b_cum = lax.associative_scan(combine, (a, bx), reverse=reverse, axis=1)
    return a_cum * h0[:, None, :] + b_cum


def _lru_mixer(u_ctx, u_lat, w_in, conv_w, conv_b, gate_w, gate_b, lam, w_out):
    def branches(u):
        gelu_in, rnn_in = jnp.split(u @ w_in, 2, axis=-1)
        return jax.nn.gelu(gelu_in), _dwconv_centred(rnn_in, conv_w, conv_b)
    g_ctx, x_ctx = branches(u_ctx)
    g_lat, x_lat = branches(u_lat)
    h0 = jnp.zeros_like(x_ctx[:, 0])
    hs_ctx, hs_lat = [], []
    for d in range(2):
        a_c, b_c = _lru_gates(x_ctx, gate_w[d], gate_b[d], lam[d])
        a_l, b_l = _lru_gates(x_lat, gate_w[d], gate_b[d], lam[d])
        h_c = _linear_scan(a_c, b_c, h0, d == 1)
        h_end = h_c[:, 0] if d == 1 else h_c[:, -1]
        hs_ctx.append(h_c)
        hs_lat.append(_linear_scan(a_l, b_l, h_end, d == 1))
    return (g_ctx * (hs_ctx[0] + hs_ctx[1])) @ w_out, (g_lat * (hs_lat[0] + hs_lat[1])) @ w_out


def _bi_shift(u):
    half = u.shape[-1] // 2
    prev = jnp.pad(u[:, :-1, :half], ((0, 0), (1, 0), (0, 0)))
    nxt = jnp.pad(u[:, 1:, half:], ((0, 0), (0, 1), (0, 0)))
    return jnp.concatenate([prev, nxt], axis=-1)


def _rwkv_prep(u, mu, w_in, w0, w_l1, w_l2, a0, a_l1, a_l2, g_l1, g_l2, k_k, k_a):
    xm = u[None] + (_bi_shift(u) - u)[None] * mu[:, None, None, :]
    rkv = jnp.einsum('cbtd,cde->cbte', xm[:3], w_in)
    r, k, v = rkv[0], rkv[1], rkv[2]
    d_w = w0[:, None, None, :] + jnp.einsum('zbtl,zld->zbtd', jnp.tanh(jnp.einsum('btd,zdl->zbtl', xm[3], w_l1)), w_l2)
    w = jnp.exp(-RWKV_DECAY_SCALE * jax.nn.sigmoid(d_w))
    a = jax.nn.sigmoid(a0[:, None, None, :] + jnp.einsum('zbtl,zld->zbtd', jnp.einsum('btd,zdl->zbtl', xm[4], a_l1), a_l2))
    g = jax.nn.sigmoid(xm[5] @ g_l1) @ g_l2
    kk = _heads(k * k_k, RWKV_HEADS)
    kk = kk * lax.rsqrt(jnp.sum(jnp.square(kk), axis=-1, keepdims=True) + 1e-12)
    k_t = k[None] * (1.0 + (a - 1.0) * k_a)
    h = RWKV_HEADS
    return _heads(r, h), _heads(v, h), kk, g, _heads(w, h), _heads(k_t, h), _heads(a, h)


def _rwkv_scan(r, w, k, v, kk, a, s0, reverse):
    xs = tuple(jnp.moveaxis(z, 1, 0) for z in (r, w, k, v, kk, a))
    def step(s, inp):
        r_t, w_t, k_t, v_t, kk_t, a_t = inp
        sa = jnp.einsum('bhvk,bhk->bhv', s, kk_t)
        s = s * w_t[:, :, None, :] - sa[..., None] * (kk_t * a_t)[:, :, None, :] + v_t[..., None] * k_t[:, :, None, :]
        return s, jnp.einsum('bhvk,bhk->bhv', s, r_t)
    s_end, o = lax.scan(step, s0, xs, reverse=reverse)
    return jnp.moveaxis(o, 0, 1), s_end


def _rwkv_mixer(u_ctx, u_lat, mu, w_in, w0, w_l1, w_l2, a0, a_l1, a_l2, g_l1, g_l2, k_k, k_a, r_k, ln_g, ln_b, w_out):
    def prep(u):
        return _rwkv_prep(u, mu, w_in, w0, w_l1, w_l2, a0, a_l1, a_l2, g_l1, g_l2, k_k, k_a)
    p_ctx = prep(u_ctx)
    p_lat = prep(u_lat)
    s0 = jnp.zeros((u_ctx.shape[0], RWKV_HEADS, RWKV_HEAD, RWKV_HEAD), u_ctx.dtype)
    o_ctx, o_lat = [], []
    for d in range(2):
        r, v, kk, _, w, k_t, a = p_ctx
        oc, s_c = _rwkv_scan(r, w[d], k_t[d], v, kk, a[d], s0, d == 1)
        r, v, kk, _, w, k_t, a = p_lat
        ol, _ = _rwkv_scan(r, w[d], k_t[d], v, kk, a[d], s_c, d == 1)
        o_ctx.append(oc)
        o_lat.append(ol)
    def readout(p, o):
        r, v, _, g, _, k_t, _ = p
        bonus = jnp.sum(r[None] * k_t * r_k, axis=-1, keepdims=True).sum(axis=0) * v
        y = _merge(_head_norm(o, RWKV_GN_EPS)) * ln_g + ln_b + _merge(bonus)
        return (y * g) @ w_out
    return readout(p_ctx, o_ctx[0] + o_ctx[1]), readout(p_lat, o_lat[0] + o_lat[1])


def _retention_chunks(q, k, v, log_gamma, r0):
    c = RET_CHUNK
    qb, kb, vb = _to_blocks(q, c), _to_blocks(k, c), _to_blocks(v, c)
    pos = jnp.arange(c, dtype=jnp.float32)
    rel = pos[:, None] - pos[None, :]
    lg = log_gamma[:, None, None]
    inner = jnp.where(rel >= 0, jnp.exp(jnp.maximum(rel, 0.0) * lg), 0.0).astype(q.dtype)
    q_dec = jnp.exp((pos + 1.0) * log_gamma[:, None]).astype(q.dtype)[:, :, None]
    k_dec = jnp.exp((c - 1.0 - pos) * log_gamma[:, None]).astype(q.dtype)[:, :, None]
    blk_dec = jnp.exp(c * log_gamma).astype(q.dtype)[:, None, None]
    def step(r, inp):
        q_c, k_c, v_c = inp
        scores = jnp.einsum('bhid,bhjd->bhij', q_c, k_c) * inner
        o = jnp.einsum('bhij,bhjv->bhiv', scores, v_c) + jnp.einsum('bhid,bhdv->bhiv', q_c, r) * q_dec
        r = r * blk_dec + jnp.einsum('bhjd,bhjv->bhdv', k_c * k_dec, v_c)
        return r, o
    r_end, o = lax.scan(step, r0, (qb, kb, vb))
    return _from_blocks(o), r_end


def _retention_mixer(u_ctx, u_lat, rope_cos, rope_sin, w_in, decay_logit, gn_g, gn_b, w_out):
    def proj(u):
        q, k, v, g = jnp.split(u @ w_in, [D_MODEL, 2 * D_MODEL, 2 * D_MODEL + RET_HEADS * RET_V], axis=-1)
        return _heads(q, RET_HEADS), _heads(k, RET_HEADS) * (RET_QK ** -0.5), _heads(v, RET_HEADS), g
    q_c, k_c, v_c, g_c = proj(u_ctx)
    q_l, k_l, v_l, g_l = proj(u_lat)
    q_l = _rope(q_l, rope_cos, rope_sin)
    k_l = _rope(k_l, rope_cos, rope_sin)
    log_gamma = jax.nn.log_sigmoid(decay_logit.astype(jnp.float32))
    r0 = jnp.zeros((u_ctx.shape[0], RET_HEADS, RET_QK, RET_V), u_ctx.dtype)
    o_ctx, o_lat = [], []
    for d in range(2):
        oc, r_c = _retention_chunks(_rev(q_c, d), _rev(k_c, d), _rev(v_c, d), log_gamma[d], r0)
        ol, _ = _retention_chunks(_rev(q_l, d), _rev(k_l, d), _rev(v_l, d), log_gamma[d], r_c)
        o_ctx.append(_rev(oc, d))
        o_lat.append(_rev(ol, d))
    def readout(o, g):
        y = _merge(_head_norm(o, LN_EPS)) * gn_g + gn_b
        return (jax.nn.silu(g) * y) @ w_out
    return readout(o_ctx[0] + o_ctx[1], g_c), readout(o_lat[0] + o_lat[1], g_l)


def _gla_chunks(q, k, v, log_f, s0):
    c = HGRN_CHUNK
    qb, kb, vb = _to_blocks(q, c), _to_blocks(k, c), _to_blocks(v, c)
    cum = jnp.cumsum(_to_blocks(log_f, c), axis=3)
    causal = jnp.tril(jnp.ones((c, c), dtype=bool))[:, :, None]
    def step(s, inp):
        q_c, k_c, v_c, b_c = inp
        diff = b_c[:, :, :, None, :] - b_c[:, :, None, :, :]
        decay = jnp.exp(jnp.where(causal, diff, -jnp.inf))
        scores = jnp.einsum('bhtk,bhsk,bhtsk->bhts', q_c, k_c, decay)
        o = jnp.einsum('bhts,bhsv->bhtv', scores, v_c) + jnp.einsum('bhtk,bhkv->bhtv', q_c * jnp.exp(b_c), s)
        b_last = b_c[:, :, -1:, :]
        s = s * jnp.exp(b_last[:, :, 0, :])[..., None] + jnp.einsum('bhsk,bhsv->bhkv', k_c * jnp.exp(b_last - b_c), v_c)
        return s, o
    s_end, o = lax.scan(step, s0, (qb, kb, vb, cum))
    return _from_blocks(o), s_end


def _hgrn_mixer(u_ctx, u_lat, lb, w_in, b_f, norm_g, w_out):
    def proj(u):
        q, f_fwd, f_bwd, i_in, g = jnp.split(u @ w_in, 5, axis=-1)
        f = lb + (1.0 - lb) * jax.nn.sigmoid(jnp.stack([f_fwd, f_bwd]) + b_f[:, None, None, :])
        return _heads(jax.nn.silu(q), HGRN_HEADS), _heads(i_in, HGRN_HEADS), _heads(f, HGRN_HEADS), g
    q_c, v_c, f_c, g_c = proj(u_ctx)
    q_l, v_l, f_l, g_l = proj(u_lat)
    s0 = jnp.zeros((u_ctx.shape[0], HGRN_HEADS, HGRN_EXPAND, HGRN_HEAD_V), u_ctx.dtype)
    o_ctx, o_lat = [], []
    for d in range(2):
        oc, s_c = _gla_chunks(_rev(q_c, d), _rev(1.0 - f_c[d], d), _rev(v_c, d), _rev(jnp.log(f_c[d]), d), s0)
        ol, _ = _gla_chunks(_rev(q_l, d), _rev(1.0 - f_l[d], d), _rev(v_l, d), _rev(jnp.log(f_l[d]), d), s_c)
        o_ctx.append(_rev(oc, d))
        o_lat.append(_rev(ol, d))
    def readout(o, g):
        return (_merge(_head_rms(o, LN_EPS) * norm_g) * jax.nn.silu(g)) @ w_out
    return readout(o_ctx[0] + o_ctx[1], g_c), readout(o_lat[0] + o_lat[1], g_l)


def _swiglu(z, w_gu, w_down):
    gate, up = jnp.split(z @ w_gu, 2, axis=-1)
    return (jax.nn.silu(gate) * up) @ w_down


def _routed_experts(u, top_idx, top_w, w_gu, w_down):
    n = u.shape[0]
    n_pairs = n * TOP_K
    n_blocks = -(-(n_pairs + N_EXPERTS * (MOE_BLOCK - 1)) // MOE_BLOCK)
    flat_e = top_idx.reshape(-1)
    flat_tok = jnp.arange(n_pairs, dtype=jnp.int32) // TOP_K
    order = jnp.argsort(flat_e)
    e_sorted = flat_e[order]
    counts = jnp.bincount(flat_e, length=N_EXPERTS)
    padded = (counts + MOE_BLOCK - 1) // MOE_BLOCK * MOE_BLOCK
    start = jnp.cumsum(counts) - counts
    padded_end = jnp.cumsum(padded)
    dest = (padded_end - padded)[e_sorted] + jnp.arange(n_pairs, dtype=jnp.int32) - start[e_sorted]
    row_tok = jnp.zeros((n_blocks * MOE_BLOCK,), jnp.int32).at[dest].set(flat_tok[order])
    row_w = jnp.zeros((n_blocks * MOE_BLOCK,), u.dtype).at[dest].set(top_w.reshape(-1)[order])
    block_e = jnp.minimum(jnp.searchsorted(padded_end, jnp.arange(n_blocks, dtype=jnp.int32) * MOE_BLOCK, side='right'), N_EXPERTS - 1)
    def body(acc, inp):
        e, toks, wts = inp
        y = _swiglu(u[toks], w_gu[e], w_down[e])
        return acc.at[toks].add(y * wts[:, None]), None
    acc, _ = lax.scan(body, jnp.zeros_like(u), (block_e, row_tok.reshape(n_blocks, MOE_BLOCK), row_w.reshape(n_blocks, MOE_BLOCK)))
    return acc


def _moe_ffn(u, router_w, router_bias, w_gu, w_down, sh_gu, sh_down):
    n = u.shape[0]
    scores = jax.nn.sigmoid((u @ router_w).astype(jnp.float32))
    choice = scores + router_bias.astype(jnp.float32)
    group_score = lax.top_k(choice.reshape(n, N_GROUPS, N_EXPERTS // N_GROUPS), 2)[0].sum(axis=-1)
    _, top_groups = lax.top_k(group_score, TOPK_GROUPS)
    group_mask = jnp.any(top_groups[:, :, None] == jnp.arange(N_GROUPS)[None, None, :], axis=1)
    expert_mask = jnp.repeat(group_mask, N_EXPERTS // N_GROUPS, axis=1)
    _, top_idx = lax.top_k(jnp.where(expert_mask, choice, -jnp.inf), TOP_K)
    top_w = jnp.take_along_axis(scores, top_idx, axis=1)
    top_w = ROUTED_SCALE * top_w / jnp.sum(top_w, axis=-1, keepdims=True)
    return _routed_experts(u, top_idx, top_w.astype(u.dtype), w_gu, w_down) + _swiglu(u, sh_gu, sh_down)


def kernel(x, c, ctx, c_ctx, ada_w, ada_b, post_ln_g, post_ln_b, lru_w_in, lru_conv_w, lru_conv_b, lru_gate_w, lru_gate_b, lru_lambda, lru_w_out, rwkv_mu, rwkv_w_in, rwkv_w0, rwkv_w_l1, rwkv_w_l2, rwkv_a0, rwkv_a_l1, rwkv_a_l2, rwkv_g_l1, rwkv_g_l2, rwkv_k_k, rwkv_k_a, rwkv_r_k, rwkv_ln_g, rwkv_ln_b, rwkv_w_out, ret_w_in, ret_decay, ret_gn_g, ret_gn_b, ret_w_out, hgrn_w_in, hgrn_b_f, hgrn_lb, hgrn_norm_g, hgrn_w_out, moe_router, moe_bias, moe_w_gu, moe_w_down, moe_sh_gu, moe_sh_down):
    n_ctx = ctx.shape[1]
    n_lat = x.shape[1]
    rows = n_lat // GRID_W
    pos_row = jnp.repeat(jnp.arange(rows, dtype=jnp.float32), GRID_W)
    pos_col = jnp.tile(jnp.arange(GRID_W, dtype=jnp.float32), rows)
    n_freq = RET_QK // 4
    freqs = ROPE_BASE ** (-jnp.arange(n_freq, dtype=jnp.float32) / n_freq)
    ang = jnp.concatenate([pos_row[:, None] * freqs, pos_col[:, None] * freqs], axis=-1)
    ang = jnp.concatenate([ang, ang], axis=-1)
    rope_cos = jnp.cos(ang).astype(x.dtype)
    rope_sin = jnp.sin(ang).astype(x.dtype)
    lb_cum = jnp.cumsum(jax.nn.softmax(hgrn_lb.astype(jnp.float32), axis=0), axis=0).astype(x.dtype)

    s_lat = jax.nn.silu(c)[:, None, :]
    s_ctx = jax.nn.silu(c_ctx)[None, None, :]
    h_ctx, h_lat = ctx, x
    for i in range(DEPTH):
        kind, j = i % N_MIXERS, i // N_MIXERS
        m_lat = jnp.split(s_lat @ ada_w[i] + ada_b[i], 6, axis=-1)
        m_ctx = jnp.split(s_ctx @ ada_w[i] + ada_b[i], 6, axis=-1)
        u_ctx = _modulate(h_ctx, m_ctx[0], m_ctx[1])
        u_lat = _modulate(h_lat, m_lat[0], m_lat[1])
        if kind == 0:
            y_ctx, y_lat = _lru_mixer(u_ctx, u_lat, lru_w_in[j], lru_conv_w[j], lru_conv_b[j], lru_gate_w[j], lru_gate_b[j], lru_lambda[j], lru_w_out[j])
        elif kind == 1:
            y_ctx, y_lat = _rwkv_mixer(u_ctx, u_lat, rwkv_mu[j], rwkv_w_in[j], rwkv_w0[j], rwkv_w_l1[j], rwkv_w_l2[j], rwkv_a0[j], rwkv_a_l1[j], rwkv_a_l2[j], rwkv_g_l1[j], rwkv_g_l2[j], rwkv_k_k[j], rwkv_k_a[j], rwkv_r_k[j], rwkv_ln_g[j], rwkv_ln_b[j], rwkv_w_out[j])
        elif kind == 2:
            y_ctx, y_lat = _retention_mixer(u_ctx, u_lat, rope_cos, rope_sin, ret_w_in[j], ret_decay[j], ret_gn_g[j], ret_gn_b[j], ret_w_out[j])
        else:
            y_ctx, y_lat = _hgrn_mixer(u_ctx, u_lat, lb_cum[i] - lb_cum[0], hgrn_w_in[j], hgrn_b_f[j], hgrn_norm_g[j], hgrn_w_out[j])
        h_lat = _layer_norm(DEEPNORM_ALPHA * h_lat + m_lat[2] * y_lat, post_ln_g[i, 0], post_ln_b[i, 0])
        moe_args = (moe_router[i], moe_bias[i], moe_w_gu[i], moe_w_down[i], moe_sh_gu[i], moe_sh_down[i])
        if i < DEPTH - 1:
            h_ctx = _layer_norm(DEEPNORM_ALPHA * h_ctx + m_ctx[2] * y_ctx, post_ln_g[i, 0], post_ln_b[i, 0])
            u = jnp.concatenate([_modulate(h_ctx, m_ctx[3], m_ctx[4]), _modulate(h_lat, m_lat[3], m_lat[4])], axis=1)
            y = _moe_ffn(u.reshape(-1, D_MODEL), *moe_args).reshape(u.shape)
            h_ctx = _layer_norm(DEEPNORM_ALPHA * h_ctx + m_ctx[5] * y[:, :n_ctx], post_ln_g[i, 1], post_ln_b[i, 1])
            h_lat = _layer_norm(DEEPNORM_ALPHA * h_lat + m_lat[5] * y[:, n_ctx:], post_ln_g[i, 1], post_ln_b[i, 1])
        else:
            u = _modulate(h_lat, m_lat[3], m_lat[4])
            y = _moe_ffn(u.reshape(-1, D_MODEL), *moe_args).reshape(u.shape)
            h_lat = _post_ln(h_lat[0], y[0], m_lat[5][0], post_ln_g[i, 1][None], post_ln_b[i, 1][None])[None]
    return h_lat
```

```python
import math
import functools
import jax
import jax.numpy as jnp
from jax import lax
from jax.experimental import pallas as pl
from jax.experimental.pallas import tpu as pltpu

F32 = jnp.float32
MXU_DT = jnp.bfloat16
LANES = 128
TM = 256
VMEM_LIMIT = 56 * 2 ** 20

D_MODEL = 1024
DEPTH = 4
GRID_W = 64
N_MIXERS = 4
DEEPNORM_ALPHA = (2.0 * DEPTH) ** 0.25
LN_EPS = 1e-5
LRU_WIDTH = D_MODEL
LRU_BLOCKS = 16
LRU_BLOCK = LRU_WIDTH // LRU_BLOCKS
LRU_C = 8.0
RWKV_HEAD = 64
RWKV_HEADS = D_MODEL // RWKV_HEAD
RWKV_DECAY_SCALE = math.exp(-0.5)
RWKV_GN_EPS = 64e-5
RWKV_CHUNK = 64
RET_HEADS = 4
RET_QK = D_MODEL // RET_HEADS
RET_V = 2 * RET_QK
RET_CHUNK = 128
ROPE_BASE = 10000.0
HGRN_HEADS = 8
HGRN_HEAD = D_MODEL // HGRN_HEADS
HGRN_BLOCK = 16
N_EXPERTS = 64
TOP_K = 8
N_GROUPS = 8
TOPK_GROUPS = 4
EXPERT_DIM = 256
ROUTED_SCALE = 2.5
MOE_TM = 1280


def _cparams(*sem):
    return pltpu.CompilerParams(dimension_semantics=sem, vmem_limit_bytes=VMEM_LIMIT)


def _dot(a, b):
    return jnp.dot(a.astype(MXU_DT), b.astype(MXU_DT), preferred_element_type=F32)


def _dot_nt(a, b):
    return lax.dot_general(a.astype(MXU_DT), b.astype(MXU_DT), (((1,), (1,)), ((), ())), preferred_element_type=F32)


def _dot_tn(a, b):
    return lax.dot_general(a.astype(MXU_DT), b.astype(MXU_DT), (((0,), (0,)), ((), ())), preferred_element_type=F32)


def _split(x, n):
    parts = []
    for _ in range(n):
        p = x.astype(MXU_DT)
        parts.append(p)
        x = x - p.astype(F32)
    return parts


def _dot_sel(sel, x, n):
    return sum(jnp.dot(sel.astype(MXU_DT), p, preferred_element_type=F32) for p in _split(x, n))


def _dot_xsel(x, sel, n):
    return sum(jnp.dot(p, sel.astype(MXU_DT), preferred_element_type=F32) for p in _split(x, n))


def _modulate(h, m, shift_idx):
    return h * (1.0 + m[shift_idx + 1:shift_idx + 2]) + m[shift_idx:shift_idx + 1]


def _ln_rows(z, g, b):
    mu = jnp.mean(z, axis=-1, keepdims=True)
    zc = z - mu
    var = jnp.mean(zc * zc, axis=-1, keepdims=True)
    return zc * lax.rsqrt(var + LN_EPS) * g + b


def _silu(x):
    return x * jax.nn.sigmoid(x)


def _shift_down(x, first_row):
    rows = lax.broadcasted_iota(jnp.int32, (x.shape[0], 1), 0)
    return jnp.where(rows == 0, first_row, pltpu.roll(x, 1, 0))


def _shift_up(x, last_row):
    n = x.shape[0]
    rows = lax.broadcasted_iota(jnp.int32, (n, 1), 0)
    return jnp.where(rows == n - 1, last_row, pltpu.roll(x, n - 1, 0))


def _tile_of(g, nct, nt, reverse):
    if not reverse:
        return g
    return jnp.where(g < nct, nct - 1 - g, nt - 1 - (g - nct))


def _halo_flags(t, nct, nt):
    prev_ok = jnp.logical_and(t != 0, t != nct).astype(F32)
    next_ok = jnp.logical_and(t != nct - 1, t != nt - 1).astype(F32)
    return prev_ok, next_ok


def _ada_kernel(s_ref, w_ref, b_ref, o_ref):
    o_ref[0] = _dot(_silu(s_ref[...]), w_ref[0]) + b_ref[0]


def _ada_mods(cond, ada_w, ada_b):
    nl, d, n6 = ada_w.shape
    out = pl.pallas_call(
        _ada_kernel, grid=(nl, n6 // d),
        in_specs=[pl.BlockSpec((8, d), lambda l, j: (0, 0)),
                  pl.BlockSpec((1, d, d), lambda l, j: (l, 0, j)),
                  pl.BlockSpec((1, 1, d), lambda l, j: (l, 0, j))],
        out_specs=pl.BlockSpec((1, 8, d), lambda l, j: (l, 0, j)),
        out_shape=jax.ShapeDtypeStruct((nl, 8, n6), F32),
        compiler_params=_cparams("arbitrary", "arbitrary"), name="ada_mods",
    )(cond, ada_w, ada_b.reshape(nl, 1, n6))
    return out[:, :2].reshape(nl, 2, 6, d)


def _row_spec(width, tm=TM):
    return pl.BlockSpec((tm, width), lambda i: (i, 0))


def _full_spec(shape):
    nd = len(shape)
    return pl.BlockSpec(tuple(shape), lambda *_: (0,) * nd)


def _mod_spec(nct):
    return pl.BlockSpec((1, 6, D_MODEL), lambda i: (jnp.minimum(i // nct, 1), 0, 0))


def _out_ln_kernel(p_ref, w_ref, h_ref, mod_ref, lng_ref, lnb_ref, o_ref):
    y = _dot(p_ref[...], w_ref[...])
    z = DEEPNORM_ALPHA * h_ref[...] + mod_ref[0][2:3] * y
    o_ref[...] = _ln_rows(z, lng_ref[...], lnb_ref[...])


def _out_ln(p, w_out, h, mods, ln_g, ln_b, nct):
    t, din = p.shape
    d = D_MODEL
    return pl.pallas_call(
        _out_ln_kernel, grid=(t // TM,),
        in_specs=[_row_spec(din), _full_spec((din, d)), _row_spec(d), _mod_spec(nct),
                  _full_spec((1, d)), _full_spec((1, d))],
        out_specs=_row_spec(d), out_shape=jax.ShapeDtypeStruct((t, d), F32),
        compiler_params=_cparams("arbitrary"), name="out_ln",
    )(p, w_out.astype(MXU_DT), h, mods, ln_g[None], ln_b[None])


def _lru_in_kernel(h_ref, mod_ref, w_ref, g_ref, x_ref):
    u = _modulate(h_ref[...], mod_ref[0], 0)
    z = _dot(u, w_ref[...])
    g_ref[...] = jax.nn.gelu(z[:, :LRU_WIDTH], approximate=True)
    x_ref[...] = z[:, LRU_WIDTH:]


def _lru_scan_kernel(x_ref, xp_ref, xn_ref, cw_ref, cb_ref, gw_ref, gb_ref, lam_ref, *rest, nct, nt, reverse, final):
    if final:
        hf_ref, g_ref, o_ref, a_s, b_s, h_s, st_s = rest
    else:
        o_ref, a_s, b_s, h_s, st_s = rest
    g = pl.program_id(0)
    t = _tile_of(g, nct, nt, reverse)
    prev_ok, next_ok = _halo_flags(t, nct, nt)

    @pl.when(g == 0)
    def _():
        st_s[...] = jnp.zeros_like(st_s)

    x = x_ref[...]
    xm1 = _shift_down(x, xp_ref[7:8, :] * prev_ok)
    n0 = xn_ref[0:1, :] * next_ok
    n1 = xn_ref[1:2, :] * next_ok
    xp1 = _shift_up(x, n0)
    xp2 = _shift_up(xp1, n1)
    cw = cw_ref[...]
    xc = cw[0:1] * xm1 + cw[1:2] * x + cw[2:3] * xp1 + cw[3:4] * xp2 + cb_ref[...]
    gates = jax.nn.sigmoid(_dot(xc, gw_ref[...]) + gb_ref[...])
    lam = lam_ref[...]
    softplus = jnp.maximum(-lam, 0.0) + jnp.log(1.0 + jnp.exp(-jnp.abs(lam)))
    log_a = -LRU_C * gates[:, :LRU_WIDTH] * softplus
    a_s[...] = jnp.exp(log_a)
    b_s[...] = jnp.sqrt(1.0 - jnp.exp(2.0 * log_a)) * (gates[:, LRU_WIDTH:] * xc)

    def row(r, hcur):
        tt = (TM - 1 - r) if reverse else r
        hcur = a_s[pl.ds(tt, 1), :] * hcur + b_s[pl.ds(tt, 1), :]
        h_s[pl.ds(tt, 1), :] = hcur
        return hcur

    st_s[...] = lax.fori_loop(0, TM, row, st_s[...], unroll=8)
    if final:
        o_ref[...] = g_ref[...] * (hf_ref[...] + h_s[...])
    else:
        o_ref[...] = h_s[...]


def _lru_mixer(h, mods, nct, w_in, conv_w, conv_b, gate_w, gate_b, lam, w_out, ln_g, ln_b):
    t, d = h.shape
    nt = t // TM
    w = LRU_WIDTH
    gelu, rnn = pl.pallas_call(
        _lru_in_kernel, grid=(nt,),
        in_specs=[_row_spec(d), _mod_spec(nct), _full_spec((d, 2 * w))],
        out_specs=[_row_spec(w), _row_spec(w)],
        out_shape=[jax.ShapeDtypeStruct((t, w), F32)] * 2,
        compiler_params=_cparams("arbitrary"), name="lru_in",
    )(h, mods, w_in.astype(MXU_DT))
    eye = jnp.eye(LRU_BLOCKS, dtype=F32)
    gw = jnp.einsum('dgnij,nm->dgnimj', gate_w, eye).reshape(2, 2, w, w)
    gw = jnp.concatenate([gw[:, 0], gw[:, 1]], axis=-1).astype(MXU_DT)
    gb = gate_b.reshape(2, 1, 2 * w)
    hf = None
    for d_ in range(2):
        reverse = d_ == 1
        final = d_ == 1
        tile = lambda g: _tile_of(g, nct, nt, reverse)
        ins = [pl.BlockSpec((TM, w), lambda g: (tile(g), 0)),
               pl.BlockSpec((8, w), lambda g: (jnp.maximum(tile(g) * (TM // 8) - 1, 0), 0)),
               pl.BlockSpec((8, w), lambda g: (jnp.minimum((tile(g) + 1) * (TM // 8), t // 8 - 1), 0)),
               _full_spec((4, w)), _full_spec((1, w)), _full_spec((w, 2 * w)), _full_spec((1, 2 * w)),
               _full_spec((1, w))]
        args = [rnn, rnn, rnn, conv_w, conv_b[None], gw[d_], gb[d_], lam[d_][None]]
        if final:
            ins += [pl.BlockSpec((TM, w), lambda g: (tile(g), 0))] * 2
            args += [hf, gelu]
        out = pl.pallas_call(
            functools.partial(_lru_scan_kernel, nct=nct, nt=nt, reverse=reverse, final=final),
            grid=(nt,), in_specs=ins,
            out_specs=pl.BlockSpec((TM, w), lambda g: (tile(g), 0)),
            out_shape=jax.ShapeDtypeStruct((t, w), F32),
            scratch_shapes=[pltpu.VMEM((TM, w), F32)] * 3 + [pltpu.VMEM((1, w), F32)],
            compiler_params=_cparams("arbitrary"), name="lru_scan_%d" % d_,
        )(*args)
        hf = out
    return _out_ln(hf, w_out, h, mods, ln_g, ln_b, nct)


def _seg_sum(x, e_ref, et_ref):
    s = _dot_xsel(x, e_ref[...], 2)
    return _dot_xsel(s, et_ref[...], 2)


def _rwkv_prep_kernel(h_ref, hp_ref, hn_ref, mod_ref, mu_ref, win_ref, wl1_ref, wl2_ref, w0_ref, al1_ref, al2_ref,
                      a0_ref, gl1_ref, gl2_ref, kk_ref, ka_ref, rk_ref, e_ref, et_ref,
                      r_o, v_o, kk_o, g_o, bv_o, lw0_o, lw1_o, kt0_o, kt1_o, ab0_o, ab1_o, *, nct, nt):
    i = pl.program_id(0)
    prev_ok, next_ok = _halo_flags(i, nct, nt)
    m = mod_ref[0]
    u = _modulate(h_ref[...], m, 0)
    up = _modulate(hp_ref[7:8, :], m, 0) * prev_ok
    un = _modulate(hn_ref[0:1, :], m, 0) * next_ok
    lane = lax.broadcasted_iota(jnp.int32, (1, D_MODEL), 1)
    sh = jnp.where(lane < D_MODEL // 2, _shift_down(u, up), _shift_up(u, un))
    dx = sh - u
    mu = mu_ref[...]
    xm = [u + dx * mu[c:c + 1] for c in range(6)]
    r = _dot(xm[0], win_ref[0])
    k = _dot(xm[1], win_ref[1])
    v = _dot(xm[2], win_ref[2])
    t1 = jnp.tanh(_dot(xm[3], wl1_ref[...]))
    t2 = _dot(xm[4], al1_ref[...])
    g = _dot(jax.nn.sigmoid(_dot(xm[5], gl1_ref[...])), gl2_ref[...])
    kk = k * kk_ref[...]
    kk = kk * lax.rsqrt(_seg_sum(kk * kk, e_ref, et_ref) + 1e-12)
    ktsum = None
    for z, (lw_o, kt_o, ab_o) in enumerate(((lw0_o, kt0_o, ab0_o), (lw1_o, kt1_o, ab1_o))):
        d_w = w0_ref[z:z + 1, :] + _dot(t1, wl2_ref[z])
        lw_o[...] = -RWKV_DECAY_SCALE * jax.nn.sigmoid(d_w)
        a = jax.nn.sigmoid(a0_ref[z:z + 1, :] + _dot(t2, al2_ref[z]))
        kt = k * (1.0 + (a - 1.0) * ka_ref[...])
        kt_o[...] = kt
        ab_o[...] = kk * a
        ktsum = kt if ktsum is None else ktsum + kt
    r_o[...] = r
    v_o[...] = v
    kk_o[...] = kk
    g_o[...] = g
    bv_o[...] = _seg_sum(r * ktsum * rk_ref[...], e_ref, et_ref) * v


def _rwkv_scan_kernel(r_ref, v_ref, kk_ref, lw_ref, kt_ref, ab_ref, o_ref, s_ref, *, reverse):
    c = RWKV_CHUNK

    @pl.when(pl.program_id(0) == 0)
    def _():
        s_ref[...] = jnp.zeros_like(s_ref)

    ri = lax.broadcasted_iota(jnp.int32, (c, c), 0)
    ci = lax.broadcasted_iota(jnp.int32, (c, c), 1)
    incl = (ci >= ri) if reverse else (ci <= ri)
    ri2 = lax.broadcasted_iota(jnp.int32, (c, 2 * c), 0)
    ci2 = jnp.bitwise_and(lax.broadcasted_iota(jnp.int32, (c, 2 * c), 1), c - 1)
    incl2 = (ci2 >= ri2) if reverse else (ci2 <= ri2)
    strict2 = (ci2 > ri2) if reverse else (ci2 < ri2)
    lw = lw_ref[...]
    cl = _dot_sel(jnp.where(incl, 1.0, 0.0), lw, 3)
    tot = cl[0:1, :] if reverse else cl[c - 1:c, :]
    e_in = jnp.exp(cl)
    e_out = jnp.exp(-cl)
    e_end = jnp.exp(tot - cl)
    kk = kk_ref[...]
    kt = kt_ref[...]
    ab = ab_ref[...]
    kap = kk * jnp.exp(cl - lw)
    rh = r_ref[...] * e_in
    kh = kt * e_out
    bh = ab * e_out
    kb = kt * e_end
    bb = ab * e_end
    e_tot = jnp.exp(tot)
    vv = v_ref[...]
    lane_a = lax.broadcasted_iota(jnp.int32, (1, LANES), 1) < RWKV_HEAD
    bi = lax.broadcasted_iota(jnp.int32, (LANES, LANES), 0) < RWKV_HEAD
    bj = lax.broadcasted_iota(jnp.int32, (LANES, LANES), 1) < RWKV_HEAD
    blockdiag = bi == bj

    def stack2(x):
        return jnp.concatenate([jnp.where(lane_a, x, 0.0), jnp.where(lane_a, 0.0, x)], axis=0)

    for p in range(D_MODEL // LANES):
        sl = slice(p * LANES, (p + 1) * LANES)
        s = s_ref[p]
        xq = jnp.concatenate([kap[:, sl], rh[:, sl]], axis=0)
        yk = jnp.concatenate([stack2(kh[:, sl]), stack2(bh[:, sl])], axis=0)
        gm = _dot_nt(xq, yk)
        l_kk = jnp.where(strict2, gm[:c, :2 * c], 0.0)
        l_bk = jnp.where(strict2, gm[:c, 2 * c:], 0.0)
        a_rk = jnp.where(incl2, gm[c:, :2 * c], 0.0)
        a_rb = jnp.where(incl2, gm[c:, 2 * c:], 0.0)
        xs = _dot_nt(xq, s)
        v2 = stack2(vv[:, sl])
        x = xs[:c] + _dot(l_kk, v2)
        x = x - _dot(l_bk, stack2(x))
        lp = l_bk
        for _ in range(5):
            lp = _dot(lp, stack2(lp))
            x = x + _dot(lp, stack2(x))
        o_ref[:, sl] = xs[c:] + _dot(jnp.concatenate([a_rk, -a_rb], axis=1),
                                     jnp.concatenate([v2, stack2(x)], axis=0))
        upd = _dot_tn(jnp.concatenate([vv[:, sl], -x], axis=0), jnp.concatenate([kb[:, sl], bb[:, sl]], axis=0))
        s_ref[p] = s * e_tot[:, sl] + jnp.where(blockdiag, upd, 0.0)


def _rwkv_out_kernel(of_ref, ob_ref, bv_ref, g_ref, lg_ref, lb_ref, e_ref, et_ref, w_ref, h_ref, mod_ref, lng_ref,
                     lnb_ref, o_ref):
    o = of_ref[...] + ob_ref[...]
    inv = 1.0 / RWKV_HEAD
    oc = o - _seg_sum(o, e_ref, et_ref) * inv
    var = _seg_sum(oc * oc, e_ref, et_ref) * inv
    y = oc * lax.rsqrt(var + RWKV_GN_EPS) * lg_ref[...] + lb_ref[...] + bv_ref[...]
    yo = _dot(y * g_ref[...], w_ref[...])
    z = DEEPNORM_ALPHA * h_ref[...] + mod_ref[0][2:3] * yo
    o_ref[...] = _ln_rows(z, lng_ref[...], lnb_ref[...])


def _rwkv_mixer(h, mods, nct, mu, w_in, w0, w_l1, w_l2, a0, a_l1, a_l2, g_l1, g_l2, k_k, k_a, r_k, gn_g, gn_b, w_out,
                ln_g, ln_b):
    t, d = h.shape
    nt = t // TM
    bf = MXU_DT
    lw_ = w_l1.shape[-1]
    la_ = a_l1.shape[-1]
    zw = jnp.zeros((lw_, d), F32)
    za = jnp.zeros((la_, d), F32)
    wl1 = jnp.concatenate([w_l1[0], w_l1[1]], axis=1).astype(bf)
    wl2 = jnp.stack([jnp.concatenate([w_l2[0], zw], 0), jnp.concatenate([zw, w_l2[1]], 0)]).astype(bf)
    al1 = jnp.concatenate([a_l1[0], a_l1[1]], axis=1).astype(bf)
    al2 = jnp.stack([jnp.concatenate([a_l2[0], za], 0), jnp.concatenate([za, a_l2[1]], 0)]).astype(bf)
    head_of = jnp.arange(d) // RWKV_HEAD
    e = (head_of[:, None] == jnp.arange(LANES)[None, :]).astype(bf)
    et = e.T
    halo_p = pl.BlockSpec((8, d), lambda i: (jnp.maximum(i * (TM // 8) - 1, 0), 0))
    halo_n = pl.BlockSpec((8, d), lambda i: (jnp.minimum((i + 1) * (TM // 8), t // 8 - 1), 0))
    args = [h, h, h, mods, mu, w_in.astype(bf), wl1, wl2, w0, al1, al2, a0, g_l1.astype(bf), g_l2.astype(bf),
            k_k[None], k_a[None], r_k.reshape(1, d), e, et]
    ins = [_row_spec(d), halo_p, halo_n, _mod_spec(nct)] + [_full_spec(a.shape) for a in args[4:]]
    outs = pl.pallas_call(
        functools.partial(_rwkv_prep_kernel, nct=nct, nt=nt), grid=(nt,), in_specs=ins,
        out_specs=[_row_spec(d)] * 11, out_shape=[jax.ShapeDtypeStruct((t, d), F32)] * 11,
        compiler_params=_cparams("arbitrary"), name="rwkv_prep",
    )(*args)
    r, v, kk, g, bv, lw0, lw1, kt0, kt1, ab0, ab1 = outs
    c = RWKV_CHUNK
    ncc, nc = nct * (TM // c), t // c
    o_dir = []
    for d_, (lw, kt, ab) in enumerate(((lw0, kt0, ab0), (lw1, kt1, ab1))):
        reverse = d_ == 1
        spec = pl.BlockSpec((c, d), lambda g_, reverse=reverse: (_tile_of(g_, ncc, nc, reverse), 0))
        o_dir.append(pl.pallas_call(
            functools.partial(_rwkv_scan_kernel, reverse=reverse), grid=(nc,), in_specs=[spec] * 6, out_specs=spec,
            out_shape=jax.ShapeDtypeStruct((t, d), F32),
            scratch_shapes=[pltpu.VMEM((d // LANES, LANES, LANES), F32)],
            compiler_params=_cparams("arbitrary"), name="rwkv_scan_%d" % d_,
        )(r, v, kk, lw, kt, ab))
    args = [o_dir[0], o_dir[1], bv, g, gn_g[None], gn_b[None], e, et, w_out.astype(bf), h, mods, ln_g[None], ln_b[None]]
    ins = [_row_spec(d)] * 4 + [_full_spec(a.shape) for a in args[4:9]] + [_row_spec(d), _mod_spec(nct),
                                                                          _full_spec((1, d)), _full_spec((1, d))]
    return pl.pallas_call(
        _rwkv_out_kernel, grid=(nt,), in_specs=ins, out_specs=_row_spec(d),
        out_shape=jax.ShapeDtypeStruct((t, d), F32), compiler_params=_cparams("arbitrary"), name="rwkv_out",
    )(*args)


def _ret_in_kernel(h_ref, mod_ref, w_ref, cos_ref, sin_ref, q_o, k_o, v_o, g_o):
    d = D_MODEL
    u = _modulate(h_ref[...], mod_ref[0], 0).astype(MXU_DT)
    q = _dot(u, w_ref[:, 0:d])
    k = _dot(u, w_ref[:, d:2 * d]) * (RET_QK ** -0.5)
    v_o[...] = _dot(u, w_ref[:, 2 * d:4 * d])
    g_o[...] = _silu(_dot(u, w_ref[:, 4 * d:6 * d]))
    cos = cos_ref[...]
    sin = sin_ref[...]
    half = RET_QK // 2
    for z, z_o in ((q, q_o), (k, k_o)):
        for hh in range(RET_HEADS):
            lo = z[:, hh * RET_QK:hh * RET_QK + half]
            hi = z[:, hh * RET_QK + half:(hh + 1) * RET_QK]
            zh = jnp.concatenate([lo, hi], axis=1)
            rot = jnp.concatenate([-hi, lo], axis=1)
            z_o[:, hh * RET_QK:(hh + 1) * RET_QK] = zh * cos + rot * sin


def _ret_scan_kernel(q_ref, k_ref, v_ref, inner_ref, qd_ref, kd_ref, bd_ref, o_ref, r_ref):
    @pl.when(pl.program_id(0) == 0)
    def _():
        r_ref[...] = jnp.zeros_like(r_ref)

    for hh in range(RET_HEADS):
        q = q_ref[:, hh * RET_QK:(hh + 1) * RET_QK]
        k = k_ref[:, hh * RET_QK:(hh + 1) * RET_QK]
        v = v_ref[:, hh * RET_V:(hh + 1) * RET_V]
        state = r_ref[hh]
        scores = _dot_nt(q, k) * inner_ref[hh]
        o_ref[:, hh * RET_V:(hh + 1) * RET_V] = _dot(scores, v) + _dot(q, state) * qd_ref[hh]
        r_ref[hh] = state * bd_ref[hh] + _dot_tn(k * kd_ref[hh], v)


def _ret_out_kernel(of_ref, ob_ref, g_ref, gg_ref, gb_ref, w_ref, h_ref, mod_ref, lng_ref, lnb_ref, o_ref):
    parts = []
    for hh in range(RET_HEADS):
        sl = slice(hh * RET_V, (hh + 1) * RET_V)
        o = of_ref[:, sl] + ob_ref[:, sl]
        mu = jnp.mean(o, axis=-1, keepdims=True)
        oc = o - mu
        var = jnp.mean(oc * oc, axis=-1, keepdims=True)
        y = oc * lax.rsqrt(var + LN_EPS) * gg_ref[:, sl] + gb_ref[:, sl]
        parts.append((g_ref[:, sl] * y).astype(MXU_DT))
    yo = _dot(jnp.concatenate(parts, axis=1), w_ref[...])
    z = DEEPNORM_ALPHA * h_ref[...] + mod_ref[0][2:3] * yo
    o_ref[...] = _ln_rows(z, lng_ref[...], lnb_ref[...])


def _ret_mixer(h, mods, nct, rope_cos, rope_sin, w_in, decay_logit, gn_g, gn_b, w_out, ln_g, ln_b):
    t, d = h.shape
    nt = t // TM
    hv = RET_HEADS * RET_V
    q, k, v, sg = pl.pallas_call(
        _ret_in_kernel, grid=(nt,),
        in_specs=[_row_spec(d), _mod_spec(nct), _full_spec(w_in.shape), _row_spec(RET_QK), _row_spec(RET_QK)],
        out_specs=[_row_spec(d), _row_spec(d), _row_spec(hv), _row_spec(hv)],
        out_shape=[jax.ShapeDtypeStruct((t, w), F32) for w in (d, d, hv, hv)],
        compiler_params=_cparams("arbitrary"), name="ret_in",
    )(h, mods, w_in.astype(MXU_DT), rope_cos, rope_sin)
    c = RET_CHUNK
    ncc, nc = nct * (TM // c), t // c
    log_gamma = jax.nn.log_sigmoid(decay_logit.astype(F32))
    pos = jnp.arange(c, dtype=F32)
    o_dir = []
    for d_ in range(2):
        reverse = d_ == 1
        lg = log_gamma[d_][:, None, None]
        p = (c - 1.0 - pos) if reverse else pos
        rel = p[:, None] - p[None, :]
        inner = jnp.where(rel >= 0, jnp.exp(jnp.maximum(rel, 0.0) * lg), 0.0)
        q_dec = jnp.exp((p + 1.0) * log_gamma[d_][:, None])[:, :, None]
        k_dec = jnp.exp((c - 1.0 - p) * log_gamma[d_][:, None])[:, :, None]
        blk_dec = jnp.exp(c * log_gamma[d_])[:, None, None]
        cs = lambda w, reverse=reverse: pl.BlockSpec((c, w), lambda g_: (_tile_of(g_, ncc, nc, reverse), 0))
        o_dir.append(pl.pallas_call(
            _ret_scan_kernel, grid=(nc,),
            in_specs=[cs(d), cs(d), cs(hv), _full_spec(inner.shape), _full_spec(q_dec.shape), _full_spec(k_dec.shape),
                      _full_spec(blk_dec.shape)],
            out_specs=cs(hv), out_shape=jax.ShapeDtypeStruct((t, hv), F32),
            scratch_shapes=[pltpu.VMEM((RET_HEADS, RET_QK, RET_V), F32)],
            compiler_params=_cparams("arbitrary"), name="ret_scan_%d" % d_,
        )(q, k, v, inner, q_dec, k_dec, blk_dec))
    return pl.pallas_call(
        _ret_out_kernel, grid=(nt,),
        in_specs=[_row_spec(hv)] * 3 + [_full_spec((1, hv)), _full_spec((1, hv)), _full_spec((hv, d)), _row_spec(d),
                                        _mod_spec(nct), _full_spec((1, d)), _full_spec((1, d))],
        out_specs=_row_spec(d), out_shape=jax.ShapeDtypeStruct((t, d), F32),
        compiler_params=_cparams("arbitrary"), name="ret_out",
    )(o_dir[0], o_dir[1], sg, gn_g[None], gn_b[None], w_out.astype(MXU_DT), h, mods, ln_g[None], ln_b[None])


def _hgrn_in_kernel(h_ref, mod_ref, w_ref, lb_ref, bf_ref, q_o, v_o, g_o, f0_o, f1_o):
    d = D_MODEL
    u = _modulate(h_ref[...], mod_ref[0], 0).astype(MXU_DT)
    lb = lb_ref[...]
    q_o[...] = _silu(_dot(u, w_ref[:, 0:d]))
    f0_o[...] = lb + (1.0 - lb) * jax.nn.sigmoid(_dot(u, w_ref[:, d:2 * d]) + bf_ref[0:1, :])
    f1_o[...] = lb + (1.0 - lb) * jax.nn.sigmoid(_dot(u, w_ref[:, 2 * d:3 * d]) + bf_ref[1:2, :])
    v_o[...] = _dot(u, w_ref[:, 3 * d:4 * d])
    g_o[...] = _silu(_dot(u, w_ref[:, 4 * d:5 * d]))


def _hgrn_scan_kernel(q_ref, v_ref, f_ref, o_ref, s_ref, *, reverse):
    hb = HGRN_BLOCK
    nb = TM // hb

    @pl.when(pl.program_id(0) == 0)
    def _():
        s_ref[...] = jnp.zeros_like(s_ref)

    ri = lax.broadcasted_iota(jnp.int32, (hb, hb), 0)
    ci = lax.broadcasted_iota(jnp.int32, (hb, hb), 1)
    tri = jnp.where((ci >= ri) if reverse else (ci <= ri), 1.0, 0.0)
    t3 = lax.broadcasted_iota(jnp.int32, (hb, hb, 1), 0)
    s3 = lax.broadcasted_iota(jnp.int32, (hb, hb, 1), 1)
    causal3 = (s3 >= t3) if reverse else (s3 <= t3)

    def block(bi, carry):
        blk = (nb - 1 - bi) if reverse else bi
        r0 = pl.multiple_of(blk * hb, hb)
        f = f_ref[pl.ds(r0, hb), :]
        q = q_ref[pl.ds(r0, hb), :]
        v = v_ref[pl.ds(r0, hb), :]
        kx = 1.0 - f
        b = _dot_sel(tri, jnp.log(f), 3)
        tot = b[0:1, :] if reverse else b[hb - 1:hb, :]
        qe = q * jnp.exp(b)
        kb = kx * jnp.exp(tot - b)
        e_tot = jnp.exp(tot)
        for hh in range(HGRN_HEADS):
            sl = slice(hh * HGRN_HEAD, (hh + 1) * HGRN_HEAD)
            s = s_ref[hh]
            bh = b[:, sl]
            diff = bh[:, None, :] - bh[None, :, :]
            dec = jnp.exp(jnp.where(causal3, diff, -jnp.inf))
            e3 = q[:, sl][:, None, :] * kx[:, sl][None, :, :] * dec
            sc = jnp.sum(e3, axis=-1, keepdims=True)
            o_diag = jnp.sum(sc * v[:, sl][None, :, :], axis=1)
            o_ref[pl.ds(r0, hb), sl] = _dot_nt(qe[:, sl], s) + o_diag
            s_ref[hh] = s * e_tot[:, sl] + _dot_tn(v[:, sl], kb[:, sl])
        return carry

    lax.fori_loop(0, nb, block, 0)


def _hgrn_out_kernel(of_ref, ob_ref, g_ref, ng_ref, w_ref, h_ref, mod_ref, lng_ref, lnb_ref, o_ref):
    parts = []
    for hh in range(HGRN_HEADS):
        sl = slice(hh * HGRN_HEAD, (hh + 1) * HGRN_HEAD)
        o = of_ref[:, sl] + ob_ref[:, sl]
        y = o * lax.rsqrt(jnp.mean(o * o, axis=-1, keepdims=True) + LN_EPS) * ng_ref[...]
        parts.append((y * g_ref[:, sl]).astype(MXU_DT))
    yo = _dot(jnp.concatenate(parts, axis=1), w_ref[...])
    z = DEEPNORM_ALPHA * h_ref[...] + mod_ref[0][2:3] * yo
    o_ref[...] = _ln_rows(z, lng_ref[...], lnb_ref[...])


def _hgrn_mixer(h, mods, nct, lb, w_in, b_f, norm_g, w_out, ln_g, ln_b):
    t, d = h.shape
    nt = t // TM
    q, v, sg, f0, f1 = pl.pallas_call(
        _hgrn_in_kernel, grid=(nt,),
        in_specs=[_row_spec(d), _mod_spec(nct), _full_spec(w_in.shape), _full_spec((1, d)), _full_spec((2, d))],
        out_specs=[_row_spec(d)] * 5, out_shape=[jax.ShapeDtypeStruct((t, d), F32)] * 5,
        compiler_params=_cparams("arbitrary"), name="hgrn_in",
    )(h, mods, w_in.astype(MXU_DT), lb[None], b_f)
    o_dir = []
    for d_, f in enumerate((f0, f1)):
        reverse = d_ == 1
        spec = pl.BlockSpec((TM, d), lambda g_, reverse=reverse: (_tile_of(g_, nct, nt, reverse), 0))
        o_dir.append(pl.pallas_call(
            functools.partial(_hgrn_scan_kernel, reverse=reverse), grid=(nt,), in_specs=[spec] * 3, out_specs=spec,
            out_shape=jax.ShapeDtypeStruct((t, d), F32),
            scratch_shapes=[pltpu.VMEM((HGRN_HEADS, HGRN_HEAD, HGRN_HEAD), F32)],
            compiler_params=_cparams("arbitrary"), name="hgrn_scan_%d" % d_,
        )(q, v, f))
    return pl.pallas_call(
        _hgrn_out_kernel, grid=(nt,),
        in_specs=[_row_spec(d)] * 3 + [_full_spec((1, HGRN_HEAD)), _full_spec((d, d)), _row_spec(d), _mod_spec(nct),
                                       _full_spec((1, d)), _full_spec((1, d))],
        out_specs=_row_spec(d), out_shape=jax.ShapeDtypeStruct((t, d), F32),
        compiler_params=_cparams("arbitrary"), name="hgrn_out",
    )(o_dir[0], o_dir[1], sg, norm_g[None], w_out.astype(MXU_DT), h, mods, ln_g[None], ln_b[None])


def _router_kernel(h_ref, mod_ref, rw_ref, rb_ref, u_o, gate_o):
    u = _modulate(h_ref[...], mod_ref[0], 3)
    u_o[...] = u.astype(u_o.dtype)
    w_hi, w_lo = _split(rw_ref[...], 2)
    u_hi, u_lo = _split(u, 2)
    nt_dims = (((1,), (1,)), ((), ()))
    logits = (lax.dot_general(w_hi, u_hi, nt_dims, preferred_element_type=F32)
              + lax.dot_general(w_hi, u_lo, nt_dims, preferred_element_type=F32)
              + lax.dot_general(w_lo, u_hi, nt_dims, preferred_element_type=F32))
    ne, gs = N_EXPERTS, N_EXPERTS // N_GROUPS
    neg = -jnp.inf
    scores = jax.nn.sigmoid(logits[:ne])
    choice = scores + rb_ref[:ne]
    c3 = choice.reshape(N_GROUPS, gs, TM)
    mi = lax.broadcasted_iota(jnp.int32, c3.shape, 1).astype(F32)
    m1 = jnp.max(c3, axis=1, keepdims=True)
    i1 = jnp.min(jnp.where(c3 == m1, mi, float(gs)), axis=1, keepdims=True)
    m2 = jnp.max(jnp.where(mi == i1, neg, c3), axis=1, keepdims=True)
    gscore = m1 + m2
    gi = lax.broadcasted_iota(jnp.int32, gscore.shape, 0).astype(F32)
    gsel = jnp.zeros(gscore.shape, F32)
    for _ in range(TOPK_GROUPS):
        gm = jnp.max(gscore, axis=0, keepdims=True)
        pick = gi == jnp.min(jnp.where(gscore == gm, gi, float(N_GROUPS)), axis=0, keepdims=True)
        gsel = jnp.where(pick, 1.0, gsel)
        gscore = jnp.where(pick, neg, gscore)
    emask = jnp.broadcast_to(gsel, c3.shape).reshape(ne, TM)
    masked = jnp.where(emask > 0.5, choice, neg)
    ei = lax.broadcasted_iota(jnp.int32, masked.shape, 0).astype(F32)
    chosen = jnp.zeros(masked.shape, F32)
    for _ in range(TOP_K):
        em = jnp.max(masked, axis=0, keepdims=True)
        pick = ei == jnp.min(jnp.where(masked == em, ei, float(ne)), axis=0, keepdims=True)
        chosen = jnp.where(pick, 1.0, chosen)
        masked = jnp.where(pick, neg, masked)
    top_w = scores * chosen
    top_w = ROUTED_SCALE * top_w / jnp.sum(top_w, axis=0, keepdims=True)
    gates = jnp.concatenate([top_w, jnp.zeros((LANES - ne, TM), F32)], axis=0)
    gate_o[...] = gates.T


def _moe_kernel(u_ref, gate_ref, wgu_ref, wd_ref, sgu_ref, sd_ref, h_ref, mod_ref, lng_ref, lnb_ref, o_ref, acc_ref,
                *, n_ctx):
    e = pl.program_id(1)
    u = u_ref[...]
    ed = EXPERT_DIM

    @pl.when(e == 0)
    def _():
        gu = _dot(u, sgu_ref[...])
        acc_ref[...] = _dot(_silu(gu[:, :ed]) * gu[:, ed:], sd_ref[...])

    gu = _dot(u, wgu_ref[0])
    lane = lax.broadcasted_iota(jnp.int32, (1, LANES), 1)
    col = jnp.sum(jnp.where(lane == e, gate_ref[...], 0.0), axis=1, keepdims=True)
    acc_ref[...] += _dot(_silu(gu[:, :ed]) * gu[:, ed:] * col, wd_ref[0])

    @pl.when(e == N_EXPERTS - 1)
    def _():
        tm = u.shape[0]
        row = pl.program_id(0) * tm + lax.broadcasted_iota(jnp.int32, (tm, 1), 0)
        gate = jnp.where(row < n_ctx, mod_ref[0, 5:6, :], mod_ref[1, 5:6, :])
        z = DEEPNORM_ALPHA * h_ref[...] + gate * acc_ref[...]
        o_ref[...] = _ln_rows(z, lng_ref[...], lnb_ref[...])


def _moe_layer(h, mods, nct, router_w, router_b, w_gu, w_down, sh_gu, sh_down, ln_g, ln_b):
    t, d = h.shape
    nt = t // TM
    ne = N_EXPERTS
    rw = jnp.concatenate([router_w.T, jnp.zeros((LANES - ne, d), F32)], axis=0)
    rb = jnp.concatenate([router_b, jnp.zeros((LANES - ne,), F32)])[:, None]
    u, gates = pl.pallas_call(
        _router_kernel, grid=(nt,),
        in_specs=[_row_spec(d), _mod_spec(nct), _full_spec((LANES, d)), _full_spec((LANES, 1))],
        out_specs=[_row_spec(d), _row_spec(LANES)],
        out_shape=[jax.ShapeDtypeStruct((t, d), MXU_DT), jax.ShapeDtypeStruct((t, LANES), F32)],
        compiler_params=_cparams("arbitrary"), name="moe_router",
    )(h, mods, rw, rb)
    tm = MOE_TM if t % MOE_TM == 0 else TM
    row = lambda w: pl.BlockSpec((tm, w), lambda i, e: (i, 0))
    const = lambda shape: pl.BlockSpec(tuple(shape), lambda i, e: (0,) * len(shape))
    return pl.pallas_call(
        functools.partial(_moe_kernel, n_ctx=nct * TM), grid=(t // tm, ne),
        in_specs=[row(d), row(LANES),
                  pl.BlockSpec((1, d, 2 * EXPERT_DIM), lambda i, e: (e, 0, 0)),
                  pl.BlockSpec((1, EXPERT_DIM, d), lambda i, e: (e, 0, 0)),
                  const(sh_gu.shape), const(sh_down.shape), row(d), const((2, 6, d)), const((1, d)), const((1, d))],
        out_specs=row(d), out_shape=jax.ShapeDtypeStruct((t, d), F32),
        scratch_shapes=[pltpu.VMEM((tm, d), F32)],
        compiler_params=_cparams("arbitrary", "arbitrary"), name="moe_experts",
    )(u, gates, w_gu, w_down, sh_gu.astype(MXU_DT), sh_down.astype(MXU_DT), h, mods, ln_g[None], ln_b[None])


def kernel(x, c, ctx, c_ctx, ada_w, ada_b, post_ln_g, post_ln_b, lru_w_in, lru_conv_w, lru_conv_b, lru_gate_w, lru_gate_b, lru_lambda, lru_w_out, rwkv_mu, rwkv_w_in, rwkv_w0, rwkv_w_l1, rwkv_w_l2, rwkv_a0, rwkv_a_l1, rwkv_a_l2, rwkv_g_l1, rwkv_g_l2, rwkv_k_k, rwkv_k_a, rwkv_r_k, rwkv_ln_g, rwkv_ln_b, rwkv_w_out, ret_w_in, ret_decay, ret_gn_g, ret_gn_b, ret_w_out, hgrn_w_in, hgrn_b_f, hgrn_lb, hgrn_norm_g, hgrn_w_out, moe_router, moe_bias, moe_w_gu, moe_w_down, moe_sh_gu, moe_sh_down):
    assert x.shape[0] == 1 and ctx.shape[0] == 1
    n_ctx, n_lat, d = ctx.shape[1], x.shape[1], x.shape[2]
    assert n_ctx % TM == 0 and n_lat % TM == 0 and d == D_MODEL
    nct = n_ctx // TM
    rows = n_lat // GRID_W
    pos_row = jnp.repeat(jnp.arange(rows, dtype=F32), GRID_W)
    pos_col = jnp.tile(jnp.arange(GRID_W, dtype=F32), rows)
    n_freq = RET_QK // 4
    freqs = ROPE_BASE ** (-jnp.arange(n_freq, dtype=F32) / n_freq)
    ang = jnp.concatenate([pos_row[:, None] * freqs, pos_col[:, None] * freqs], axis=-1)
    ang = jnp.concatenate([ang, ang], axis=-1)
    rope_cos = jnp.concatenate([jnp.ones((n_ctx, RET_QK), F32), jnp.cos(ang)], axis=0)
    rope_sin = jnp.concatenate([jnp.zeros((n_ctx, RET_QK), F32), jnp.sin(ang)], axis=0)
    lb_cum = jnp.cumsum(jax.nn.softmax(hgrn_lb.astype(F32), axis=0), axis=0)

    cond = jnp.concatenate([c_ctx[None], c, jnp.zeros((6, d), F32)], axis=0)
    mods_all = _ada_mods(cond, ada_w, ada_b)
    h = jnp.concatenate([ctx[0], x[0]], axis=0)
    for i in range(DEPTH):
        kind, j = i % N_MIXERS, i // N_MIXERS
        mods = mods_all[i]
        lng, lnb = post_ln_g[i, 0], post_ln_b[i, 0]
        if kind == 0:
            h = _lru_mixer(h, mods, nct, lru_w_in[j], lru_conv_w[j], lru_conv_b[j], lru_gate_w[j], lru_gate_b[j],
                           lru_lambda[j], lru_w_out[j], lng, lnb)
        elif kind == 1:
            h = _rwkv_mixer(h, mods, nct, rwkv_mu[j], rwkv_w_in[j], rwkv_w0[j], rwkv_w_l1[j], rwkv_w_l2[j], rwkv_a0[j],
                            rwkv_a_l1[j], rwkv_a_l2[j], rwkv_g_l1[j], rwkv_g_l2[j], rwkv_k_k[j], rwkv_k_a[j],
                            rwkv_r_k[j], rwkv_ln_g[j], rwkv_ln_b[j], rwkv_w_out[j], lng, lnb)
        elif kind == 2:
            h = _ret_mixer(h, mods, nct, rope_cos, rope_sin, ret_w_in[j], ret_decay[j], ret_gn_g[j], ret_gn_b[j],
                           ret_w_out[j], lng, lnb)
        else:
            h = _hgrn_mixer(h, mods, nct, lb_cum[i] - lb_cum[0], hgrn_w_in[j], hgrn_b_f[j], hgrn_norm_g[j],
                            hgrn_w_out[j], lng, lnb)
        h = _moe_layer(h, mods, nct, moe_router[i], moe_bias[i], moe_w_gu[i], moe_w_down[i], moe_sh_gu[i],
                       moe_sh_down[i], post_ln_g[i, 1], post_ln_b[i, 1])
    return h[n_ctx:][None]
```

```python
import math
import functools
import jax
import jax.numpy as jnp
from jax import lax
from jax.experimental import pallas as pl
from jax.experimental.pallas import tpu as pltpu

F32 = jnp.float32
MXU_DT = jnp.bfloat16
LANES = 128
TM = 256
VMEM_LIMIT = 56 * 2 ** 20

D_MODEL = 1024
DEPTH = 4
GRID_W = 64
N_MIXERS = 4
DEEPNORM_ALPHA = (2.0 * DEPTH) ** 0.25
LN_EPS = 1e-5
LRU_WIDTH = D_MODEL
LRU_BLOCKS = 16
LRU_BLOCK = LRU_WIDTH // LRU_BLOCKS
LRU_C = 8.0
RWKV_HEAD = 64
RWKV_HEADS = D_MODEL // RWKV_HEAD
RWKV_DECAY_SCALE = math.exp(-0.5)
RWKV_GN_EPS = 64e-5
RWKV_CHUNK = 64
RET_HEADS = 4
RET_QK = D_MODEL // RET_HEADS
RET_V = 2 * RET_QK
RET_CHUNK = 128
ROPE_BASE = 10000.0
HGRN_HEADS = 8
HGRN_HEAD = D_MODEL // HGRN_HEADS
HGRN_BLOCK = 16
N_EXPERTS = 64
TOP_K = 8
N_GROUPS = 8
TOPK_GROUPS = 4
EXPERT_DIM = 256
ROUTED_SCALE = 2.5
MOE_TM = 1280


def _cparams(*sem):
    return pltpu.CompilerParams(dimension_semantics=sem, vmem_limit_bytes=VMEM_LIMIT)


def _dot(a, b):
    return jnp.dot(a.astype(MXU_DT), b.astype(MXU_DT), preferred_element_type=F32)


def _dot_nt(a, b):
    return lax.dot_general(a.astype(MXU_DT), b.astype(MXU_DT), (((1,), (1,)), ((), ())), preferred_element_type=F32)


def _dot_tn(a, b):
    return lax.dot_general(a.astype(MXU_DT), b.astype(MXU_DT), (((0,), (0,)), ((), ())), preferred_element_type=F32)


def _split(x, n):
    parts = []
    for _ in range(n):
        p = x.astype(MXU_DT)
        parts.append(p)
        x = x - p.astype(F32)
    return parts


def _dot_sel(sel, x, n):
    return sum(jnp.dot(sel.astype(MXU_DT), p, preferred_element_type=F32) for p in _split(x, n))


def _dot_xsel(x, sel, n):
    return sum(jnp.dot(p, sel.astype(MXU_DT), preferred_element_type=F32) for p in _split(x, n))


def _modulate(h, m, shift_idx):
    return h * (1.0 + m[shift_idx + 1:shift_idx + 2]) + m[shift_idx:shift_idx + 1]


def _ln_rows(z, g, b):
    mu = jnp.mean(z, axis=-1, keepdims=True)
    zc = z - mu
    var = jnp.mean(zc * zc, axis=-1, keepdims=True)
    return zc * lax.rsqrt(var + LN_EPS) * g + b


def _silu(x):
    return x * jax.nn.sigmoid(x)


def _shift_down(x, first_row):
    rows = lax.broadcasted_iota(jnp.int32, (x.shape[0], 1), 0)
    return jnp.where(rows == 0, first_row, pltpu.roll(x, 1, 0))


def _shift_up(x, last_row):
    n = x.shape[0]
    rows = lax.broadcasted_iota(jnp.int32, (n, 1), 0)
    return jnp.where(rows == n - 1, last_row, pltpu.roll(x, n - 1, 0))


def _tile_of(g, nct, nt, reverse):
    if not reverse:
        return g
    return jnp.where(g < nct, nct - 1 - g, nt - 1 - (g - nct))


def _halo_flags(t, nct, nt):
    prev_ok = jnp.logical_and(t != 0, t != nct).astype(F32)
    next_ok = jnp.logical_and(t != nct - 1, t != nt - 1).astype(F32)
    return prev_ok, next_ok


def _ada_kernel(s_ref, w_ref, b_ref, o_ref):
    o_ref[0] = _dot(_silu(s_ref[...]), w_ref[0]) + b_ref[0]


def _ada_mods(cond, ada_w, ada_b):
    nl, d, n6 = ada_w.shape
    out = pl.pallas_call(
        _ada_kernel, grid=(nl, n6 // d),
        in_specs=[pl.BlockSpec((8, d), lambda l, j: (0, 0)),
                  pl.BlockSpec((1, d, d), lambda l, j: (l, 0, j)),
                  pl.BlockSpec((1, 1, d), lambda l, j: (l, 0, j))],
        out_specs=pl.BlockSpec((1, 8, d), lambda l, j: (l, 0, j)),
        out_shape=jax.ShapeDtypeStruct((nl, 8, n6), F32),
        compiler_params=_cparams("arbitrary", "arbitrary"), name="ada_mods",
    )(cond, ada_w, ada_b.reshape(nl, 1, n6))
    return out[:, :2].reshape(nl, 2, 6, d)


def _row_spec(width, tm=TM):
    return pl.BlockSpec((tm, width), lambda i: (i, 0))


def _full_spec(shape):
    nd = len(shape)
    return pl.BlockSpec(tuple(shape), lambda *_: (0,) * nd)


def _mod_spec(nct):
    return pl.BlockSpec((1, 6, D_MODEL), lambda i: (jnp.minimum(i // nct, 1), 0, 0))


def _out_ln_kernel(p_ref, w_ref, h_ref, mod_ref, lng_ref, lnb_ref, o_ref):
    y = _dot(p_ref[...], w_ref[...])
    z = DEEPNORM_ALPHA * h_ref[...] + mod_ref[0][2:3] * y
    o_ref[...] = _ln_rows(z, lng_ref[...], lnb_ref[...])


def _out_ln(p, w_out, h, mods, ln_g, ln_b, nct):
    t, din = p.shape
    d = D_MODEL
    return pl.pallas_call(
        _out_ln_kernel, grid=(t // TM,),
        in_specs=[_row_spec(din), _full_spec((din, d)), _row_spec(d), _mod_spec(nct),
                  _full_spec((1, d)), _full_spec((1, d))],
        out_specs=_row_spec(d), out_shape=jax.ShapeDtypeStruct((t, d), F32),
        compiler_params=_cparams("arbitrary"), name="out_ln",
    )(p, w_out.astype(MXU_DT), h, mods, ln_g[None], ln_b[None])


def _lru_in_kernel(h_ref, mod_ref, w_ref, g_ref, x_ref):
    u = _modulate(h_ref[...], mod_ref[0], 0)
    z = _dot(u, w_ref[...])
    g_ref[...] = jax.nn.gelu(z[:, :LRU_WIDTH], approximate=True)
    x_ref[...] = z[:, LRU_WIDTH:]


def _lru_scan_kernel(x_ref, xp_ref, xn_ref, cw_ref, cb_ref, gw_ref, gb_ref, lam_ref, *rest, nct, nt, reverse, final):
    if final:
        hf_ref, g_ref, o_ref, a_s, b_s, h_s, st_s = rest
    else:
        o_ref, a_s, b_s, h_s, st_s = rest
    g = pl.program_id(0)
    t = _tile_of(g, nct, nt, reverse)
    prev_ok, next_ok = _halo_flags(t, nct, nt)

    @pl.when(g == 0)
    def _():
        st_s[...] = jnp.zeros_like(st_s)

    x = x_ref[...]
    xm1 = _shift_down(x, xp_ref[7:8, :] * prev_ok)
    n0 = xn_ref[0:1, :] * next_ok
    n1 = xn_ref[1:2, :] * next_ok
    xp1 = _shift_up(x, n0)
    xp2 = _shift_up(xp1, n1)
    cw = cw_ref[...]
    xc = cw[0:1] * xm1 + cw[1:2] * x + cw[2:3] * xp1 + cw[3:4] * xp2 + cb_ref[...]
    gates = jax.nn.sigmoid(_dot(xc, gw_ref[...]) + gb_ref[...])
    lam = lam_ref[...]
    softplus = jnp.maximum(-lam, 0.0) + jnp.log(1.0 + jnp.exp(-jnp.abs(lam)))
    log_a = -LRU_C * gates[:, :LRU_WIDTH] * softplus
    a_s[...] = jnp.exp(log_a)
    b_s[...] = jnp.sqrt(1.0 - jnp.exp(2.0 * log_a)) * (gates[:, LRU_WIDTH:] * xc)

    def row(r, hcur):
        tt = (TM - 1 - r) if reverse else r
        hcur = a_s[pl.ds(tt, 1), :] * hcur + b_s[pl.ds(tt, 1), :]
        h_s[pl.ds(tt, 1), :] = hcur
        return hcur

    st_s[...] = lax.fori_loop(0, TM, row, st_s[...], unroll=8)
    if final:
        o_ref[...] = g_ref[...] * (hf_ref[...] + h_s[...])
    else:
        o_ref[...] = h_s[...]


def _lru_mixer(h, mods, nct, w_in, conv_w, conv_b, gate_w, gate_b, lam, w_out, ln_g, ln_b):
    t, d = h.shape
    nt = t // TM
    w = LRU_WIDTH
    gelu, rnn = pl.pallas_call(
        _lru_in_kernel, grid=(nt,),
        in_specs=[_row_spec(d), _mod_spec(nct), _full_spec((d, 2 * w))],
        out_specs=[_row_spec(w), _row_spec(w)],
        out_shape=[jax.ShapeDtypeStruct((t, w), F32)] * 2,
        compiler_params=_cparams("arbitrary"), name="lru_in",
    )(h, mods, w_in.astype(MXU_DT))
    eye = jnp.eye(LRU_BLOCKS, dtype=F32)
    gw = jnp.einsum('dgnij,nm->dgnimj', gate_w, eye).reshape(2, 2, w, w)
    gw = jnp.concatenate([gw[:, 0], gw[:, 1]], axis=-1).astype(MXU_DT)
    gb = gate_b.reshape(2, 1, 2 * w)
    hf = None
    for d_ in range(2):
        reverse = d_ == 1
        final = d_ == 1
        tile = lambda g: _tile_of(g, nct, nt, reverse)
        ins = [pl.BlockSpec((TM, w), lambda g: (tile(g), 0)),
               pl.BlockSpec((8, w), lambda g: (jnp.maximum(tile(g) * (TM // 8) - 1, 0), 0)),
               pl.BlockSpec((8, w), lambda g: (jnp.minimum((tile(g) + 1) * (TM // 8), t // 8 - 1), 0)),
               _full_spec((4, w)), _full_spec((1, w)), _full_spec((w, 2 * w)), _full_spec((1, 2 * w)),
               _full_spec((1, w))]
        args = [rnn, rnn, rnn, conv_w, conv_b[None], gw[d_], gb[d_], lam[d_][None]]
        if final:
            ins += [pl.BlockSpec((TM, w), lambda g: (tile(g), 0))] * 2
            args += [hf, gelu]
        out = pl.pallas_call(
            functools.partial(_lru_scan_kernel, nct=nct, nt=nt, reverse=reverse, final=final),
            grid=(nt,), in_specs=ins,
            out_specs=pl.BlockSpec((TM, w), lambda g: (tile(g), 0)),
            out_shape=jax.ShapeDtypeStruct((t, w), F32),
            scratch_shapes=[pltpu.VMEM((TM, w), F32)] * 3 + [pltpu.VMEM((1, w), F32)],
            compiler_params=_cparams("arbitrary"), name="lru_scan_%d" % d_,
        )(*args)
        hf = out
    return _out_ln(hf, w_out, h, mods, ln_g, ln_b, nct)


def _seg_sum(x, e_ref, et_ref):
    s = _dot_xsel(x, e_ref[...], 2)
    return _dot_xsel(s, et_ref[...], 2)


def _rwkv_prep_kernel(h_ref, hp_ref, hn_ref, mod_ref, mu_ref, win_ref, wl1_ref, wl2_ref, w0_ref, al1_ref, al2_ref,
                      a0_ref, gl1_ref, gl2_ref, kk_ref, ka_ref, rk_ref, e_ref, et_ref,
                      r_o, v_o, kk_o, g_o, bv_o, lw0_o, lw1_o, kt0_o, kt1_o, ab0_o, ab1_o, *, nct, nt):
    i = pl.program_id(0)
    prev_ok, next_ok = _halo_flags(i, nct, nt)
    m = mod_ref[0]
    u = _modulate(h_ref[...], m, 0)
    up = _modulate(hp_ref[7:8, :], m, 0) * prev_ok
    un = _modulate(hn_ref[0:1, :], m, 0) * next_ok
    lane = lax.broadcasted_iota(jnp.int32, (1, D_MODEL), 1)
    sh = jnp.where(lane < D_MODEL // 2, _shift_down(u, up), _shift_up(u, un))
    dx = sh - u
    mu = mu_ref[...]
    xm = [u + dx * mu[c:c + 1] for c in range(6)]
    r = _dot(xm[0], win_ref[0])
    k = _dot(xm[1], win_ref[1])
    v = _dot(xm[2], win_ref[2])
    t1 = jnp.tanh(_dot(xm[3], wl1_ref[...]))
    t2 = _dot(xm[4], al1_ref[...])
    g = _dot(jax.nn.sigmoid(_dot(xm[5], gl1_ref[...])), gl2_ref[...])
    kk = k * kk_ref[...]
    kk = kk * lax.rsqrt(_seg_sum(kk * kk, e_ref, et_ref) + 1e-12)
    ktsum = None
    for z, (lw_o, kt_o, ab_o) in enumerate(((lw0_o, kt0_o, ab0_o), (lw1_o, kt1_o, ab1_o))):
        d_w = w0_ref[z:z + 1, :] + _dot(t1, wl2_ref[z])
        lw_o[...] = -RWKV_DECAY_SCALE * jax.nn.sigmoid(d_w)
        a = jax.nn.sigmoid(a0_ref[z:z + 1, :] + _dot(t2, al2_ref[z]))
        kt = k * (1.0 + (a - 1.0) * ka_ref[...])
        kt_o[...] = kt
        ab_o[...] = kk * a
        ktsum = kt if ktsum is None else ktsum + kt
    r_o[...] = r
    v_o[...] = v
    kk_o[...] = kk
    g_o[...] = g
    bv_o[...] = _seg_sum(r * ktsum * rk_ref[...], e_ref, et_ref) * v


def _rwkv_scan_kernel(r_ref, v_ref, kk_ref, lw_ref, kt_ref, ab_ref, o_ref, s_ref, *, reverse):
    c = RWKV_CHUNK

    @pl.when(pl.program_id(0) == 0)
    def _():
        s_ref[...] = jnp.zeros_like(s_ref)

    ri = lax.broadcasted_iota(jnp.int32, (c, c), 0)
    ci = lax.broadcasted_iota(jnp.int32, (c, c), 1)
    incl = (ci >= ri) if reverse else (ci <= ri)
    ri2 = lax.broadcasted_iota(jnp.int32, (c, 2 * c), 0)
    ci2 = jnp.bitwise_and(lax.broadcasted_iota(jnp.int32, (c, 2 * c), 1), c - 1)
    incl2 = (ci2 >= ri2) if reverse else (ci2 <= ri2)
    strict2 = (ci2 > ri2) if reverse else (ci2 < ri2)
    lw = lw_ref[...]
    cl = _dot_sel(jnp.where(incl, 1.0, 0.0), lw, 3)
    tot = cl[0:1, :] if reverse else cl[c - 1:c, :]
    e_in = jnp.exp(cl)
    e_out = jnp.exp(-cl)
    e_end = jnp.exp(tot - cl)
    kk = kk_ref[...]
    kt = kt_ref[...]
    ab = ab_ref[...]
    kap = kk * jnp.exp(cl - lw)
    rh = r_ref[...] * e_in
    kh = kt * e_out
    bh = ab * e_out
    kb = kt * e_end
    bb = ab * e_end
    e_tot = jnp.exp(tot)
    vv = v_ref[...]
    lane_a = lax.broadcasted_iota(jnp.int32, (1, LANES), 1) < RWKV_HEAD
    bi = lax.broadcasted_iota(jnp.int32, (LANES, LANES), 0) < RWKV_HEAD
    bj = lax.broadcasted_iota(jnp.int32, (LANES, LANES), 1) < RWKV_HEAD
    blockdiag = bi == bj

    def stack2(x):
        return jnp.concatenate([jnp.where(lane_a, x, 0.0), jnp.where(lane_a, 0.0, x)], axis=0)

    pairs = range(D_MODEL // LANES)
    sls = [slice(p * LANES, (p + 1) * LANES) for p in pairs]
    s = [s_ref[p] for p in pairs]
    xq = [jnp.concatenate([kap[:, sl], rh[:, sl]], axis=0) for sl in sls]
    yk = [jnp.concatenate([stack2(kh[:, sl]), stack2(bh[:, sl])], axis=0) for sl in sls]
    gm = [_dot_nt(xq[p], yk[p]) for p in pairs]
    xs = [_dot_nt(xq[p], s[p]) for p in pairs]
    l_kk = [jnp.where(strict2, g[:c, :2 * c], 0.0) for g in gm]
    l_bk = [jnp.where(strict2, g[:c, 2 * c:], 0.0) for g in gm]
    a_rk = [jnp.where(incl2, g[c:, :2 * c], 0.0) for g in gm]
    a_rb = [jnp.where(incl2, g[c:, 2 * c:], 0.0) for g in gm]
    v2 = [stack2(vv[:, sl]) for sl in sls]
    x = [xs[p][:c] + _dot(l_kk[p], v2[p]) for p in pairs]
    lp = [_dot(l_bk[p], stack2(l_bk[p])) for p in pairs]
    x = [x[p] - _dot(l_bk[p], stack2(x[p])) for p in pairs]
    for it in range(5):
        x = [x[p] + _dot(lp[p], stack2(x[p])) for p in pairs]
        if it < 4:
            lp = [_dot(lp[p], stack2(lp[p])) for p in pairs]
    o = [xs[p][c:] + _dot(jnp.concatenate([a_rk[p], -a_rb[p]], axis=1), jnp.concatenate([v2[p], stack2(x[p])], axis=0))
         for p in pairs]
    upd = [_dot_tn(jnp.concatenate([vv[:, sls[p]], -x[p]], axis=0),
                   jnp.concatenate([kb[:, sls[p]], bb[:, sls[p]]], axis=0)) for p in pairs]
    for p in pairs:
        o_ref[:, sls[p]] = o[p]
        s_ref[p] = s[p] * e_tot[:, sls[p]] + jnp.where(blockdiag, upd[p], 0.0)


def _rwkv_out_kernel(of_ref, ob_ref, bv_ref, g_ref, lg_ref, lb_ref, e_ref, et_ref, w_ref, h_ref, mod_ref, lng_ref,
                     lnb_ref, o_ref):
    o = of_ref[...] + ob_ref[...]
    inv = 1.0 / RWKV_HEAD
    oc = o - _seg_sum(o, e_ref, et_ref) * inv
    var = _seg_sum(oc * oc, e_ref, et_ref) * inv
    y = oc * lax.rsqrt(var + RWKV_GN_EPS) * lg_ref[...] + lb_ref[...] + bv_ref[...]
    yo = _dot(y * g_ref[...], w_ref[...])
    z = DEEPNORM_ALPHA * h_ref[...] + mod_ref[0][2:3] * yo
    o_ref[...] = _ln_rows(z, lng_ref[...], lnb_ref[...])


def _rwkv_mixer(h, mods, nct, mu, w_in, w0, w_l1, w_l2, a0, a_l1, a_l2, g_l1, g_l2, k_k, k_a, r_k, gn_g, gn_b, w_out,
                ln_g, ln_b):
    t, d = h.shape
    nt = t // TM
    bf = MXU_DT
    lw_ = w_l1.shape[-1]
    la_ = a_l1.shape[-1]
    zw = jnp.zeros((lw_, d), F32)
    za = jnp.zeros((la_, d), F32)
    wl1 = jnp.concatenate([w_l1[0], w_l1[1]], axis=1).astype(bf)
    wl2 = jnp.stack([jnp.concatenate([w_l2[0], zw], 0), jnp.concatenate([zw, w_l2[1]], 0)]).astype(bf)
    al1 = jnp.concatenate([a_l1[0], a_l1[1]], axis=1).astype(bf)
    al2 = jnp.stack([jnp.concatenate([a_l2[0], za], 0), jnp.concatenate([za, a_l2[1]], 0)]).astype(bf)
    head_of = jnp.arange(d) // RWKV_HEAD
    e = (head_of[:, None] == jnp.arange(LANES)[None, :]).astype(bf)
    et = e.T
    halo_p = pl.BlockSpec((8, d), lambda i: (jnp.maximum(i * (TM // 8) - 1, 0), 0))
    halo_n = pl.BlockSpec((8, d), lambda i: (jnp.minimum((i + 1) * (TM // 8), t // 8 - 1), 0))
    args = [h, h, h, mods, mu, w_in.astype(bf), wl1, wl2, w0, al1, al2, a0, g_l1.astype(bf), g_l2.astype(bf),
            k_k[None], k_a[None], r_k.reshape(1, d), e, et]
    ins = [_row_spec(d), halo_p, halo_n, _mod_spec(nct)] + [_full_spec(a.shape) for a in args[4:]]
    outs = pl.pallas_call(
        functools.partial(_rwkv_prep_kernel, nct=nct, nt=nt), grid=(nt,), in_specs=ins,
        out_specs=[_row_spec(d)] * 11, out_shape=[jax.ShapeDtypeStruct((t, d), F32)] * 11,
        compiler_params=_cparams("arbitrary"), name="rwkv_prep",
    )(*args)
    r, v, kk, g, bv, lw0, lw1, kt0, kt1, ab0, ab1 = outs
    c = RWKV_CHUNK
    ncc, nc = nct * (TM // c), t // c
    o_dir = []
    for d_, (lw, kt, ab) in enumerate(((lw0, kt0, ab0), (lw1, kt1, ab1))):
        reverse = d_ == 1
        spec = pl.BlockSpec((c, d), lambda g_, reverse=reverse: (_tile_of(g_, ncc, nc, reverse), 0))
        o_dir.append(pl.pallas_call(
            functools.partial(_rwkv_scan_kernel, reverse=reverse), grid=(nc,), in_specs=[spec] * 6, out_specs=spec,
            out_shape=jax.ShapeDtypeStruct((t, d), F32),
            scratch_shapes=[pltpu.VMEM((d // LANES, LANES, LANES), F32)],
            compiler_params=_cparams("arbitrary"), name="rwkv_scan_%d" % d_,
        )(r, v, kk, lw, kt, ab))
    args = [o_dir[0], o_dir[1], bv, g, gn_g[None], gn_b[None], e, et, w_out.astype(bf), h, mods, ln_g[None], ln_b[None]]
    ins = [_row_spec(d)] * 4 + [_full_spec(a.shape) for a in args[4:9]] + [_row_spec(d), _mod_spec(nct),
                                                                          _full_spec((1, d)), _full_spec((1, d))]
    return pl.pallas_call(
        _rwkv_out_kernel, grid=(nt,), in_specs=ins, out_specs=_row_spec(d),
        out_shape=jax.ShapeDtypeStruct((t, d), F32), compiler_params=_cparams("arbitrary"), name="rwkv_out",
    )(*args)


def _ret_in_kernel(h_ref, mod_ref, w_ref, cos_ref, sin_ref, q_o, k_o, v_o, g_o):
    d = D_MODEL
    u = _modulate(h_ref[...], mod_ref[0], 0).astype(MXU_DT)
    q = _dot(u, w_ref[:, 0:d])
    k = _dot(u, w_ref[:, d:2 * d]) * (RET_QK ** -0.5)
    v_o[...] = _dot(u, w_ref[:, 2 * d:4 * d])
    g_o[...] = _silu(_dot(u, w_ref[:, 4 * d:6 * d]))
    cos = cos_ref[...]
    sin = sin_ref[...]
    half = RET_QK // 2
    for z, z_o in ((q, q_o), (k, k_o)):
        for hh in range(RET_HEADS):
            lo = z[:, hh * RET_QK:hh * RET_QK + half]
            hi = z[:, hh * RET_QK + half:(hh + 1) * RET_QK]
            zh = jnp.concatenate([lo, hi], axis=1)
            rot = jnp.concatenate([-hi, lo], axis=1)
            z_o[:, hh * RET_QK:(hh + 1) * RET_QK] = zh * cos + rot * sin


def _ret_scan_kernel(q_ref, k_ref, v_ref, inner_ref, qd_ref, kd_ref, bd_ref, o_ref, r_ref):
    @pl.when(pl.program_id(0) == 0)
    def _():
        r_ref[...] = jnp.zeros_like(r_ref)

    for hh in range(RET_HEADS):
        q = q_ref[:, hh * RET_QK:(hh + 1) * RET_QK]
        k = k_ref[:, hh * RET_QK:(hh + 1) * RET_QK]
        v = v_ref[:, hh * RET_V:(hh + 1) * RET_V]
        state = r_ref[hh]
        scores = _dot_nt(q, k) * inner_ref[hh]
        o_ref[:, hh * RET_V:(hh + 1) * RET_V] = _dot(scores, v) + _dot(q, state) * qd_ref[hh]
        r_ref[hh] = state * bd_ref[hh] + _dot_tn(k * kd_ref[hh], v)


def _ret_out_kernel(of_ref, ob_ref, g_ref, gg_ref, gb_ref, w_ref, h_ref, mod_ref, lng_ref, lnb_ref, o_ref):
    parts = []
    for hh in range(RET_HEADS):
        sl = slice(hh * RET_V, (hh + 1) * RET_V)
        o = of_ref[:, sl] + ob_ref[:, sl]
        mu = jnp.mean(o, axis=-1, keepdims=True)
        oc = o - mu
        var = jnp.mean(oc * oc, axis=-1, keepdims=True)
        y = oc * lax.rsqrt(var + LN_EPS) * gg_ref[:, sl] + gb_ref[:, sl]
        parts.append((g_ref[:, sl] * y).astype(MXU_DT))
    yo = _dot(jnp.concatenate(parts, axis=1), w_ref[...])
    z = DEEPNORM_ALPHA * h_ref[...] + mod_ref[0][2:3] * yo
    o_ref[...] = _ln_rows(z, lng_ref[...], lnb_ref[...])


def _ret_mixer(h, mods, nct, rope_cos, rope_sin, w_in, decay_logit, gn_g, gn_b, w_out, ln_g, ln_b):
    t, d = h.shape
    nt = t // TM
    hv = RET_HEADS * RET_V
    q, k, v, sg = pl.pallas_call(
        _ret_in_kernel, grid=(nt,),
        in_specs=[_row_spec(d), _mod_spec(nct), _full_spec(w_in.shape), _row_spec(RET_QK), _row_spec(RET_QK)],
        out_specs=[_row_spec(d), _row_spec(d), _row_spec(hv), _row_spec(hv)],
        out_shape=[jax.ShapeDtypeStruct((t, w), F32) for w in (d, d, hv, hv)],
        compiler_params=_cparams("arbitrary"), name="ret_in",
    )(h, mods, w_in.astype(MXU_DT), rope_cos, rope_sin)
    c = RET_CHUNK
    ncc, nc = nct * (TM // c), t // c
    log_gamma = jax.nn.log_sigmoid(decay_logit.astype(F32))
    pos = jnp.arange(c, dtype=F32)
    o_dir = []
    for d_ in range(2):
        reverse = d_ == 1
        lg = log_gamma[d_][:, None, None]
        p = (c - 1.0 - pos) if reverse else pos
        rel = p[:, None] - p[None, :]
        inner = jnp.where(rel >= 0, jnp.exp(jnp.maximum(rel, 0.0) * lg), 0.0)
        q_dec = jnp.exp((p + 1.0) * log_gamma[d_][:, None])[:, :, None]
        k_dec = jnp.exp((c - 1.0 - p) * log_gamma[d_][:, None])[:, :, None]
        blk_dec = jnp.exp(c * log_gamma[d_])[:, None, None]
        cs = lambda w, reverse=reverse: pl.BlockSpec((c, w), lambda g_: (_tile_of(g_, ncc, nc, reverse), 0))
        o_dir.append(pl.pallas_call(
            _ret_scan_kernel, grid=(nc,),
            in_specs=[cs(d), cs(d), cs(hv), _full_spec(inner.shape), _full_spec(q_dec.shape), _full_spec(k_dec.shape),
                      _full_spec(blk_dec.shape)],
            out_specs=cs(hv), out_shape=jax.ShapeDtypeStruct((t, hv), F32),
            scratch_shapes=[pltpu.VMEM((RET_HEADS, RET_QK, RET_V), F32)],
            compiler_params=_cparams("arbitrary"), name="ret_scan_%d" % d_,
        )(q, k, v, inner, q_dec, k_dec, blk_dec))
    return pl.pallas_call(
        _ret_out_kernel, grid=(nt,),
        in_specs=[_row_spec(hv)] * 3 + [_full_spec((1, hv)), _full_spec((1, hv)), _full_spec((hv, d)), _row_spec(d),
                                        _mod_spec(nct), _full_spec((1, d)), _full_spec((1, d))],
        out_specs=_row_spec(d), out_shape=jax.ShapeDtypeStruct((t, d), F32),
        compiler_params=_cparams("arbitrary"), name="ret_out",
    )(o_dir[0], o_dir[1], sg, gn_g[None], gn_b[None], w_out.astype(MXU_DT), h, mods, ln_g[None], ln_b[None])


def _hgrn_in_kernel(h_ref, mod_ref, w_ref, lb_ref, bf_ref, q_o, v_o, g_o, f0_o, f1_o):
    d = D_MODEL
    u = _modulate(h_ref[...], mod_ref[0], 0).astype(MXU_DT)
    lb = lb_ref[...]
    q_o[...] = _silu(_dot(u, w_ref[:, 0:d]))
    f0_o[...] = lb + (1.0 - lb) * jax.nn.sigmoid(_dot(u, w_ref[:, d:2 * d]) + bf_ref[0:1, :])
    f1_o[...] = lb + (1.0 - lb) * jax.nn.sigmoid(_dot(u, w_ref[:, 2 * d:3 * d]) + bf_ref[1:2, :])
    v_o[...] = _dot(u, w_ref[:, 3 * d:4 * d])
    g_o[...] = _silu(_dot(u, w_ref[:, 4 * d:5 * d]))


def _hgrn_scan_kernel(q_ref, v_ref, f_ref, o_ref, s_ref, *, reverse):
    hb = HGRN_BLOCK
    nb = TM // hb

    @pl.when(pl.program_id(0) == 0)
    def _():
        s_ref[...] = jnp.zeros_like(s_ref)

    ri = lax.broadcasted_iota(jnp.int32, (hb, hb), 0)
    ci = lax.broadcasted_iota(jnp.int32, (hb, hb), 1)
    tri = jnp.where((ci >= ri) if reverse else (ci <= ri), 1.0, 0.0)
    t3 = lax.broadcasted_iota(jnp.int32, (hb, hb, 1), 0)
    s3 = lax.broadcasted_iota(jnp.int32, (hb, hb, 1), 1)
    causal3 = (s3 >= t3) if reverse else (s3 <= t3)

    def block(bi, carry):
        blk = (nb - 1 - bi) if reverse else bi
        r0 = pl.multiple_of(blk * hb, hb)
        f = f_ref[pl.ds(r0, hb), :]
        q = q_ref[pl.ds(r0, hb), :]
        v = v_ref[pl.ds(r0, hb), :]
        kx = 1.0 - f
        b = _dot_sel(tri, jnp.log(f), 3)
        tot = b[0:1, :] if reverse else b[hb - 1:hb, :]
        qe = q * jnp.exp(b)
        kb = kx * jnp.exp(tot - b)
        e_tot = jnp.exp(tot)
        for hh in range(HGRN_HEADS):
            sl = slice(hh * HGRN_HEAD, (hh + 1) * HGRN_HEAD)
            s = s_ref[hh]
            bh = b[:, sl]
            diff = bh[:, None, :] - bh[None, :, :]
            dec = jnp.exp(jnp.where(causal3, diff, -jnp.inf))
            e3 = q[:, sl][:, None, :] * kx[:, sl][None, :, :] * dec
            sc = jnp.sum(e3, axis=-1, keepdims=True)
            o_diag = jnp.sum(sc * v[:, sl][None, :, :], axis=1)
            o_ref[pl.ds(r0, hb), sl] = _dot_nt(qe[:, sl], s) + o_diag
            s_ref[hh] = s * e_tot[:, sl] + _dot_tn(v[:, sl], kb[:, sl])
        return carry

    lax.fori_loop(0, nb, block, 0)


def _hgrn_out_kernel(of_ref, ob_ref, g_ref, ng_ref, w_ref, h_ref, mod_ref, lng_ref, lnb_ref, o_ref):
    parts = []
    for hh in range(HGRN_HEADS):
        sl = slice(hh * HGRN_HEAD, (hh + 1) * HGRN_HEAD)
        o = of_ref[:, sl] + ob_ref[:, sl]
        y = o * lax.rsqrt(jnp.mean(o * o, axis=-1, keepdims=True) + LN_EPS) * ng_ref[...]
        parts.append((y * g_ref[:, sl]).astype(MXU_DT))
    yo = _dot(jnp.concatenate(parts, axis=1), w_ref[...])
    z = DEEPNORM_ALPHA * h_ref[...] + mod_ref[0][2:3] * yo
    o_ref[...] = _ln_rows(z, lng_ref[...], lnb_ref[...])


def _hgrn_mixer(h, mods, nct, lb, w_in, b_f, norm_g, w_out, ln_g, ln_b):
    t, d = h.shape
    nt = t // TM
    q, v, sg, f0, f1 = pl.pallas_call(
        _hgrn_in_kernel, grid=(nt,),
        in_specs=[_row_spec(d), _mod_spec(nct), _full_spec(w_in.shape), _full_spec((1, d)), _full_spec((2, d))],
        out_specs=[_row_spec(d)] * 5, out_shape=[jax.ShapeDtypeStruct((t, d), F32)] * 5,
        compiler_params=_cparams("arbitrary"), name="hgrn_in",
    )(h, mods, w_in.astype(MXU_DT), lb[None], b_f)
    o_dir = []
    for d_, f in enumerate((f0, f1)):
        reverse = d_ == 1
        spec = pl.BlockSpec((TM, d), lambda g_, reverse=reverse: (_tile_of(g_, nct, nt, reverse), 0))
        o_dir.append(pl.pallas_call(
            functools.partial(_hgrn_scan_kernel, reverse=reverse), grid=(nt,), in_specs=[spec] * 3, out_specs=spec,
            out_shape=jax.ShapeDtypeStruct((t, d), F32),
            scratch_shapes=[pltpu.VMEM((HGRN_HEADS, HGRN_HEAD, HGRN_HEAD), F32)],
            compiler_params=_cparams("arbitrary"), name="hgrn_scan_%d" % d_,
        )(q, v, f))
    return pl.pallas_call(
        _hgrn_out_kernel, grid=(nt,),
        in_specs=[_row_spec(d)] * 3 + [_full_spec((1, HGRN_HEAD)), _full_spec((d, d)), _row_spec(d), _mod_spec(nct),
                                       _full_spec((1, d)), _full_spec((1, d))],
        out_specs=_row_spec(d), out_shape=jax.ShapeDtypeStruct((t, d), F32),
        compiler_params=_cparams("arbitrary"), name="hgrn_out",
    )(o_dir[0], o_dir[1], sg, norm_g[None], w_out.astype(MXU_DT), h, mods, ln_g[None], ln_b[None])


def _router_kernel(h_ref, mod_ref, rw_ref, rb_ref, u_o, gate_o):
    u = _modulate(h_ref[...], mod_ref[0], 3)
    u_o[...] = u.astype(u_o.dtype)
    w_hi, w_lo = _split(rw_ref[...], 2)
    u_hi, u_lo = _split(u, 2)
    nt_dims = (((1,), (1,)), ((), ()))
    logits = (lax.dot_general(w_hi, u_hi, nt_dims, preferred_element_type=F32)
              + lax.dot_general(w_hi, u_lo, nt_dims, preferred_element_type=F32)
              + lax.dot_general(w_lo, u_hi, nt_dims, preferred_element_type=F32))
    ne, gs = N_EXPERTS, N_EXPERTS // N_GROUPS
    neg = -jnp.inf
    scores = jax.nn.sigmoid(logits[:ne])
    choice = scores + rb_ref[:ne]
    c3 = choice.reshape(N_GROUPS, gs, TM)
    mi = lax.broadcasted_iota(jnp.int32, c3.shape, 1).astype(F32)
    m1 = jnp.max(c3, axis=1, keepdims=True)
    i1 = jnp.min(jnp.where(c3 == m1, mi, float(gs)), axis=1, keepdims=True)
    m2 = jnp.max(jnp.where(mi == i1, neg, c3), axis=1, keepdims=True)
    gscore = m1 + m2
    gi = lax.broadcasted_iota(jnp.int32, gscore.shape, 0).astype(F32)
    gsel = jnp.zeros(gscore.shape, F32)
    for _ in range(TOPK_GROUPS):
        gm = jnp.max(gscore, axis=0, keepdims=True)
        pick = gi == jnp.min(jnp.where(gscore == gm, gi, float(N_GROUPS)), axis=0, keepdims=True)
        gsel = jnp.where(pick, 1.0, gsel)
        gscore = jnp.where(pick, neg, gscore)
    emask = jnp.broadcast_to(gsel, c3.shape).reshape(ne, TM)
    masked = jnp.where(emask > 0.5, choice, neg)
    ei = lax.broadcasted_iota(jnp.int32, masked.shape, 0).astype(F32)
    chosen = jnp.zeros(masked.shape, F32)
    for _ in range(TOP_K):
        em = jnp.max(masked, axis=0, keepdims=True)
        pick = ei == jnp.min(jnp.where(masked == em, ei, float(ne)), axis=0, keepdims=True)
        chosen = jnp.where(pick, 1.0, chosen)
        masked = jnp.where(pick, neg, masked)
    top_w = scores * chosen
    top_w = ROUTED_SCALE * top_w / jnp.sum(top_w, axis=0, keepdims=True)
    gates = jnp.concatenate([top_w, jnp.zeros((LANES - ne, TM), F32)], axis=0)
    gate_o[...] = gates.T


def _moe_kernel(u_ref, gate_ref, wgu_ref, wd_ref, sgu_ref, sd_ref, h_ref, mod_ref, lng_ref, lnb_ref, o_ref, acc_ref,
                *, n_ctx):
    e = pl.program_id(1)
    u = u_ref[...]
    ed = EXPERT_DIM

    @pl.when(e == 0)
    def _():
        gu = _dot(u, sgu_ref[...])
        acc_ref[...] = _dot(_silu(gu[:, :ed]) * gu[:, ed:], sd_ref[...])

    gu = _dot(u, wgu_ref[0, 0])
    lane = lax.broadcasted_iota(jnp.int32, (1, LANES), 1)
    col = jnp.sum(jnp.where(lane == e, gate_ref[...], 0.0), axis=1, keepdims=True)
    acc_ref[...] += _dot(_silu(gu[:, :ed]) * gu[:, ed:] * col, wd_ref[0, 0])

    @pl.when(e == N_EXPERTS - 1)
    def _():
        tm = u.shape[0]
        row = pl.program_id(0) * tm + lax.broadcasted_iota(jnp.int32, (tm, 1), 0)
        gate = jnp.where(row < n_ctx, mod_ref[0, 5:6, :], mod_ref[1, 5:6, :])
        z = DEEPNORM_ALPHA * h_ref[...] + gate * acc_ref[...]
        o_ref[...] = _ln_rows(z, lng_ref[...], lnb_ref[...])


def _moe_layer(h, mods, nct, layer, router_w, router_b, w_gu, w_down, sh_gu, sh_down, ln_g, ln_b):
    t, d = h.shape
    nt = t // TM
    ne = N_EXPERTS
    rw = jnp.concatenate([router_w.T, jnp.zeros((LANES - ne, d), F32)], axis=0)
    rb = jnp.concatenate([router_b, jnp.zeros((LANES - ne,), F32)])[:, None]
    u, gates = pl.pallas_call(
        _router_kernel, grid=(nt,),
        in_specs=[_row_spec(d), _mod_spec(nct), _full_spec((LANES, d)), _full_spec((LANES, 1))],
        out_specs=[_row_spec(d), _row_spec(LANES)],
        out_shape=[jax.ShapeDtypeStruct((t, d), MXU_DT), jax.ShapeDtypeStruct((t, LANES), F32)],
        compiler_params=_cparams("arbitrary"), name="moe_router",
    )(h, mods, rw, rb)
    tm = MOE_TM if t % MOE_TM == 0 else TM
    row = lambda w: pl.BlockSpec((tm, w), lambda i, e: (i, 0))
    const = lambda shape: pl.BlockSpec(tuple(shape), lambda i, e: (0,) * len(shape))
    return pl.pallas_call(
        functools.partial(_moe_kernel, n_ctx=nct * TM), grid=(t // tm, ne),
        in_specs=[row(d), row(LANES),
                  pl.BlockSpec((1, 1, d, 2 * EXPERT_DIM), lambda i, e: (layer, e, 0, 0)),
                  pl.BlockSpec((1, 1, EXPERT_DIM, d), lambda i, e: (layer, e, 0, 0)),
                  const(sh_gu.shape), const(sh_down.shape), row(d), const((2, 6, d)), const((1, d)), const((1, d))],
        out_specs=row(d), out_shape=jax.ShapeDtypeStruct((t, d), F32),
        scratch_shapes=[pltpu.VMEM((tm, d), F32)],
        compiler_params=_cparams("arbitrary", "arbitrary"), name="moe_experts",
    )(u, gates, w_gu, w_down, sh_gu.astype(MXU_DT), sh_down.astype(MXU_DT), h, mods, ln_g[None], ln_b[None])


def kernel(x, c, ctx, c_ctx, ada_w, ada_b, post_ln_g, post_ln_b, lru_w_in, lru_conv_w, lru_conv_b, lru_gate_w, lru_gate_b, lru_lambda, lru_w_out, rwkv_mu, rwkv_w_in, rwkv_w0, rwkv_w_l1, rwkv_w_l2, rwkv_a0, rwkv_a_l1, rwkv_a_l2, rwkv_g_l1, rwkv_g_l2, rwkv_k_k, rwkv_k_a, rwkv_r_k, rwkv_ln_g, rwkv_ln_b, rwkv_w_out, ret_w_in, ret_decay, ret_gn_g, ret_gn_b, ret_w_out, hgrn_w_in, hgrn_b_f, hgrn_lb, hgrn_norm_g, hgrn_w_out, moe_router, moe_bias, moe_w_gu, moe_w_down, moe_sh_gu, moe_sh_down):
    assert x.shape[0] == 1 and ctx.shape[0] == 1
    n_ctx, n_lat, d = ctx.shape[1], x.shape[1], x.shape[2]
    assert n_ctx % TM == 0 and n_lat % TM == 0 and d == D_MODEL
    nct = n_ctx // TM
    rows = n_lat // GRID_W
    pos_row = jnp.repeat(jnp.arange(rows, dtype=F32), GRID_W)
    pos_col = jnp.tile(jnp.arange(GRID_W, dtype=F32), rows)
    n_freq = RET_QK // 4
    freqs = ROPE_BASE ** (-jnp.arange(n_freq, dtype=F32) / n_freq)
    ang = jnp.concatenate([pos_row[:, None] * freqs, pos_col[:, None] * freqs], axis=-1)
    ang = jnp.concatenate([ang, ang], axis=-1)
    rope_cos = jnp.concatenate([jnp.ones((n_ctx, RET_QK), F32), jnp.cos(ang)], axis=0)
    rope_sin = jnp.concatenate([jnp.zeros((n_ctx, RET_QK), F32), jnp.sin(ang)], axis=0)
    lb_cum = jnp.cumsum(jax.nn.softmax(hgrn_lb.astype(F32), axis=0), axis=0)

    cond = jnp.concatenate([c_ctx[None], c, jnp.zeros((6, d), F32)], axis=0)
    mods_all = _ada_mods(cond, ada_w, ada_b)
    h = jnp.concatenate([ctx[0], x[0]], axis=0)
    for i in range(DEPTH):
        kind, j = i % N_MIXERS, i // N_MIXERS
        mods = mods_all[i]
        lng, lnb = post_ln_g[i, 0], post_ln_b[i, 0]
        if kind == 0:
            h = _lru_mixer(h, mods, nct, lru_w_in[j], lru_conv_w[j], lru_conv_b[j], lru_gate_w[j], lru_gate_b[j],
                           lru_lambda[j], lru_w_out[j], lng, lnb)
        elif kind == 1:
            h = _rwkv_mixer(h, mods, nct, rwkv_mu[j], rwkv_w_in[j], rwkv_w0[j], rwkv_w_l1[j], rwkv_w_l2[j], rwkv_a0[j],
                            rwkv_a_l1[j], rwkv_a_l2[j], rwkv_g_l1[j], rwkv_g_l2[j], rwkv_k_k[j], rwkv_k_a[j],
                            rwkv_r_k[j], rwkv_ln_g[j], rwkv_ln_b[j], rwkv_w_out[j], lng, lnb)
        elif kind == 2:
            h = _ret_mixer(h, mods, nct, rope_cos, rope_sin, ret_w_in[j], ret_decay[j], ret_gn_g[j], ret_gn_b[j],
                           ret_w_out[j], lng, lnb)
        else:
            h = _hgrn_mixer(h, mods, nct, lb_cum[i] - lb_cum[0], hgrn_w_in[j], hgrn_b_f[j], hgrn_norm_g[j],
                            hgrn_w_out[j], lng, lnb)
        h = _moe_layer(h, mods, nct, i, moe_router[i], moe_bias[i], moe_w_gu, moe_w_down, moe_sh_gu[i],
                       moe_sh_down[i], post_ln_g[i, 1], post_ln_b[i, 1])
    return h[n_ctx:][None]
```

```python
import math
import functools
import jax
import jax.numpy as jnp
from jax import lax
from jax.experimental import pallas as pl
from jax.experimental.pallas import tpu as pltpu

F32 = jnp.float32
MXU_DT = jnp.bfloat16
LANES = 128
TM = 256
VMEM_LIMIT = 56 * 2 ** 20

D_MODEL = 1024
DEPTH = 4
GRID_W = 64
N_MIXERS = 4
DEEPNORM_ALPHA = (2.0 * DEPTH) ** 0.25
LN_EPS = 1e-5
LRU_WIDTH = D_MODEL
LRU_BLOCKS = 16
LRU_BLOCK = LRU_WIDTH // LRU_BLOCKS
LRU_C = 8.0
RWKV_HEAD = 64
RWKV_HEADS = D_MODEL // RWKV_HEAD
RWKV_DECAY_SCALE = math.exp(-0.5)
RWKV_GN_EPS = 64e-5
RWKV_CHUNK = 64
RET_HEADS = 4
RET_QK = D_MODEL // RET_HEADS
RET_V = 2 * RET_QK
RET_CHUNK = 128
ROPE_BASE = 10000.0
HGRN_HEADS = 8
HGRN_HEAD = D_MODEL // HGRN_HEADS
HGRN_BLOCK = 16
N_EXPERTS = 64
TOP_K = 8
N_GROUPS = 8
TOPK_GROUPS = 4
EXPERT_DIM = 256
ROUTED_SCALE = 2.5
MOE_TM = 1280
MOE_CAP = 48


def _cparams(*sem):
    return pltpu.CompilerParams(dimension_semantics=sem, vmem_limit_bytes=VMEM_LIMIT)


def _dot(a, b):
    return jnp.dot(a.astype(MXU_DT), b.astype(MXU_DT), preferred_element_type=F32)


def _dot_nt(a, b):
    return lax.dot_general(a.astype(MXU_DT), b.astype(MXU_DT), (((1,), (1,)), ((), ())), preferred_element_type=F32)


def _dot_tn(a, b):
    return lax.dot_general(a.astype(MXU_DT), b.astype(MXU_DT), (((0,), (0,)), ((), ())), preferred_element_type=F32)


def _split(x, n):
    parts = []
    for _ in range(n):
        p = x.astype(MXU_DT)
        parts.append(p)
        x = x - p.astype(F32)
    return parts


def _dot_sel(sel, x, n):
    return sum(jnp.dot(sel.astype(MXU_DT), p, preferred_element_type=F32) for p in _split(x, n))


def _dot_xsel(x, sel, n):
    return sum(jnp.dot(p, sel.astype(MXU_DT), preferred_element_type=F32) for p in _split(x, n))


def _modulate(h, m, shift_idx):
    return h * (1.0 + m[shift_idx + 1:shift_idx + 2]) + m[shift_idx:shift_idx + 1]


def _ln_rows(z, g, b):
    mu = jnp.mean(z, axis=-1, keepdims=True)
    zc = z - mu
    var = jnp.mean(zc * zc, axis=-1, keepdims=True)
    return zc * lax.rsqrt(var + LN_EPS) * g + b


def _silu(x):
    return x * jax.nn.sigmoid(x)


def _shift_down(x, first_row):
    rows = lax.broadcasted_iota(jnp.int32, (x.shape[0], 1), 0)
    return jnp.where(rows == 0, first_row, pltpu.roll(x, 1, 0))


def _shift_up(x, last_row):
    n = x.shape[0]
    rows = lax.broadcasted_iota(jnp.int32, (n, 1), 0)
    return jnp.where(rows == n - 1, last_row, pltpu.roll(x, n - 1, 0))


def _tile_of(g, nct, nt, reverse):
    if not reverse:
        return g
    return jnp.where(g < nct, nct - 1 - g, nt - 1 - (g - nct))


def _halo_flags(t, nct, nt):
    prev_ok = jnp.logical_and(t != 0, t != nct).astype(F32)
    next_ok = jnp.logical_and(t != nct - 1, t != nt - 1).astype(F32)
    return prev_ok, next_ok


def _ada_kernel(s_ref, w_ref, b_ref, o_ref):
    o_ref[0] = _dot(_silu(s_ref[...]), w_ref[0]) + b_ref[0]


def _ada_mods(cond, ada_w, ada_b):
    nl, d, n6 = ada_w.shape
    out = pl.pallas_call(
        _ada_kernel, grid=(nl, n6 // d),
        in_specs=[pl.BlockSpec((8, d), lambda l, j: (0, 0)),
                  pl.BlockSpec((1, d, d), lambda l, j: (l, 0, j)),
                  pl.BlockSpec((1, 1, d), lambda l, j: (l, 0, j))],
        out_specs=pl.BlockSpec((1, 8, d), lambda l, j: (l, 0, j)),
        out_shape=jax.ShapeDtypeStruct((nl, 8, n6), F32),
        compiler_params=_cparams("arbitrary", "arbitrary"), name="ada_mods",
    )(cond, ada_w, ada_b.reshape(nl, 1, n6))
    return out[:, :2].reshape(nl, 2, 6, d)


def _row_spec(width, tm=TM):
    return pl.BlockSpec((tm, width), lambda i: (i, 0))


def _full_spec(shape):
    nd = len(shape)
    return pl.BlockSpec(tuple(shape), lambda *_: (0,) * nd)


def _mod_spec(nct):
    return pl.BlockSpec((1, 6, D_MODEL), lambda i: (jnp.minimum(i // nct, 1), 0, 0))


def _out_ln_kernel(p_ref, w_ref, h_ref, mod_ref, lng_ref, lnb_ref, o_ref):
    y = _dot(p_ref[...], w_ref[...])
    z = DEEPNORM_ALPHA * h_ref[...] + mod_ref[0][2:3] * y
    o_ref[...] = _ln_rows(z, lng_ref[...], lnb_ref[...])


def _out_ln(p, w_out, h, mods, ln_g, ln_b, nct):
    t, din = p.shape
    d = D_MODEL
    return pl.pallas_call(
        _out_ln_kernel, grid=(t // TM,),
        in_specs=[_row_spec(din), _full_spec((din, d)), _row_spec(d), _mod_spec(nct),
                  _full_spec((1, d)), _full_spec((1, d))],
        out_specs=_row_spec(d), out_shape=jax.ShapeDtypeStruct((t, d), F32),
        compiler_params=_cparams("arbitrary"), name="out_ln",
    )(p, w_out.astype(MXU_DT), h, mods, ln_g[None], ln_b[None])


def _lru_in_kernel(h_ref, mod_ref, w_ref, g_ref, x_ref):
    u = _modulate(h_ref[...], mod_ref[0], 0)
    z = _dot(u, w_ref[...])
    g_ref[...] = jax.nn.gelu(z[:, :LRU_WIDTH], approximate=True)
    x_ref[...] = z[:, LRU_WIDTH:]


def _lru_scan_kernel(x_ref, xp_ref, xn_ref, cw_ref, cb_ref, gw_ref, gb_ref, lam_ref, *rest, nct, nt, reverse, final):
    if final:
        hf_ref, g_ref, o_ref, a_s, b_s, h_s, st_s = rest
    else:
        o_ref, a_s, b_s, h_s, st_s = rest
    g = pl.program_id(0)
    t = _tile_of(g, nct, nt, reverse)
    prev_ok, next_ok = _halo_flags(t, nct, nt)

    @pl.when(g == 0)
    def _():
        st_s[...] = jnp.zeros_like(st_s)

    x = x_ref[...]
    xm1 = _shift_down(x, xp_ref[7:8, :] * prev_ok)
    n0 = xn_ref[0:1, :] * next_ok
    n1 = xn_ref[1:2, :] * next_ok
    xp1 = _shift_up(x, n0)
    xp2 = _shift_up(xp1, n1)
    cw = cw_ref[...]
    xc = cw[0:1] * xm1 + cw[1:2] * x + cw[2:3] * xp1 + cw[3:4] * xp2 + cb_ref[...]
    gates = jax.nn.sigmoid(_dot(xc, gw_ref[...]) + gb_ref[...])
    lam = lam_ref[...]
    softplus = jnp.maximum(-lam, 0.0) + jnp.log(1.0 + jnp.exp(-jnp.abs(lam)))
    log_a = -LRU_C * gates[:, :LRU_WIDTH] * softplus
    a_s[...] = jnp.exp(log_a)
    b_s[...] = jnp.sqrt(1.0 - jnp.exp(2.0 * log_a)) * (gates[:, LRU_WIDTH:] * xc)

    def row(r, hcur):
        tt = (TM - 1 - r) if reverse else r
        hcur = a_s[pl.ds(tt, 1), :] * hcur + b_s[pl.ds(tt, 1), :]
        h_s[pl.ds(tt, 1), :] = hcur
        return hcur

    st_s[...] = lax.fori_loop(0, TM, row, st_s[...], unroll=8)
    if final:
        o_ref[...] = g_ref[...] * (hf_ref[...] + h_s[...])
    else:
        o_ref[...] = h_s[...]


def _lru_mixer(h, mods, nct, w_in, conv_w, conv_b, gate_w, gate_b, lam, w_out, ln_g, ln_b):
    t, d = h.shape
    nt = t // TM
    w = LRU_WIDTH
    gelu, rnn = pl.pallas_call(
        _lru_in_kernel, grid=(nt,),
        in_specs=[_row_spec(d), _mod_spec(nct), _full_spec((d, 2 * w))],
        out_specs=[_row_spec(w), _row_spec(w)],
        out_shape=[jax.ShapeDtypeStruct((t, w), F32)] * 2,
        compiler_params=_cparams("arbitrary"), name="lru_in",
    )(h, mods, w_in.astype(MXU_DT))
    eye = jnp.eye(LRU_BLOCKS, dtype=F32)
    gw = jnp.einsum('dgnij,nm->dgnimj', gate_w, eye).reshape(2, 2, w, w)
    gw = jnp.concatenate([gw[:, 0], gw[:, 1]], axis=-1).astype(MXU_DT)
    gb = gate_b.reshape(2, 1, 2 * w)
    hf = None
    for d_ in range(2):
        reverse = d_ == 1
        final = d_ == 1
        tile = lambda g: _tile_of(g, nct, nt, reverse)
        ins = [pl.BlockSpec((TM, w), lambda g: (tile(g), 0)),
               pl.BlockSpec((8, w), lambda g: (jnp.maximum(tile(g) * (TM // 8) - 1, 0), 0)),
               pl.BlockSpec((8, w), lambda g: (jnp.minimum((tile(g) + 1) * (TM // 8), t // 8 - 1), 0)),
               _full_spec((4, w)), _full_spec((1, w)), _full_spec((w, 2 * w)), _full_spec((1, 2 * w)),
               _full_spec((1, w))]
        args = [rnn, rnn, rnn, conv_w, conv_b[None], gw[d_], gb[d_], lam[d_][None]]
        if final:
            ins += [pl.BlockSpec((TM, w), lambda g: (tile(g), 0))] * 2
            args += [hf, gelu]
        out = pl.pallas_call(
            functools.partial(_lru_scan_kernel, nct=nct, nt=nt, reverse=reverse, final=final),
            grid=(nt,), in_specs=ins,
            out_specs=pl.BlockSpec((TM, w), lambda g: (tile(g), 0)),
            out_shape=jax.ShapeDtypeStruct((t, w), F32),
            scratch_shapes=[pltpu.VMEM((TM, w), F32)] * 3 + [pltpu.VMEM((1, w), F32)],
            compiler_params=_cparams("arbitrary"), name="lru_scan_%d" % d_,
        )(*args)
        hf = out
    return _out_ln(hf, w_out, h, mods, ln_g, ln_b, nct)


def _seg_sum(x, e_ref, et_ref):
    s = _dot_xsel(x, e_ref[...], 2)
    return _dot_xsel(s, et_ref[...], 2)


def _rwkv_prep_kernel(h_ref, hp_ref, hn_ref, mod_ref, mu_ref, win_ref, wl1_ref, wl2_ref, w0_ref, al1_ref, al2_ref,
                      a0_ref, gl1_ref, gl2_ref, kk_ref, ka_ref, rk_ref, e_ref, et_ref,
                      r_o, v_o, kk_o, g_o, bv_o, lw0_o, lw1_o, kt0_o, kt1_o, ab0_o, ab1_o, *, nct, nt):
    i = pl.program_id(0)
    prev_ok, next_ok = _halo_flags(i, nct, nt)
    m = mod_ref[0]
    u = _modulate(h_ref[...], m, 0)
    up = _modulate(hp_ref[7:8, :], m, 0) * prev_ok
    un = _modulate(hn_ref[0:1, :], m, 0) * next_ok
    lane = lax.broadcasted_iota(jnp.int32, (1, D_MODEL), 1)
    sh = jnp.where(lane < D_MODEL // 2, _shift_down(u, up), _shift_up(u, un))
    dx = sh - u
    mu = mu_ref[...]
    xm = [u + dx * mu[c:c + 1] for c in range(6)]
    r = _dot(xm[0], win_ref[0])
    k = _dot(xm[1], win_ref[1])
    v = _dot(xm[2], win_ref[2])
    t1 = jnp.tanh(_dot(xm[3], wl1_ref[...]))
    t2 = _dot(xm[4], al1_ref[...])
    g = _dot(jax.nn.sigmoid(_dot(xm[5], gl1_ref[...])), gl2_ref[...])
    kk = k * kk_ref[...]
    kk = kk * lax.rsqrt(_seg_sum(kk * kk, e_ref, et_ref) + 1e-12)
    ktsum = None
    for z, (lw_o, kt_o, ab_o) in enumerate(((lw0_o, kt0_o, ab0_o), (lw1_o, kt1_o, ab1_o))):
        d_w = w0_ref[z:z + 1, :] + _dot(t1, wl2_ref[z])
        lw_o[...] = -RWKV_DECAY_SCALE * jax.nn.sigmoid(d_w)
        a = jax.nn.sigmoid(a0_ref[z:z + 1, :] + _dot(t2, al2_ref[z]))
        kt = k * (1.0 + (a - 1.0) * ka_ref[...])
        kt_o[...] = kt
        ab_o[...] = kk * a
        ktsum = kt if ktsum is None else ktsum + kt
    r_o[...] = r
    v_o[...] = v
    kk_o[...] = kk
    g_o[...] = g
    bv_o[...] = _seg_sum(r * ktsum * rk_ref[...], e_ref, et_ref) * v


def _rwkv_scan_kernel(r_ref, v_ref, kk_ref, lw_ref, kt_ref, ab_ref, o_ref, s_ref, *, reverse):
    c = RWKV_CHUNK

    @pl.when(pl.program_id(0) == 0)
    def _():
        s_ref[...] = jnp.zeros_like(s_ref)

    ri = lax.broadcasted_iota(jnp.int32, (c, c), 0)
    ci = lax.broadcasted_iota(jnp.int32, (c, c), 1)
    incl = (ci >= ri) if reverse else (ci <= ri)
    ri2 = lax.broadcasted_iota(jnp.int32, (c, 2 * c), 0)
    ci2 = jnp.bitwise_and(lax.broadcasted_iota(jnp.int32, (c, 2 * c), 1), c - 1)
    incl2 = (ci2 >= ri2) if reverse else (ci2 <= ri2)
    strict2 = (ci2 > ri2) if reverse else (ci2 < ri2)
    lw = lw_ref[...]
    cl = _dot_sel(jnp.where(incl, 1.0, 0.0), lw, 3)
    tot = cl[0:1, :] if reverse else cl[c - 1:c, :]
    e_in = jnp.exp(cl)
    e_out = jnp.exp(-cl)
    e_end = jnp.exp(tot - cl)
    kk = kk_ref[...]
    kt = kt_ref[...]
    ab = ab_ref[...]
    kap = kk * jnp.exp(cl - lw)
    rh = r_ref[...] * e_in
    kh = kt * e_out
    bh = ab * e_out
    kb = kt * e_end
    bb = ab * e_end
    e_tot = jnp.exp(tot)
    vv = v_ref[...]
    lane_a = lax.broadcasted_iota(jnp.int32, (1, LANES), 1) < RWKV_HEAD
    bi = lax.broadcasted_iota(jnp.int32, (LANES, LANES), 0) < RWKV_HEAD
    bj = lax.broadcasted_iota(jnp.int32, (LANES, LANES), 1) < RWKV_HEAD
    blockdiag = bi == bj

    def stack2(x):
        return jnp.concatenate([jnp.where(lane_a, x, 0.0), jnp.where(lane_a, 0.0, x)], axis=0)

    pairs = range(D_MODEL // LANES)
    sls = [slice(p * LANES, (p + 1) * LANES) for p in pairs]
    s = [s_ref[p] for p in pairs]
    xq = [jnp.concatenate([kap[:, sl], rh[:, sl]], axis=0) for sl in sls]
    yk = [jnp.concatenate([stack2(kh[:, sl]), stack2(bh[:, sl])], axis=0) for sl in sls]
    gm = [_dot_nt(xq[p], yk[p]) for p in pairs]
    xs = [_dot_nt(xq[p], s[p]) for p in pairs]
    l_kk = [jnp.where(strict2, g[:c, :2 * c], 0.0) for g in gm]
    l_bk = [jnp.where(strict2, g[:c, 2 * c:], 0.0) for g in gm]
    a_rk = [jnp.where(incl2, g[c:, :2 * c], 0.0) for g in gm]
    a_rb = [jnp.where(incl2, g[c:, 2 * c:], 0.0) for g in gm]
    v2 = [stack2(vv[:, sl]) for sl in sls]
    x = [xs[p][:c] + _dot(l_kk[p], v2[p]) for p in pairs]
    lp = [_dot(l_bk[p], stack2(l_bk[p])) for p in pairs]
    x = [x[p] - _dot(l_bk[p], stack2(x[p])) for p in pairs]
    for it in range(5):
        x = [x[p] + _dot(lp[p], stack2(x[p])) for p in pairs]
        if it < 4:
            lp = [_dot(lp[p], stack2(lp[p])) for p in pairs]
    o = [xs[p][c:] + _dot(jnp.concatenate([a_rk[p], -a_rb[p]], axis=1), jnp.concatenate([v2[p], stack2(x[p])], axis=0))
         for p in pairs]
    upd = [_dot_tn(jnp.concatenate([vv[:, sls[p]], -x[p]], axis=0),
                   jnp.concatenate([kb[:, sls[p]], bb[:, sls[p]]], axis=0)) for p in pairs]
    for p in pairs:
        o_ref[:, sls[p]] = o[p]
        s_ref[p] = s[p] * e_tot[:, sls[p]] + jnp.where(blockdiag, upd[p], 0.0)


def _rwkv_out_kernel(of_ref, ob_ref, bv_ref, g_ref, lg_ref, lb_ref, e_ref, et_ref, w_ref, h_ref, mod_ref, lng_ref,
                     lnb_ref, o_ref):
    o = of_ref[...] + ob_ref[...]
    inv = 1.0 / RWKV_HEAD
    oc = o - _seg_sum(o, e_ref, et_ref) * inv
    var = _seg_sum(oc * oc, e_ref, et_ref) * inv
    y = oc * lax.rsqrt(var + RWKV_GN_EPS) * lg_ref[...] + lb_ref[...] + bv_ref[...]
    yo = _dot(y * g_ref[...], w_ref[...])
    z = DEEPNORM_ALPHA * h_ref[...] + mod_ref[0][2:3] * yo
    o_ref[...] = _ln_rows(z, lng_ref[...], lnb_ref[...])


def _rwkv_mixer(h, mods, nct, mu, w_in, w0, w_l1, w_l2, a0, a_l1, a_l2, g_l1, g_l2, k_k, k_a, r_k, gn_g, gn_b, w_out,
                ln_g, ln_b):
    t, d = h.shape
    nt = t // TM
    bf = MXU_DT
    lw_ = w_l1.shape[-1]
    la_ = a_l1.shape[-1]
    zw = jnp.zeros((lw_, d), F32)
    za = jnp.zeros((la_, d), F32)
    wl1 = jnp.concatenate([w_l1[0], w_l1[1]], axis=1).astype(bf)
    wl2 = jnp.stack([jnp.concatenate([w_l2[0], zw], 0), jnp.concatenate([zw, w_l2[1]], 0)]).astype(bf)
    al1 = jnp.concatenate([a_l1[0], a_l1[1]], axis=1).astype(bf)
    al2 = jnp.stack([jnp.concatenate([a_l2[0], za], 0), jnp.concatenate([za, a_l2[1]], 0)]).astype(bf)
    head_of = jnp.arange(d) // RWKV_HEAD
    e = (head_of[:, None] == jnp.arange(LANES)[None, :]).astype(bf)
    et = e.T
    halo_p = pl.BlockSpec((8, d), lambda i: (jnp.maximum(i * (TM // 8) - 1, 0), 0))
    halo_n = pl.BlockSpec((8, d), lambda i: (jnp.minimum((i + 1) * (TM // 8), t // 8 - 1), 0))
    args = [h, h, h, mods, mu, w_in.astype(bf), wl1, wl2, w0, al1, al2, a0, g_l1.astype(bf), g_l2.astype(bf),
            k_k[None], k_a[None], r_k.reshape(1, d), e, et]
    ins = [_row_spec(d), halo_p, halo_n, _mod_spec(nct)] + [_full_spec(a.shape) for a in args[4:]]
    outs = pl.pallas_call(
        functools.partial(_rwkv_prep_kernel, nct=nct, nt=nt), grid=(nt,), in_specs=ins,
        out_specs=[_row_spec(d)] * 11, out_shape=[jax.ShapeDtypeStruct((t, d), F32)] * 11,
        compiler_params=_cparams("arbitrary"), name="rwkv_prep",
    )(*args)
    r, v, kk, g, bv, lw0, lw1, kt0, kt1, ab0, ab1 = outs
    c = RWKV_CHUNK
    ncc, nc = nct * (TM // c), t // c
    o_dir = []
    for d_, (lw, kt, ab) in enumerate(((lw0, kt0, ab0), (lw1, kt1, ab1))):
        reverse = d_ == 1
        spec = pl.BlockSpec((c, d), lambda g_, reverse=reverse: (_tile_of(g_, ncc, nc, reverse), 0))
        o_dir.append(pl.pallas_call(
            functools.partial(_rwkv_scan_kernel, reverse=reverse), grid=(nc,), in_specs=[spec] * 6, out_specs=spec,
            out_shape=jax.ShapeDtypeStruct((t, d), F32),
            scratch_shapes=[pltpu.VMEM((d // LANES, LANES, LANES), F32)],
            compiler_params=_cparams("arbitrary"), name="rwkv_scan_%d" % d_,
        )(r, v, kk, lw, kt, ab))
    args = [o_dir[0], o_dir[1], bv, g, gn_g[None], gn_b[None], e, et, w_out.astype(bf), h, mods, ln_g[None], ln_b[None]]
    ins = [_row_spec(d)] * 4 + [_full_spec(a.shape) for a in args[4:9]] + [_row_spec(d), _mod_spec(nct),
                                                                          _full_spec((1, d)), _full_spec((1, d))]
    return pl.pallas_call(
        _rwkv_out_kernel, grid=(nt,), in_specs=ins, out_specs=_row_spec(d),
        out_shape=jax.ShapeDtypeStruct((t, d), F32), compiler_params=_cparams("arbitrary"), name="rwkv_out",
    )(*args)


def _ret_in_kernel(h_ref, mod_ref, w_ref, cos_ref, sin_ref, q_o, k_o, v_o, g_o):
    d = D_MODEL
    u = _modulate(h_ref[...], mod_ref[0], 0).astype(MXU_DT)
    q = _dot(u, w_ref[:, 0:d])
    k = _dot(u, w_ref[:, d:2 * d]) * (RET_QK ** -0.5)
    v_o[...] = _dot(u, w_ref[:, 2 * d:4 * d])
    g_o[...] = _silu(_dot(u, w_ref[:, 4 * d:6 * d]))
    cos = cos_ref[...]
    sin = sin_ref[...]
    half = RET_QK // 2
    for z, z_o in ((q, q_o), (k, k_o)):
        for hh in range(RET_HEADS):
            lo = z[:, hh * RET_QK:hh * RET_QK + half]
            hi = z[:, hh * RET_QK + half:(hh + 1) * RET_QK]
            zh = jnp.concatenate([lo, hi], axis=1)
            rot = jnp.concatenate([-hi, lo], axis=1)
            z_o[:, hh * RET_QK:(hh + 1) * RET_QK] = zh * cos + rot * sin


def _ret_scan_kernel(q_ref, k_ref, v_ref, inner_ref, qd_ref, kd_ref, bd_ref, o_ref, r_ref):
    @pl.when(pl.program_id(0) == 0)
    def _():
        r_ref[...] = jnp.zeros_like(r_ref)

    for hh in range(RET_HEADS):
        q = q_ref[:, hh * RET_QK:(hh + 1) * RET_QK]
        k = k_ref[:, hh * RET_QK:(hh + 1) * RET_QK]
        v = v_ref[:, hh * RET_V:(hh + 1) * RET_V]
        state = r_ref[hh]
        scores = _dot_nt(q, k) * inner_ref[hh]
        o_ref[:, hh * RET_V:(hh + 1) * RET_V] = _dot(scores, v) + _dot(q, state) * qd_ref[hh]
        r_ref[hh] = state * bd_ref[hh] + _dot_tn(k * kd_ref[hh], v)


def _ret_out_kernel(of_ref, ob_ref, g_ref, gg_ref, gb_ref, w_ref, h_ref, mod_ref, lng_ref, lnb_ref, o_ref):
    parts = []
    for hh in range(RET_HEADS):
        sl = slice(hh * RET_V, (hh + 1) * RET_V)
        o = of_ref[:, sl] + ob_ref[:, sl]
        mu = jnp.mean(o, axis=-1, keepdims=True)
        oc = o - mu
        var = jnp.mean(oc * oc, axis=-1, keepdims=True)
        y = oc * lax.rsqrt(var + LN_EPS) * gg_ref[:, sl] + gb_ref[:, sl]
        parts.append((g_ref[:, sl] * y).astype(MXU_DT))
    yo = _dot(jnp.concatenate(parts, axis=1), w_ref[...])
    z = DEEPNORM_ALPHA * h_ref[...] + mod_ref[0][2:3] * yo
    o_ref[...] = _ln_rows(z, lng_ref[...], lnb_ref[...])


def _ret_mixer(h, mods, nct, rope_cos, rope_sin, w_in, decay_logit, gn_g, gn_b, w_out, ln_g, ln_b):
    t, d = h.shape
    nt = t // TM
    hv = RET_HEADS * RET_V
    q, k, v, sg = pl.pallas_call(
        _ret_in_kernel, grid=(nt,),
        in_specs=[_row_spec(d), _mod_spec(nct), _full_spec(w_in.shape), _row_spec(RET_QK), _row_spec(RET_QK)],
        out_specs=[_row_spec(d), _row_spec(d), _row_spec(hv), _row_spec(hv)],
        out_shape=[jax.ShapeDtypeStruct((t, w), F32) for w in (d, d, hv, hv)],
        compiler_params=_cparams("arbitrary"), name="ret_in",
    )(h, mods, w_in.astype(MXU_DT), rope_cos, rope_sin)
    c = RET_CHUNK
    ncc, nc = nct * (TM // c), t // c
    log_gamma = jax.nn.log_sigmoid(decay_logit.astype(F32))
    pos = jnp.arange(c, dtype=F32)
    o_dir = []
    for d_ in range(2):
        reverse = d_ == 1
        lg = log_gamma[d_][:, None, None]
        p = (c - 1.0 - pos) if reverse else pos
        rel = p[:, None] - p[None, :]
        inner = jnp.where(rel >= 0, jnp.exp(jnp.maximum(rel, 0.0) * lg), 0.0)
        q_dec = jnp.exp((p + 1.0) * log_gamma[d_][:, None])[:, :, None]
        k_dec = jnp.exp((c - 1.0 - p) * log_gamma[d_][:, None])[:, :, None]
        blk_dec = jnp.exp(c * log_gamma[d_])[:, None, None]
        cs = lambda w, reverse=reverse: pl.BlockSpec((c, w), lambda g_: (_tile_of(g_, ncc, nc, reverse), 0))
        o_dir.append(pl.pallas_call(
            _ret_scan_kernel, grid=(nc,),
            in_specs=[cs(d), cs(d), cs(hv), _full_spec(inner.shape), _full_spec(q_dec.shape), _full_spec(k_dec.shape),
                      _full_spec(blk_dec.shape)],
            out_specs=cs(hv), out_shape=jax.ShapeDtypeStruct((t, hv), F32),
            scratch_shapes=[pltpu.VMEM((RET_HEADS, RET_QK, RET_V), F32)],
            compiler_params=_cparams("arbitrary"), name="ret_scan_%d" % d_,
        )(q, k, v, inner, q_dec, k_dec, blk_dec))
    return pl.pallas_call(
        _ret_out_kernel, grid=(nt,),
        in_specs=[_row_spec(hv)] * 3 + [_full_spec((1, hv)), _full_spec((1, hv)), _full_spec((hv, d)), _row_spec(d),
                                        _mod_spec(nct), _full_spec((1, d)), _full_spec((1, d))],
        out_specs=_row_spec(d), out_shape=jax.ShapeDtypeStruct((t, d), F32),
        compiler_params=_cparams("arbitrary"), name="ret_out",
    )(o_dir[0], o_dir[1], sg, gn_g[None], gn_b[None], w_out.astype(MXU_DT), h, mods, ln_g[None], ln_b[None])


def _hgrn_in_kernel(h_ref, mod_ref, w_ref, lb_ref, bf_ref, q_o, v_o, g_o, f0_o, f1_o):
    d = D_MODEL
    u = _modulate(h_ref[...], mod_ref[0], 0).astype(MXU_DT)
    lb = lb_ref[...]
    q_o[...] = _silu(_dot(u, w_ref[:, 0:d]))
    f0_o[...] = lb + (1.0 - lb) * jax.nn.sigmoid(_dot(u, w_ref[:, d:2 * d]) + bf_ref[0:1, :])
    f1_o[...] = lb + (1.0 - lb) * jax.nn.sigmoid(_dot(u, w_ref[:, 2 * d:3 * d]) + bf_ref[1:2, :])
    v_o[...] = _dot(u, w_ref[:, 3 * d:4 * d])
    g_o[...] = _silu(_dot(u, w_ref[:, 4 * d:5 * d]))


def _hgrn_scan_kernel(q_ref, v_ref, f_ref, o_ref, s_ref, *, reverse):
    hb = HGRN_BLOCK
    nb = TM // hb

    @pl.when(pl.program_id(0) == 0)
    def _():
        s_ref[...] = jnp.zeros_like(s_ref)

    ri = lax.broadcasted_iota(jnp.int32, (hb, hb), 0)
    ci = lax.broadcasted_iota(jnp.int32, (hb, hb), 1)
    tri = jnp.where((ci >= ri) if reverse else (ci <= ri), 1.0, 0.0)
    t3 = lax.broadcasted_iota(jnp.int32, (hb, hb, 1), 0)
    s3 = lax.broadcasted_iota(jnp.int32, (hb, hb, 1), 1)
    causal3 = (s3 >= t3) if reverse else (s3 <= t3)

    def block(bi, carry):
        blk = (nb - 1 - bi) if reverse else bi
        r0 = pl.multiple_of(blk * hb, hb)
        f = f_ref[pl.ds(r0, hb), :]
        q = q_ref[pl.ds(r0, hb), :]
        v = v_ref[pl.ds(r0, hb), :]
        kx = 1.0 - f
        b = _dot_sel(tri, jnp.log(f), 3)
        tot = b[0:1, :] if reverse else b[hb - 1:hb, :]
        qe = q * jnp.exp(b)
        kb = kx * jnp.exp(tot - b)
        e_tot = jnp.exp(tot)
        for hh in range(HGRN_HEADS):
            sl = slice(hh * HGRN_HEAD, (hh + 1) * HGRN_HEAD)
            s = s_ref[hh]
            bh = b[:, sl]
            diff = bh[:, None, :] - bh[None, :, :]
            dec = jnp.exp(jnp.where(causal3, diff, -jnp.inf))
            e3 = q[:, sl][:, None, :] * kx[:, sl][None, :, :] * dec
            sc = jnp.sum(e3, axis=-1, keepdims=True)
            o_diag = jnp.sum(sc * v[:, sl][None, :, :], axis=1)
            o_ref[pl.ds(r0, hb), sl] = _dot_nt(qe[:, sl], s) + o_diag
            s_ref[hh] = s * e_tot[:, sl] + _dot_tn(v[:, sl], kb[:, sl])
        return carry

    lax.fori_loop(0, nb, block, 0)


def _hgrn_out_kernel(of_ref, ob_ref, g_ref, ng_ref, w_ref, h_ref, mod_ref, lng_ref, lnb_ref, o_ref):
    parts = []
    for hh in range(HGRN_HEADS):
        sl = slice(hh * HGRN_HEAD, (hh + 1) * HGRN_HEAD)
        o = of_ref[:, sl] + ob_ref[:, sl]
        y = o * lax.rsqrt(jnp.mean(o * o, axis=-1, keepdims=True) + LN_EPS) * ng_ref[...]
        parts.append((y * g_ref[:, sl]).astype(MXU_DT))
    yo = _dot(jnp.concatenate(parts, axis=1), w_ref[...])
    z = DEEPNORM_ALPHA * h_ref[...] + mod_ref[0][2:3] * yo
    o_ref[...] = _ln_rows(z, lng_ref[...], lnb_ref[...])


def _hgrn_mixer(h, mods, nct, lb, w_in, b_f, norm_g, w_out, ln_g, ln_b):
    t, d = h.shape
    nt = t // TM
    q, v, sg, f0, f1 = pl.pallas_call(
        _hgrn_in_kernel, grid=(nt,),
        in_specs=[_row_spec(d), _mod_spec(nct), _full_spec(w_in.shape), _full_spec((1, d)), _full_spec((2, d))],
        out_specs=[_row_spec(d)] * 5, out_shape=[jax.ShapeDtypeStruct((t, d), F32)] * 5,
        compiler_params=_cparams("arbitrary"), name="hgrn_in",
    )(h, mods, w_in.astype(MXU_DT), lb[None], b_f)
    o_dir = []
    for d_, f in enumerate((f0, f1)):
        reverse = d_ == 1
        spec = pl.BlockSpec((TM, d), lambda g_, reverse=reverse: (_tile_of(g_, nct, nt, reverse), 0))
        o_dir.append(pl.pallas_call(
            functools.partial(_hgrn_scan_kernel, reverse=reverse), grid=(nt,), in_specs=[spec] * 3, out_specs=spec,
            out_shape=jax.ShapeDtypeStruct((t, d), F32),
            scratch_shapes=[pltpu.VMEM((HGRN_HEADS, HGRN_HEAD, HGRN_HEAD), F32)],
            compiler_params=_cparams("arbitrary"), name="hgrn_scan_%d" % d_,
        )(q, v, f))
    return pl.pallas_call(
        _hgrn_out_kernel, grid=(nt,),
        in_specs=[_row_spec(d)] * 3 + [_full_spec((1, HGRN_HEAD)), _full_spec((d, d)), _row_spec(d), _mod_spec(nct),
                                       _full_spec((1, d)), _full_spec((1, d))],
        out_specs=_row_spec(d), out_shape=jax.ShapeDtypeStruct((t, d), F32),
        compiler_params=_cparams("arbitrary"), name="hgrn_out",
    )(o_dir[0], o_dir[1], sg, norm_g[None], w_out.astype(MXU_DT), h, mods, ln_g[None], ln_b[None])


def _router_kernel(h_ref, mod_ref, rw_ref, rb_ref, u_o, gate_o, rank_o):
    u = _modulate(h_ref[...], mod_ref[0], 3)
    u_o[...] = u.astype(u_o.dtype)
    w_hi, w_lo = _split(rw_ref[...], 2)
    u_hi, u_lo = _split(u, 2)
    nt_dims = (((1,), (1,)), ((), ()))
    logits = (lax.dot_general(w_hi, u_hi, nt_dims, preferred_element_type=F32)
              + lax.dot_general(w_hi, u_lo, nt_dims, preferred_element_type=F32)
              + lax.dot_general(w_lo, u_hi, nt_dims, preferred_element_type=F32))
    ne, gs = N_EXPERTS, N_EXPERTS // N_GROUPS
    neg = -jnp.inf
    scores = jax.nn.sigmoid(logits[:ne])
    choice = scores + rb_ref[:ne]
    c3 = choice.reshape(N_GROUPS, gs, TM)
    mi = lax.broadcasted_iota(jnp.int32, c3.shape, 1).astype(F32)
    m1 = jnp.max(c3, axis=1, keepdims=True)
    i1 = jnp.min(jnp.where(c3 == m1, mi, float(gs)), axis=1, keepdims=True)
    m2 = jnp.max(jnp.where(mi == i1, neg, c3), axis=1, keepdims=True)
    gscore = m1 + m2
    gi = lax.broadcasted_iota(jnp.int32, gscore.shape, 0).astype(F32)
    gsel = jnp.zeros(gscore.shape, F32)
    for _ in range(TOPK_GROUPS):
        gm = jnp.max(gscore, axis=0, keepdims=True)
        pick = gi == jnp.min(jnp.where(gscore == gm, gi, float(N_GROUPS)), axis=0, keepdims=True)
        gsel = jnp.where(pick, 1.0, gsel)
        gscore = jnp.where(pick, neg, gscore)
    emask = jnp.broadcast_to(gsel, c3.shape).reshape(ne, TM)
    masked = jnp.where(emask > 0.5, choice, neg)
    ei = lax.broadcasted_iota(jnp.int32, masked.shape, 0).astype(F32)
    chosen = jnp.zeros(masked.shape, F32)
    for _ in range(TOP_K):
        em = jnp.max(masked, axis=0, keepdims=True)
        pick = ei == jnp.min(jnp.where(masked == em, ei, float(ne)), axis=0, keepdims=True)
        chosen = jnp.where(pick, 1.0, chosen)
        masked = jnp.where(pick, neg, masked)
    top_w = scores * chosen
    gate_o[0] = ROUTED_SCALE * top_w / jnp.sum(top_w, axis=0, keepdims=True)
    ti = lax.broadcasted_iota(jnp.int32, (TM, TM), 0)
    tj = lax.broadcasted_iota(jnp.int32, (TM, TM), 1)
    before = jnp.where(ti < tj, 1.0, 0.0).astype(MXU_DT)
    prefix = jnp.dot(chosen.astype(MXU_DT), before, preferred_element_type=F32)
    rank_o[0] = jnp.where(chosen > 0.5, prefix, -1.0)


def _moe_kernel(nr_ref, u_ref, gate_ref, rank_ref, wgu_ref, wd_ref, sgu_ref, sd_ref, h_ref, mod_ref, lng_ref, lnb_ref,
                o_ref, acc_ref, *, n_ctx, nsub):
    i = pl.program_id(0)
    e = pl.program_id(1)
    ed = EXPERT_DIM
    cap = MOE_CAP

    @pl.when(e == 0)
    def _():
        gu = _dot(u_ref[...], sgu_ref[...])
        acc_ref[...] = _dot(_silu(gu[:, :ed]) * gu[:, ed:], sd_ref[...])

    wgu = wgu_ref[0, 0].astype(MXU_DT)
    wd = wd_ref[0, 0].astype(MXU_DT)
    subs = [slice(j * TM, (j + 1) * TM) for j in range(nsub)]

    def one_round(r, carry):
        slot = lax.broadcasted_iota(jnp.int32, (cap, TM), 0).astype(F32) + (r * cap).astype(F32)
        hit = [slot == rank_ref[j, pl.ds(e, 1), :] for j in range(nsub)]
        x = jnp.concatenate([_dot(jnp.where(hit[j], 1.0, 0.0), u_ref[subs[j], :]) for j in range(nsub)], axis=0)
        gu = _dot(x, wgu)
        y = _dot(_silu(gu[:, :ed]) * gu[:, ed:], wd)
        for j in range(nsub):
            pw = jnp.where(hit[j], gate_ref[j, pl.ds(e, 1), :], 0.0)
            acc_ref[subs[j], :] += _dot_tn(pw, y[j * cap:(j + 1) * cap])
        return carry

    lax.fori_loop(0, nr_ref[i, e], one_round, 0)

    @pl.when(e == N_EXPERTS - 1)
    def _():
        tm = nsub * TM
        row = i * tm + lax.broadcasted_iota(jnp.int32, (tm, 1), 0)
        gate = jnp.where(row < n_ctx, mod_ref[0, 5:6, :], mod_ref[1, 5:6, :])
        z = DEEPNORM_ALPHA * h_ref[...] + gate * acc_ref[...]
        o_ref[...] = _ln_rows(z, lng_ref[...], lnb_ref[...])


def _moe_layer(h, mods, nct, layer, router_w, router_b, w_gu, w_down, sh_gu, sh_down, ln_g, ln_b):
    t, d = h.shape
    nt = t // TM
    ne = N_EXPERTS
    rw = jnp.concatenate([router_w.T, jnp.zeros((LANES - ne, d), F32)], axis=0)
    rb = jnp.concatenate([router_b, jnp.zeros((LANES - ne,), F32)])[:, None]
    per_tile = pl.BlockSpec((1, ne, TM), lambda i: (i, 0, 0))
    u, gates, ranks = pl.pallas_call(
        _router_kernel, grid=(nt,),
        in_specs=[_row_spec(d), _mod_spec(nct), _full_spec((LANES, d)), _full_spec((LANES, 1))],
        out_specs=[_row_spec(d), per_tile, per_tile],
        out_shape=[jax.ShapeDtypeStruct((t, d), MXU_DT), jax.ShapeDtypeStruct((nt, ne, TM), F32),
                   jax.ShapeDtypeStruct((nt, ne, TM), F32)],
        compiler_params=_cparams("arbitrary"), name="moe_router",
    )(h, mods, rw, rb)
    tm = MOE_TM if t % MOE_TM == 0 else TM
    nsub = tm // TM
    count = jnp.max(ranks, axis=-1).astype(jnp.int32) + 1
    rounds = jnp.max(((count + MOE_CAP - 1) // MOE_CAP).reshape(t // tm, nsub, ne), axis=1)
    row = lambda w: pl.BlockSpec((tm, w), lambda i, e, nr: (i, 0))
    const = lambda shape: pl.BlockSpec(tuple(shape), lambda i, e, nr: (0,) * len(shape))
    sub = pl.BlockSpec((nsub, ne, TM), lambda i, e, nr: (i, 0, 0))
    grid_spec = pltpu.PrefetchScalarGridSpec(
        num_scalar_prefetch=1, grid=(t // tm, ne),
        in_specs=[row(d), sub, sub,
                  pl.BlockSpec((1, 1, d, 2 * EXPERT_DIM), lambda i, e, nr: (layer, e, 0, 0)),
                  pl.BlockSpec((1, 1, EXPERT_DIM, d), lambda i, e, nr: (layer, e, 0, 0)),
                  const(sh_gu.shape), const(sh_down.shape), row(d), const((2, 6, d)), const((1, d)), const((1, d))],
        out_specs=row(d), scratch_shapes=[pltpu.VMEM((tm, d), F32)])
    return pl.pallas_call(
        functools.partial(_moe_kernel, n_ctx=nct * TM, nsub=nsub), grid_spec=grid_spec,
        out_shape=jax.ShapeDtypeStruct((t, d), F32),
        compiler_params=_cparams("arbitrary", "arbitrary"), name="moe_experts",
    )(rounds, u, gates, ranks, w_gu, w_down, sh_gu.astype(MXU_DT), sh_down.astype(MXU_DT), h, mods, ln_g[None],
      ln_b[None])


def kernel(x, c, ctx, c_ctx, ada_w, ada_b, post_ln_g, post_ln_b, lru_w_in, lru_conv_w, lru_conv_b, lru_gate_w, lru_gate_b, lru_lambda, lru_w_out, rwkv_mu, rwkv_w_in, rwkv_w0, rwkv_w_l1, rwkv_w_l2, rwkv_a0, rwkv_a_l1, rwkv_a_l2, rwkv_g_l1, rwkv_g_l2, rwkv_k_k, rwkv_k_a, rwkv_r_k, rwkv_ln_g, rwkv_ln_b, rwkv_w_out, ret_w_in, ret_decay, ret_gn_g, ret_gn_b, ret_w_out, hgrn_w_in, hgrn_b_f, hgrn_lb, hgrn_norm_g, hgrn_w_out, moe_router, moe_bias, moe_w_gu, moe_w_down, moe_sh_gu, moe_sh_down):
    assert x.shape[0] == 1 and ctx.shape[0] == 1
    n_ctx, n_lat, d = ctx.shape[1], x.shape[1], x.shape[2]
    assert n_ctx % TM == 0 and n_lat % TM == 0 and d == D_MODEL
    nct = n_ctx // TM
    rows = n_lat // GRID_W
    pos_row = jnp.repeat(jnp.arange(rows, dtype=F32), GRID_W)
    pos_col = jnp.tile(jnp.arange(GRID_W, dtype=F32), rows)
    n_freq = RET_QK // 4
    freqs = ROPE_BASE ** (-jnp.arange(n_freq, dtype=F32) / n_freq)
    ang = jnp.concatenate([pos_row[:, None] * freqs, pos_col[:, None] * freqs], axis=-1)
    ang = jnp.concatenate([ang, ang], axis=-1)
    rope_cos = jnp.concatenate([jnp.ones((n_ctx, RET_QK), F32), jnp.cos(ang)], axis=0)
    rope_sin = jnp.concatenate([jnp.zeros((n_ctx, RET_QK), F32), jnp.sin(ang)], axis=0)
    lb_cum = jnp.cumsum(jax.nn.softmax(hgrn_lb.astype(F32), axis=0), axis=0)

    cond = jnp.concatenate([c_ctx[None], c, jnp.zeros((6, d), F32)], axis=0)
    mods_all = _ada_mods(cond, ada_w, ada_b)
    h = jnp.concatenate([ctx[0], x[0]], axis=0)
    for i in range(DEPTH):
        kind, j = i % N_MIXERS, i // N_MIXERS
        mods = mods_all[i]
        lng, lnb = post_ln_g[i, 0], post_ln_b[i, 0]
        if kind == 0:
            h = _lru_mixer(h, mods, nct, lru_w_in[j], lru_conv_w[j], lru_conv_b[j], lru_gate_w[j], lru_gate_b[j],
                           lru_lambda[j], lru_w_out[j], lng, lnb)
        elif kind == 1:
            h = _rwkv_mixer(h, mods, nct, rwkv_mu[j], rwkv_w_in[j], rwkv_w0[j], rwkv_w_l1[j], rwkv_w_l2[j], rwkv_a0[j],
                            rwkv_a_l1[j], rwkv_a_l2[j], rwkv_g_l1[j], rwkv_g_l2[j], rwkv_k_k[j], rwkv_k_a[j],
                            rwkv_r_k[j], rwkv_ln_g[j], rwkv_ln_b[j], rwkv_w_out[j], lng, lnb)
        elif kind == 2:
            h = _ret_mixer(h, mods, nct, rope_cos, rope_sin, ret_w_in[j], ret_decay[j], ret_gn_g[j], ret_gn_b[j],
                           ret_w_out[j], lng, lnb)
        else:
            h = _hgrn_mixer(h, mods, nct, lb_cum[i] - lb_cum[0], hgrn_w_in[j], hgrn_b_f[j], hgrn_norm_g[j],
                            hgrn_w_out[j], lng, lnb)
        h = _moe_layer(h, mods, nct, i, moe_router[i], moe_bias[i], moe_w_gu, moe_w_down, moe_sh_gu[i],
                       moe_sh_down[i], post_ln_g[i, 1], post_ln_b[i, 1])
    return h[n_ctx:][None]
```

```python
import math
import functools
import jax
import jax.numpy as jnp
from jax import lax
from jax.experimental import pallas as pl
from jax.experimental.pallas import tpu as pltpu

F32 = jnp.float32
MXU_DT = jnp.bfloat16
LANES = 128
TM = 256
VMEM_LIMIT = 56 * 2 ** 20

D_MODEL = 1024
DEPTH = 4
GRID_W = 64
N_MIXERS = 4
DEEPNORM_ALPHA = (2.0 * DEPTH) ** 0.25
LN_EPS = 1e-5
LRU_WIDTH = D_MODEL
LRU_BLOCKS = 16
LRU_BLOCK = LRU_WIDTH // LRU_BLOCKS
LRU_C = 8.0
RWKV_HEAD = 64
RWKV_HEADS = D_MODEL // RWKV_HEAD
RWKV_DECAY_SCALE = math.exp(-0.5)
RWKV_GN_EPS = 64e-5
RWKV_CHUNK = 64
RET_HEADS = 4
RET_QK = D_MODEL // RET_HEADS
RET_V = 2 * RET_QK
RET_CHUNK = 128
ROPE_BASE = 10000.0
HGRN_HEADS = 8
HGRN_HEAD = D_MODEL // HGRN_HEADS
HGRN_BLOCK = 16
N_EXPERTS = 64
TOP_K = 8
N_GROUPS = 8
TOPK_GROUPS = 4
EXPERT_DIM = 256
ROUTED_SCALE = 2.5
MOE_TM = 1280
MOE_CAP = 64
MOE_EGROUP = 8
MOE_RUN = 13


def _cparams(*sem):
    return pltpu.CompilerParams(dimension_semantics=sem, vmem_limit_bytes=VMEM_LIMIT)


def _dot(a, b):
    return jnp.dot(a.astype(MXU_DT), b.astype(MXU_DT), preferred_element_type=F32)


def _dot_nt(a, b):
    return lax.dot_general(a.astype(MXU_DT), b.astype(MXU_DT), (((1,), (1,)), ((), ())), preferred_element_type=F32)


def _dot_tn(a, b):
    return lax.dot_general(a.astype(MXU_DT), b.astype(MXU_DT), (((0,), (0,)), ((), ())), preferred_element_type=F32)


def _split(x, n):
    parts = []
    for _ in range(n):
        p = x.astype(MXU_DT)
        parts.append(p)
        x = x - p.astype(F32)
    return parts


def _dot_sel(sel, x, n):
    return sum(jnp.dot(sel.astype(MXU_DT), p, preferred_element_type=F32) for p in _split(x, n))


def _dot_xsel(x, sel, n):
    return sum(jnp.dot(p, sel.astype(MXU_DT), preferred_element_type=F32) for p in _split(x, n))


def _modulate(h, m, shift_idx):
    return h * (1.0 + m[shift_idx + 1:shift_idx + 2]) + m[shift_idx:shift_idx + 1]


def _ln_rows(z, g, b):
    mu = jnp.mean(z, axis=-1, keepdims=True)
    zc = z - mu
    var = jnp.mean(zc * zc, axis=-1, keepdims=True)
    return zc * lax.rsqrt(var + LN_EPS) * g + b


def _silu(x):
    return x * jax.nn.sigmoid(x)


def _shift_down(x, first_row):
    rows = lax.broadcasted_iota(jnp.int32, (x.shape[0], 1), 0)
    return jnp.where(rows == 0, first_row, pltpu.roll(x, 1, 0))


def _shift_up(x, last_row):
    n = x.shape[0]
    rows = lax.broadcasted_iota(jnp.int32, (n, 1), 0)
    return jnp.where(rows == n - 1, last_row, pltpu.roll(x, n - 1, 0))


def _tile_of(g, nct, nt, reverse):
    if not reverse:
        return g
    return jnp.where(g < nct, nct - 1 - g, nt - 1 - (g - nct))


def _halo_flags(t, nct, nt):
    prev_ok = jnp.logical_and(t != 0, t != nct).astype(F32)
    next_ok = jnp.logical_and(t != nct - 1, t != nt - 1).astype(F32)
    return prev_ok, next_ok


def _ada_kernel(s_ref, w_ref, b_ref, o_ref):
    o_ref[0] = _dot(_silu(s_ref[...]), w_ref[0]) + b_ref[0]


def _ada_mods(cond, ada_w, ada_b):
    nl, d, n6 = ada_w.shape
    out = pl.pallas_call(
        _ada_kernel, grid=(nl, n6 // d),
        in_specs=[pl.BlockSpec((8, d), lambda l, j: (0, 0)),
                  pl.BlockSpec((1, d, d), lambda l, j: (l, 0, j)),
                  pl.BlockSpec((1, 1, d), lambda l, j: (l, 0, j))],
        out_specs=pl.BlockSpec((1, 8, d), lambda l, j: (l, 0, j)),
        out_shape=jax.ShapeDtypeStruct((nl, 8, n6), F32),
        compiler_params=_cparams("arbitrary", "arbitrary"), name="ada_mods",
    )(cond, ada_w, ada_b.reshape(nl, 1, n6))
    return out[:, :2].reshape(nl, 2, 6, d)


def _row_spec(width, tm=TM):
    return pl.BlockSpec((tm, width), lambda i: (i, 0))


def _full_spec(shape):
    nd = len(shape)
    return pl.BlockSpec(tuple(shape), lambda *_: (0,) * nd)


def _mod_spec(nct):
    return pl.BlockSpec((1, 6, D_MODEL), lambda i: (jnp.minimum(i // nct, 1), 0, 0))


def _out_ln_kernel(p_ref, w_ref, h_ref, mod_ref, lng_ref, lnb_ref, o_ref):
    y = _dot(p_ref[...], w_ref[...])
    z = DEEPNORM_ALPHA * h_ref[...] + mod_ref[0][2:3] * y
    o_ref[...] = _ln_rows(z, lng_ref[...], lnb_ref[...])


def _out_ln(p, w_out, h, mods, ln_g, ln_b, nct):
    t, din = p.shape
    d = D_MODEL
    return pl.pallas_call(
        _out_ln_kernel, grid=(t // TM,),
        in_specs=[_row_spec(din), _full_spec((din, d)), _row_spec(d), _mod_spec(nct),
                  _full_spec((1, d)), _full_spec((1, d))],
        out_specs=_row_spec(d), out_shape=jax.ShapeDtypeStruct((t, d), F32),
        compiler_params=_cparams("arbitrary"), name="out_ln",
    )(p, w_out.astype(MXU_DT), h, mods, ln_g[None], ln_b[None])


def _lru_in_kernel(h_ref, mod_ref, w_ref, g_ref, x_ref):
    u = _modulate(h_ref[...], mod_ref[0], 0)
    z = _dot(u, w_ref[...])
    g_ref[...] = jax.nn.gelu(z[:, :LRU_WIDTH], approximate=True)
    x_ref[...] = z[:, LRU_WIDTH:]


def _lru_scan_kernel(x_ref, xp_ref, xn_ref, cw_ref, cb_ref, gw_ref, gb_ref, lam_ref, *rest, nct, nt, reverse, final):
    if final:
        hf_ref, g_ref, o_ref, a_s, b_s, h_s, st_s = rest
    else:
        o_ref, a_s, b_s, h_s, st_s = rest
    g = pl.program_id(0)
    t = _tile_of(g, nct, nt, reverse)
    prev_ok, next_ok = _halo_flags(t, nct, nt)

    @pl.when(g == 0)
    def _():
        st_s[...] = jnp.zeros_like(st_s)

    x = x_ref[...]
    xm1 = _shift_down(x, xp_ref[7:8, :] * prev_ok)
    n0 = xn_ref[0:1, :] * next_ok
    n1 = xn_ref[1:2, :] * next_ok
    xp1 = _shift_up(x, n0)
    xp2 = _shift_up(xp1, n1)
    cw = cw_ref[...]
    xc = cw[0:1] * xm1 + cw[1:2] * x + cw[2:3] * xp1 + cw[3:4] * xp2 + cb_ref[...]
    gates = jax.nn.sigmoid(_dot(xc, gw_ref[...]) + gb_ref[...])
    lam = lam_ref[...]
    softplus = jnp.maximum(-lam, 0.0) + jnp.log(1.0 + jnp.exp(-jnp.abs(lam)))
    log_a = -LRU_C * gates[:, :LRU_WIDTH] * softplus
    a_s[...] = jnp.exp(log_a)
    b_s[...] = jnp.sqrt(1.0 - jnp.exp(2.0 * log_a)) * (gates[:, LRU_WIDTH:] * xc)

    def row(r, hcur):
        tt = (TM - 1 - r) if reverse else r
        hcur = a_s[pl.ds(tt, 1), :] * hcur + b_s[pl.ds(tt, 1), :]
        h_s[pl.ds(tt, 1), :] = hcur
        return hcur

    st_s[...] = lax.fori_loop(0, TM, row, st_s[...], unroll=8)
    if final:
        o_ref[...] = g_ref[...] * (hf_ref[...] + h_s[...])
    else:
        o_ref[...] = h_s[...]


def _lru_mixer(h, mods, nct, w_in, conv_w, conv_b, gate_w, gate_b, lam, w_out, ln_g, ln_b):
    t, d = h.shape
    nt = t // TM
    w = LRU_WIDTH
    gelu, rnn = pl.pallas_call(
        _lru_in_kernel, grid=(nt,),
        in_specs=[_row_spec(d), _mod_spec(nct), _full_spec((d, 2 * w))],
        out_specs=[_row_spec(w), _row_spec(w)],
        out_shape=[jax.ShapeDtypeStruct((t, w), F32)] * 2,
        compiler_params=_cparams("arbitrary"), name="lru_in",
    )(h, mods, w_in.astype(MXU_DT))
    eye = jnp.eye(LRU_BLOCKS, dtype=F32)
    gw = jnp.einsum('dgnij,nm->dgnimj', gate_w, eye).reshape(2, 2, w, w)
    gw = jnp.concatenate([gw[:, 0], gw[:, 1]], axis=-1).astype(MXU_DT)
    gb = gate_b.reshape(2, 1, 2 * w)
    hf = None
    for d_ in range(2):
        reverse = d_ == 1
        final = d_ == 1
        tile = lambda g: _tile_of(g, nct, nt, reverse)
        ins = [pl.BlockSpec((TM, w), lambda g: (tile(g), 0)),
               pl.BlockSpec((8, w), lambda g: (jnp.maximum(tile(g) * (TM // 8) - 1, 0), 0)),
               pl.BlockSpec((8, w), lambda g: (jnp.minimum((tile(g) + 1) * (TM // 8), t // 8 - 1), 0)),
               _full_spec((4, w)), _full_spec((1, w)), _full_spec((w, 2 * w)), _full_spec((1, 2 * w)),
               _full_spec((1, w))]
        args = [rnn, rnn, rnn, conv_w, conv_b[None], gw[d_], gb[d_], lam[d_][None]]
        if final:
            ins += [pl.BlockSpec((TM, w), lambda g: (tile(g), 0))] * 2
            args += [hf, gelu]
        out = pl.pallas_call(
            functools.partial(_lru_scan_kernel, nct=nct, nt=nt, reverse=reverse, final=final),
            grid=(nt,), in_specs=ins,
            out_specs=pl.BlockSpec((TM, w), lambda g: (tile(g), 0)),
            out_shape=jax.ShapeDtypeStruct((t, w), F32),
            scratch_shapes=[pltpu.VMEM((TM, w), F32)] * 3 + [pltpu.VMEM((1, w), F32)],
            compiler_params=_cparams("arbitrary"), name="lru_scan_%d" % d_,
        )(*args)
        hf = out
    return _out_ln(hf, w_out, h, mods, ln_g, ln_b, nct)


def _seg_sum(x, e_ref, et_ref):
    s = _dot_xsel(x, e_ref[...], 2)
    return _dot_xsel(s, et_ref[...], 2)


def _rwkv_prep_kernel(h_ref, hp_ref, hn_ref, mod_ref, mu_ref, win_ref, wl1_ref, wl2_ref, w0_ref, al1_ref, al2_ref,
                      a0_ref, gl1_ref, gl2_ref, kk_ref, ka_ref, rk_ref, e_ref, et_ref,
                      r_o, v_o, kk_o, g_o, bv_o, lw0_o, lw1_o, kt0_o, kt1_o, ab0_o, ab1_o, *, nct, nt):
    i = pl.program_id(0)
    prev_ok, next_ok = _halo_flags(i, nct, nt)
    m = mod_ref[0]
    u = _modulate(h_ref[...], m, 0)
    up = _modulate(hp_ref[7:8, :], m, 0) * prev_ok
    un = _modulate(hn_ref[0:1, :], m, 0) * next_ok
    lane = lax.broadcasted_iota(jnp.int32, (1, D_MODEL), 1)
    sh = jnp.where(lane < D_MODEL // 2, _shift_down(u, up), _shift_up(u, un))
    dx = sh - u
    mu = mu_ref[...]
    xm = [u + dx * mu[c:c + 1] for c in range(6)]
    r = _dot(xm[0], win_ref[0])
    k = _dot(xm[1], win_ref[1])
    v = _dot(xm[2], win_ref[2])
    t1 = jnp.tanh(_dot(xm[3], wl1_ref[...]))
    t2 = _dot(xm[4], al1_ref[...])
    g = _dot(jax.nn.sigmoid(_dot(xm[5], gl1_ref[...])), gl2_ref[...])
    kk = k * kk_ref[...]
    kk = kk * lax.rsqrt(_seg_sum(kk * kk, e_ref, et_ref) + 1e-12)
    ktsum = None
    for z, (lw_o, kt_o, ab_o) in enumerate(((lw0_o, kt0_o, ab0_o), (lw1_o, kt1_o, ab1_o))):
        d_w = w0_ref[z:z + 1, :] + _dot(t1, wl2_ref[z])
        lw_o[...] = -RWKV_DECAY_SCALE * jax.nn.sigmoid(d_w)
        a = jax.nn.sigmoid(a0_ref[z:z + 1, :] + _dot(t2, al2_ref[z]))
        kt = k * (1.0 + (a - 1.0) * ka_ref[...])
        kt_o[...] = kt
        ab_o[...] = kk * a
        ktsum = kt if ktsum is None else ktsum + kt
    r_o[...] = r
    v_o[...] = v
    kk_o[...] = kk
    g_o[...] = g
    bv_o[...] = _seg_sum(r * ktsum * rk_ref[...], e_ref, et_ref) * v


def _rwkv_scan_kernel(r_ref, v_ref, kk_ref, lw_ref, kt_ref, ab_ref, o_ref, s_ref, *, reverse):
    c = RWKV_CHUNK

    @pl.when(pl.program_id(0) == 0)
    def _():
        s_ref[...] = jnp.zeros_like(s_ref)

    ri = lax.broadcasted_iota(jnp.int32, (c, c), 0)
    ci = lax.broadcasted_iota(jnp.int32, (c, c), 1)
    incl = (ci >= ri) if reverse else (ci <= ri)
    ri2 = lax.broadcasted_iota(jnp.int32, (c, 2 * c), 0)
    ci2 = jnp.bitwise_and(lax.broadcasted_iota(jnp.int32, (c, 2 * c), 1), c - 1)
    incl2 = (ci2 >= ri2) if reverse else (ci2 <= ri2)
    strict2 = (ci2 > ri2) if reverse else (ci2 < ri2)
    lw = lw_ref[...]
    cl = _dot_sel(jnp.where(incl, 1.0, 0.0), lw, 3)
    tot = cl[0:1, :] if reverse else cl[c - 1:c, :]
    e_in = jnp.exp(cl)
    e_out = jnp.exp(-cl)
    e_end = jnp.exp(tot - cl)
    kk = kk_ref[...]
    kt = kt_ref[...]
    ab = ab_ref[...]
    kap = kk * jnp.exp(cl - lw)
    rh = r_ref[...] * e_in
    kh = kt * e_out
    bh = ab * e_out
    kb = kt * e_end
    bb = ab * e_end
    e_tot = jnp.exp(tot)
    vv = v_ref[...]
    lane_a = lax.broadcasted_iota(jnp.int32, (1, LANES), 1) < RWKV_HEAD
    bi = lax.broadcasted_iota(jnp.int32, (LANES, LANES), 0) < RWKV_HEAD
    bj = lax.broadcasted_iota(jnp.int32, (LANES, LANES), 1) < RWKV_HEAD
    blockdiag = bi == bj

    def stack2(x):
        return jnp.concatenate([jnp.where(lane_a, x, 0.0), jnp.where(lane_a, 0.0, x)], axis=0)

    pairs = range(D_MODEL // LANES)
    sls = [slice(p * LANES, (p + 1) * LANES) for p in pairs]
    s = [s_ref[p] for p in pairs]
    xq = [jnp.concatenate([kap[:, sl], rh[:, sl]], axis=0) for sl in sls]
    yk = [jnp.concatenate([stack2(kh[:, sl]), stack2(bh[:, sl])], axis=0) for sl in sls]
    gm = [_dot_nt(xq[p], yk[p]) for p in pairs]
    xs = [_dot_nt(xq[p], s[p]) for p in pairs]
    l_kk = [jnp.where(strict2, g[:c, :2 * c], 0.0) for g in gm]
    l_bk = [jnp.where(strict2, g[:c, 2 * c:], 0.0) for g in gm]
    a_rk = [jnp.where(incl2, g[c:, :2 * c], 0.0) for g in gm]
    a_rb = [jnp.where(incl2, g[c:, 2 * c:], 0.0) for g in gm]
    v2 = [stack2(vv[:, sl]) for sl in sls]
    x = [xs[p][:c] + _dot(l_kk[p], v2[p]) for p in pairs]
    lp = [_dot(l_bk[p], stack2(l_bk[p])) for p in pairs]
    x = [x[p] - _dot(l_bk[p], stack2(x[p])) for p in pairs]
    for it in range(5):
        x = [x[p] + _dot(lp[p], stack2(x[p])) for p in pairs]
        if it < 4:
            lp = [_dot(lp[p], stack2(lp[p])) for p in pairs]
    o = [xs[p][c:] + _dot(jnp.concatenate([a_rk[p], -a_rb[p]], axis=1), jnp.concatenate([v2[p], stack2(x[p])], axis=0))
         for p in pairs]
    upd = [_dot_tn(jnp.concatenate([vv[:, sls[p]], -x[p]], axis=0),
                   jnp.concatenate([kb[:, sls[p]], bb[:, sls[p]]], axis=0)) for p in pairs]
    for p in pairs:
        o_ref[:, sls[p]] = o[p]
        s_ref[p] = s[p] * e_tot[:, sls[p]] + jnp.where(blockdiag, upd[p], 0.0)


def _rwkv_out_kernel(of_ref, ob_ref, bv_ref, g_ref, lg_ref, lb_ref, e_ref, et_ref, w_ref, h_ref, mod_ref, lng_ref,
                     lnb_ref, o_ref):
    o = of_ref[...] + ob_ref[...]
    inv = 1.0 / RWKV_HEAD
    oc = o - _seg_sum(o, e_ref, et_ref) * inv
    var = _seg_sum(oc * oc, e_ref, et_ref) * inv
    y = oc * lax.rsqrt(var + RWKV_GN_EPS) * lg_ref[...] + lb_ref[...] + bv_ref[...]
    yo = _dot(y * g_ref[...], w_ref[...])
    z = DEEPNORM_ALPHA * h_ref[...] + mod_ref[0][2:3] * yo
    o_ref[...] = _ln_rows(z, lng_ref[...], lnb_ref[...])


def _rwkv_mixer(h, mods, nct, mu, w_in, w0, w_l1, w_l2, a0, a_l1, a_l2, g_l1, g_l2, k_k, k_a, r_k, gn_g, gn_b, w_out,
                ln_g, ln_b):
    t, d = h.shape
    nt = t // TM
    bf = MXU_DT
    lw_ = w_l1.shape[-1]
    la_ = a_l1.shape[-1]
    zw = jnp.zeros((lw_, d), F32)
    za = jnp.zeros((la_, d), F32)
    wl1 = jnp.concatenate([w_l1[0], w_l1[1]], axis=1).astype(bf)
    wl2 = jnp.stack([jnp.concatenate([w_l2[0], zw], 0), jnp.concatenate([zw, w_l2[1]], 0)]).astype(bf)
    al1 = jnp.concatenate([a_l1[0], a_l1[1]], axis=1).astype(bf)
    al2 = jnp.stack([jnp.concatenate([a_l2[0], za], 0), jnp.concatenate([za, a_l2[1]], 0)]).astype(bf)
    head_of = jnp.arange(d) // RWKV_HEAD
    e = (head_of[:, None] == jnp.arange(LANES)[None, :]).astype(bf)
    et = e.T
    halo_p = pl.BlockSpec((8, d), lambda i: (jnp.maximum(i * (TM // 8) - 1, 0), 0))
    halo_n = pl.BlockSpec((8, d), lambda i: (jnp.minimum((i + 1) * (TM // 8), t // 8 - 1), 0))
    args = [h, h, h, mods, mu, w_in.astype(bf), wl1, wl2, w0, al1, al2, a0, g_l1.astype(bf), g_l2.astype(bf),
            k_k[None], k_a[None], r_k.reshape(1, d), e, et]
    ins = [_row_spec(d), halo_p, halo_n, _mod_spec(nct)] + [_full_spec(a.shape) for a in args[4:]]
    outs = pl.pallas_call(
        functools.partial(_rwkv_prep_kernel, nct=nct, nt=nt), grid=(nt,), in_specs=ins,
        out_specs=[_row_spec(d)] * 11, out_shape=[jax.ShapeDtypeStruct((t, d), F32)] * 11,
        compiler_params=_cparams("arbitrary"), name="rwkv_prep",
    )(*args)
    r, v, kk, g, bv, lw0, lw1, kt0, kt1, ab0, ab1 = outs
    c = RWKV_CHUNK
    ncc, nc = nct * (TM // c), t // c
    o_dir = []
    for d_, (lw, kt, ab) in enumerate(((lw0, kt0, ab0), (lw1, kt1, ab1))):
        reverse = d_ == 1
        spec = pl.BlockSpec((c, d), lambda g_, reverse=reverse: (_tile_of(g_, ncc, nc, reverse), 0))
        o_dir.append(pl.pallas_call(
            functools.partial(_rwkv_scan_kernel, reverse=reverse), grid=(nc,), in_specs=[spec] * 6, out_specs=spec,
            out_shape=jax.ShapeDtypeStruct((t, d), F32),
            scratch_shapes=[pltpu.VMEM((d // LANES, LANES, LANES), F32)],
            compiler_params=_cparams("arbitrary"), name="rwkv_scan_%d" % d_,
        )(r, v, kk, lw, kt, ab))
    args = [o_dir[0], o_dir[1], bv, g, gn_g[None], gn_b[None], e, et, w_out.astype(bf), h, mods, ln_g[None], ln_b[None]]
    ins = [_row_spec(d)] * 4 + [_full_spec(a.shape) for a in args[4:9]] + [_row_spec(d), _mod_spec(nct),
                                                                          _full_spec((1, d)), _full_spec((1, d))]
    return pl.pallas_call(
        _rwkv_out_kernel, grid=(nt,), in_specs=ins, out_specs=_row_spec(d),
        out_shape=jax.ShapeDtypeStruct((t, d), F32), compiler_params=_cparams("arbitrary"), name="rwkv_out",
    )(*args)


def _ret_in_kernel(h_ref, mod_ref, w_ref, cos_ref, sin_ref, q_o, k_o, v_o, g_o):
    d = D_MODEL
    u = _modulate(h_ref[...], mod_ref[0], 0).astype(MXU_DT)
    q = _dot(u, w_ref[:, 0:d])
    k = _dot(u, w_ref[:, d:2 * d]) * (RET_QK ** -0.5)
    v_o[...] = _dot(u, w_ref[:, 2 * d:4 * d])
    g_o[...] = _silu(_dot(u, w_ref[:, 4 * d:6 * d]))
    cos = cos_ref[...]
    sin = sin_ref[...]
    half = RET_QK // 2
    for z, z_o in ((q, q_o), (k, k_o)):
        for hh in range(RET_HEADS):
            lo = z[:, hh * RET_QK:hh * RET_QK + half]
            hi = z[:, hh * RET_QK + half:(hh + 1) * RET_QK]
            zh = jnp.concatenate([lo, hi], axis=1)
            rot = jnp.concatenate([-hi, lo], axis=1)
            z_o[:, hh * RET_QK:(hh + 1) * RET_QK] = zh * cos + rot * sin


def _ret_scan_kernel(q_ref, k_ref, v_ref, inner_ref, qd_ref, kd_ref, bd_ref, o_ref, r_ref):
    @pl.when(pl.program_id(0) == 0)
    def _():
        r_ref[...] = jnp.zeros_like(r_ref)

    for hh in range(RET_HEADS):
        q = q_ref[:, hh * RET_QK:(hh + 1) * RET_QK]
        k = k_ref[:, hh * RET_QK:(hh + 1) * RET_QK]
        v = v_ref[:, hh * RET_V:(hh + 1) * RET_V]
        state = r_ref[hh]
        scores = _dot_nt(q, k) * inner_ref[hh]
        o_ref[:, hh * RET_V:(hh + 1) * RET_V] = _dot(scores, v) + _dot(q, state) * qd_ref[hh]
        r_ref[hh] = state * bd_ref[hh] + _dot_tn(k * kd_ref[hh], v)


def _ret_out_kernel(of_ref, ob_ref, g_ref, gg_ref, gb_ref, w_ref, h_ref, mod_ref, lng_ref, lnb_ref, o_ref):
    parts = []
    for hh in range(RET_HEADS):
        sl = slice(hh * RET_V, (hh + 1) * RET_V)
        o = of_ref[:, sl] + ob_ref[:, sl]
        mu = jnp.mean(o, axis=-1, keepdims=True)
        oc = o - mu
        var = jnp.mean(oc * oc, axis=-1, keepdims=True)
        y = oc * lax.rsqrt(var + LN_EPS) * gg_ref[:, sl] + gb_ref[:, sl]
        parts.append((g_ref[:, sl] * y).astype(MXU_DT))
    yo = _dot(jnp.concatenate(parts, axis=1), w_ref[...])
    z = DEEPNORM_ALPHA * h_ref[...] + mod_ref[0][2:3] * yo
    o_ref[...] = _ln_rows(z, lng_ref[...], lnb_ref[...])


def _ret_mixer(h, mods, nct, rope_cos, rope_sin, w_in, decay_logit, gn_g, gn_b, w_out, ln_g, ln_b):
    t, d = h.shape
    nt = t // TM
    hv = RET_HEADS * RET_V
    q, k, v, sg = pl.pallas_call(
        _ret_in_kernel, grid=(nt,),
        in_specs=[_row_spec(d), _mod_spec(nct), _full_spec(w_in.shape), _row_spec(RET_QK), _row_spec(RET_QK)],
        out_specs=[_row_spec(d), _row_spec(d), _row_spec(hv), _row_spec(hv)],
        out_shape=[jax.ShapeDtypeStruct((t, w), F32) for w in (d, d, hv, hv)],
        compiler_params=_cparams("arbitrary"), name="ret_in",
    )(h, mods, w_in.astype(MXU_DT), rope_cos, rope_sin)
    c = RET_CHUNK
    ncc, nc = nct * (TM // c), t // c
    log_gamma = jax.nn.log_sigmoid(decay_logit.astype(F32))
    pos = jnp.arange(c, dtype=F32)
    o_dir = []
    for d_ in range(2):
        reverse = d_ == 1
        lg = log_gamma[d_][:, None, None]
        p = (c - 1.0 - pos) if reverse else pos
        rel = p[:, None] - p[None, :]
        inner = jnp.where(rel >= 0, jnp.exp(jnp.maximum(rel, 0.0) * lg), 0.0)
        q_dec = jnp.exp((p + 1.0) * log_gamma[d_][:, None])[:, :, None]
        k_dec = jnp.exp((c - 1.0 - p) * log_gamma[d_][:, None])[:, :, None]
        blk_dec = jnp.exp(c * log_gamma[d_])[:, None, None]
        cs = lambda w, reverse=reverse: pl.BlockSpec((c, w), lambda g_: (_tile_of(g_, ncc, nc, reverse), 0))
        o_dir.append(pl.pallas_call(
            _ret_scan_kernel, grid=(nc,),
            in_specs=[cs(d), cs(d), cs(hv), _full_spec(inner.shape), _full_spec(q_dec.shape), _full_spec(k_dec.shape),
                      _full_spec(blk_dec.shape)],
            out_specs=cs(hv), out_shape=jax.ShapeDtypeStruct((t, hv), F32),
            scratch_shapes=[pltpu.VMEM((RET_HEADS, RET_QK, RET_V), F32)],
            compiler_params=_cparams("arbitrary"), name="ret_scan_%d" % d_,
        )(q, k, v, inner, q_dec, k_dec, blk_dec))
    return pl.pallas_call(
        _ret_out_kernel, grid=(nt,),
        in_specs=[_row_spec(hv)] * 3 + [_full_spec((1, hv)), _full_spec((1, hv)), _full_spec((hv, d)), _row_spec(d),
                                        _mod_spec(nct), _full_spec((1, d)), _full_spec((1, d))],
        out_specs=_row_spec(d), out_shape=jax.ShapeDtypeStruct((t, d), F32),
        compiler_params=_cparams("arbitrary"), name="ret_out",
    )(o_dir[0], o_dir[1], sg, gn_g[None], gn_b[None], w_out.astype(MXU_DT), h, mods, ln_g[None], ln_b[None])


def _hgrn_in_kernel(h_ref, mod_ref, w_ref, lb_ref, bf_ref, q_o, v_o, g_o, f0_o, f1_o):
    d = D_MODEL
    u = _modulate(h_ref[...], mod_ref[0], 0).astype(MXU_DT)
    lb = lb_ref[...]
    q_o[...] = _silu(_dot(u, w_ref[:, 0:d]))
    f0_o[...] = lb + (1.0 - lb) * jax.nn.sigmoid(_dot(u, w_ref[:, d:2 * d]) + bf_ref[0:1, :])
    f1_o[...] = lb + (1.0 - lb) * jax.nn.sigmoid(_dot(u, w_ref[:, 2 * d:3 * d]) + bf_ref[1:2, :])
    v_o[...] = _dot(u, w_ref[:, 3 * d:4 * d])
    g_o[...] = _silu(_dot(u, w_ref[:, 4 * d:5 * d]))


def _hgrn_scan_kernel(q_ref, v_ref, f_ref, o_ref, s_ref, *, reverse):
    hb = HGRN_BLOCK
    nb = TM // hb

    @pl.when(pl.program_id(0) == 0)
    def _():
        s_ref[...] = jnp.zeros_like(s_ref)

    ri = lax.broadcasted_iota(jnp.int32, (hb, hb), 0)
    ci = lax.broadcasted_iota(jnp.int32, (hb, hb), 1)
    tri = jnp.where((ci >= ri) if reverse else (ci <= ri), 1.0, 0.0)
    t3 = lax.broadcasted_iota(jnp.int32, (hb, hb, 1), 0)
    s3 = lax.broadcasted_iota(jnp.int32, (hb, hb, 1), 1)
    causal3 = (s3 >= t3) if reverse else (s3 <= t3)

    def block(bi, carry):
        blk = (nb - 1 - bi) if reverse else bi
        r0 = pl.multiple_of(blk * hb, hb)
        f = f_ref[pl.ds(r0, hb), :]
        q = q_ref[pl.ds(r0, hb), :]
        v = v_ref[pl.ds(r0, hb), :]
        kx = 1.0 - f
        b = _dot_sel(tri, jnp.log(f), 3)
        tot = b[0:1, :] if reverse else b[hb - 1:hb, :]
        qe = q * jnp.exp(b)
        kb = kx * jnp.exp(tot - b)
        e_tot = jnp.exp(tot)
        for hh in range(HGRN_HEADS):
            sl = slice(hh * HGRN_HEAD, (hh + 1) * HGRN_HEAD)
            s = s_ref[hh]
            bh = b[:, sl]
            diff = bh[:, None, :] - bh[None, :, :]
            dec = jnp.exp(jnp.where(causal3, diff, -jnp.inf))
            e3 = q[:, sl][:, None, :] * kx[:, sl][None, :, :] * dec
            sc = jnp.sum(e3, axis=-1, keepdims=True)
            o_diag = jnp.sum(sc * v[:, sl][None, :, :], axis=1)
            o_ref[pl.ds(r0, hb), sl] = _dot_nt(qe[:, sl], s) + o_diag
            s_ref[hh] = s * e_tot[:, sl] + _dot_tn(v[:, sl], kb[:, sl])
        return carry

    lax.fori_loop(0, nb, block, 0)


def _hgrn_out_kernel(of_ref, ob_ref, g_ref, ng_ref, w_ref, h_ref, mod_ref, lng_ref, lnb_ref, o_ref):
    parts = []
    for hh in range(HGRN_HEADS):
        sl = slice(hh * HGRN_HEAD, (hh + 1) * HGRN_HEAD)
        o = of_ref[:, sl] + ob_ref[:, sl]
        y = o * lax.rsqrt(jnp.mean(o * o, axis=-1, keepdims=True) + LN_EPS) * ng_ref[...]
        parts.append((y * g_ref[:, sl]).astype(MXU_DT))
    yo = _dot(jnp.concatenate(parts, axis=1), w_ref[...])
    z = DEEPNORM_ALPHA * h_ref[...] + mod_ref[0][2:3] * yo
    o_ref[...] = _ln_rows(z, lng_ref[...], lnb_ref[...])


def _hgrn_mixer(h, mods, nct, lb, w_in, b_f, norm_g, w_out, ln_g, ln_b):
    t, d = h.shape
    nt = t // TM
    q, v, sg, f0, f1 = pl.pallas_call(
        _hgrn_in_kernel, grid=(nt,),
        in_specs=[_row_spec(d), _mod_spec(nct), _full_spec(w_in.shape), _full_spec((1, d)), _full_spec((2, d))],
        out_specs=[_row_spec(d)] * 5, out_shape=[jax.ShapeDtypeStruct((t, d), F32)] * 5,
        compiler_params=_cparams("arbitrary"), name="hgrn_in",
    )(h, mods, w_in.astype(MXU_DT), lb[None], b_f)
    o_dir = []
    for d_, f in enumerate((f0, f1)):
        reverse = d_ == 1
        spec = pl.BlockSpec((TM, d), lambda g_, reverse=reverse: (_tile_of(g_, nct, nt, reverse), 0))
        o_dir.append(pl.pallas_call(
            functools.partial(_hgrn_scan_kernel, reverse=reverse), grid=(nt,), in_specs=[spec] * 3, out_specs=spec,
            out_shape=jax.ShapeDtypeStruct((t, d), F32),
            scratch_shapes=[pltpu.VMEM((HGRN_HEADS, HGRN_HEAD, HGRN_HEAD), F32)],
            compiler_params=_cparams("arbitrary"), name="hgrn_scan_%d" % d_,
        )(q, v, f))
    return pl.pallas_call(
        _hgrn_out_kernel, grid=(nt,),
        in_specs=[_row_spec(d)] * 3 + [_full_spec((1, HGRN_HEAD)), _full_spec((d, d)), _row_spec(d), _mod_spec(nct),
                                       _full_spec((1, d)), _full_spec((1, d))],
        out_specs=_row_spec(d), out_shape=jax.ShapeDtypeStruct((t, d), F32),
        compiler_params=_cparams("arbitrary"), name="hgrn_out",
    )(o_dir[0], o_dir[1], sg, norm_g[None], w_out.astype(MXU_DT), h, mods, ln_g[None], ln_b[None])


def _router_kernel(h_ref, mod_ref, rw_ref, rb_ref, u_o, gate_o, rank_o, gate_t_o, rank_t_o, x_o):
    u = _modulate(h_ref[...], mod_ref[0], 3)
    u_o[...] = u.astype(u_o.dtype)
    w_hi, w_lo = _split(rw_ref[...], 2)
    u_hi, u_lo = _split(u, 2)
    nt_dims = (((1,), (1,)), ((), ()))
    logits = (lax.dot_general(w_hi, u_hi, nt_dims, preferred_element_type=F32)
              + lax.dot_general(w_hi, u_lo, nt_dims, preferred_element_type=F32)
              + lax.dot_general(w_lo, u_hi, nt_dims, preferred_element_type=F32))
    ne, gs = N_EXPERTS, N_EXPERTS // N_GROUPS
    neg = -jnp.inf
    scores = jax.nn.sigmoid(logits[:ne])
    choice = scores + rb_ref[:ne]
    c3 = choice.reshape(N_GROUPS, gs, TM)
    mi = lax.broadcasted_iota(jnp.int32, c3.shape, 1).astype(F32)
    m1 = jnp.max(c3, axis=1, keepdims=True)
    i1 = jnp.min(jnp.where(c3 == m1, mi, float(gs)), axis=1, keepdims=True)
    m2 = jnp.max(jnp.where(mi == i1, neg, c3), axis=1, keepdims=True)
    gscore = m1 + m2
    gi = lax.broadcasted_iota(jnp.int32, gscore.shape, 0).astype(F32)
    gsel = jnp.zeros(gscore.shape, F32)
    for _ in range(TOPK_GROUPS):
        gm = jnp.max(gscore, axis=0, keepdims=True)
        pick = gi == jnp.min(jnp.where(gscore == gm, gi, float(N_GROUPS)), axis=0, keepdims=True)
        gsel = jnp.where(pick, 1.0, gsel)
        gscore = jnp.where(pick, neg, gscore)
    emask = jnp.broadcast_to(gsel, c3.shape).reshape(ne, TM)
    masked = jnp.where(emask > 0.5, choice, neg)
    ei = lax.broadcasted_iota(jnp.int32, masked.shape, 0).astype(F32)
    chosen = jnp.zeros(masked.shape, F32)
    for _ in range(TOP_K):
        em = jnp.max(masked, axis=0, keepdims=True)
        pick = ei == jnp.min(jnp.where(masked == em, ei, float(ne)), axis=0, keepdims=True)
        chosen = jnp.where(pick, 1.0, chosen)
        masked = jnp.where(pick, neg, masked)
    top_w = scores * chosen
    gates = ROUTED_SCALE * top_w / jnp.sum(top_w, axis=0, keepdims=True)
    ti = lax.broadcasted_iota(jnp.int32, (TM, TM), 0)
    tj = lax.broadcasted_iota(jnp.int32, (TM, TM), 1)
    before = jnp.where(ti < tj, 1.0, 0.0).astype(MXU_DT)
    prefix = jnp.dot(chosen.astype(MXU_DT), before, preferred_element_type=F32)
    rank = jnp.where(chosen > 0.5, prefix, -1.0)
    gate_o[0] = gates
    rank_o[0] = rank
    pad = LANES - ne
    gate_t_o[...] = jnp.concatenate([gates, jnp.zeros((pad, TM), F32)], axis=0).T
    rank_t_o[...] = jnp.concatenate([rank, jnp.full((pad, TM), -1.0, F32)], axis=0).T
    cap = MOE_CAP
    slot = lax.broadcasted_iota(jnp.int32, (cap, TM), 0).astype(F32)
    ub = u.astype(MXU_DT)
    for g0 in range(0, ne, MOE_EGROUP):
        onehot = jnp.concatenate([jnp.where(slot == rank[e:e + 1, :], 1.0, 0.0).astype(MXU_DT)
                                  for e in range(g0, g0 + MOE_EGROUP)], axis=0)
        xg = jnp.dot(onehot, ub, preferred_element_type=F32)
        x_o[0, g0:g0 + MOE_EGROUP] = xg.reshape(MOE_EGROUP, cap, D_MODEL).astype(x_o.dtype)


def _expert_kernel(x_ref, wgu_ref, wd_ref, y_ref, wgu_s, wd_s):
    @pl.when(pl.program_id(1) == 0)
    def _():
        wgu_s[...] = wgu_ref[0, 0].astype(MXU_DT)
        wd_s[...] = wd_ref[0, 0].astype(MXU_DT)

    g = x_ref.shape[0]
    ed = EXPERT_DIM
    x = x_ref[...].reshape(g * MOE_CAP, D_MODEL)
    gu = _dot(x, wgu_s[...])
    y = _dot(_silu(gu[:, :ed]) * gu[:, ed:], wd_s[...])
    y_ref[...] = y.reshape(g, 1, MOE_CAP, D_MODEL).astype(y_ref.dtype)


def _combine_kernel(u_ref, gt_ref, rt_ref, y_ref, sgu_ref, sd_ref, h_ref, mod_ref, lng_ref, lnb_ref, *rest, extra):
    if extra:
        ex_ref, o_ref = rest
    else:
        (o_ref,) = rest
    ed = EXPERT_DIM
    cap = MOE_CAP
    per_vreg = LANES // cap
    gu = _dot(u_ref[...], sgu_ref[...])
    acc = _dot(_silu(gu[:, :ed]) * gu[:, ed:], sd_ref[...])
    if extra:
        acc = acc + ex_ref[...]
    lane = lax.broadcasted_iota(jnp.int32, (1, LANES), 1)
    which = lane // cap
    slot = (lane - which * cap).astype(F32)
    rt = rt_ref[...]
    gt = gt_ref[...]
    for g0 in range(0, N_EXPERTS, MOE_EGROUP):
        pieces = []
        for e0 in range(g0, g0 + MOE_EGROUP, per_vreg):
            rsel = rt[:, e0:e0 + 1]
            gsel = gt[:, e0:e0 + 1]
            for q in range(1, per_vreg):
                rsel = jnp.where(which == q, rt[:, e0 + q:e0 + q + 1], rsel)
                gsel = jnp.where(which == q, gt[:, e0 + q:e0 + q + 1], gsel)
            pieces.append(jnp.where(rsel == slot, gsel, 0.0).astype(MXU_DT))
        pw = jnp.concatenate(pieces, axis=1)
        yg = y_ref[0, g0:g0 + MOE_EGROUP].reshape(MOE_EGROUP * cap, D_MODEL)
        acc = acc + jnp.dot(pw, yg.astype(MXU_DT), preferred_element_type=F32)
    z = DEEPNORM_ALPHA * h_ref[...] + mod_ref[0][5:6] * acc
    o_ref[...] = _ln_rows(z, lng_ref[...], lnb_ref[...])


def _overflow_kernel(nr_ref, u_ref, gate_ref, rank_ref, wgu_ref, wd_ref, o_ref, *, nsub):
    i = pl.program_id(0)
    e = pl.program_id(1)
    ed = EXPERT_DIM
    cap = MOE_CAP

    @pl.when(e == 0)
    def _():
        o_ref[...] = jnp.zeros_like(o_ref)

    wgu = wgu_ref[0, 0].astype(MXU_DT)
    wd = wd_ref[0, 0].astype(MXU_DT)
    subs = [slice(j * TM, (j + 1) * TM) for j in range(nsub)]

    def one_round(r, carry):
        slot = lax.broadcasted_iota(jnp.int32, (cap, TM), 0).astype(F32) + (r * cap).astype(F32)
        hit = [slot == rank_ref[j, pl.ds(e, 1), :] for j in range(nsub)]
        x = jnp.concatenate([_dot(jnp.where(hit[j], 1.0, 0.0), u_ref[subs[j], :]) for j in range(nsub)], axis=0)
        gu = _dot(x, wgu)
        y = _dot(_silu(gu[:, :ed]) * gu[:, ed:], wd)
        for j in range(nsub):
            pw = jnp.where(hit[j], gate_ref[j, pl.ds(e, 1), :], 0.0)
            o_ref[subs[j], :] += _dot_tn(pw, y[j * cap:(j + 1) * cap])
        return carry

    lax.fori_loop(1, nr_ref[i, e], one_round, 0)


def _moe_layer(h, mods, nct, layer, router_w, router_b, w_gu, w_down, sh_gu, sh_down, ln_g, ln_b):
    t, d = h.shape
    nt = t // TM
    ne, cap = N_EXPERTS, MOE_CAP
    rw = jnp.concatenate([router_w.T, jnp.zeros((LANES - ne, d), F32)], axis=0)
    rb = jnp.concatenate([router_b, jnp.zeros((LANES - ne,), F32)])[:, None]
    per_tile = pl.BlockSpec((1, ne, TM), lambda i: (i, 0, 0))
    slots = pl.BlockSpec((1, ne, cap, d), lambda i: (i, 0, 0, 0))
    u, gates, ranks, gates_t, ranks_t, xs = pl.pallas_call(
        _router_kernel, grid=(nt,),
        in_specs=[_row_spec(d), _mod_spec(nct), _full_spec((LANES, d)), _full_spec((LANES, 1))],
        out_specs=[_row_spec(d), per_tile, per_tile, _row_spec(LANES), _row_spec(LANES), slots],
        out_shape=[jax.ShapeDtypeStruct((t, d), MXU_DT), jax.ShapeDtypeStruct((nt, ne, TM), F32),
                   jax.ShapeDtypeStruct((nt, ne, TM), F32), jax.ShapeDtypeStruct((t, LANES), F32),
                   jax.ShapeDtypeStruct((t, LANES), F32), jax.ShapeDtypeStruct((nt, ne, cap, d), MXU_DT)],
        compiler_params=_cparams("arbitrary"), name="moe_router",
    )(h, mods, rw, rb)

    run = max(g for g in range(1, MOE_RUN + 1) if nt % g == 0)
    ys = pl.pallas_call(
        _expert_kernel, grid=(ne, nt // run),
        in_specs=[pl.BlockSpec((run, 1, cap, d), lambda e, c: (c, e, 0, 0)),
                  pl.BlockSpec((1, 1, d, 2 * EXPERT_DIM), lambda e, c: (layer, e, 0, 0)),
                  pl.BlockSpec((1, 1, EXPERT_DIM, d), lambda e, c: (layer, e, 0, 0))],
        out_specs=pl.BlockSpec((run, 1, cap, d), lambda e, c: (c, e, 0, 0)),
        out_shape=jax.ShapeDtypeStruct((nt, ne, cap, d), MXU_DT),
        scratch_shapes=[pltpu.VMEM((d, 2 * EXPERT_DIM), MXU_DT), pltpu.VMEM((EXPERT_DIM, d), MXU_DT)],
        compiler_params=_cparams("arbitrary", "arbitrary"), name="moe_experts",
    )(xs, w_gu, w_down)

    sgu, sd = sh_gu.astype(MXU_DT), sh_down.astype(MXU_DT)
    base_specs = [_row_spec(d), _row_spec(LANES), _row_spec(LANES), slots, _full_spec(sgu.shape), _full_spec(sd.shape),
                  _row_spec(d), _mod_spec(nct), _full_spec((1, d)), _full_spec((1, d))]
    base_args = (u, gates_t, ranks_t, ys, sgu, sd, h, mods, ln_g[None], ln_b[None])

    def combine(*extra):
        return pl.pallas_call(
            functools.partial(_combine_kernel, extra=bool(extra)), grid=(nt,),
            in_specs=base_specs + [_row_spec(d)] * len(extra), out_specs=_row_spec(d),
            out_shape=jax.ShapeDtypeStruct((t, d), F32), compiler_params=_cparams("arbitrary"), name="moe_combine",
        )(*base_args, *extra)

    def with_overflow():
        tm = MOE_TM if t % MOE_TM == 0 else TM
        nsub = tm // TM
        rounds = jnp.max(((count + cap - 1) // cap).reshape(t // tm, nsub, ne), axis=1)
        sub = pl.BlockSpec((nsub, ne, TM), lambda i, e, nr: (i, 0, 0))
        grid_spec = pltpu.PrefetchScalarGridSpec(
            num_scalar_prefetch=1, grid=(t // tm, ne),
            in_specs=[pl.BlockSpec((tm, d), lambda i, e, nr: (i, 0)), sub, sub,
                      pl.BlockSpec((1, 1, d, 2 * EXPERT_DIM), lambda i, e, nr: (layer, e, 0, 0)),
                      pl.BlockSpec((1, 1, EXPERT_DIM, d), lambda i, e, nr: (layer, e, 0, 0))],
            out_specs=pl.BlockSpec((tm, d), lambda i, e, nr: (i, 0)))
        extra = pl.pallas_call(
            functools.partial(_overflow_kernel, nsub=nsub), grid_spec=grid_spec,
            out_shape=jax.ShapeDtypeStruct((t, d), F32),
            compiler_params=_cparams("arbitrary", "arbitrary"), name="moe_overflow",
        )(rounds, u, gates, ranks, w_gu, w_down)
        return combine(extra)

    count = jnp.max(ranks, axis=-1).astype(jnp.int32) + 1
    return lax.cond(jnp.max(count) > cap, with_overflow, combine)


def kernel(x, c, ctx, c_ctx, ada_w, ada_b, post_ln_g, post_ln_b, lru_w_in, lru_conv_w, lru_conv_b, lru_gate_w, lru_gate_b, lru_lambda, lru_w_out, rwkv_mu, rwkv_w_in, rwkv_w0, rwkv_w_l1, rwkv_w_l2, rwkv_a0, rwkv_a_l1, rwkv_a_l2, rwkv_g_l1, rwkv_g_l2, rwkv_k_k, rwkv_k_a, rwkv_r_k, rwkv_ln_g, rwkv_ln_b, rwkv_w_out, ret_w_in, ret_decay, ret_gn_g, ret_gn_b, ret_w_out, hgrn_w_in, hgrn_b_f, hgrn_lb, hgrn_norm_g, hgrn_w_out, moe_router, moe_bias, moe_w_gu, moe_w_down, moe_sh_gu, moe_sh_down):
    assert x.shape[0] == 1 and ctx.shape[0] == 1
    n_ctx, n_lat, d = ctx.shape[1], x.shape[1], x.shape[2]
    assert n_ctx % TM == 0 and n_lat % TM == 0 and d == D_MODEL
    nct = n_ctx // TM
    rows = n_lat // GRID_W
    pos_row = jnp.repeat(jnp.arange(rows, dtype=F32), GRID_W)
    pos_col = jnp.tile(jnp.arange(GRID_W, dtype=F32), rows)
    n_freq = RET_QK // 4
    freqs = ROPE_BASE ** (-jnp.arange(n_freq, dtype=F32) / n_freq)
    ang = jnp.concatenate([pos_row[:, None] * freqs, pos_col[:, None] * freqs], axis=-1)
    ang = jnp.concatenate([ang, ang], axis=-1)
    rope_cos = jnp.concatenate([jnp.ones((n_ctx, RET_QK), F32), jnp.cos(ang)], axis=0)
    rope_sin = jnp.concatenate([jnp.zeros((n_ctx, RET_QK), F32), jnp.sin(ang)], axis=0)
    lb_cum = jnp.cumsum(jax.nn.softmax(hgrn_lb.astype(F32), axis=0), axis=0)

    cond = jnp.concatenate([c_ctx[None], c, jnp.zeros((6, d), F32)], axis=0)
    mods_all = _ada_mods(cond, ada_w, ada_b)
    h = jnp.concatenate([ctx[0], x[0]], axis=0)
    for i in range(DEPTH):
        kind, j = i % N_MIXERS, i // N_MIXERS
        mods = mods_all[i]
        lng, lnb = post_ln_g[i, 0], post_ln_b[i, 0]
        if kind == 0:
            h = _lru_mixer(h, mods, nct, lru_w_in[j], lru_conv_w[j], lru_conv_b[j], lru_gate_w[j], lru_gate_b[j],
                           lru_lambda[j], lru_w_out[j], lng, lnb)
        elif kind == 1:
            h = _rwkv_mixer(h, mods, nct, rwkv_mu[j], rwkv_w_in[j], rwkv_w0[j], rwkv_w_l1[j], rwkv_w_l2[j], rwkv_a0[j],
                            rwkv_a_l1[j], rwkv_a_l2[j], rwkv_g_l1[j], rwkv_g_l2[j], rwkv_k_k[j], rwkv_k_a[j],
                            rwkv_r_k[j], rwkv_ln_g[j], rwkv_ln_b[j], rwkv_w_out[j], lng, lnb)
        elif kind == 2:
            h = _ret_mixer(h, mods, nct, rope_cos, rope_sin, ret_w_in[j], ret_decay[j], ret_gn_g[j], ret_gn_b[j],
                           ret_w_out[j], lng, lnb)
        else:
            h = _hgrn_mixer(h, mods, nct, lb_cum[i] - lb_cum[0], hgrn_w_in[j], hgrn_b_f[j], hgrn_norm_g[j],
                            hgrn_w_out[j], lng, lnb)
        h = _moe_layer(h, mods, nct, i, moe_router[i], moe_bias[i], moe_w_gu, moe_w_down, moe_sh_gu[i],
                       moe_sh_down[i], post_ln_g[i, 1], post_ln_b[i, 1])
    return h[n_ctx:][None]
```

```python
import math
import functools
import jax
import jax.numpy as jnp
from jax import lax
from jax.experimental import pallas as pl
from jax.experimental.pallas import tpu as pltpu

F32 = jnp.float32
MXU_DT = jnp.bfloat16
LANES = 128
TM = 256
VMEM_LIMIT = 56 * 2 ** 20

D_MODEL = 1024
DEPTH = 4
GRID_W = 64
N_MIXERS = 4
DEEPNORM_ALPHA = (2.0 * DEPTH) ** 0.25
LN_EPS = 1e-5
LRU_WIDTH = D_MODEL
LRU_BLOCKS = 16
LRU_BLOCK = LRU_WIDTH // LRU_BLOCKS
LRU_C = 8.0
RWKV_HEAD = 64
RWKV_HEADS = D_MODEL // RWKV_HEAD
RWKV_DECAY_SCALE = math.exp(-0.5)
RWKV_GN_EPS = 64e-5
RWKV_CHUNK = 64
RET_HEADS = 4
RET_QK = D_MODEL // RET_HEADS
RET_V = 2 * RET_QK
RET_CHUNK = 128
ROPE_BASE = 10000.0
HGRN_HEADS = 8
HGRN_HEAD = D_MODEL // HGRN_HEADS
HGRN_BLOCK = 16
N_EXPERTS = 64
TOP_K = 8
N_GROUPS = 8
TOPK_GROUPS = 4
EXPERT_DIM = 256
ROUTED_SCALE = 2.5
MOE_OVER_STEPS = 256
MOE_CAP = 64
MOE_EGROUP = 8
MOE_RUN = 13


def _cparams(*sem):
    return pltpu.CompilerParams(dimension_semantics=sem, vmem_limit_bytes=VMEM_LIMIT)


def _dot(a, b):
    return jnp.dot(a.astype(MXU_DT), b.astype(MXU_DT), preferred_element_type=F32)


def _dot_nt(a, b):
    return lax.dot_general(a.astype(MXU_DT), b.astype(MXU_DT), (((1,), (1,)), ((), ())), preferred_element_type=F32)


def _dot_tn(a, b):
    return lax.dot_general(a.astype(MXU_DT), b.astype(MXU_DT), (((0,), (0,)), ((), ())), preferred_element_type=F32)


def _split(x, n):
    parts = []
    for _ in range(n):
        p = x.astype(MXU_DT)
        parts.append(p)
        x = x - p.astype(F32)
    return parts


def _dot_sel(sel, x, n):
    return sum(jnp.dot(sel.astype(MXU_DT), p, preferred_element_type=F32) for p in _split(x, n))


def _dot_xsel(x, sel, n):
    return sum(jnp.dot(p, sel.astype(MXU_DT), preferred_element_type=F32) for p in _split(x, n))


def _modulate(h, m, shift_idx):
    return h * (1.0 + m[shift_idx + 1:shift_idx + 2]) + m[shift_idx:shift_idx + 1]


def _ln_rows(z, g, b):
    mu = jnp.mean(z, axis=-1, keepdims=True)
    zc = z - mu
    var = jnp.mean(zc * zc, axis=-1, keepdims=True)
    return zc * lax.rsqrt(var + LN_EPS) * g + b


def _silu(x):
    return x * jax.nn.sigmoid(x)


def _shift_down(x, first_row):
    rows = lax.broadcasted_iota(jnp.int32, (x.shape[0], 1), 0)
    return jnp.where(rows == 0, first_row, pltpu.roll(x, 1, 0))


def _shift_up(x, last_row):
    n = x.shape[0]
    rows = lax.broadcasted_iota(jnp.int32, (n, 1), 0)
    return jnp.where(rows == n - 1, last_row, pltpu.roll(x, n - 1, 0))


def _tile_of(g, nct, nt, reverse):
    if not reverse:
        return g
    return jnp.where(g < nct, nct - 1 - g, nt - 1 - (g - nct))


def _halo_flags(t, nct, nt):
    prev_ok = jnp.logical_and(t != 0, t != nct).astype(F32)
    next_ok = jnp.logical_and(t != nct - 1, t != nt - 1).astype(F32)
    return prev_ok, next_ok


def _ada_kernel(s_ref, w_ref, b_ref, o_ref):
    o_ref[0] = _dot(_silu(s_ref[...]), w_ref[0]) + b_ref[0]


def _ada_mods(cond, ada_w, ada_b):
    nl, d, n6 = ada_w.shape
    out = pl.pallas_call(
        _ada_kernel, grid=(nl, n6 // d),
        in_specs=[pl.BlockSpec((8, d), lambda l, j: (0, 0)),
                  pl.BlockSpec((1, d, d), lambda l, j: (l, 0, j)),
                  pl.BlockSpec((1, 1, d), lambda l, j: (l, 0, j))],
        out_specs=pl.BlockSpec((1, 8, d), lambda l, j: (l, 0, j)),
        out_shape=jax.ShapeDtypeStruct((nl, 8, n6), F32),
        compiler_params=_cparams("arbitrary", "arbitrary"), name="ada_mods",
    )(cond, ada_w, ada_b.reshape(nl, 1, n6))
    return out[:, :2].reshape(nl, 2, 6, d)


def _row_spec(width, tm=TM):
    return pl.BlockSpec((tm, width), lambda i: (i, 0))


def _full_spec(shape):
    nd = len(shape)
    return pl.BlockSpec(tuple(shape), lambda *_: (0,) * nd)


def _mod_spec(nct):
    return pl.BlockSpec((1, 6, D_MODEL), lambda i: (jnp.minimum(i // nct, 1), 0, 0))


def _out_ln_kernel(p_ref, w_ref, h_ref, mod_ref, lng_ref, lnb_ref, o_ref):
    y = _dot(p_ref[...], w_ref[...])
    z = DEEPNORM_ALPHA * h_ref[...] + mod_ref[0][2:3] * y
    o_ref[...] = _ln_rows(z, lng_ref[...], lnb_ref[...])


def _out_ln(p, w_out, h, mods, ln_g, ln_b, nct):
    t, din = p.shape
    d = D_MODEL
    return pl.pallas_call(
        _out_ln_kernel, grid=(t // TM,),
        in_specs=[_row_spec(din), _full_spec((din, d)), _row_spec(d), _mod_spec(nct),
                  _full_spec((1, d)), _full_spec((1, d))],
        out_specs=_row_spec(d), out_shape=jax.ShapeDtypeStruct((t, d), F32),
        compiler_params=_cparams("arbitrary"), name="out_ln",
    )(p, w_out.astype(MXU_DT), h, mods, ln_g[None], ln_b[None])


def _lru_in_kernel(h_ref, mod_ref, w_ref, g_ref, x_ref):
    u = _modulate(h_ref[...], mod_ref[0], 0)
    z = _dot(u, w_ref[...])
    g_ref[...] = jax.nn.gelu(z[:, :LRU_WIDTH], approximate=True)
    x_ref[...] = z[:, LRU_WIDTH:]


def _lru_scan_kernel(x_ref, xp_ref, xn_ref, cw_ref, cb_ref, gw_ref, gb_ref, lam_ref, *rest, nct, nt, reverse, final):
    if final:
        hf_ref, g_ref, o_ref, a_s, b_s, h_s, st_s = rest
    else:
        o_ref, a_s, b_s, h_s, st_s = rest
    g = pl.program_id(0)
    t = _tile_of(g, nct, nt, reverse)
    prev_ok, next_ok = _halo_flags(t, nct, nt)

    @pl.when(g == 0)
    def _():
        st_s[...] = jnp.zeros_like(st_s)

    x = x_ref[...]
    xm1 = _shift_down(x, xp_ref[7:8, :] * prev_ok)
    n0 = xn_ref[0:1, :] * next_ok
    n1 = xn_ref[1:2, :] * next_ok
    xp1 = _shift_up(x, n0)
    xp2 = _shift_up(xp1, n1)
    cw = cw_ref[...]
    xc = cw[0:1] * xm1 + cw[1:2] * x + cw[2:3] * xp1 + cw[3:4] * xp2 + cb_ref[...]
    gates = jax.nn.sigmoid(_dot(xc, gw_ref[...]) + gb_ref[...])
    lam = lam_ref[...]
    softplus = jnp.maximum(-lam, 0.0) + jnp.log(1.0 + jnp.exp(-jnp.abs(lam)))
    log_a = -LRU_C * gates[:, :LRU_WIDTH] * softplus
    a_s[...] = jnp.exp(log_a)
    b_s[...] = jnp.sqrt(1.0 - jnp.exp(2.0 * log_a)) * (gates[:, LRU_WIDTH:] * xc)

    def row(r, hcur):
        tt = (TM - 1 - r) if reverse else r
        hcur = a_s[pl.ds(tt, 1), :] * hcur + b_s[pl.ds(tt, 1), :]
        h_s[pl.ds(tt, 1), :] = hcur
        return hcur

    st_s[...] = lax.fori_loop(0, TM, row, st_s[...], unroll=8)
    if final:
        o_ref[...] = g_ref[...] * (hf_ref[...] + h_s[...])
    else:
        o_ref[...] = h_s[...]


def _lru_mixer(h, mods, nct, w_in, conv_w, conv_b, gate_w, gate_b, lam, w_out, ln_g, ln_b):
    t, d = h.shape
    nt = t // TM
    w = LRU_WIDTH
    gelu, rnn = pl.pallas_call(
        _lru_in_kernel, grid=(nt,),
        in_specs=[_row_spec(d), _mod_spec(nct), _full_spec((d, 2 * w))],
        out_specs=[_row_spec(w), _row_spec(w)],
        out_shape=[jax.ShapeDtypeStruct((t, w), F32)] * 2,
        compiler_params=_cparams("arbitrary"), name="lru_in",
    )(h, mods, w_in.astype(MXU_DT))
    eye = jnp.eye(LRU_BLOCKS, dtype=F32)
    gw = jnp.einsum('dgnij,nm->dgnimj', gate_w, eye).reshape(2, 2, w, w)
    gw = jnp.concatenate([gw[:, 0], gw[:, 1]], axis=-1).astype(MXU_DT)
    gb = gate_b.reshape(2, 1, 2 * w)
    hf = None
    for d_ in range(2):
        reverse = d_ == 1
        final = d_ == 1
        tile = lambda g: _tile_of(g, nct, nt, reverse)
        ins = [pl.BlockSpec((TM, w), lambda g: (tile(g), 0)),
               pl.BlockSpec((8, w), lambda g: (jnp.maximum(tile(g) * (TM // 8) - 1, 0), 0)),
               pl.BlockSpec((8, w), lambda g: (jnp.minimum((tile(g) + 1) * (TM // 8), t // 8 - 1), 0)),
               _full_spec((4, w)), _full_spec((1, w)), _full_spec((w, 2 * w)), _full_spec((1, 2 * w)),
               _full_spec((1, w))]
        args = [rnn, rnn, rnn, conv_w, conv_b[None], gw[d_], gb[d_], lam[d_][None]]
        if final:
            ins += [pl.BlockSpec((TM, w), lambda g: (tile(g), 0))] * 2
            args += [hf, gelu]
        out = pl.pallas_call(
            functools.partial(_lru_scan_kernel, nct=nct, nt=nt, reverse=reverse, final=final),
            grid=(nt,), in_specs=ins,
            out_specs=pl.BlockSpec((TM, w), lambda g: (tile(g), 0)),
            out_shape=jax.ShapeDtypeStruct((t, w), F32),
            scratch_shapes=[pltpu.VMEM((TM, w), F32)] * 3 + [pltpu.VMEM((1, w), F32)],
            compiler_params=_cparams("arbitrary"), name="lru_scan_%d" % d_,
        )(*args)
        hf = out
    return _out_ln(hf, w_out, h, mods, ln_g, ln_b, nct)


def _seg_sum(x, e_ref, et_ref):
    s = _dot_xsel(x, e_ref[...], 2)
    return _dot_xsel(s, et_ref[...], 2)


def _rwkv_prep_kernel(h_ref, hp_ref, hn_ref, mod_ref, mu_ref, win_ref, wl1_ref, wl2_ref, w0_ref, al1_ref, al2_ref,
                      a0_ref, gl1_ref, gl2_ref, kk_ref, ka_ref, rk_ref, e_ref, et_ref,
                      r_o, v_o, kk_o, g_o, bv_o, lw0_o, lw1_o, kt0_o, kt1_o, ab0_o, ab1_o, *, nct, nt):
    i = pl.program_id(0)
    prev_ok, next_ok = _halo_flags(i, nct, nt)
    m = mod_ref[0]
    u = _modulate(h_ref[...], m, 0)
    up = _modulate(hp_ref[7:8, :], m, 0) * prev_ok
    un = _modulate(hn_ref[0:1, :], m, 0) * next_ok
    lane = lax.broadcasted_iota(jnp.int32, (1, D_MODEL), 1)
    sh = jnp.where(lane < D_MODEL // 2, _shift_down(u, up), _shift_up(u, un))
    dx = sh - u
    mu = mu_ref[...]
    xm = [u + dx * mu[c:c + 1] for c in range(6)]
    r = _dot(xm[0], win_ref[0])
    k = _dot(xm[1], win_ref[1])
    v = _dot(xm[2], win_ref[2])
    t1 = jnp.tanh(_dot(xm[3], wl1_ref[...]))
    t2 = _dot(xm[4], al1_ref[...])
    g = _dot(jax.nn.sigmoid(_dot(xm[5], gl1_ref[...])), gl2_ref[...])
    kk = k * kk_ref[...]
    kk = kk * lax.rsqrt(_seg_sum(kk * kk, e_ref, et_ref) + 1e-12)
    ktsum = None
    for z, (lw_o, kt_o, ab_o) in enumerate(((lw0_o, kt0_o, ab0_o), (lw1_o, kt1_o, ab1_o))):
        d_w = w0_ref[z:z + 1, :] + _dot(t1, wl2_ref[z])
        lw_o[...] = -RWKV_DECAY_SCALE * jax.nn.sigmoid(d_w)
        a = jax.nn.sigmoid(a0_ref[z:z + 1, :] + _dot(t2, al2_ref[z]))
        kt = k * (1.0 + (a - 1.0) * ka_ref[...])
        kt_o[...] = kt
        ab_o[...] = kk * a
        ktsum = kt if ktsum is None else ktsum + kt
    r_o[...] = r
    v_o[...] = v
    kk_o[...] = kk
    g_o[...] = g
    bv_o[...] = _seg_sum(r * ktsum * rk_ref[...], e_ref, et_ref) * v


def _rwkv_scan_kernel(r_ref, v_ref, kk_ref, lw_ref, kt_ref, ab_ref, o_ref, s_ref, *, reverse):
    c = RWKV_CHUNK

    @pl.when(pl.program_id(0) == 0)
    def _():
        s_ref[...] = jnp.zeros_like(s_ref)

    ri = lax.broadcasted_iota(jnp.int32, (c, c), 0)
    ci = lax.broadcasted_iota(jnp.int32, (c, c), 1)
    incl = (ci >= ri) if reverse else (ci <= ri)
    ri2 = lax.broadcasted_iota(jnp.int32, (c, 2 * c), 0)
    ci2 = jnp.bitwise_and(lax.broadcasted_iota(jnp.int32, (c, 2 * c), 1), c - 1)
    incl2 = (ci2 >= ri2) if reverse else (ci2 <= ri2)
    strict2 = (ci2 > ri2) if reverse else (ci2 < ri2)
    lw = lw_ref[...]
    cl = _dot_sel(jnp.where(incl, 1.0, 0.0), lw, 3)
    tot = cl[0:1, :] if reverse else cl[c - 1:c, :]
    e_in = jnp.exp(cl)
    e_out = jnp.exp(-cl)
    e_end = jnp.exp(tot - cl)
    kk = kk_ref[...]
    kt = kt_ref[...]
    ab = ab_ref[...]
    kap = kk * jnp.exp(cl - lw)
    rh = r_ref[...] * e_in
    kh = kt * e_out
    bh = ab * e_out
    kb = kt * e_end
    bb = ab * e_end
    e_tot = jnp.exp(tot)
    vv = v_ref[...]
    lane_a = lax.broadcasted_iota(jnp.int32, (1, LANES), 1) < RWKV_HEAD
    bi = lax.broadcasted_iota(jnp.int32, (LANES, LANES), 0) < RWKV_HEAD
    bj = lax.broadcasted_iota(jnp.int32, (LANES, LANES), 1) < RWKV_HEAD
    blockdiag = bi == bj

    def stack2(x):
        return jnp.concatenate([jnp.where(lane_a, x, 0.0), jnp.where(lane_a, 0.0, x)], axis=0)

    pairs = range(D_MODEL // LANES)
    sls = [slice(p * LANES, (p + 1) * LANES) for p in pairs]
    s = [s_ref[p] for p in pairs]
    xq = [jnp.concatenate([kap[:, sl], rh[:, sl]], axis=0) for sl in sls]
    yk = [jnp.concatenate([stack2(kh[:, sl]), stack2(bh[:, sl])], axis=0) for sl in sls]
    gm = [_dot_nt(xq[p], yk[p]) for p in pairs]
    xs = [_dot_nt(xq[p], s[p]) for p in pairs]
    l_kk = [jnp.where(strict2, g[:c, :2 * c], 0.0) for g in gm]
    l_bk = [jnp.where(strict2, g[:c, 2 * c:], 0.0) for g in gm]
    a_rk = [jnp.where(incl2, g[c:, :2 * c], 0.0) for g in gm]
    a_rb = [jnp.where(incl2, g[c:, 2 * c:], 0.0) for g in gm]
    v2 = [stack2(vv[:, sl]) for sl in sls]
    x = [xs[p][:c] + _dot(l_kk[p], v2[p]) for p in pairs]
    lp = [_dot(l_bk[p], stack2(l_bk[p])) for p in pairs]
    x = [x[p] - _dot(l_bk[p], stack2(x[p])) for p in pairs]
    for it in range(5):
        x = [x[p] + _dot(lp[p], stack2(x[p])) for p in pairs]
        if it < 4:
            lp = [_dot(lp[p], stack2(lp[p])) for p in pairs]
    o = [xs[p][c:] + _dot(jnp.concatenate([a_rk[p], -a_rb[p]], axis=1), jnp.concatenate([v2[p], stack2(x[p])], axis=0))
         for p in pairs]
    upd = [_dot_tn(jnp.concatenate([vv[:, sls[p]], -x[p]], axis=0),
                   jnp.concatenate([kb[:, sls[p]], bb[:, sls[p]]], axis=0)) for p in pairs]
    for p in pairs:
        o_ref[:, sls[p]] = o[p]
        s_ref[p] = s[p] * e_tot[:, sls[p]] + jnp.where(blockdiag, upd[p], 0.0)


def _rwkv_out_kernel(of_ref, ob_ref, bv_ref, g_ref, lg_ref, lb_ref, e_ref, et_ref, w_ref, h_ref, mod_ref, lng_ref,
                     lnb_ref, o_ref):
    o = of_ref[...] + ob_ref[...]
    inv = 1.0 / RWKV_HEAD
    oc = o - _seg_sum(o, e_ref, et_ref) * inv
    var = _seg_sum(oc * oc, e_ref, et_ref) * inv
    y = oc * lax.rsqrt(var + RWKV_GN_EPS) * lg_ref[...] + lb_ref[...] + bv_ref[...]
    yo = _dot(y * g_ref[...], w_ref[...])
    z = DEEPNORM_ALPHA * h_ref[...] + mod_ref[0][2:3] * yo
    o_ref[...] = _ln_rows(z, lng_ref[...], lnb_ref[...])


def _rwkv_mixer(h, mods, nct, mu, w_in, w0, w_l1, w_l2, a0, a_l1, a_l2, g_l1, g_l2, k_k, k_a, r_k, gn_g, gn_b, w_out,
                ln_g, ln_b):
    t, d = h.shape
    nt = t // TM
    bf = MXU_DT
    lw_ = w_l1.shape[-1]
    la_ = a_l1.shape[-1]
    zw = jnp.zeros((lw_, d), F32)
    za = jnp.zeros((la_, d), F32)
    wl1 = jnp.concatenate([w_l1[0], w_l1[1]], axis=1).astype(bf)
    wl2 = jnp.stack([jnp.concatenate([w_l2[0], zw], 0), jnp.concatenate([zw, w_l2[1]], 0)]).astype(bf)
    al1 = jnp.concatenate([a_l1[0], a_l1[1]], axis=1).astype(bf)
    al2 = jnp.stack([jnp.concatenate([a_l2[0], za], 0), jnp.concatenate([za, a_l2[1]], 0)]).astype(bf)
    head_of = jnp.arange(d) // RWKV_HEAD
    e = (head_of[:, None] == jnp.arange(LANES)[None, :]).astype(bf)
    et = e.T
    halo_p = pl.BlockSpec((8, d), lambda i: (jnp.maximum(i * (TM // 8) - 1, 0), 0))
    halo_n = pl.BlockSpec((8, d), lambda i: (jnp.minimum((i + 1) * (TM // 8), t // 8 - 1), 0))
    args = [h, h, h, mods, mu, w_in.astype(bf), wl1, wl2, w0, al1, al2, a0, g_l1.astype(bf), g_l2.astype(bf),
            k_k[None], k_a[None], r_k.reshape(1, d), e, et]
    ins = [_row_spec(d), halo_p, halo_n, _mod_spec(nct)] + [_full_spec(a.shape) for a in args[4:]]
    outs = pl.pallas_call(
        functools.partial(_rwkv_prep_kernel, nct=nct, nt=nt), grid=(nt,), in_specs=ins,
        out_specs=[_row_spec(d)] * 11, out_shape=[jax.ShapeDtypeStruct((t, d), F32)] * 11,
        compiler_params=_cparams("arbitrary"), name="rwkv_prep",
    )(*args)
    r, v, kk, g, bv, lw0, lw1, kt0, kt1, ab0, ab1 = outs
    c = RWKV_CHUNK
    ncc, nc = nct * (TM // c), t // c
    o_dir = []
    for d_, (lw, kt, ab) in enumerate(((lw0, kt0, ab0), (lw1, kt1, ab1))):
        reverse = d_ == 1
        spec = pl.BlockSpec((c, d), lambda g_, reverse=reverse: (_tile_of(g_, ncc, nc, reverse), 0))
        o_dir.append(pl.pallas_call(
            functools.partial(_rwkv_scan_kernel, reverse=reverse), grid=(nc,), in_specs=[spec] * 6, out_specs=spec,
            out_shape=jax.ShapeDtypeStruct((t, d), F32),
            scratch_shapes=[pltpu.VMEM((d // LANES, LANES, LANES), F32)],
            compiler_params=_cparams("arbitrary"), name="rwkv_scan_%d" % d_,
        )(r, v, kk, lw, kt, ab))
    args = [o_dir[0], o_dir[1], bv, g, gn_g[None], gn_b[None], e, et, w_out.astype(bf), h, mods, ln_g[None], ln_b[None]]
    ins = [_row_spec(d)] * 4 + [_full_spec(a.shape) for a in args[4:9]] + [_row_spec(d), _mod_spec(nct),
                                                                          _full_spec((1, d)), _full_spec((1, d))]
    return pl.pallas_call(
        _rwkv_out_kernel, grid=(nt,), in_specs=ins, out_specs=_row_spec(d),
        out_shape=jax.ShapeDtypeStruct((t, d), F32), compiler_params=_cparams("arbitrary"), name="rwkv_out",
    )(*args)


def _ret_in_kernel(h_ref, mod_ref, w_ref, cos_ref, sin_ref, q_o, k_o, v_o, g_o):
    d = D_MODEL
    u = _modulate(h_ref[...], mod_ref[0], 0).astype(MXU_DT)
    q = _dot(u, w_ref[:, 0:d])
    k = _dot(u, w_ref[:, d:2 * d]) * (RET_QK ** -0.5)
    v_o[...] = _dot(u, w_ref[:, 2 * d:4 * d])
    g_o[...] = _silu(_dot(u, w_ref[:, 4 * d:6 * d]))
    cos = cos_ref[...]
    sin = sin_ref[...]
    half = RET_QK // 2
    for z, z_o in ((q, q_o), (k, k_o)):
        for hh in range(RET_HEADS):
            lo = z[:, hh * RET_QK:hh * RET_QK + half]
            hi = z[:, hh * RET_QK + half:(hh + 1) * RET_QK]
            zh = jnp.concatenate([lo, hi], axis=1)
            rot = jnp.concatenate([-hi, lo], axis=1)
            z_o[:, hh * RET_QK:(hh + 1) * RET_QK] = zh * cos + rot * sin


def _ret_scan_kernel(q_ref, k_ref, v_ref, inner_ref, qd_ref, kd_ref, bd_ref, o_ref, r_ref):
    @pl.when(pl.program_id(0) == 0)
    def _():
        r_ref[...] = jnp.zeros_like(r_ref)

    for hh in range(RET_HEADS):
        q = q_ref[:, hh * RET_QK:(hh + 1) * RET_QK]
        k = k_ref[:, hh * RET_QK:(hh + 1) * RET_QK]
        v = v_ref[:, hh * RET_V:(hh + 1) * RET_V]
        state = r_ref[hh]
        scores = _dot_nt(q, k) * inner_ref[hh]
        o_ref[:, hh * RET_V:(hh + 1) * RET_V] = _dot(scores, v) + _dot(q, state) * qd_ref[hh]
        r_ref[hh] = state * bd_ref[hh] + _dot_tn(k * kd_ref[hh], v)


def _ret_out_kernel(of_ref, ob_ref, g_ref, gg_ref, gb_ref, w_ref, h_ref, mod_ref, lng_ref, lnb_ref, o_ref):
    parts = []
    for hh in range(RET_HEADS):
        sl = slice(hh * RET_V, (hh + 1) * RET_V)
        o = of_ref[:, sl] + ob_ref[:, sl]
        mu = jnp.mean(o, axis=-1, keepdims=True)
        oc = o - mu
        var = jnp.mean(oc * oc, axis=-1, keepdims=True)
        y = oc * lax.rsqrt(var + LN_EPS) * gg_ref[:, sl] + gb_ref[:, sl]
        parts.append((g_ref[:, sl] * y).astype(MXU_DT))
    yo = _dot(jnp.concatenate(parts, axis=1), w_ref[...])
    z = DEEPNORM_ALPHA * h_ref[...] + mod_ref[0][2:3] * yo
    o_ref[...] = _ln_rows(z, lng_ref[...], lnb_ref[...])


def _ret_mixer(h, mods, nct, rope_cos, rope_sin, w_in, decay_logit, gn_g, gn_b, w_out, ln_g, ln_b):
    t, d = h.shape
    nt = t // TM
    hv = RET_HEADS * RET_V
    q, k, v, sg = pl.pallas_call(
        _ret_in_kernel, grid=(nt,),
        in_specs=[_row_spec(d), _mod_spec(nct), _full_spec(w_in.shape), _row_spec(RET_QK), _row_spec(RET_QK)],
        out_specs=[_row_spec(d), _row_spec(d), _row_spec(hv), _row_spec(hv)],
        out_shape=[jax.ShapeDtypeStruct((t, w), F32) for w in (d, d, hv, hv)],
        compiler_params=_cparams("arbitrary"), name="ret_in",
    )(h, mods, w_in.astype(MXU_DT), rope_cos, rope_sin)
    c = RET_CHUNK
    ncc, nc = nct * (TM // c), t // c
    log_gamma = jax.nn.log_sigmoid(decay_logit.astype(F32))
    pos = jnp.arange(c, dtype=F32)
    o_dir = []
    for d_ in range(2):
        reverse = d_ == 1
        lg = log_gamma[d_][:, None, None]
        p = (c - 1.0 - pos) if reverse else pos
        rel = p[:, None] - p[None, :]
        inner = jnp.where(rel >= 0, jnp.exp(jnp.maximum(rel, 0.0) * lg), 0.0)
        q_dec = jnp.exp((p + 1.0) * log_gamma[d_][:, None])[:, :, None]
        k_dec = jnp.exp((c - 1.0 - p) * log_gamma[d_][:, None])[:, :, None]
        blk_dec = jnp.exp(c * log_gamma[d_])[:, None, None]
        cs = lambda w, reverse=reverse: pl.BlockSpec((c, w), lambda g_: (_tile_of(g_, ncc, nc, reverse), 0))
        o_dir.append(pl.pallas_call(
            _ret_scan_kernel, grid=(nc,),
            in_specs=[cs(d), cs(d), cs(hv), _full_spec(inner.shape), _full_spec(q_dec.shape), _full_spec(k_dec.shape),
                      _full_spec(blk_dec.shape)],
            out_specs=cs(hv), out_shape=jax.ShapeDtypeStruct((t, hv), F32),
            scratch_shapes=[pltpu.VMEM((RET_HEADS, RET_QK, RET_V), F32)],
            compiler_params=_cparams("arbitrary"), name="ret_scan_%d" % d_,
        )(q, k, v, inner, q_dec, k_dec, blk_dec))
    return pl.pallas_call(
        _ret_out_kernel, grid=(nt,),
        in_specs=[_row_spec(hv)] * 3 + [_full_spec((1, hv)), _full_spec((1, hv)), _full_spec((hv, d)), _row_spec(d),
                                        _mod_spec(nct), _full_spec((1, d)), _full_spec((1, d))],
        out_specs=_row_spec(d), out_shape=jax.ShapeDtypeStruct((t, d), F32),
        compiler_params=_cparams("arbitrary"), name="ret_out",
    )(o_dir[0], o_dir[1], sg, gn_g[None], gn_b[None], w_out.astype(MXU_DT), h, mods, ln_g[None], ln_b[None])


def _hgrn_in_kernel(h_ref, mod_ref, w_ref, lb_ref, bf_ref, q_o, v_o, g_o, f0_o, f1_o):
    d = D_MODEL
    u = _modulate(h_ref[...], mod_ref[0], 0).astype(MXU_DT)
    lb = lb_ref[...]
    q_o[...] = _silu(_dot(u, w_ref[:, 0:d]))
    f0_o[...] = lb + (1.0 - lb) * jax.nn.sigmoid(_dot(u, w_ref[:, d:2 * d]) + bf_ref[0:1, :])
    f1_o[...] = lb + (1.0 - lb) * jax.nn.sigmoid(_dot(u, w_ref[:, 2 * d:3 * d]) + bf_ref[1:2, :])
    v_o[...] = _dot(u, w_ref[:, 3 * d:4 * d])
    g_o[...] = _silu(_dot(u, w_ref[:, 4 * d:5 * d]))


def _hgrn_scan_kernel(q_ref, v_ref, f_ref, o_ref, s_ref, *, reverse):
    hb = HGRN_BLOCK
    nb = TM // hb

    @pl.when(pl.program_id(0) == 0)
    def _():
        s_ref[...] = jnp.zeros_like(s_ref)

    ri = lax.broadcasted_iota(jnp.int32, (hb, hb), 0)
    ci = lax.broadcasted_iota(jnp.int32, (hb, hb), 1)
    tri = jnp.where((ci >= ri) if reverse else (ci <= ri), 1.0, 0.0)
    t3 = lax.broadcasted_iota(jnp.int32, (hb, hb, 1), 0)
    s3 = lax.broadcasted_iota(jnp.int32, (hb, hb, 1), 1)
    causal3 = (s3 >= t3) if reverse else (s3 <= t3)

    def block(bi, carry):
        blk = (nb - 1 - bi) if reverse else bi
        r0 = pl.multiple_of(blk * hb, hb)
        f = f_ref[pl.ds(r0, hb), :]
        q = q_ref[pl.ds(r0, hb), :]
        v = v_ref[pl.ds(r0, hb), :]
        kx = 1.0 - f
        b = _dot_sel(tri, jnp.log(f), 3)
        tot = b[0:1, :] if reverse else b[hb - 1:hb, :]
        qe = q * jnp.exp(b)
        kb = kx * jnp.exp(tot - b)
        e_tot = jnp.exp(tot)
        for hh in range(HGRN_HEADS):
            sl = slice(hh * HGRN_HEAD, (hh + 1) * HGRN_HEAD)
            s = s_ref[hh]
            bh = b[:, sl]
            diff = bh[:, None, :] - bh[None, :, :]
            dec = jnp.exp(jnp.where(causal3, diff, -jnp.inf))
            e3 = q[:, sl][:, None, :] * kx[:, sl][None, :, :] * dec
            sc = jnp.sum(e3, axis=-1, keepdims=True)
            o_diag = jnp.sum(sc * v[:, sl][None, :, :], axis=1)
            o_ref[pl.ds(r0, hb), sl] = _dot_nt(qe[:, sl], s) + o_diag
            s_ref[hh] = s * e_tot[:, sl] + _dot_tn(v[:, sl], kb[:, sl])
        return carry

    lax.fori_loop(0, nb, block, 0)


def _hgrn_out_kernel(of_ref, ob_ref, g_ref, ng_ref, w_ref, h_ref, mod_ref, lng_ref, lnb_ref, o_ref):
    parts = []
    for hh in range(HGRN_HEADS):
        sl = slice(hh * HGRN_HEAD, (hh + 1) * HGRN_HEAD)
        o = of_ref[:, sl] + ob_ref[:, sl]
        y = o * lax.rsqrt(jnp.mean(o * o, axis=-1, keepdims=True) + LN_EPS) * ng_ref[...]
        parts.append((y * g_ref[:, sl]).astype(MXU_DT))
    yo = _dot(jnp.concatenate(parts, axis=1), w_ref[...])
    z = DEEPNORM_ALPHA * h_ref[...] + mod_ref[0][2:3] * yo
    o_ref[...] = _ln_rows(z, lng_ref[...], lnb_ref[...])


def _hgrn_mixer(h, mods, nct, lb, w_in, b_f, norm_g, w_out, ln_g, ln_b):
    t, d = h.shape
    nt = t // TM
    q, v, sg, f0, f1 = pl.pallas_call(
        _hgrn_in_kernel, grid=(nt,),
        in_specs=[_row_spec(d), _mod_spec(nct), _full_spec(w_in.shape), _full_spec((1, d)), _full_spec((2, d))],
        out_specs=[_row_spec(d)] * 5, out_shape=[jax.ShapeDtypeStruct((t, d), F32)] * 5,
        compiler_params=_cparams("arbitrary"), name="hgrn_in",
    )(h, mods, w_in.astype(MXU_DT), lb[None], b_f)
    o_dir = []
    for d_, f in enumerate((f0, f1)):
        reverse = d_ == 1
        spec = pl.BlockSpec((TM, d), lambda g_, reverse=reverse: (_tile_of(g_, nct, nt, reverse), 0))
        o_dir.append(pl.pallas_call(
            functools.partial(_hgrn_scan_kernel, reverse=reverse), grid=(nt,), in_specs=[spec] * 3, out_specs=spec,
            out_shape=jax.ShapeDtypeStruct((t, d), F32),
            scratch_shapes=[pltpu.VMEM((HGRN_HEADS, HGRN_HEAD, HGRN_HEAD), F32)],
            compiler_params=_cparams("arbitrary"), name="hgrn_scan_%d" % d_,
        )(q, v, f))
    return pl.pallas_call(
        _hgrn_out_kernel, grid=(nt,),
        in_specs=[_row_spec(d)] * 3 + [_full_spec((1, HGRN_HEAD)), _full_spec((d, d)), _row_spec(d), _mod_spec(nct),
                                       _full_spec((1, d)), _full_spec((1, d))],
        out_specs=_row_spec(d), out_shape=jax.ShapeDtypeStruct((t, d), F32),
        compiler_params=_cparams("arbitrary"), name="hgrn_out",
    )(o_dir[0], o_dir[1], sg, norm_g[None], w_out.astype(MXU_DT), h, mods, ln_g[None], ln_b[None])


def _router_kernel(h_ref, mod_ref, rw_ref, rb_ref, u_o, gate_o, rank_o, gate_t_o, rank_t_o, x_o):
    u = _modulate(h_ref[...], mod_ref[0], 3)
    u_o[...] = u.astype(u_o.dtype)
    w_hi, w_lo = _split(rw_ref[...], 2)
    u_hi, u_lo = _split(u, 2)
    nt_dims = (((1,), (1,)), ((), ()))
    logits = (lax.dot_general(w_hi, u_hi, nt_dims, preferred_element_type=F32)
              + lax.dot_general(w_hi, u_lo, nt_dims, preferred_element_type=F32)
              + lax.dot_general(w_lo, u_hi, nt_dims, preferred_element_type=F32))
    ne, gs = N_EXPERTS, N_EXPERTS // N_GROUPS
    neg = -jnp.inf
    scores = jax.nn.sigmoid(logits[:ne])
    choice = scores + rb_ref[:ne]
    c3 = choice.reshape(N_GROUPS, gs, TM)
    mi = lax.broadcasted_iota(jnp.int32, c3.shape, 1).astype(F32)
    m1 = jnp.max(c3, axis=1, keepdims=True)
    i1 = jnp.min(jnp.where(c3 == m1, mi, float(gs)), axis=1, keepdims=True)
    m2 = jnp.max(jnp.where(mi == i1, neg, c3), axis=1, keepdims=True)
    gscore = m1 + m2
    gi = lax.broadcasted_iota(jnp.int32, gscore.shape, 0).astype(F32)
    gsel = jnp.zeros(gscore.shape, F32)
    for _ in range(TOPK_GROUPS):
        gm = jnp.max(gscore, axis=0, keepdims=True)
        pick = gi == jnp.min(jnp.where(gscore == gm, gi, float(N_GROUPS)), axis=0, keepdims=True)
        gsel = jnp.where(pick, 1.0, gsel)
        gscore = jnp.where(pick, neg, gscore)
    emask = jnp.broadcast_to(gsel, c3.shape).reshape(ne, TM)
    masked = jnp.where(emask > 0.5, choice, neg)
    ei = lax.broadcasted_iota(jnp.int32, masked.shape, 0).astype(F32)
    chosen = jnp.zeros(masked.shape, F32)
    for _ in range(TOP_K):
        em = jnp.max(masked, axis=0, keepdims=True)
        pick = ei == jnp.min(jnp.where(masked == em, ei, float(ne)), axis=0, keepdims=True)
        chosen = jnp.where(pick, 1.0, chosen)
        masked = jnp.where(pick, neg, masked)
    top_w = scores * chosen
    gates = ROUTED_SCALE * top_w / jnp.sum(top_w, axis=0, keepdims=True)
    ti = lax.broadcasted_iota(jnp.int32, (TM, TM), 0)
    tj = lax.broadcasted_iota(jnp.int32, (TM, TM), 1)
    before = jnp.where(ti < tj, 1.0, 0.0).astype(MXU_DT)
    prefix = jnp.dot(chosen.astype(MXU_DT), before, preferred_element_type=F32)
    rank = jnp.where(chosen > 0.5, prefix, -1.0)
    gate_o[0] = gates
    rank_o[0] = rank
    pad = LANES - ne
    gate_t_o[...] = jnp.concatenate([gates, jnp.zeros((pad, TM), F32)], axis=0).T
    rank_t_o[...] = jnp.concatenate([rank, jnp.full((pad, TM), -1.0, F32)], axis=0).T
    cap = MOE_CAP
    slot = lax.broadcasted_iota(jnp.int32, (cap, TM), 0).astype(F32)
    ub = u.astype(MXU_DT)
    for g0 in range(0, ne, MOE_EGROUP):
        onehot = jnp.concatenate([jnp.where(slot == rank[e:e + 1, :], 1.0, 0.0).astype(MXU_DT)
                                  for e in range(g0, g0 + MOE_EGROUP)], axis=0)
        xg = jnp.dot(onehot, ub, preferred_element_type=F32)
        x_o[0, g0:g0 + MOE_EGROUP] = xg.reshape(MOE_EGROUP, cap, D_MODEL).astype(x_o.dtype)


def _expert_kernel(x_ref, wgu_ref, wd_ref, y_ref, wgu_s, wd_s):
    @pl.when(pl.program_id(1) == 0)
    def _():
        wgu_s[...] = wgu_ref[0, 0].astype(MXU_DT)
        wd_s[...] = wd_ref[0, 0].astype(MXU_DT)

    g = x_ref.shape[0]
    ed = EXPERT_DIM
    x = x_ref[...].reshape(g * MOE_CAP, D_MODEL)
    gu = _dot(x, wgu_s[...])
    y = _dot(_silu(gu[:, :ed]) * gu[:, ed:], wd_s[...])
    y_ref[...] = y.reshape(g, 1, MOE_CAP, D_MODEL).astype(y_ref.dtype)


def _combine_kernel(u_ref, gt_ref, rt_ref, y_ref, sgu_ref, sd_ref, h_ref, mod_ref, lng_ref, lnb_ref, *rest, extra):
    if extra:
        ex_ref, o_ref = rest
    else:
        (o_ref,) = rest
    ed = EXPERT_DIM
    cap = MOE_CAP
    per_vreg = LANES // cap
    gu = _dot(u_ref[...], sgu_ref[...])
    acc = _dot(_silu(gu[:, :ed]) * gu[:, ed:], sd_ref[...])
    if extra:
        acc = acc + ex_ref[...]
    lane = lax.broadcasted_iota(jnp.int32, (1, LANES), 1)
    which = lane // cap
    slot = (lane - which * cap).astype(F32)
    rt = rt_ref[...]
    gt = gt_ref[...]
    for g0 in range(0, N_EXPERTS, MOE_EGROUP):
        pieces = []
        for e0 in range(g0, g0 + MOE_EGROUP, per_vreg):
            rsel = rt[:, e0:e0 + 1]
            gsel = gt[:, e0:e0 + 1]
            for q in range(1, per_vreg):
                rsel = jnp.where(which == q, rt[:, e0 + q:e0 + q + 1], rsel)
                gsel = jnp.where(which == q, gt[:, e0 + q:e0 + q + 1], gsel)
            pieces.append(jnp.where(rsel == slot, gsel, 0.0).astype(MXU_DT))
        pw = jnp.concatenate(pieces, axis=1)
        yg = y_ref[0, g0:g0 + MOE_EGROUP].reshape(MOE_EGROUP * cap, D_MODEL)
        acc = acc + jnp.dot(pw, yg.astype(MXU_DT), preferred_element_type=F32)
    z = DEEPNORM_ALPHA * h_ref[...] + mod_ref[0][5:6] * acc
    o_ref[...] = _ln_rows(z, lng_ref[...], lnb_ref[...])


def _overflow_kernel(tile_ref, exp_ref, nr_ref, n_ref, u_ref, gate_ref, rank_ref, wgu_ref, wd_ref, zero_ref, o_ref):
    del zero_ref
    s = pl.program_id(0)
    tile = tile_ref[s]
    e = exp_ref[s]
    ed = EXPERT_DIM
    cap = MOE_CAP
    active = s < n_ref[0]
    first = jnp.logical_or(s == 0, tile_ref[jnp.maximum(s - 1, 0)] != tile)

    @pl.when(jnp.logical_and(active, first))
    def _():
        o_ref[...] = jnp.zeros_like(o_ref)

    @pl.when(active)
    def _():
        wgu = wgu_ref[0, 0].astype(MXU_DT)
        wd = wd_ref[0, 0].astype(MXU_DT)
        rank = rank_ref[0, pl.ds(e, 1), :]
        gate = gate_ref[0, pl.ds(e, 1), :]

        def one_round(r, carry):
            slot = lax.broadcasted_iota(jnp.int32, (cap, TM), 0).astype(F32) + (r * cap).astype(F32)
            hit = slot == rank
            x = _dot(jnp.where(hit, 1.0, 0.0), u_ref[...])
            gu = _dot(x, wgu)
            y = _dot(_silu(gu[:, :ed]) * gu[:, ed:], wd)
            o_ref[...] += _dot_tn(jnp.where(hit, gate, 0.0), y)
            return carry

        lax.fori_loop(1, nr_ref[s], one_round, 0)


def _moe_layer(h, mods, nct, layer, router_w, router_b, w_gu, w_down, sh_gu, sh_down, ln_g, ln_b):
    t, d = h.shape
    nt = t // TM
    ne, cap = N_EXPERTS, MOE_CAP
    rw = jnp.concatenate([router_w.T, jnp.zeros((LANES - ne, d), F32)], axis=0)
    rb = jnp.concatenate([router_b, jnp.zeros((LANES - ne,), F32)])[:, None]
    per_tile = pl.BlockSpec((1, ne, TM), lambda i: (i, 0, 0))
    slots = pl.BlockSpec((1, ne, cap, d), lambda i: (i, 0, 0, 0))
    u, gates, ranks, gates_t, ranks_t, xs = pl.pallas_call(
        _router_kernel, grid=(nt,),
        in_specs=[_row_spec(d), _mod_spec(nct), _full_spec((LANES, d)), _full_spec((LANES, 1))],
        out_specs=[_row_spec(d), per_tile, per_tile, _row_spec(LANES), _row_spec(LANES), slots],
        out_shape=[jax.ShapeDtypeStruct((t, d), MXU_DT), jax.ShapeDtypeStruct((nt, ne, TM), F32),
                   jax.ShapeDtypeStruct((nt, ne, TM), F32), jax.ShapeDtypeStruct((t, LANES), F32),
                   jax.ShapeDtypeStruct((t, LANES), F32), jax.ShapeDtypeStruct((nt, ne, cap, d), MXU_DT)],
        compiler_params=_cparams("arbitrary"), name="moe_router",
    )(h, mods, rw, rb)

    run = max(g for g in range(1, MOE_RUN + 1) if nt % g == 0)
    ys = pl.pallas_call(
        _expert_kernel, grid=(ne, nt // run),
        in_specs=[pl.BlockSpec((run, 1, cap, d), lambda e, c: (c, e, 0, 0)),
                  pl.BlockSpec((1, 1, d, 2 * EXPERT_DIM), lambda e, c: (layer, e, 0, 0)),
                  pl.BlockSpec((1, 1, EXPERT_DIM, d), lambda e, c: (layer, e, 0, 0))],
        out_specs=pl.BlockSpec((run, 1, cap, d), lambda e, c: (c, e, 0, 0)),
        out_shape=jax.ShapeDtypeStruct((nt, ne, cap, d), MXU_DT),
        scratch_shapes=[pltpu.VMEM((d, 2 * EXPERT_DIM), MXU_DT), pltpu.VMEM((EXPERT_DIM, d), MXU_DT)],
        compiler_params=_cparams("arbitrary", "arbitrary"), name="moe_experts",
    )(xs, w_gu, w_down)

    sgu, sd = sh_gu.astype(MXU_DT), sh_down.astype(MXU_DT)
    base_specs = [_row_spec(d), _row_spec(LANES), _row_spec(LANES), slots, _full_spec(sgu.shape), _full_spec(sd.shape),
                  _row_spec(d), _mod_spec(nct), _full_spec((1, d)), _full_spec((1, d))]
    base_args = (u, gates_t, ranks_t, ys, sgu, sd, h, mods, ln_g[None], ln_b[None])

    def combine(*extra):
        return pl.pallas_call(
            functools.partial(_combine_kernel, extra=bool(extra)), grid=(nt,),
            in_specs=base_specs + [_row_spec(d)] * len(extra), out_specs=_row_spec(d),
            out_shape=jax.ShapeDtypeStruct((t, d), F32), compiler_params=_cparams("arbitrary"), name="moe_combine",
        )(*base_args, *extra)

    count = (jnp.max(ranks, axis=-1).astype(jnp.int32) + 1).reshape(-1)
    over = count > cap
    n_over = jnp.sum(over.astype(jnp.int32))

    def with_overflow(size):
        def run():
            idx = jnp.nonzero(over, size=size, fill_value=0)[0].astype(jnp.int32)
            idx = jnp.where(jnp.arange(size) < n_over, idx, idx[jnp.maximum(n_over - 1, 0)])
            tiles, exps = idx // ne, idx % ne
            rounds = (count[idx] + cap - 1) // cap
            grid_spec = pltpu.PrefetchScalarGridSpec(
                num_scalar_prefetch=4, grid=(size,),
                in_specs=[pl.BlockSpec((TM, d), lambda s, tl, ex, nr, n: (tl[s], 0)),
                          pl.BlockSpec((1, ne, TM), lambda s, tl, ex, nr, n: (tl[s], 0, 0)),
                          pl.BlockSpec((1, ne, TM), lambda s, tl, ex, nr, n: (tl[s], 0, 0)),
                          pl.BlockSpec((1, 1, d, 2 * EXPERT_DIM), lambda s, tl, ex, nr, n: (layer, ex[s], 0, 0)),
                          pl.BlockSpec((1, 1, EXPERT_DIM, d), lambda s, tl, ex, nr, n: (layer, ex[s], 0, 0)),
                          pl.BlockSpec(memory_space=pl.ANY)],
                out_specs=pl.BlockSpec((TM, d), lambda s, tl, ex, nr, n: (tl[s], 0)))
            extra = pl.pallas_call(
                _overflow_kernel, grid_spec=grid_spec, out_shape=jax.ShapeDtypeStruct((t, d), F32),
                input_output_aliases={9: 0}, compiler_params=_cparams("arbitrary"), name="moe_overflow",
            )(tiles, exps, rounds, n_over[None], u, gates, ranks, w_gu, w_down, jnp.zeros((t, d), F32))
            return combine(extra)
        return run

    sizes = sorted({min(MOE_OVER_STEPS, nt * ne), nt * ne})
    branch = sum((n_over > sz).astype(jnp.int32) for sz in [0] + sizes[:-1])
    return lax.switch(branch, [combine] + [with_overflow(sz) for sz in sizes])


def kernel(x, c, ctx, c_ctx, ada_w, ada_b, post_ln_g, post_ln_b, lru_w_in, lru_conv_w, lru_conv_b, lru_gate_w, lru_gate_b, lru_lambda, lru_w_out, rwkv_mu, rwkv_w_in, rwkv_w0, rwkv_w_l1, rwkv_w_l2, rwkv_a0, rwkv_a_l1, rwkv_a_l2, rwkv_g_l1, rwkv_g_l2, rwkv_k_k, rwkv_k_a, rwkv_r_k, rwkv_ln_g, rwkv_ln_b, rwkv_w_out, ret_w_in, ret_decay, ret_gn_g, ret_gn_b, ret_w_out, hgrn_w_in, hgrn_b_f, hgrn_lb, hgrn_norm_g, hgrn_w_out, moe_router, moe_bias, moe_w_gu, moe_w_down, moe_sh_gu, moe_sh_down):
    assert x.shape[0] == 1 and ctx.shape[0] == 1
    n_ctx, n_lat, d = ctx.shape[1], x.shape[1], x.shape[2]
    assert n_ctx % TM == 0 and n_lat % TM == 0 and d == D_MODEL
    nct = n_ctx // TM
    rows = n_lat // GRID_W
    pos_row = jnp.repeat(jnp.arange(rows, dtype=F32), GRID_W)
    pos_col = jnp.tile(jnp.arange(GRID_W, dtype=F32), rows)
    n_freq = RET_QK // 4
    freqs = ROPE_BASE ** (-jnp.arange(n_freq, dtype=F32) / n_freq)
    ang = jnp.concatenate([pos_row[:, None] * freqs, pos_col[:, None] * freqs], axis=-1)
    ang = jnp.concatenate([ang, ang], axis=-1)
    rope_cos = jnp.concatenate([jnp.ones((n_ctx, RET_QK), F32), jnp.cos(ang)], axis=0)
    rope_sin = jnp.concatenate([jnp.zeros((n_ctx, RET_QK), F32), jnp.sin(ang)], axis=0)
    lb_cum = jnp.cumsum(jax.nn.softmax(hgrn_lb.astype(F32), axis=0), axis=0)

    cond = jnp.concatenate([c_ctx[None], c, jnp.zeros((6, d), F32)], axis=0)
    mods_all = _ada_mods(cond, ada_w, ada_b)
    h = jnp.concatenate([ctx[0], x[0]], axis=0)
    for i in range(DEPTH):
        kind, j = i % N_MIXERS, i // N_MIXERS
        mods = mods_all[i]
        lng, lnb = post_ln_g[i, 0], post_ln_b[i, 0]
        if kind == 0:
            h = _lru_mixer(h, mods, nct, lru_w_in[j], lru_conv_w[j], lru_conv_b[j], lru_gate_w[j], lru_gate_b[j],
                           lru_lambda[j], lru_w_out[j], lng, lnb)
        elif kind == 1:
            h = _rwkv_mixer(h, mods, nct, rwkv_mu[j], rwkv_w_in[j], rwkv_w0[j], rwkv_w_l1[j], rwkv_w_l2[j], rwkv_a0[j],
                            rwkv_a_l1[j], rwkv_a_l2[j], rwkv_g_l1[j], rwkv_g_l2[j], rwkv_k_k[j], rwkv_k_a[j],
                            rwkv_r_k[j], rwkv_ln_g[j], rwkv_ln_b[j], rwkv_w_out[j], lng, lnb)
        elif kind == 2:
            h = _ret_mixer(h, mods, nct, rope_cos, rope_sin, ret_w_in[j], ret_decay[j], ret_gn_g[j], ret_gn_b[j],
                           ret_w_out[j], lng, lnb)
        else:
            h = _hgrn_mixer(h, mods, nct, lb_cum[i] - lb_cum[0], hgrn_w_in[j], hgrn_b_f[j], hgrn_norm_g[j],
                            hgrn_w_out[j], lng, lnb)
        h = _moe_layer(h, mods, nct, i, moe_router[i], moe_bias[i], moe_w_gu, moe_w_down, moe_sh_gu[i],
                       moe_sh_down[i], post_ln_g[i, 1], post_ln_b[i, 1])
    return h[n_ctx:][None]
```

```python
import math
import functools
import jax
import jax.numpy as jnp
from jax import lax
from jax.experimental import pallas as pl
from jax.experimental.pallas import tpu as pltpu

F32 = jnp.float32
MXU_DT = jnp.bfloat16
LANES = 128
TM = 256
VMEM_LIMIT = 56 * 2 ** 20

D_MODEL = 1024
DEPTH = 4
GRID_W = 64
N_MIXERS = 4
DEEPNORM_ALPHA = (2.0 * DEPTH) ** 0.25
LN_EPS = 1e-5
LRU_WIDTH = D_MODEL
LRU_BLOCKS = 16
LRU_BLOCK = LRU_WIDTH // LRU_BLOCKS
LRU_C = 8.0
RWKV_HEAD = 64
RWKV_HEADS = D_MODEL // RWKV_HEAD
RWKV_DECAY_SCALE = math.exp(-0.5)
RWKV_GN_EPS = 64e-5
RWKV_CHUNK = 64
RET_HEADS = 4
RET_QK = D_MODEL // RET_HEADS
RET_V = 2 * RET_QK
RET_CHUNK = 128
ROPE_BASE = 10000.0
HGRN_HEADS = 8
HGRN_HEAD = D_MODEL // HGRN_HEADS
HGRN_BLOCK = 16
N_EXPERTS = 64
TOP_K = 8
N_GROUPS = 8
TOPK_GROUPS = 4
EXPERT_DIM = 256
ROUTED_SCALE = 2.5
MOE_OVER_STEPS = 256
MOE_CAP = 64
MOE_EGROUP = 8
MOE_RUN = 13


def _cparams(*sem):
    return pltpu.CompilerParams(dimension_semantics=sem, vmem_limit_bytes=VMEM_LIMIT)


def _dot(a, b):
    return jnp.dot(a.astype(MXU_DT), b.astype(MXU_DT), preferred_element_type=F32)


def _dot_nt(a, b):
    return lax.dot_general(a.astype(MXU_DT), b.astype(MXU_DT), (((1,), (1,)), ((), ())), preferred_element_type=F32)


def _dot_tn(a, b):
    return lax.dot_general(a.astype(MXU_DT), b.astype(MXU_DT), (((0,), (0,)), ((), ())), preferred_element_type=F32)


def _split(x, n):
    parts = []
    for _ in range(n):
        p = x.astype(MXU_DT)
        parts.append(p)
        x = x - p.astype(F32)
    return parts


def _dot_sel(sel, x, n):
    return sum(jnp.dot(sel.astype(MXU_DT), p, preferred_element_type=F32) for p in _split(x, n))


def _dot_xsel(x, sel, n):
    return sum(jnp.dot(p, sel.astype(MXU_DT), preferred_element_type=F32) for p in _split(x, n))


def _modulate(h, m, shift_idx):
    return h * (1.0 + m[shift_idx + 1:shift_idx + 2]) + m[shift_idx:shift_idx + 1]


def _ln_rows(z, g, b):
    mu = jnp.mean(z, axis=-1, keepdims=True)
    zc = z - mu
    var = jnp.mean(zc * zc, axis=-1, keepdims=True)
    return zc * lax.rsqrt(var + LN_EPS) * g + b


def _silu(x):
    return x * jax.nn.sigmoid(x)


def _shift_down(x, first_row):
    rows = lax.broadcasted_iota(jnp.int32, (x.shape[0], 1), 0)
    return jnp.where(rows == 0, first_row, pltpu.roll(x, 1, 0))


def _shift_up(x, last_row):
    n = x.shape[0]
    rows = lax.broadcasted_iota(jnp.int32, (n, 1), 0)
    return jnp.where(rows == n - 1, last_row, pltpu.roll(x, n - 1, 0))


def _tile_of(g, nct, nt, reverse):
    if not reverse:
        return g
    return jnp.where(g < nct, nct - 1 - g, nt - 1 - (g - nct))


def _halo_flags(t, nct, nt):
    prev_ok = jnp.logical_and(t != 0, t != nct).astype(F32)
    next_ok = jnp.logical_and(t != nct - 1, t != nt - 1).astype(F32)
    return prev_ok, next_ok


def _ada_kernel(s_ref, w_ref, b_ref, o_ref):
    o_ref[0] = _dot(_silu(s_ref[...]), w_ref[0]) + b_ref[0]


def _ada_mods(cond, ada_w, ada_b):
    nl, d, n6 = ada_w.shape
    out = pl.pallas_call(
        _ada_kernel, grid=(nl, n6 // d),
        in_specs=[pl.BlockSpec((8, d), lambda l, j: (0, 0)),
                  pl.BlockSpec((1, d, d), lambda l, j: (l, 0, j)),
                  pl.BlockSpec((1, 1, d), lambda l, j: (l, 0, j))],
        out_specs=pl.BlockSpec((1, 8, d), lambda l, j: (l, 0, j)),
        out_shape=jax.ShapeDtypeStruct((nl, 8, n6), F32),
        compiler_params=_cparams("arbitrary", "arbitrary"), name="ada_mods",
    )(cond, ada_w, ada_b.reshape(nl, 1, n6))
    return out[:, :2].reshape(nl, 2, 6, d)


def _row_spec(width, tm=TM):
    return pl.BlockSpec((tm, width), lambda i: (i, 0))


def _full_spec(shape):
    nd = len(shape)
    return pl.BlockSpec(tuple(shape), lambda *_: (0,) * nd)


def _mod_spec(nct):
    return pl.BlockSpec((1, 6, D_MODEL), lambda i: (jnp.minimum(i // nct, 1), 0, 0))


def _out_ln_kernel(p_ref, w_ref, h_ref, mod_ref, lng_ref, lnb_ref, o_ref):
    y = _dot(p_ref[...], w_ref[...])
    z = DEEPNORM_ALPHA * h_ref[...] + mod_ref[0][2:3] * y
    o_ref[...] = _ln_rows(z, lng_ref[...], lnb_ref[...])


def _out_ln(p, w_out, h, mods, ln_g, ln_b, nct):
    t, din = p.shape
    d = D_MODEL
    return pl.pallas_call(
        _out_ln_kernel, grid=(t // TM,),
        in_specs=[_row_spec(din), _full_spec((din, d)), _row_spec(d), _mod_spec(nct),
                  _full_spec((1, d)), _full_spec((1, d))],
        out_specs=_row_spec(d), out_shape=jax.ShapeDtypeStruct((t, d), F32),
        compiler_params=_cparams("arbitrary"), name="out_ln",
    )(p, w_out.astype(MXU_DT), h, mods, ln_g[None], ln_b[None])


def _lru_in_kernel(h_ref, mod_ref, w_ref, g_ref, x_ref):
    u = _modulate(h_ref[...], mod_ref[0], 0)
    z = _dot(u, w_ref[...])
    g_ref[...] = jax.nn.gelu(z[:, :LRU_WIDTH], approximate=True)
    x_ref[...] = z[:, LRU_WIDTH:]


def _lru_scan_kernel(x_ref, xp_ref, xn_ref, cw_ref, cb_ref, gw_ref, gb_ref, lam_ref, *rest, nct, nt, reverse, final):
    if final:
        hf_ref, g_ref, o_ref, a_s, b_s, h_s, st_s = rest
    else:
        o_ref, a_s, b_s, h_s, st_s = rest
    g = pl.program_id(0)
    t = _tile_of(g, nct, nt, reverse)
    prev_ok, next_ok = _halo_flags(t, nct, nt)

    @pl.when(g == 0)
    def _():
        st_s[...] = jnp.zeros_like(st_s)

    x = x_ref[...]
    xm1 = _shift_down(x, xp_ref[7:8, :] * prev_ok)
    n0 = xn_ref[0:1, :] * next_ok
    n1 = xn_ref[1:2, :] * next_ok
    xp1 = _shift_up(x, n0)
    xp2 = _shift_up(xp1, n1)
    cw = cw_ref[...]
    xc = cw[0:1] * xm1 + cw[1:2] * x + cw[2:3] * xp1 + cw[3:4] * xp2 + cb_ref[...]
    gates = jax.nn.sigmoid(_dot(xc, gw_ref[...]) + gb_ref[...])
    lam = lam_ref[...]
    softplus = jnp.maximum(-lam, 0.0) + jnp.log(1.0 + jnp.exp(-jnp.abs(lam)))
    log_a = -LRU_C * gates[:, :LRU_WIDTH] * softplus
    a_s[...] = jnp.exp(log_a)
    b_s[...] = jnp.sqrt(1.0 - jnp.exp(2.0 * log_a)) * (gates[:, LRU_WIDTH:] * xc)

    def row(r, hcur):
        tt = (TM - 1 - r) if reverse else r
        hcur = a_s[pl.ds(tt, 1), :] * hcur + b_s[pl.ds(tt, 1), :]
        h_s[pl.ds(tt, 1), :] = hcur
        return hcur

    st_s[...] = lax.fori_loop(0, TM, row, st_s[...], unroll=8)
    if final:
        o_ref[...] = g_ref[...] * (hf_ref[...] + h_s[...])
    else:
        o_ref[...] = h_s[...]


def _lru_mixer(h, mods, nct, w_in, conv_w, conv_b, gate_w, gate_b, lam, w_out, ln_g, ln_b):
    t, d = h.shape
    nt = t // TM
    w = LRU_WIDTH
    gelu, rnn = pl.pallas_call(
        _lru_in_kernel, grid=(nt,),
        in_specs=[_row_spec(d), _mod_spec(nct), _full_spec((d, 2 * w))],
        out_specs=[_row_spec(w), _row_spec(w)],
        out_shape=[jax.ShapeDtypeStruct((t, w), F32)] * 2,
        compiler_params=_cparams("arbitrary"), name="lru_in",
    )(h, mods, w_in.astype(MXU_DT))
    eye = jnp.eye(LRU_BLOCKS, dtype=F32)
    gw = jnp.einsum('dgnij,nm->dgnimj', gate_w, eye).reshape(2, 2, w, w)
    gw = jnp.concatenate([gw[:, 0], gw[:, 1]], axis=-1).astype(MXU_DT)
    gb = gate_b.reshape(2, 1, 2 * w)
    hf = None
    for d_ in range(2):
        reverse = d_ == 1
        final = d_ == 1
        tile = lambda g: _tile_of(g, nct, nt, reverse)
        ins = [pl.BlockSpec((TM, w), lambda g: (tile(g), 0)),
               pl.BlockSpec((8, w), lambda g: (jnp.maximum(tile(g) * (TM // 8) - 1, 0), 0)),
               pl.BlockSpec((8, w), lambda g: (jnp.minimum((tile(g) + 1) * (TM // 8), t // 8 - 1), 0)),
               _full_spec((4, w)), _full_spec((1, w)), _full_spec((w, 2 * w)), _full_spec((1, 2 * w)),
               _full_spec((1, w))]
        args = [rnn, rnn, rnn, conv_w, conv_b[None], gw[d_], gb[d_], lam[d_][None]]
        if final:
            ins += [pl.BlockSpec((TM, w), lambda g: (tile(g), 0))] * 2
            args += [hf, gelu]
        out = pl.pallas_call(
            functools.partial(_lru_scan_kernel, nct=nct, nt=nt, reverse=reverse, final=final),
            grid=(nt,), in_specs=ins,
            out_specs=pl.BlockSpec((TM, w), lambda g: (tile(g), 0)),
            out_shape=jax.ShapeDtypeStruct((t, w), F32),
            scratch_shapes=[pltpu.VMEM((TM, w), F32)] * 3 + [pltpu.VMEM((1, w), F32)],
            compiler_params=_cparams("arbitrary"), name="lru_scan_%d" % d_,
        )(*args)
        hf = out
    return _out_ln(hf, w_out, h, mods, ln_g, ln_b, nct)


def _seg_sum(x, e_ref, et_ref):
    s = _dot_xsel(x, e_ref[...], 2)
    return _dot_xsel(s, et_ref[...], 2)


def _rwkv_prep_kernel(h_ref, hp_ref, hn_ref, mod_ref, mu_ref, win_ref, wl1_ref, wl2_ref, w0_ref, al1_ref, al2_ref,
                      a0_ref, gl1_ref, gl2_ref, kk_ref, ka_ref, rk_ref, e_ref, et_ref,
                      r_o, v_o, kk_o, g_o, bv_o, lw0_o, lw1_o, kt0_o, kt1_o, ab0_o, ab1_o, *, nct, nt):
    i = pl.program_id(0)
    prev_ok, next_ok = _halo_flags(i, nct, nt)
    m = mod_ref[0]
    u = _modulate(h_ref[...], m, 0)
    up = _modulate(hp_ref[7:8, :], m, 0) * prev_ok
    un = _modulate(hn_ref[0:1, :], m, 0) * next_ok
    lane = lax.broadcasted_iota(jnp.int32, (1, D_MODEL), 1)
    sh = jnp.where(lane < D_MODEL // 2, _shift_down(u, up), _shift_up(u, un))
    dx = sh - u
    mu = mu_ref[...]
    xm = [u + dx * mu[c:c + 1] for c in range(6)]
    r = _dot(xm[0], win_ref[0])
    k = _dot(xm[1], win_ref[1])
    v = _dot(xm[2], win_ref[2])
    t1 = jnp.tanh(_dot(xm[3], wl1_ref[...]))
    t2 = _dot(xm[4], al1_ref[...])
    g = _dot(jax.nn.sigmoid(_dot(xm[5], gl1_ref[...])), gl2_ref[...])
    kk = k * kk_ref[...]
    kk = kk * lax.rsqrt(_seg_sum(kk * kk, e_ref, et_ref) + 1e-12)
    ktsum = None
    for z, (lw_o, kt_o, ab_o) in enumerate(((lw0_o, kt0_o, ab0_o), (lw1_o, kt1_o, ab1_o))):
        d_w = w0_ref[z:z + 1, :] + _dot(t1, wl2_ref[z])
        lw_o[...] = -RWKV_DECAY_SCALE * jax.nn.sigmoid(d_w)
        a = jax.nn.sigmoid(a0_ref[z:z + 1, :] + _dot(t2, al2_ref[z]))
        kt = k * (1.0 + (a - 1.0) * ka_ref[...])
        kt_o[...] = kt
        ab_o[...] = kk * a
        ktsum = kt if ktsum is None else ktsum + kt
    r_o[...] = r
    v_o[...] = v
    kk_o[...] = kk
    g_o[...] = g
    bv_o[...] = _seg_sum(r * ktsum * rk_ref[...], e_ref, et_ref) * v


def _rwkv_scan_kernel(r_ref, v_ref, kk_ref, lw_ref, kt_ref, ab_ref, o_ref, s_ref, *, reverse):
    c = RWKV_CHUNK

    @pl.when(pl.program_id(0) == 0)
    def _():
        s_ref[...] = jnp.zeros_like(s_ref)

    ri = lax.broadcasted_iota(jnp.int32, (c, c), 0)
    ci = lax.broadcasted_iota(jnp.int32, (c, c), 1)
    incl = (ci >= ri) if reverse else (ci <= ri)
    ri2 = lax.broadcasted_iota(jnp.int32, (c, 2 * c), 0)
    ci2 = jnp.bitwise_and(lax.broadcasted_iota(jnp.int32, (c, 2 * c), 1), c - 1)
    incl2 = (ci2 >= ri2) if reverse else (ci2 <= ri2)
    strict2 = (ci2 > ri2) if reverse else (ci2 < ri2)
    lw = lw_ref[...]
    cl = _dot_sel(jnp.where(incl, 1.0, 0.0), lw, 3)
    tot = cl[0:1, :] if reverse else cl[c - 1:c, :]
    e_in = jnp.exp(cl)
    e_out = jnp.exp(-cl)
    e_end = jnp.exp(tot - cl)
    kk = kk_ref[...]
    kt = kt_ref[...]
    ab = ab_ref[...]
    kap = kk * jnp.exp(cl - lw)
    rh = r_ref[...] * e_in
    kh = kt * e_out
    bh = ab * e_out
    kb = kt * e_end
    bb = ab * e_end
    e_tot = jnp.exp(tot)
    vv = v_ref[...]
    lane_a = lax.broadcasted_iota(jnp.int32, (1, LANES), 1) < RWKV_HEAD
    bi = lax.broadcasted_iota(jnp.int32, (LANES, LANES), 0) < RWKV_HEAD
    bj = lax.broadcasted_iota(jnp.int32, (LANES, LANES), 1) < RWKV_HEAD
    blockdiag = bi == bj

    def stack2(x):
        return jnp.concatenate([jnp.where(lane_a, x, 0.0), jnp.where(lane_a, 0.0, x)], axis=0)

    pairs = range(D_MODEL // LANES)
    sls = [slice(p * LANES, (p + 1) * LANES) for p in pairs]
    s = [s_ref[p] for p in pairs]
    xq = [jnp.concatenate([kap[:, sl], rh[:, sl]], axis=0) for sl in sls]
    yk = [jnp.concatenate([stack2(kh[:, sl]), stack2(bh[:, sl])], axis=0) for sl in sls]
    gm = [_dot_nt(xq[p], yk[p]) for p in pairs]
    xs = [_dot_nt(xq[p], s[p]) for p in pairs]
    l_kk = [jnp.where(strict2, g[:c, :2 * c], 0.0) for g in gm]
    l_bk = [jnp.where(strict2, g[:c, 2 * c:], 0.0) for g in gm]
    a_rk = [jnp.where(incl2, g[c:, :2 * c], 0.0) for g in gm]
    a_rb = [jnp.where(incl2, g[c:, 2 * c:], 0.0) for g in gm]
    v2 = [stack2(vv[:, sl]) for sl in sls]
    x = [xs[p][:c] + _dot(l_kk[p], v2[p]) for p in pairs]
    lp = [_dot(l_bk[p], stack2(l_bk[p])) for p in pairs]
    x = [x[p] - _dot(l_bk[p], stack2(x[p])) for p in pairs]
    for it in range(5):
        x = [x[p] + _dot(lp[p], stack2(x[p])) for p in pairs]
        if it < 4:
            lp = [_dot(lp[p], stack2(lp[p])) for p in pairs]
    o = [xs[p][c:] + _dot(jnp.concatenate([a_rk[p], -a_rb[p]], axis=1), jnp.concatenate([v2[p], stack2(x[p])], axis=0))
         for p in pairs]
    upd = [_dot_tn(jnp.concatenate([vv[:, sls[p]], -x[p]], axis=0),
                   jnp.concatenate([kb[:, sls[p]], bb[:, sls[p]]], axis=0)) for p in pairs]
    for p in pairs:
        o_ref[:, sls[p]] = o[p]
        s_ref[p] = s[p] * e_tot[:, sls[p]] + jnp.where(blockdiag, upd[p], 0.0)


def _rwkv_out_kernel(of_ref, ob_ref, bv_ref, g_ref, lg_ref, lb_ref, e_ref, et_ref, w_ref, h_ref, mod_ref, lng_ref,
                     lnb_ref, o_ref):
    o = of_ref[...] + ob_ref[...]
    inv = 1.0 / RWKV_HEAD
    oc = o - _seg_sum(o, e_ref, et_ref) * inv
    var = _seg_sum(oc * oc, e_ref, et_ref) * inv
    y = oc * lax.rsqrt(var + RWKV_GN_EPS) * lg_ref[...] + lb_ref[...] + bv_ref[...]
    yo = _dot(y * g_ref[...], w_ref[...])
    z = DEEPNORM_ALPHA * h_ref[...] + mod_ref[0][2:3] * yo
    o_ref[...] = _ln_rows(z, lng_ref[...], lnb_ref[...])


def _rwkv_mixer(h, mods, nct, mu, w_in, w0, w_l1, w_l2, a0, a_l1, a_l2, g_l1, g_l2, k_k, k_a, r_k, gn_g, gn_b, w_out,
                ln_g, ln_b):
    t, d = h.shape
    nt = t // TM
    bf = MXU_DT
    lw_ = w_l1.shape[-1]
    la_ = a_l1.shape[-1]
    zw = jnp.zeros((lw_, d), F32)
    za = jnp.zeros((la_, d), F32)
    wl1 = jnp.concatenate([w_l1[0], w_l1[1]], axis=1).astype(bf)
    wl2 = jnp.stack([jnp.concatenate([w_l2[0], zw], 0), jnp.concatenate([zw, w_l2[1]], 0)]).astype(bf)
    al1 = jnp.concatenate([a_l1[0], a_l1[1]], axis=1).astype(bf)
    al2 = jnp.stack([jnp.concatenate([a_l2[0], za], 0), jnp.concatenate([za, a_l2[1]], 0)]).astype(bf)
    head_of = jnp.arange(d) // RWKV_HEAD
    e = (head_of[:, None] == jnp.arange(LANES)[None, :]).astype(bf)
    et = e.T
    halo_p = pl.BlockSpec((8, d), lambda i: (jnp.maximum(i * (TM // 8) - 1, 0), 0))
    halo_n = pl.BlockSpec((8, d), lambda i: (jnp.minimum((i + 1) * (TM // 8), t // 8 - 1), 0))
    args = [h, h, h, mods, mu, w_in.astype(bf), wl1, wl2, w0, al1, al2, a0, g_l1.astype(bf), g_l2.astype(bf),
            k_k[None], k_a[None], r_k.reshape(1, d), e, et]
    ins = [_row_spec(d), halo_p, halo_n, _mod_spec(nct)] + [_full_spec(a.shape) for a in args[4:]]
    outs = pl.pallas_call(
        functools.partial(_rwkv_prep_kernel, nct=nct, nt=nt), grid=(nt,), in_specs=ins,
        out_specs=[_row_spec(d)] * 11, out_shape=[jax.ShapeDtypeStruct((t, d), F32)] * 11,
        compiler_params=_cparams("arbitrary"), name="rwkv_prep",
    )(*args)
    r, v, kk, g, bv, lw0, lw1, kt0, kt1, ab0, ab1 = outs
    c = RWKV_CHUNK
    ncc, nc = nct * (TM // c), t // c
    o_dir = []
    for d_, (lw, kt, ab) in enumerate(((lw0, kt0, ab0), (lw1, kt1, ab1))):
        reverse = d_ == 1
        spec = pl.BlockSpec((c, d), lambda g_, reverse=reverse: (_tile_of(g_, ncc, nc, reverse), 0))
        o_dir.append(pl.pallas_call(
            functools.partial(_rwkv_scan_kernel, reverse=reverse), grid=(nc,), in_specs=[spec] * 6, out_specs=spec,
            out_shape=jax.ShapeDtypeStruct((t, d), F32),
            scratch_shapes=[pltpu.VMEM((d // LANES, LANES, LANES), F32)],
            compiler_params=_cparams("arbitrary"), name="rwkv_scan_%d" % d_,
        )(r, v, kk, lw, kt, ab))
    args = [o_dir[0], o_dir[1], bv, g, gn_g[None], gn_b[None], e, et, w_out.astype(bf), h, mods, ln_g[None], ln_b[None]]
    ins = [_row_spec(d)] * 4 + [_full_spec(a.shape) for a in args[4:9]] + [_row_spec(d), _mod_spec(nct),
                                                                          _full_spec((1, d)), _full_spec((1, d))]
    return pl.pallas_call(
        _rwkv_out_kernel, grid=(nt,), in_specs=ins, out_specs=_row_spec(d),
        out_shape=jax.ShapeDtypeStruct((t, d), F32), compiler_params=_cparams("arbitrary"), name="rwkv_out",
    )(*args)


def _ret_in_kernel(h_ref, mod_ref, w_ref, cos_ref, sin_ref, q_o, k_o, v_o, g_o):
    d = D_MODEL
    u = _modulate(h_ref[...], mod_ref[0], 0).astype(MXU_DT)
    q = _dot(u, w_ref[:, 0:d])
    k = _dot(u, w_ref[:, d:2 * d]) * (RET_QK ** -0.5)
    v_o[...] = _dot(u, w_ref[:, 2 * d:4 * d])
    g_o[...] = _silu(_dot(u, w_ref[:, 4 * d:6 * d]))
    cos = cos_ref[...]
    sin = sin_ref[...]
    half = RET_QK // 2
    for z, z_o in ((q, q_o), (k, k_o)):
        for hh in range(RET_HEADS):
            lo = z[:, hh * RET_QK:hh * RET_QK + half]
            hi = z[:, hh * RET_QK + half:(hh + 1) * RET_QK]
            zh = jnp.concatenate([lo, hi], axis=1)
            rot = jnp.concatenate([-hi, lo], axis=1)
            z_o[:, hh * RET_QK:(hh + 1) * RET_QK] = zh * cos + rot * sin


def _ret_scan_kernel(q_ref, k_ref, v_ref, inner_ref, qd_ref, kd_ref, bd_ref, o_ref, r_ref):
    @pl.when(pl.program_id(0) == 0)
    def _():
        r_ref[...] = jnp.zeros_like(r_ref)

    for hh in range(RET_HEADS):
        q = q_ref[:, hh * RET_QK:(hh + 1) * RET_QK]
        k = k_ref[:, hh * RET_QK:(hh + 1) * RET_QK]
        v = v_ref[:, hh * RET_V:(hh + 1) * RET_V]
        state = r_ref[hh]
        scores = _dot_nt(q, k) * inner_ref[hh]
        o_ref[:, hh * RET_V:(hh + 1) * RET_V] = _dot(scores, v) + _dot(q, state) * qd_ref[hh]
        r_ref[hh] = state * bd_ref[hh] + _dot_tn(k * kd_ref[hh], v)


def _ret_out_kernel(of_ref, ob_ref, g_ref, gg_ref, gb_ref, w_ref, h_ref, mod_ref, lng_ref, lnb_ref, o_ref):
    parts = []
    for hh in range(RET_HEADS):
        sl = slice(hh * RET_V, (hh + 1) * RET_V)
        o = of_ref[:, sl] + ob_ref[:, sl]
        mu = jnp.mean(o, axis=-1, keepdims=True)
        oc = o - mu
        var = jnp.mean(oc * oc, axis=-1, keepdims=True)
        y = oc * lax.rsqrt(var + LN_EPS) * gg_ref[:, sl] + gb_ref[:, sl]
        parts.append((g_ref[:, sl] * y).astype(MXU_DT))
    yo = _dot(jnp.concatenate(parts, axis=1), w_ref[...])
    z = DEEPNORM_ALPHA * h_ref[...] + mod_ref[0][2:3] * yo
    o_ref[...] = _ln_rows(z, lng_ref[...], lnb_ref[...])


def _ret_mixer(h, mods, nct, rope_cos, rope_sin, w_in, decay_logit, gn_g, gn_b, w_out, ln_g, ln_b):
    t, d = h.shape
    nt = t // TM
    hv = RET_HEADS * RET_V
    q, k, v, sg = pl.pallas_call(
        _ret_in_kernel, grid=(nt,),
        in_specs=[_row_spec(d), _mod_spec(nct), _full_spec(w_in.shape), _row_spec(RET_QK), _row_spec(RET_QK)],
        out_specs=[_row_spec(d), _row_spec(d), _row_spec(hv), _row_spec(hv)],
        out_shape=[jax.ShapeDtypeStruct((t, w), F32) for w in (d, d, hv, hv)],
        compiler_params=_cparams("arbitrary"), name="ret_in",
    )(h, mods, w_in.astype(MXU_DT), rope_cos, rope_sin)
    c = RET_CHUNK
    ncc, nc = nct * (TM // c), t // c
    log_gamma = jax.nn.log_sigmoid(decay_logit.astype(F32))
    pos = jnp.arange(c, dtype=F32)
    o_dir = []
    for d_ in range(2):
        reverse = d_ == 1
        lg = log_gamma[d_][:, None, None]
        p = (c - 1.0 - pos) if reverse else pos
        rel = p[:, None] - p[None, :]
        inner = jnp.where(rel >= 0, jnp.exp(jnp.maximum(rel, 0.0) * lg), 0.0)
        q_dec = jnp.exp((p + 1.0) * log_gamma[d_][:, None])[:, :, None]
        k_dec = jnp.exp((c - 1.0 - p) * log_gamma[d_][:, None])[:, :, None]
        blk_dec = jnp.exp(c * log_gamma[d_])[:, None, None]
        cs = lambda w, reverse=reverse: pl.BlockSpec((c, w), lambda g_: (_tile_of(g_, ncc, nc, reverse), 0))
        o_dir.append(pl.pallas_call(
            _ret_scan_kernel, grid=(nc,),
            in_specs=[cs(d), cs(d), cs(hv), _full_spec(inner.shape), _full_spec(q_dec.shape), _full_spec(k_dec.shape),
                      _full_spec(blk_dec.shape)],
            out_specs=cs(hv), out_shape=jax.ShapeDtypeStruct((t, hv), F32),
            scratch_shapes=[pltpu.VMEM((RET_HEADS, RET_QK, RET_V), F32)],
            compiler_params=_cparams("arbitrary"), name="ret_scan_%d" % d_,
        )(q, k, v, inner, q_dec, k_dec, blk_dec))
    return pl.pallas_call(
        _ret_out_kernel, grid=(nt,),
        in_specs=[_row_spec(hv)] * 3 + [_full_spec((1, hv)), _full_spec((1, hv)), _full_spec((hv, d)), _row_spec(d),
                                        _mod_spec(nct), _full_spec((1, d)), _full_spec((1, d))],
        out_specs=_row_spec(d), out_shape=jax.ShapeDtypeStruct((t, d), F32),
        compiler_params=_cparams("arbitrary"), name="ret_out",
    )(o_dir[0], o_dir[1], sg, gn_g[None], gn_b[None], w_out.astype(MXU_DT), h, mods, ln_g[None], ln_b[None])


def _hgrn_in_kernel(h_ref, mod_ref, w_ref, lb_ref, bf_ref, q_o, v_o, g_o, f0_o, f1_o):
    d = D_MODEL
    u = _modulate(h_ref[...], mod_ref[0], 0).astype(MXU_DT)
    lb = lb_ref[...]
    q_o[...] = _silu(_dot(u, w_ref[:, 0:d]))
    f0_o[...] = lb + (1.0 - lb) * jax.nn.sigmoid(_dot(u, w_ref[:, d:2 * d]) + bf_ref[0:1, :])
    f1_o[...] = lb + (1.0 - lb) * jax.nn.sigmoid(_dot(u, w_ref[:, 2 * d:3 * d]) + bf_ref[1:2, :])
    v_o[...] = _dot(u, w_ref[:, 3 * d:4 * d])
    g_o[...] = _silu(_dot(u, w_ref[:, 4 * d:5 * d]))


def _hgrn_scan_kernel(qf_ref, vf_ref, ff_ref, qb_ref, vb_ref, fb_ref, of_ref, ob_ref, s_ref, b_s, rb_s, rk_s, rv_s):
    hb = HGRN_BLOCK
    nb = TM // hb
    half = hb // 2
    dirs = ((qf_ref, vf_ref, ff_ref, of_ref, False), (qb_ref, vb_ref, fb_ref, ob_ref, True))

    @pl.when(pl.program_id(0) == 0)
    def _():
        s_ref[...] = jnp.zeros_like(s_ref)

    ri = lax.broadcasted_iota(jnp.int32, (TM, TM), 0)
    ci = lax.broadcasted_iota(jnp.int32, (TM, TM), 1)
    same_block = (ri // hb) == (ci // hb)
    ti = lax.broadcasted_iota(jnp.int32, (half, 1), 0)
    rowi = lax.broadcasted_iota(jnp.int32, (hb, 1), 0)
    heads = range(HGRN_HEADS)
    sls = [slice(hh * HGRN_HEAD, (hh + 1) * HGRN_HEAD) for hh in heads]
    cells = [(d, hh) for d in range(2) for hh in heads]

    for d, (q_ref, v_ref, f_ref, o_ref, reverse) in enumerate(dirs):
        tri = jnp.where(jnp.logical_and(same_block, (ci >= ri) if reverse else (ci <= ri)), 1.0, 0.0)
        b_s[d] = _dot_sel(tri, jnp.log(f_ref[...]), 3)

    def block(bi, par):
        pre = []
        for d, (q_ref, v_ref, f_ref, o_ref, reverse) in enumerate(dirs):
            blk = (nb - 1 - bi) if reverse else bi
            r0 = pl.multiple_of(blk * hb, hb)
            kx = 1.0 - f_ref[pl.ds(r0, hb), :]
            q = q_ref[pl.ds(r0, hb), :]
            v = v_ref[pl.ds(r0, hb), :]
            b = b_s[d, pl.ds(r0, hb), :]
            rb_s[d, par] = b
            rk_s[d, par] = kx
            rv_s[d, par] = v
            tot = b[0:1, :] if reverse else b[hb - 1:hb, :]
            first = (rowi >= half) if reverse else (rowi < half)
            beta = b[half:half + 1, :] if reverse else b[half - 1:half, :]
            pre.append(dict(
                r0=r0, q=q, v=v, b=b, qe=q * jnp.exp(b), kb=kx * jnp.exp(tot - b), e_tot=jnp.exp(tot),
                k_first=kx * jnp.exp(jnp.where(first, beta - b, -jnp.inf)),
                q_second=q * jnp.exp(jnp.where(first, -jnp.inf, b - beta)),
                causal=[(ti <= si) if reverse else (ti >= si) for si in range(half)]))
        s = {c: s_ref[c[0], c[1]] for c in cells}
        m_first = {(d, hh): _dot_tn(pre[d]['k_first'][:, sls[hh]], pre[d]['v'][:, sls[hh]]) for d, hh in cells}
        o = {(d, hh): _dot_nt(pre[d]['qe'][:, sls[hh]], s[d, hh]) + _dot(pre[d]['q_second'][:, sls[hh]], m_first[d, hh])
             for d, hh in cells}
        upd = {(d, hh): _dot_tn(pre[d]['v'][:, sls[hh]], pre[d]['kb'][:, sls[hh]]) for d, hh in cells}
        for d, hh in cells:
            sl = sls[hh]
            p = pre[d]
            parts = []
            for lo in (0, half):
                bt = p['b'][lo:lo + half, sl]
                qt = p['q'][lo:lo + half, sl]
                acc = jnp.zeros((half, HGRN_HEAD), F32)
                for si in range(half):
                    row = slice(lo + si, lo + si + 1)
                    dec = jnp.exp(jnp.where(p['causal'][si], bt - rb_s[d, par, row, sl], -jnp.inf))
                    sc = jnp.sum(qt * rk_s[d, par, row, sl] * dec, axis=-1, keepdims=True)
                    acc = acc + sc * rv_s[d, par, row, sl]
                parts.append(acc)
            dirs[d][3][pl.ds(p['r0'], hb), sl] = o[d, hh] + jnp.concatenate(parts, axis=0)
            s_ref[d, hh] = s[d, hh] * p['e_tot'][:, sl] + upd[d, hh]

    def two_blocks(bj, carry):
        block(2 * bj, 0)
        block(2 * bj + 1, 1)
        return carry

    lax.fori_loop(0, nb // 2, two_blocks, 0)


def _hgrn_out_kernel(of_ref, ob_ref, g_ref, ng_ref, w_ref, h_ref, mod_ref, lng_ref, lnb_ref, o_ref):
    parts = []
    for hh in range(HGRN_HEADS):
        sl = slice(hh * HGRN_HEAD, (hh + 1) * HGRN_HEAD)
        o = of_ref[:, sl] + ob_ref[:, sl]
        y = o * lax.rsqrt(jnp.mean(o * o, axis=-1, keepdims=True) + LN_EPS) * ng_ref[...]
        parts.append((y * g_ref[:, sl]).astype(MXU_DT))
    yo = _dot(jnp.concatenate(parts, axis=1), w_ref[...])
    z = DEEPNORM_ALPHA * h_ref[...] + mod_ref[0][2:3] * yo
    o_ref[...] = _ln_rows(z, lng_ref[...], lnb_ref[...])


def _hgrn_mixer(h, mods, nct, lb, w_in, b_f, norm_g, w_out, ln_g, ln_b):
    t, d = h.shape
    nt = t // TM
    q, v, sg, f0, f1 = pl.pallas_call(
        _hgrn_in_kernel, grid=(nt,),
        in_specs=[_row_spec(d), _mod_spec(nct), _full_spec(w_in.shape), _full_spec((1, d)), _full_spec((2, d))],
        out_specs=[_row_spec(d)] * 5, out_shape=[jax.ShapeDtypeStruct((t, d), F32)] * 5,
        compiler_params=_cparams("arbitrary"), name="hgrn_in",
    )(h, mods, w_in.astype(MXU_DT), lb[None], b_f)
    fwd = pl.BlockSpec((TM, d), lambda g_: (g_, 0))
    bwd = pl.BlockSpec((TM, d), lambda g_: (_tile_of(g_, nct, nt, True), 0))
    o_dir = pl.pallas_call(
        _hgrn_scan_kernel, grid=(nt,), in_specs=[fwd] * 3 + [bwd] * 3, out_specs=[fwd, bwd],
        out_shape=[jax.ShapeDtypeStruct((t, d), F32)] * 2,
        scratch_shapes=[pltpu.VMEM((2, HGRN_HEADS, HGRN_HEAD, HGRN_HEAD), F32)] + [pltpu.VMEM((2, TM, d), F32)]
        + [pltpu.VMEM((2, 2, HGRN_BLOCK, d), F32)] * 3,
        compiler_params=_cparams("arbitrary"), name="hgrn_scan",
    )(q, v, f0, q, v, f1)
    return pl.pallas_call(
        _hgrn_out_kernel, grid=(nt,),
        in_specs=[_row_spec(d)] * 3 + [_full_spec((1, HGRN_HEAD)), _full_spec((d, d)), _row_spec(d), _mod_spec(nct),
                                       _full_spec((1, d)), _full_spec((1, d))],
        out_specs=_row_spec(d), out_shape=jax.ShapeDtypeStruct((t, d), F32),
        compiler_params=_cparams("arbitrary"), name="hgrn_out",
    )(o_dir[0], o_dir[1], sg, norm_g[None], w_out.astype(MXU_DT), h, mods, ln_g[None], ln_b[None])


def _router_kernel(h_ref, mod_ref, rw_ref, rb_ref, u_o, gate_o, rank_o, gate_t_o, rank_t_o, x_o):
    u = _modulate(h_ref[...], mod_ref[0], 3)
    u_o[...] = u.astype(u_o.dtype)
    w_hi, w_lo = _split(rw_ref[...], 2)
    u_hi, u_lo = _split(u, 2)
    nt_dims = (((1,), (1,)), ((), ()))
    logits = (lax.dot_general(w_hi, u_hi, nt_dims, preferred_element_type=F32)
              + lax.dot_general(w_hi, u_lo, nt_dims, preferred_element_type=F32)
              + lax.dot_general(w_lo, u_hi, nt_dims, preferred_element_type=F32))
    ne, gs = N_EXPERTS, N_EXPERTS // N_GROUPS
    neg = -jnp.inf
    scores = jax.nn.sigmoid(logits[:ne])
    choice = scores + rb_ref[:ne]
    c3 = choice.reshape(N_GROUPS, gs, TM)
    mi = lax.broadcasted_iota(jnp.int32, c3.shape, 1).astype(F32)
    m1 = jnp.max(c3, axis=1, keepdims=True)
    i1 = jnp.min(jnp.where(c3 == m1, mi, float(gs)), axis=1, keepdims=True)
    m2 = jnp.max(jnp.where(mi == i1, neg, c3), axis=1, keepdims=True)
    gscore = m1 + m2
    gi = lax.broadcasted_iota(jnp.int32, gscore.shape, 0).astype(F32)
    gsel = jnp.zeros(gscore.shape, F32)
    for _ in range(TOPK_GROUPS):
        gm = jnp.max(gscore, axis=0, keepdims=True)
        pick = gi == jnp.min(jnp.where(gscore == gm, gi, float(N_GROUPS)), axis=0, keepdims=True)
        gsel = jnp.where(pick, 1.0, gsel)
        gscore = jnp.where(pick, neg, gscore)
    emask = jnp.broadcast_to(gsel, c3.shape).reshape(ne, TM)
    masked = jnp.where(emask > 0.5, choice, neg)
    ei = lax.broadcasted_iota(jnp.int32, masked.shape, 0).astype(F32)
    chosen = jnp.zeros(masked.shape, F32)
    for _ in range(TOP_K):
        em = jnp.max(masked, axis=0, keepdims=True)
        pick = ei == jnp.min(jnp.where(masked == em, ei, float(ne)), axis=0, keepdims=True)
        chosen = jnp.where(pick, 1.0, chosen)
        masked = jnp.where(pick, neg, masked)
    top_w = scores * chosen
    gates = ROUTED_SCALE * top_w / jnp.sum(top_w, axis=0, keepdims=True)
    ti = lax.broadcasted_iota(jnp.int32, (TM, TM), 0)
    tj = lax.broadcasted_iota(jnp.int32, (TM, TM), 1)
    before = jnp.where(ti < tj, 1.0, 0.0).astype(MXU_DT)
    prefix = jnp.dot(chosen.astype(MXU_DT), before, preferred_element_type=F32)
    rank = jnp.where(chosen > 0.5, prefix, -1.0)
    gate_o[0] = gates
    rank_o[0] = rank
    pad = LANES - ne
    gate_t_o[...] = jnp.concatenate([gates, jnp.zeros((pad, TM), F32)], axis=0).T
    rank_t_o[...] = jnp.concatenate([rank, jnp.full((pad, TM), -1.0, F32)], axis=0).T
    cap = MOE_CAP
    slot = lax.broadcasted_iota(jnp.int32, (cap, TM), 0).astype(F32)
    ub = u.astype(MXU_DT)
    for g0 in range(0, ne, MOE_EGROUP):
        onehot = jnp.concatenate([jnp.where(slot == rank[e:e + 1, :], 1.0, 0.0).astype(MXU_DT)
                                  for e in range(g0, g0 + MOE_EGROUP)], axis=0)
        xg = jnp.dot(onehot, ub, preferred_element_type=F32)
        x_o[0, g0:g0 + MOE_EGROUP] = xg.reshape(MOE_EGROUP, cap, D_MODEL).astype(x_o.dtype)


def _expert_kernel(x_ref, wgu_ref, wd_ref, y_ref, wgu_s, wd_s):
    @pl.when(pl.program_id(1) == 0)
    def _():
        wgu_s[...] = wgu_ref[0, 0].astype(MXU_DT)
        wd_s[...] = wd_ref[0, 0].astype(MXU_DT)

    g = x_ref.shape[0]
    ed = EXPERT_DIM
    x = x_ref[...].reshape(g * MOE_CAP, D_MODEL)
    gu = _dot(x, wgu_s[...])
    y = _dot(_silu(gu[:, :ed]) * gu[:, ed:], wd_s[...])
    y_ref[...] = y.reshape(g, 1, MOE_CAP, D_MODEL).astype(y_ref.dtype)


def _combine_kernel(u_ref, gt_ref, rt_ref, y_ref, sgu_ref, sd_ref, h_ref, mod_ref, lng_ref, lnb_ref, *rest, extra):
    if extra:
        ex_ref, o_ref = rest
    else:
        (o_ref,) = rest
    ed = EXPERT_DIM
    cap = MOE_CAP
    per_vreg = LANES // cap
    gu = _dot(u_ref[...], sgu_ref[...])
    acc = _dot(_silu(gu[:, :ed]) * gu[:, ed:], sd_ref[...])
    if extra:
        acc = acc + ex_ref[...]
    lane = lax.broadcasted_iota(jnp.int32, (1, LANES), 1)
    which = lane // cap
    slot = (lane - which * cap).astype(F32)
    rt = rt_ref[...]
    gt = gt_ref[...]
    for g0 in range(0, N_EXPERTS, MOE_EGROUP):
        pieces = []
        for e0 in range(g0, g0 + MOE_EGROUP, per_vreg):
            rsel = rt[:, e0:e0 + 1]
            gsel = gt[:, e0:e0 + 1]
            for q in range(1, per_vreg):
                rsel = jnp.where(which == q, rt[:, e0 + q:e0 + q + 1], rsel)
                gsel = jnp.where(which == q, gt[:, e0 + q:e0 + q + 1], gsel)
            pieces.append(jnp.where(rsel == slot, gsel, 0.0).astype(MXU_DT))
        pw = jnp.concatenate(pieces, axis=1)
        yg = y_ref[0, g0:g0 + MOE_EGROUP].reshape(MOE_EGROUP * cap, D_MODEL)
        acc = acc + jnp.dot(pw, yg.astype(MXU_DT), preferred_element_type=F32)
    z = DEEPNORM_ALPHA * h_ref[...] + mod_ref[0][5:6] * acc
    o_ref[...] = _ln_rows(z, lng_ref[...], lnb_ref[...])


def _overflow_kernel(tile_ref, exp_ref, nr_ref, n_ref, u_ref, gate_ref, rank_ref, wgu_ref, wd_ref, zero_ref, o_ref):
    del zero_ref
    s = pl.program_id(0)
    tile = tile_ref[s]
    e = exp_ref[s]
    ed = EXPERT_DIM
    cap = MOE_CAP
    active = s < n_ref[0]
    first = jnp.logical_or(s == 0, tile_ref[jnp.maximum(s - 1, 0)] != tile)

    @pl.when(jnp.logical_and(active, first))
    def _():
        o_ref[...] = jnp.zeros_like(o_ref)

    @pl.when(active)
    def _():
        wgu = wgu_ref[0, 0].astype(MXU_DT)
        wd = wd_ref[0, 0].astype(MXU_DT)
        rank = rank_ref[0, pl.ds(e, 1), :]
        gate = gate_ref[0, pl.ds(e, 1), :]

        def one_round(r, carry):
            slot = lax.broadcasted_iota(jnp.int32, (cap, TM), 0).astype(F32) + (r * cap).astype(F32)
            hit = slot == rank
            x = _dot(jnp.where(hit, 1.0, 0.0), u_ref[...])
            gu = _dot(x, wgu)
            y = _dot(_silu(gu[:, :ed]) * gu[:, ed:], wd)
            o_ref[...] += _dot_tn(jnp.where(hit, gate, 0.0), y)
            return carry

        lax.fori_loop(1, nr_ref[s], one_round, 0)


def _moe_layer(h, mods, nct, layer, router_w, router_b, w_gu, w_down, sh_gu, sh_down, ln_g, ln_b):
    t, d = h.shape
    nt = t // TM
    ne, cap = N_EXPERTS, MOE_CAP
    rw = jnp.concatenate([router_w.T, jnp.zeros((LANES - ne, d), F32)], axis=0)
    rb = jnp.concatenate([router_b, jnp.zeros((LANES - ne,), F32)])[:, None]
    per_tile = pl.BlockSpec((1, ne, TM), lambda i: (i, 0, 0))
    slots = pl.BlockSpec((1, ne, cap, d), lambda i: (i, 0, 0, 0))
    u, gates, ranks, gates_t, ranks_t, xs = pl.pallas_call(
        _router_kernel, grid=(nt,),
        in_specs=[_row_spec(d), _mod_spec(nct), _full_spec((LANES, d)), _full_spec((LANES, 1))],
        out_specs=[_row_spec(d), per_tile, per_tile, _row_spec(LANES), _row_spec(LANES), slots],
        out_shape=[jax.ShapeDtypeStruct((t, d), MXU_DT), jax.ShapeDtypeStruct((nt, ne, TM), F32),
                   jax.ShapeDtypeStruct((nt, ne, TM), F32), jax.ShapeDtypeStruct((t, LANES), F32),
                   jax.ShapeDtypeStruct((t, LANES), F32), jax.ShapeDtypeStruct((nt, ne, cap, d), MXU_DT)],
        compiler_params=_cparams("arbitrary"), name="moe_router",
    )(h, mods, rw, rb)

    run = max(g for g in range(1, MOE_RUN + 1) if nt % g == 0)
    ys = pl.pallas_call(
        _expert_kernel, grid=(ne, nt // run),
        in_specs=[pl.BlockSpec((run, 1, cap, d), lambda e, c: (c, e, 0, 0)),
                  pl.BlockSpec((1, 1, d, 2 * EXPERT_DIM), lambda e, c: (layer, e, 0, 0)),
                  pl.BlockSpec((1, 1, EXPERT_DIM, d), lambda e, c: (layer, e, 0, 0))],
        out_specs=pl.BlockSpec((run, 1, cap, d), lambda e, c: (c, e, 0, 0)),
        out_shape=jax.ShapeDtypeStruct((nt, ne, cap, d), MXU_DT),
        scratch_shapes=[pltpu.VMEM((d, 2 * EXPERT_DIM), MXU_DT), pltpu.VMEM((EXPERT_DIM, d), MXU_DT)],
        compiler_params=_cparams("arbitrary", "arbitrary"), name="moe_experts",
    )(xs, w_gu, w_down)

    sgu, sd = sh_gu.astype(MXU_DT), sh_down.astype(MXU_DT)
    base_specs = [_row_spec(d), _row_spec(LANES), _row_spec(LANES), slots, _full_spec(sgu.shape), _full_spec(sd.shape),
                  _row_spec(d), _mod_spec(nct), _full_spec((1, d)), _full_spec((1, d))]
    base_args = (u, gates_t, ranks_t, ys, sgu, sd, h, mods, ln_g[None], ln_b[None])

    def combine(*extra):
        return pl.pallas_call(
            functools.partial(_combine_kernel, extra=bool(extra)), grid=(nt,),
            in_specs=base_specs + [_row_spec(d)] * len(extra), out_specs=_row_spec(d),
            out_shape=jax.ShapeDtypeStruct((t, d), F32), compiler_params=_cparams("arbitrary"), name="moe_combine",
        )(*base_args, *extra)

    count = (jnp.max(ranks, axis=-1).astype(jnp.int32) + 1).reshape(-1)
    over = count > cap
    n_over = jnp.sum(over.astype(jnp.int32))

    def with_overflow(size):
        def run():
            idx = jnp.nonzero(over, size=size, fill_value=0)[0].astype(jnp.int32)
            idx = jnp.where(jnp.arange(size) < n_over, idx, idx[jnp.maximum(n_over - 1, 0)])
            tiles, exps = idx // ne, idx % ne
            rounds = (count[idx] + cap - 1) // cap
            grid_spec = pltpu.PrefetchScalarGridSpec(
                num_scalar_prefetch=4, grid=(size,),
                in_specs=[pl.BlockSpec((TM, d), lambda s, tl, ex, nr, n: (tl[s], 0)),
                          pl.BlockSpec((1, ne, TM), lambda s, tl, ex, nr, n: (tl[s], 0, 0)),
                          pl.BlockSpec((1, ne, TM), lambda s, tl, ex, nr, n: (tl[s], 0, 0)),
                          pl.BlockSpec((1, 1, d, 2 * EXPERT_DIM), lambda s, tl, ex, nr, n: (layer, ex[s], 0, 0)),
                          pl.BlockSpec((1, 1, EXPERT_DIM, d), lambda s, tl, ex, nr, n: (layer, ex[s], 0, 0)),
                          pl.BlockSpec(memory_space=pl.ANY)],
                out_specs=pl.BlockSpec((TM, d), lambda s, tl, ex, nr, n: (tl[s], 0)))
            extra = pl.pallas_call(
                _overflow_kernel, grid_spec=grid_spec, out_shape=jax.ShapeDtypeStruct((t, d), F32),
                input_output_aliases={9: 0}, compiler_params=_cparams("arbitrary"), name="moe_overflow",
            )(tiles, exps, rounds, n_over[None], u, gates, ranks, w_gu, w_down, jnp.zeros((t, d), F32))
            return combine(extra)
        return run

    sizes = sorted({min(MOE_OVER_STEPS, nt * ne), nt * ne})
    branch = sum((n_over > sz).astype(jnp.int32) for sz in [0] + sizes[:-1])
    return lax.switch(branch, [combine] + [with_overflow(sz) for sz in sizes])


def kernel(x, c, ctx, c_ctx, ada_w, ada_b, post_ln_g, post_ln_b, lru_w_in, lru_conv_w, lru_conv_b, lru_gate_w, lru_gate_b, lru_lambda, lru_w_out, rwkv_mu, rwkv_w_in, rwkv_w0, rwkv_w_l1, rwkv_w_l2, rwkv_a0, rwkv_a_l1, rwkv_a_l2, rwkv_g_l1, rwkv_g_l2, rwkv_k_k, rwkv_k_a, rwkv_r_k, rwkv_ln_g, rwkv_ln_b, rwkv_w_out, ret_w_in, ret_decay, ret_gn_g, ret_gn_b, ret_w_out, hgrn_w_in, hgrn_b_f, hgrn_lb, hgrn_norm_g, hgrn_w_out, moe_router, moe_bias, moe_w_gu, moe_w_down, moe_sh_gu, moe_sh_down):
    assert x.shape[0] == 1 and ctx.shape[0] == 1
    n_ctx, n_lat, d = ctx.shape[1], x.shape[1], x.shape[2]
    assert n_ctx % TM == 0 and n_lat % TM == 0 and d == D_MODEL
    nct = n_ctx // TM
    rows = n_lat // GRID_W
    pos_row = jnp.repeat(jnp.arange(rows, dtype=F32), GRID_W)
    pos_col = jnp.tile(jnp.arange(GRID_W, dtype=F32), rows)
    n_freq = RET_QK // 4
    freqs = ROPE_BASE ** (-jnp.arange(n_freq, dtype=F32) / n_freq)
    ang = jnp.concatenate([pos_row[:, None] * freqs, pos_col[:, None] * freqs], axis=-1)
    ang = jnp.concatenate([ang, ang], axis=-1)
    rope_cos = jnp.concatenate([jnp.ones((n_ctx, RET_QK), F32), jnp.cos(ang)], axis=0)
    rope_sin = jnp.concatenate([jnp.zeros((n_ctx, RET_QK), F32), jnp.sin(ang)], axis=0)
    lb_cum = jnp.cumsum(jax.nn.softmax(hgrn_lb.astype(F32), axis=0), axis=0)

    cond = jnp.concatenate([c_ctx[None], c, jnp.zeros((6, d), F32)], axis=0)
    mods_all = _ada_mods(cond, ada_w, ada_b)
    h = jnp.concatenate([ctx[0], x[0]], axis=0)
    for i in range(DEPTH):
        kind, j = i % N_MIXERS, i // N_MIXERS
        mods = mods_all[i]
        lng, lnb = post_ln_g[i, 0], post_ln_b[i, 0]
        if kind == 0:
            h = _lru_mixer(h, mods, nct, lru_w_in[j], lru_conv_w[j], lru_conv_b[j], lru_gate_w[j], lru_gate_b[j],
                           lru_lambda[j], lru_w_out[j], lng, lnb)
        elif kind == 1:
            h = _rwkv_mixer(h, mods, nct, rwkv_mu[j], rwkv_w_in[j], rwkv_w0[j], rwkv_w_l1[j], rwkv_w_l2[j], rwkv_a0[j],
                            rwkv_a_l1[j], rwkv_a_l2[j], rwkv_g_l1[j], rwkv_g_l2[j], rwkv_k_k[j], rwkv_k_a[j],
                            rwkv_r_k[j], rwkv_ln_g[j], rwkv_ln_b[j], rwkv_w_out[j], lng, lnb)
        elif kind == 2:
            h = _ret_mixer(h, mods, nct, rope_cos, rope_sin, ret_w_in[j], ret_decay[j], ret_gn_g[j], ret_gn_b[j],
                           ret_w_out[j], lng, lnb)
        else:
            h = _hgrn_mixer(h, mods, nct, lb_cum[i] - lb_cum[0], hgrn_w_in[j], hgrn_b_f[j], hgrn_norm_g[j],
                            hgrn_w_out[j], lng, lnb)
        h = _moe_layer(h, mods, nct, i, moe_router[i], moe_bias[i], moe_w_gu, moe_w_down, moe_sh_gu[i],
                       moe_sh_down[i], post_ln_g[i, 1], post_ln_b[i, 1])
    return h[n_ctx:][None]
```

```python
import math
import functools
import jax
import jax.numpy as jnp
from jax import lax
from jax.experimental import pallas as pl
from jax.experimental.pallas import tpu as pltpu

F32 = jnp.float32
MXU_DT = jnp.bfloat16
LANES = 128
TM = 256
VMEM_LIMIT = 56 * 2 ** 20

D_MODEL = 1024
DEPTH = 4
GRID_W = 64
N_MIXERS = 4
DEEPNORM_ALPHA = (2.0 * DEPTH) ** 0.25
LN_EPS = 1e-5
LRU_WIDTH = D_MODEL
LRU_BLOCKS = 16
LRU_BLOCK = LRU_WIDTH // LRU_BLOCKS
LRU_C = 8.0
RWKV_HEAD = 64
RWKV_HEADS = D_MODEL // RWKV_HEAD
RWKV_DECAY_SCALE = math.exp(-0.5)
RWKV_GN_EPS = 64e-5
RWKV_CHUNK = 64
RET_HEADS = 4
RET_QK = D_MODEL // RET_HEADS
RET_V = 2 * RET_QK
RET_CHUNK = 128
ROPE_BASE = 10000.0
HGRN_HEADS = 8
HGRN_HEAD = D_MODEL // HGRN_HEADS
HGRN_BLOCK = 16
N_EXPERTS = 64
TOP_K = 8
N_GROUPS = 8
TOPK_GROUPS = 4
EXPERT_DIM = 256
ROUTED_SCALE = 2.5
MOE_OVER_STEPS = 256
MOE_CAP = 64
MOE_EGROUP = 8
MOE_RUN = 13


def _cparams(*sem):
    return pltpu.CompilerParams(dimension_semantics=sem, vmem_limit_bytes=VMEM_LIMIT)


def _dot(a, b):
    return jnp.dot(a.astype(MXU_DT), b.astype(MXU_DT), preferred_element_type=F32)


def _dot_nt(a, b):
    return lax.dot_general(a.astype(MXU_DT), b.astype(MXU_DT), (((1,), (1,)), ((), ())), preferred_element_type=F32)


def _dot_tn(a, b):
    return lax.dot_general(a.astype(MXU_DT), b.astype(MXU_DT), (((0,), (0,)), ((), ())), preferred_element_type=F32)


def _split(x, n):
    parts = []
    for _ in range(n):
        p = x.astype(MXU_DT)
        parts.append(p)
        x = x - p.astype(F32)
    return parts


def _dot_sel(sel, x, n):
    return sum(jnp.dot(sel.astype(MXU_DT), p, preferred_element_type=F32) for p in _split(x, n))


def _dot_xsel(x, sel, n):
    return sum(jnp.dot(p, sel.astype(MXU_DT), preferred_element_type=F32) for p in _split(x, n))


def _modulate(h, m, shift_idx):
    return h * (1.0 + m[shift_idx + 1:shift_idx + 2]) + m[shift_idx:shift_idx + 1]


def _ln_rows(z, g, b):
    mu = jnp.mean(z, axis=-1, keepdims=True)
    zc = z - mu
    var = jnp.mean(zc * zc, axis=-1, keepdims=True)
    return zc * lax.rsqrt(var + LN_EPS) * g + b


def _silu(x):
    return x * jax.nn.sigmoid(x)


def _shift_down(x, first_row):
    rows = lax.broadcasted_iota(jnp.int32, (x.shape[0], 1), 0)
    return jnp.where(rows == 0, first_row, pltpu.roll(x, 1, 0))


def _shift_up(x, last_row):
    n = x.shape[0]
    rows = lax.broadcasted_iota(jnp.int32, (n, 1), 0)
    return jnp.where(rows == n - 1, last_row, pltpu.roll(x, n - 1, 0))


def _tile_of(g, nct, nt, reverse):
    if not reverse:
        return g
    return jnp.where(g < nct, nct - 1 - g, nt - 1 - (g - nct))


def _halo_flags(t, nct, nt):
    prev_ok = jnp.logical_and(t != 0, t != nct).astype(F32)
    next_ok = jnp.logical_and(t != nct - 1, t != nt - 1).astype(F32)
    return prev_ok, next_ok


def _ada_kernel(s_ref, w_ref, b_ref, o_ref):
    o_ref[0] = _dot(_silu(s_ref[...]), w_ref[0]) + b_ref[0]


def _ada_mods(cond, ada_w, ada_b):
    nl, d, n6 = ada_w.shape
    out = pl.pallas_call(
        _ada_kernel, grid=(nl, n6 // d),
        in_specs=[pl.BlockSpec((8, d), lambda l, j: (0, 0)),
                  pl.BlockSpec((1, d, d), lambda l, j: (l, 0, j)),
                  pl.BlockSpec((1, 1, d), lambda l, j: (l, 0, j))],
        out_specs=pl.BlockSpec((1, 8, d), lambda l, j: (l, 0, j)),
        out_shape=jax.ShapeDtypeStruct((nl, 8, n6), F32),
        compiler_params=_cparams("arbitrary", "arbitrary"), name="ada_mods",
    )(cond, ada_w, ada_b.reshape(nl, 1, n6))
    return out[:, :2].reshape(nl, 2, 6, d)


def _row_spec(width, tm=TM):
    return pl.BlockSpec((tm, width), lambda i: (i, 0))


def _full_spec(shape):
    nd = len(shape)
    return pl.BlockSpec(tuple(shape), lambda *_: (0,) * nd)


def _mod_spec(nct):
    return pl.BlockSpec((1, 6, D_MODEL), lambda i: (jnp.minimum(i // nct, 1), 0, 0))


def _out_ln_kernel(p_ref, w_ref, h_ref, mod_ref, lng_ref, lnb_ref, o_ref):
    y = _dot(p_ref[...], w_ref[...])
    z = DEEPNORM_ALPHA * h_ref[...] + mod_ref[0][2:3] * y
    o_ref[...] = _ln_rows(z, lng_ref[...], lnb_ref[...])


def _out_ln(p, w_out, h, mods, ln_g, ln_b, nct):
    t, din = p.shape
    d = D_MODEL
    return pl.pallas_call(
        _out_ln_kernel, grid=(t // TM,),
        in_specs=[_row_spec(din), _full_spec((din, d)), _row_spec(d), _mod_spec(nct),
                  _full_spec((1, d)), _full_spec((1, d))],
        out_specs=_row_spec(d), out_shape=jax.ShapeDtypeStruct((t, d), F32),
        compiler_params=_cparams("arbitrary"), name="out_ln",
    )(p, w_out.astype(MXU_DT), h, mods, ln_g[None], ln_b[None])


def _lru_in_kernel(h_ref, mod_ref, w_ref, g_ref, x_ref):
    u = _modulate(h_ref[...], mod_ref[0], 0)
    z = _dot(u, w_ref[...])
    g_ref[...] = jax.nn.gelu(z[:, :LRU_WIDTH], approximate=True)
    x_ref[...] = z[:, LRU_WIDTH:]


def _lru_scan_kernel(x_ref, xp_ref, xn_ref, cw_ref, cb_ref, gw_ref, gb_ref, lam_ref, *rest, nct, nt, reverse, final):
    if final:
        hf_ref, g_ref, o_ref, a_s, b_s, h_s, st_s = rest
    else:
        o_ref, a_s, b_s, h_s, st_s = rest
    g = pl.program_id(0)
    t = _tile_of(g, nct, nt, reverse)
    prev_ok, next_ok = _halo_flags(t, nct, nt)

    @pl.when(g == 0)
    def _():
        st_s[...] = jnp.zeros_like(st_s)

    x = x_ref[...]
    xm1 = _shift_down(x, xp_ref[7:8, :] * prev_ok)
    n0 = xn_ref[0:1, :] * next_ok
    n1 = xn_ref[1:2, :] * next_ok
    xp1 = _shift_up(x, n0)
    xp2 = _shift_up(xp1, n1)
    cw = cw_ref[...]
    xc = cw[0:1] * xm1 + cw[1:2] * x + cw[2:3] * xp1 + cw[3:4] * xp2 + cb_ref[...]
    gates = jax.nn.sigmoid(_dot(xc, gw_ref[...]) + gb_ref[...])
    lam = lam_ref[...]
    softplus = jnp.maximum(-lam, 0.0) + jnp.log(1.0 + jnp.exp(-jnp.abs(lam)))
    log_a = -LRU_C * gates[:, :LRU_WIDTH] * softplus
    a_s[...] = jnp.exp(log_a)
    b_s[...] = jnp.sqrt(1.0 - jnp.exp(2.0 * log_a)) * (gates[:, LRU_WIDTH:] * xc)

    def row(r, hcur):
        tt = (TM - 1 - r) if reverse else r
        hcur = a_s[pl.ds(tt, 1), :] * hcur + b_s[pl.ds(tt, 1), :]
        h_s[pl.ds(tt, 1), :] = hcur
        return hcur

    st_s[...] = lax.fori_loop(0, TM, row, st_s[...], unroll=8)
    if final:
        o_ref[...] = g_ref[...] * (hf_ref[...] + h_s[...])
    else:
        o_ref[...] = h_s[...]


def _lru_mixer(h, mods, nct, w_in, conv_w, conv_b, gate_w, gate_b, lam, w_out, ln_g, ln_b):
    t, d = h.shape
    nt = t // TM
    w = LRU_WIDTH
    gelu, rnn = pl.pallas_call(
        _lru_in_kernel, grid=(nt,),
        in_specs=[_row_spec(d), _mod_spec(nct), _full_spec((d, 2 * w))],
        out_specs=[_row_spec(w), _row_spec(w)],
        out_shape=[jax.ShapeDtypeStruct((t, w), F32)] * 2,
        compiler_params=_cparams("arbitrary"), name="lru_in",
    )(h, mods, w_in.astype(MXU_DT))
    eye = jnp.eye(LRU_BLOCKS, dtype=F32)
    gw = jnp.einsum('dgnij,nm->dgnimj', gate_w, eye).reshape(2, 2, w, w)
    gw = jnp.concatenate([gw[:, 0], gw[:, 1]], axis=-1).astype(MXU_DT)
    gb = gate_b.reshape(2, 1, 2 * w)
    hf = None
    for d_ in range(2):
        reverse = d_ == 1
        final = d_ == 1
        tile = lambda g: _tile_of(g, nct, nt, reverse)
        ins = [pl.BlockSpec((TM, w), lambda g: (tile(g), 0)),
               pl.BlockSpec((8, w), lambda g: (jnp.maximum(tile(g) * (TM // 8) - 1, 0), 0)),
               pl.BlockSpec((8, w), lambda g: (jnp.minimum((tile(g) + 1) * (TM // 8), t // 8 - 1), 0)),
               _full_spec((4, w)), _full_spec((1, w)), _full_spec((w, 2 * w)), _full_spec((1, 2 * w)),
               _full_spec((1, w))]
        args = [rnn, rnn, rnn, conv_w, conv_b[None], gw[d_], gb[d_], lam[d_][None]]
        if final:
            ins += [pl.BlockSpec((TM, w), lambda g: (tile(g), 0))] * 2
            args += [hf, gelu]
        out = pl.pallas_call(
            functools.partial(_lru_scan_kernel, nct=nct, nt=nt, reverse=reverse, final=final),
            grid=(nt,), in_specs=ins,
            out_specs=pl.BlockSpec((TM, w), lambda g: (tile(g), 0)),
            out_shape=jax.ShapeDtypeStruct((t, w), F32),
            scratch_shapes=[pltpu.VMEM((TM, w), F32)] * 3 + [pltpu.VMEM((1, w), F32)],
            compiler_params=_cparams("arbitrary"), name="lru_scan_%d" % d_,
        )(*args)
        hf = out
    return _out_ln(hf, w_out, h, mods, ln_g, ln_b, nct)


def _seg_sum(x, e_ref, et_ref):
    s = _dot_xsel(x, e_ref[...], 2)
    return _dot_xsel(s, et_ref[...], 2)


def _rwkv_prep_kernel(h_ref, hp_ref, hn_ref, mod_ref, mu_ref, win_ref, wl1_ref, wl2_ref, w0_ref, al1_ref, al2_ref,
                      a0_ref, gl1_ref, gl2_ref, kk_ref, ka_ref, rk_ref, e_ref, et_ref,
                      r_o, v_o, kk_o, g_o, bv_o, lw0_o, lw1_o, kt0_o, kt1_o, ab0_o, ab1_o, *, nct, nt):
    i = pl.program_id(0)
    prev_ok, next_ok = _halo_flags(i, nct, nt)
    m = mod_ref[0]
    u = _modulate(h_ref[...], m, 0)
    up = _modulate(hp_ref[7:8, :], m, 0) * prev_ok
    un = _modulate(hn_ref[0:1, :], m, 0) * next_ok
    lane = lax.broadcasted_iota(jnp.int32, (1, D_MODEL), 1)
    sh = jnp.where(lane < D_MODEL // 2, _shift_down(u, up), _shift_up(u, un))
    dx = sh - u
    mu = mu_ref[...]
    xm = [u + dx * mu[c:c + 1] for c in range(6)]
    r = _dot(xm[0], win_ref[0])
    k = _dot(xm[1], win_ref[1])
    v = _dot(xm[2], win_ref[2])
    t1 = jnp.tanh(_dot(xm[3], wl1_ref[...]))
    t2 = _dot(xm[4], al1_ref[...])
    g = _dot(jax.nn.sigmoid(_dot(xm[5], gl1_ref[...])), gl2_ref[...])
    kk = k * kk_ref[...]
    kk = kk * lax.rsqrt(_seg_sum(kk * kk, e_ref, et_ref) + 1e-12)
    ktsum = None
    for z, (lw_o, kt_o, ab_o) in enumerate(((lw0_o, kt0_o, ab0_o), (lw1_o, kt1_o, ab1_o))):
        d_w = w0_ref[z:z + 1, :] + _dot(t1, wl2_ref[z])
        lw_o[...] = -RWKV_DECAY_SCALE * jax.nn.sigmoid(d_w)
        a = jax.nn.sigmoid(a0_ref[z:z + 1, :] + _dot(t2, al2_ref[z]))
        kt = k * (1.0 + (a - 1.0) * ka_ref[...])
        kt_o[...] = kt
        ab_o[...] = kk * a
        ktsum = kt if ktsum is None else ktsum + kt
    r_o[...] = r
    v_o[...] = v
    kk_o[...] = kk
    g_o[...] = g
    bv_o[...] = _seg_sum(r * ktsum * rk_ref[...], e_ref, et_ref) * v


def _rwkv_scan_kernel(r_ref, v_ref, kk_ref, lw_ref, kt_ref, ab_ref, o_ref, s_ref, *, reverse):
    c = RWKV_CHUNK

    @pl.when(pl.program_id(0) == 0)
    def _():
        s_ref[...] = jnp.zeros_like(s_ref)

    ri = lax.broadcasted_iota(jnp.int32, (c, c), 0)
    ci = lax.broadcasted_iota(jnp.int32, (c, c), 1)
    incl = (ci >= ri) if reverse else (ci <= ri)
    ri2 = lax.broadcasted_iota(jnp.int32, (c, 2 * c), 0)
    ci2 = jnp.bitwise_and(lax.broadcasted_iota(jnp.int32, (c, 2 * c), 1), c - 1)
    incl2 = (ci2 >= ri2) if reverse else (ci2 <= ri2)
    strict2 = (ci2 > ri2) if reverse else (ci2 < ri2)
    lw = lw_ref[...]
    cl = _dot_sel(jnp.where(incl, 1.0, 0.0), lw, 3)
    tot = cl[0:1, :] if reverse else cl[c - 1:c, :]
    e_in = jnp.exp(cl)
    e_out = jnp.exp(-cl)
    e_end = jnp.exp(tot - cl)
    kk = kk_ref[...]
    kt = kt_ref[...]
    ab = ab_ref[...]
    kap = kk * jnp.exp(cl - lw)
    rh = r_ref[...] * e_in
    kh = kt * e_out
    bh = ab * e_out
    kb = kt * e_end
    bb = ab * e_end
    e_tot = jnp.exp(tot)
    vv = v_ref[...]
    lane_a = lax.broadcasted_iota(jnp.int32, (1, LANES), 1) < RWKV_HEAD
    bi = lax.broadcasted_iota(jnp.int32, (LANES, LANES), 0) < RWKV_HEAD
    bj = lax.broadcasted_iota(jnp.int32, (LANES, LANES), 1) < RWKV_HEAD
    blockdiag = bi == bj

    def stack2(x):
        return jnp.concatenate([jnp.where(lane_a, x, 0.0), jnp.where(lane_a, 0.0, x)], axis=0)

    pairs = range(D_MODEL // LANES)
    sls = [slice(p * LANES, (p + 1) * LANES) for p in pairs]
    s = [s_ref[p] for p in pairs]
    xq = [jnp.concatenate([kap[:, sl], rh[:, sl]], axis=0) for sl in sls]
    yk = [jnp.concatenate([stack2(kh[:, sl]), stack2(bh[:, sl])], axis=0) for sl in sls]
    gm = [_dot_nt(xq[p], yk[p]) for p in pairs]
    xs = [_dot_nt(xq[p], s[p]) for p in pairs]
    l_kk = [jnp.where(strict2, g[:c, :2 * c], 0.0) for g in gm]
    l_bk = [jnp.where(strict2, g[:c, 2 * c:], 0.0) for g in gm]
    a_rk = [jnp.where(incl2, g[c:, :2 * c], 0.0) for g in gm]
    a_rb = [jnp.where(incl2, g[c:, 2 * c:], 0.0) for g in gm]
    v2 = [stack2(vv[:, sl]) for sl in sls]
    x = [xs[p][:c] + _dot(l_kk[p], v2[p]) for p in pairs]
    lp = [_dot(l_bk[p], stack2(l_bk[p])) for p in pairs]
    x = [x[p] - _dot(l_bk[p], stack2(x[p])) for p in pairs]
    for it in range(5):
        x = [x[p] + _dot(lp[p], stack2(x[p])) for p in pairs]
        if it < 4:
            lp = [_dot(lp[p], stack2(lp[p])) for p in pairs]
    o = [xs[p][c:] + _dot(jnp.concatenate([a_rk[p], -a_rb[p]], axis=1), jnp.concatenate([v2[p], stack2(x[p])], axis=0))
         for p in pairs]
    upd = [_dot_tn(jnp.concatenate([vv[:, sls[p]], -x[p]], axis=0),
                   jnp.concatenate([kb[:, sls[p]], bb[:, sls[p]]], axis=0)) for p in pairs]
    for p in pairs:
        o_ref[:, sls[p]] = o[p]
        s_ref[p] = s[p] * e_tot[:, sls[p]] + jnp.where(blockdiag, upd[p], 0.0)


def _rwkv_out_kernel(of_ref, ob_ref, bv_ref, g_ref, lg_ref, lb_ref, e_ref, et_ref, w_ref, h_ref, mod_ref, lng_ref,
                     lnb_ref, o_ref):
    o = of_ref[...] + ob_ref[...]
    inv = 1.0 / RWKV_HEAD
    oc = o - _seg_sum(o, e_ref, et_ref) * inv
    var = _seg_sum(oc * oc, e_ref, et_ref) * inv
    y = oc * lax.rsqrt(var + RWKV_GN_EPS) * lg_ref[...] + lb_ref[...] + bv_ref[...]
    yo = _dot(y * g_ref[...], w_ref[...])
    z = DEEPNORM_ALPHA * h_ref[...] + mod_ref[0][2:3] * yo
    o_ref[...] = _ln_rows(z, lng_ref[...], lnb_ref[...])


def _rwkv_mixer(h, mods, nct, mu, w_in, w0, w_l1, w_l2, a0, a_l1, a_l2, g_l1, g_l2, k_k, k_a, r_k, gn_g, gn_b, w_out,
                ln_g, ln_b):
    t, d = h.shape
    nt = t // TM
    bf = MXU_DT
    lw_ = w_l1.shape[-1]
    la_ = a_l1.shape[-1]
    zw = jnp.zeros((lw_, d), F32)
    za = jnp.zeros((la_, d), F32)
    wl1 = jnp.concatenate([w_l1[0], w_l1[1]], axis=1).astype(bf)
    wl2 = jnp.stack([jnp.concatenate([w_l2[0], zw], 0), jnp.concatenate([zw, w_l2[1]], 0)]).astype(bf)
    al1 = jnp.concatenate([a_l1[0], a_l1[1]], axis=1).astype(bf)
    al2 = jnp.stack([jnp.concatenate([a_l2[0], za], 0), jnp.concatenate([za, a_l2[1]], 0)]).astype(bf)
    head_of = jnp.arange(d) // RWKV_HEAD
    e = (head_of[:, None] == jnp.arange(LANES)[None, :]).astype(bf)
    et = e.T
    halo_p = pl.BlockSpec((8, d), lambda i: (jnp.maximum(i * (TM // 8) - 1, 0), 0))
    halo_n = pl.BlockSpec((8, d), lambda i: (jnp.minimum((i + 1) * (TM // 8), t // 8 - 1), 0))
    args = [h, h, h, mods, mu, w_in.astype(bf), wl1, wl2, w0, al1, al2, a0, g_l1.astype(bf), g_l2.astype(bf),
            k_k[None], k_a[None], r_k.reshape(1, d), e, et]
    ins = [_row_spec(d), halo_p, halo_n, _mod_spec(nct)] + [_full_spec(a.shape) for a in args[4:]]
    outs = pl.pallas_call(
        functools.partial(_rwkv_prep_kernel, nct=nct, nt=nt), grid=(nt,), in_specs=ins,
        out_specs=[_row_spec(d)] * 11, out_shape=[jax.ShapeDtypeStruct((t, d), F32)] * 11,
        compiler_params=_cparams("arbitrary"), name="rwkv_prep",
    )(*args)
    r, v, kk, g, bv, lw0, lw1, kt0, kt1, ab0, ab1 = outs
    c = RWKV_CHUNK
    ncc, nc = nct * (TM // c), t // c
    o_dir = []
    for d_, (lw, kt, ab) in enumerate(((lw0, kt0, ab0), (lw1, kt1, ab1))):
        reverse = d_ == 1
        spec = pl.BlockSpec((c, d), lambda g_, reverse=reverse: (_tile_of(g_, ncc, nc, reverse), 0))
        o_dir.append(pl.pallas_call(
            functools.partial(_rwkv_scan_kernel, reverse=reverse), grid=(nc,), in_specs=[spec] * 6, out_specs=spec,
            out_shape=jax.ShapeDtypeStruct((t, d), F32),
            scratch_shapes=[pltpu.VMEM((d // LANES, LANES, LANES), F32)],
            compiler_params=_cparams("arbitrary"), name="rwkv_scan_%d" % d_,
        )(r, v, kk, lw, kt, ab))
    args = [o_dir[0], o_dir[1], bv, g, gn_g[None], gn_b[None], e, et, w_out.astype(bf), h, mods, ln_g[None], ln_b[None]]
    ins = [_row_spec(d)] * 4 + [_full_spec(a.shape) for a in args[4:9]] + [_row_spec(d), _mod_spec(nct),
                                                                          _full_spec((1, d)), _full_spec((1, d))]
    return pl.pallas_call(
        _rwkv_out_kernel, grid=(nt,), in_specs=ins, out_specs=_row_spec(d),
        out_shape=jax.ShapeDtypeStruct((t, d), F32), compiler_params=_cparams("arbitrary"), name="rwkv_out",
    )(*args)


def _ret_in_kernel(h_ref, mod_ref, w_ref, cos_ref, sin_ref, q_o, k_o, v_o, g_o):
    d = D_MODEL
    u = _modulate(h_ref[...], mod_ref[0], 0).astype(MXU_DT)
    q = _dot(u, w_ref[:, 0:d])
    k = _dot(u, w_ref[:, d:2 * d]) * (RET_QK ** -0.5)
    v_o[...] = _dot(u, w_ref[:, 2 * d:4 * d])
    g_o[...] = _silu(_dot(u, w_ref[:, 4 * d:6 * d]))
    cos = cos_ref[...]
    sin = sin_ref[...]
    half = RET_QK // 2
    for z, z_o in ((q, q_o), (k, k_o)):
        for hh in range(RET_HEADS):
            lo = z[:, hh * RET_QK:hh * RET_QK + half]
            hi = z[:, hh * RET_QK + half:(hh + 1) * RET_QK]
            zh = jnp.concatenate([lo, hi], axis=1)
            rot = jnp.concatenate([-hi, lo], axis=1)
            z_o[:, hh * RET_QK:(hh + 1) * RET_QK] = zh * cos + rot * sin


def _ret_scan_kernel(qf_ref, kf_ref, vf_ref, qb_ref, kb_ref, vb_ref, inner_ref, qd_ref, kd_ref, bd_ref, of_ref, ob_ref,
                     r_ref):
    @pl.when(pl.program_id(0) == 0)
    def _():
        r_ref[...] = jnp.zeros_like(r_ref)

    refs = ((qf_ref, kf_ref, vf_ref, of_ref), (qb_ref, kb_ref, vb_ref, ob_ref))
    cells = [(d, hh) for d in range(2) for hh in range(RET_HEADS)]
    qs = lambda hh: slice(hh * RET_QK, (hh + 1) * RET_QK)
    vs = lambda hh: slice(hh * RET_V, (hh + 1) * RET_V)
    q = {(d, hh): refs[d][0][:, qs(hh)] for d, hh in cells}
    k = {(d, hh): refs[d][1][:, qs(hh)] for d, hh in cells}
    v = {(d, hh): refs[d][2][:, vs(hh)] for d, hh in cells}
    state = {c: r_ref[c[0], c[1]] for c in cells}
    scores = {c: _dot_nt(q[c], k[c]) * inner_ref[c[0], c[1]] for c in cells}
    carry_in = {c: _dot(q[c], state[c]) * qd_ref[c[0], c[1]] for c in cells}
    upd = {c: _dot_tn(k[c] * kd_ref[c[0], c[1]], v[c]) for c in cells}
    for c in cells:
        refs[c[0]][3][:, vs(c[1])] = _dot(scores[c], v[c]) + carry_in[c]
        r_ref[c[0], c[1]] = state[c] * bd_ref[c[0], c[1]] + upd[c]


def _ret_out_kernel(of_ref, ob_ref, g_ref, gg_ref, gb_ref, w_ref, h_ref, mod_ref, lng_ref, lnb_ref, o_ref):
    parts = []
    for hh in range(RET_HEADS):
        sl = slice(hh * RET_V, (hh + 1) * RET_V)
        o = of_ref[:, sl] + ob_ref[:, sl]
        mu = jnp.mean(o, axis=-1, keepdims=True)
        oc = o - mu
        var = jnp.mean(oc * oc, axis=-1, keepdims=True)
        y = oc * lax.rsqrt(var + LN_EPS) * gg_ref[:, sl] + gb_ref[:, sl]
        parts.append((g_ref[:, sl] * y).astype(MXU_DT))
    yo = _dot(jnp.concatenate(parts, axis=1), w_ref[...])
    z = DEEPNORM_ALPHA * h_ref[...] + mod_ref[0][2:3] * yo
    o_ref[...] = _ln_rows(z, lng_ref[...], lnb_ref[...])


def _ret_mixer(h, mods, nct, rope_cos, rope_sin, w_in, decay_logit, gn_g, gn_b, w_out, ln_g, ln_b):
    t, d = h.shape
    nt = t // TM
    hv = RET_HEADS * RET_V
    q, k, v, sg = pl.pallas_call(
        _ret_in_kernel, grid=(nt,),
        in_specs=[_row_spec(d), _mod_spec(nct), _full_spec(w_in.shape), _row_spec(RET_QK), _row_spec(RET_QK)],
        out_specs=[_row_spec(d), _row_spec(d), _row_spec(hv), _row_spec(hv)],
        out_shape=[jax.ShapeDtypeStruct((t, w), F32) for w in (d, d, hv, hv)],
        compiler_params=_cparams("arbitrary"), name="ret_in",
    )(h, mods, w_in.astype(MXU_DT), rope_cos, rope_sin)
    c = RET_CHUNK
    ncc, nc = nct * (TM // c), t // c
    log_gamma = jax.nn.log_sigmoid(decay_logit.astype(F32))
    pos = jnp.arange(c, dtype=F32)
    tabs = []
    for d_ in range(2):
        lg = log_gamma[d_][:, None, None]
        p = (c - 1.0 - pos) if d_ == 1 else pos
        rel = p[:, None] - p[None, :]
        tabs.append((jnp.where(rel >= 0, jnp.exp(jnp.maximum(rel, 0.0) * lg), 0.0),
                     jnp.exp((p + 1.0) * log_gamma[d_][:, None])[:, :, None],
                     jnp.exp((c - 1.0 - p) * log_gamma[d_][:, None])[:, :, None],
                     jnp.exp(c * log_gamma[d_])[:, None, None]))
    inner, q_dec, k_dec, blk_dec = (jnp.stack(z) for z in zip(*tabs))
    cs = lambda w, reverse: pl.BlockSpec((c, w), lambda g_: (_tile_of(g_, ncc, nc, reverse), 0))
    o_dir = pl.pallas_call(
        _ret_scan_kernel, grid=(nc,),
        in_specs=[cs(d, False), cs(d, False), cs(hv, False), cs(d, True), cs(d, True), cs(hv, True),
                  _full_spec(inner.shape), _full_spec(q_dec.shape), _full_spec(k_dec.shape), _full_spec(blk_dec.shape)],
        out_specs=[cs(hv, False), cs(hv, True)], out_shape=[jax.ShapeDtypeStruct((t, hv), F32)] * 2,
        scratch_shapes=[pltpu.VMEM((2, RET_HEADS, RET_QK, RET_V), F32)],
        compiler_params=_cparams("arbitrary"), name="ret_scan",
    )(q, k, v, q, k, v, inner, q_dec, k_dec, blk_dec)
    return pl.pallas_call(
        _ret_out_kernel, grid=(nt,),
        in_specs=[_row_spec(hv)] * 3 + [_full_spec((1, hv)), _full_spec((1, hv)), _full_spec((hv, d)), _row_spec(d),
                                        _mod_spec(nct), _full_spec((1, d)), _full_spec((1, d))],
        out_specs=_row_spec(d), out_shape=jax.ShapeDtypeStruct((t, d), F32),
        compiler_params=_cparams("arbitrary"), name="ret_out",
    )(o_dir[0], o_dir[1], sg, gn_g[None], gn_b[None], w_out.astype(MXU_DT), h, mods, ln_g[None], ln_b[None])


def _hgrn_in_kernel(h_ref, mod_ref, w_ref, lb_ref, bf_ref, q_o, v_o, g_o, f0_o, f1_o):
    d = D_MODEL
    u = _modulate(h_ref[...], mod_ref[0], 0).astype(MXU_DT)
    lb = lb_ref[...]
    q_o[...] = _silu(_dot(u, w_ref[:, 0:d]))
    f0_o[...] = lb + (1.0 - lb) * jax.nn.sigmoid(_dot(u, w_ref[:, d:2 * d]) + bf_ref[0:1, :])
    f1_o[...] = lb + (1.0 - lb) * jax.nn.sigmoid(_dot(u, w_ref[:, 2 * d:3 * d]) + bf_ref[1:2, :])
    v_o[...] = _dot(u, w_ref[:, 3 * d:4 * d])
    g_o[...] = _silu(_dot(u, w_ref[:, 4 * d:5 * d]))


def _hgrn_scan_kernel(qf_ref, vf_ref, ff_ref, qb_ref, vb_ref, fb_ref, of_ref, ob_ref, s_ref, b_s, rb_s, rk_s, rv_s):
    hb = HGRN_BLOCK
    nb = TM // hb
    half = hb // 2
    dirs = ((qf_ref, vf_ref, ff_ref, of_ref, False), (qb_ref, vb_ref, fb_ref, ob_ref, True))

    @pl.when(pl.program_id(0) == 0)
    def _():
        s_ref[...] = jnp.zeros_like(s_ref)

    ri = lax.broadcasted_iota(jnp.int32, (TM, TM), 0)
    ci = lax.broadcasted_iota(jnp.int32, (TM, TM), 1)
    same_block = (ri // hb) == (ci // hb)
    ti = lax.broadcasted_iota(jnp.int32, (half, 1), 0)
    rowi = lax.broadcasted_iota(jnp.int32, (hb, 1), 0)
    heads = range(HGRN_HEADS)
    sls = [slice(hh * HGRN_HEAD, (hh + 1) * HGRN_HEAD) for hh in heads]
    cells = [(d, hh) for d in range(2) for hh in heads]

    for d, (q_ref, v_ref, f_ref, o_ref, reverse) in enumerate(dirs):
        tri = jnp.where(jnp.logical_and(same_block, (ci >= ri) if reverse else (ci <= ri)), 1.0, 0.0)
        b_s[d] = _dot_sel(tri, jnp.log(f_ref[...]), 3)

    def block(bi, par):
        pre = []
        for d, (q_ref, v_ref, f_ref, o_ref, reverse) in enumerate(dirs):
            blk = (nb - 1 - bi) if reverse else bi
            r0 = pl.multiple_of(blk * hb, hb)
            kx = 1.0 - f_ref[pl.ds(r0, hb), :]
            q = q_ref[pl.ds(r0, hb), :]
            v = v_ref[pl.ds(r0, hb), :]
            b = b_s[d, pl.ds(r0, hb), :]
            rb_s[d, par] = b
            rk_s[d, par] = kx
            rv_s[d, par] = v
            tot = b[0:1, :] if reverse else b[hb - 1:hb, :]
            first = (rowi >= half) if reverse else (rowi < half)
            beta = b[half:half + 1, :] if reverse else b[half - 1:half, :]
            pre.append(dict(
                r0=r0, q=q, v=v, b=b, qe=q * jnp.exp(b), kb=kx * jnp.exp(tot - b), e_tot=jnp.exp(tot),
                k_first=kx * jnp.exp(jnp.where(first, beta - b, -jnp.inf)),
                q_second=q * jnp.exp(jnp.where(first, -jnp.inf, b - beta)),
                causal=[(ti <= si) if reverse else (ti >= si) for si in range(half)]))
        s = {c: s_ref[c[0], c[1]] for c in cells}
        m_first = {(d, hh): _dot_tn(pre[d]['k_first'][:, sls[hh]], pre[d]['v'][:, sls[hh]]) for d, hh in cells}
        o = {(d, hh): _dot_nt(pre[d]['qe'][:, sls[hh]], s[d, hh]) + _dot(pre[d]['q_second'][:, sls[hh]], m_first[d, hh])
             for d, hh in cells}
        upd = {(d, hh): _dot_tn(pre[d]['v'][:, sls[hh]], pre[d]['kb'][:, sls[hh]]) for d, hh in cells}
        for d, hh in cells:
            sl = sls[hh]
            p = pre[d]
            parts = []
            for lo in (0, half):
                bt = p['b'][lo:lo + half, sl]
                qt = p['q'][lo:lo + half, sl]
                acc = jnp.zeros((half, HGRN_HEAD), F32)
                for si in range(half):
                    row = slice(lo + si, lo + si + 1)
                    dec = jnp.exp(jnp.where(p['causal'][si], bt - rb_s[d, par, row, sl], -jnp.inf))
                    sc = jnp.sum(qt * rk_s[d, par, row, sl] * dec, axis=-1, keepdims=True)
                    acc = acc + sc * rv_s[d, par, row, sl]
                parts.append(acc)
            dirs[d][3][pl.ds(p['r0'], hb), sl] = o[d, hh] + jnp.concatenate(parts, axis=0)
            s_ref[d, hh] = s[d, hh] * p['e_tot'][:, sl] + upd[d, hh]

    def two_blocks(bj, carry):
        block(2 * bj, 0)
        block(2 * bj + 1, 1)
        return carry

    lax.fori_loop(0, nb // 2, two_blocks, 0)


def _hgrn_out_kernel(of_ref, ob_ref, g_ref, ng_ref, w_ref, h_ref, mod_ref, lng_ref, lnb_ref, o_ref):
    parts = []
    for hh in range(HGRN_HEADS):
        sl = slice(hh * HGRN_HEAD, (hh + 1) * HGRN_HEAD)
        o = of_ref[:, sl] + ob_ref[:, sl]
        y = o * lax.rsqrt(jnp.mean(o * o, axis=-1, keepdims=True) + LN_EPS) * ng_ref[...]
        parts.append((y * g_ref[:, sl]).astype(MXU_DT))
    yo = _dot(jnp.concatenate(parts, axis=1), w_ref[...])
    z = DEEPNORM_ALPHA * h_ref[...] + mod_ref[0][2:3] * yo
    o_ref[...] = _ln_rows(z, lng_ref[...], lnb_ref[...])


def _hgrn_mixer(h, mods, nct, lb, w_in, b_f, norm_g, w_out, ln_g, ln_b):
    t, d = h.shape
    nt = t // TM
    q, v, sg, f0, f1 = pl.pallas_call(
        _hgrn_in_kernel, grid=(nt,),
        in_specs=[_row_spec(d), _mod_spec(nct), _full_spec(w_in.shape), _full_spec((1, d)), _full_spec((2, d))],
        out_specs=[_row_spec(d)] * 5, out_shape=[jax.ShapeDtypeStruct((t, d), F32)] * 5,
        compiler_params=_cparams("arbitrary"), name="hgrn_in",
    )(h, mods, w_in.astype(MXU_DT), lb[None], b_f)
    fwd = pl.BlockSpec((TM, d), lambda g_: (g_, 0))
    bwd = pl.BlockSpec((TM, d), lambda g_: (_tile_of(g_, nct, nt, True), 0))
    o_dir = pl.pallas_call(
        _hgrn_scan_kernel, grid=(nt,), in_specs=[fwd] * 3 + [bwd] * 3, out_specs=[fwd, bwd],
        out_shape=[jax.ShapeDtypeStruct((t, d), F32)] * 2,
        scratch_shapes=[pltpu.VMEM((2, HGRN_HEADS, HGRN_HEAD, HGRN_HEAD), F32)] + [pltpu.VMEM((2, TM, d), F32)]
        + [pltpu.VMEM((2, 2, HGRN_BLOCK, d), F32)] * 3,
        compiler_params=_cparams("arbitrary"), name="hgrn_scan",
    )(q, v, f0, q, v, f1)
    return pl.pallas_call(
        _hgrn_out_kernel, grid=(nt,),
        in_specs=[_row_spec(d)] * 3 + [_full_spec((1, HGRN_HEAD)), _full_spec((d, d)), _row_spec(d), _mod_spec(nct),
                                       _full_spec((1, d)), _full_spec((1, d))],
        out_specs=_row_spec(d), out_shape=jax.ShapeDtypeStruct((t, d), F32),
        compiler_params=_cparams("arbitrary"), name="hgrn_out",
    )(o_dir[0], o_dir[1], sg, norm_g[None], w_out.astype(MXU_DT), h, mods, ln_g[None], ln_b[None])


def _router_kernel(h_ref, mod_ref, rw_ref, rb_ref, u_o, gate_o, rank_o, x_o):
    u = _modulate(h_ref[...], mod_ref[0], 3)
    u_o[...] = u.astype(u_o.dtype)
    w_hi, w_lo = _split(rw_ref[...], 2)
    u_hi, u_lo = _split(u, 2)
    nt_dims = (((1,), (1,)), ((), ()))
    logits = (lax.dot_general(w_hi, u_hi, nt_dims, preferred_element_type=F32)
              + lax.dot_general(w_hi, u_lo, nt_dims, preferred_element_type=F32)
              + lax.dot_general(w_lo, u_hi, nt_dims, preferred_element_type=F32))
    ne, gs = N_EXPERTS, N_EXPERTS // N_GROUPS
    neg = -jnp.inf
    scores = jax.nn.sigmoid(logits[:ne])
    choice = scores + rb_ref[:ne]
    c3 = choice.reshape(N_GROUPS, gs, TM)
    mi = lax.broadcasted_iota(jnp.int32, c3.shape, 1).astype(F32)
    m1 = jnp.max(c3, axis=1, keepdims=True)
    i1 = jnp.min(jnp.where(c3 == m1, mi, float(gs)), axis=1, keepdims=True)
    m2 = jnp.max(jnp.where(mi == i1, neg, c3), axis=1, keepdims=True)
    gscore = m1 + m2
    gi = lax.broadcasted_iota(jnp.int32, gscore.shape, 0).astype(F32)
    gsel = jnp.zeros(gscore.shape, F32)
    for _ in range(TOPK_GROUPS):
        gm = jnp.max(gscore, axis=0, keepdims=True)
        pick = gi == jnp.min(jnp.where(gscore == gm, gi, float(N_GROUPS)), axis=0, keepdims=True)
        gsel = jnp.where(pick, 1.0, gsel)
        gscore = jnp.where(pick, neg, gscore)
    emask = jnp.broadcast_to(gsel, c3.shape).reshape(ne, TM)
    masked = jnp.where(emask > 0.5, choice, neg)
    ei = lax.broadcasted_iota(jnp.int32, masked.shape, 0).astype(F32)
    chosen = jnp.zeros(masked.shape, F32)
    for _ in range(TOP_K):
        em = jnp.max(masked, axis=0, keepdims=True)
        pick = ei == jnp.min(jnp.where(masked == em, ei, float(ne)), axis=0, keepdims=True)
        chosen = jnp.where(pick, 1.0, chosen)
        masked = jnp.where(pick, neg, masked)
    top_w = scores * chosen
    gates = ROUTED_SCALE * top_w / jnp.sum(top_w, axis=0, keepdims=True)
    ti = lax.broadcasted_iota(jnp.int32, (TM, TM), 0)
    tj = lax.broadcasted_iota(jnp.int32, (TM, TM), 1)
    before = jnp.where(ti < tj, 1.0, 0.0).astype(MXU_DT)
    prefix = jnp.dot(chosen.astype(MXU_DT), before, preferred_element_type=F32)
    rank = jnp.where(chosen > 0.5, prefix, -1.0)
    gate_o[0] = gates
    rank_o[0] = rank
    cap = MOE_CAP
    slot = lax.broadcasted_iota(jnp.int32, (cap, TM), 0).astype(F32)
    ub = u.astype(MXU_DT)
    for g0 in range(0, ne, MOE_EGROUP):
        onehot = jnp.concatenate([jnp.where(slot == rank[e:e + 1, :], 1.0, 0.0).astype(MXU_DT)
                                  for e in range(g0, g0 + MOE_EGROUP)], axis=0)
        xg = jnp.dot(onehot, ub, preferred_element_type=F32)
        x_o[0, g0:g0 + MOE_EGROUP] = xg.reshape(MOE_EGROUP, cap, D_MODEL).astype(x_o.dtype)


def _expert_kernel(x_ref, wgu_ref, wd_ref, y_ref, wgu_s, wd_s):
    @pl.when(pl.program_id(1) == 0)
    def _():
        wgu_s[...] = wgu_ref[0, 0].astype(MXU_DT)
        wd_s[...] = wd_ref[0, 0].astype(MXU_DT)

    g = x_ref.shape[0]
    ed = EXPERT_DIM
    x = x_ref[...].reshape(g * MOE_CAP, D_MODEL)
    gu = _dot(x, wgu_s[...])
    y = _dot(_silu(gu[:, :ed]) * gu[:, ed:], wd_s[...])
    y_ref[...] = y.reshape(g, 1, MOE_CAP, D_MODEL).astype(y_ref.dtype)


def _combine_kernel(u_ref, gt_ref, rt_ref, y_ref, sgu_ref, sd_ref, h_ref, mod_ref, lng_ref, lnb_ref, *rest, extra):
    if extra:
        ex_ref, o_ref = rest
    else:
        (o_ref,) = rest
    ed = EXPERT_DIM
    cap = MOE_CAP
    gu = _dot(u_ref[...], sgu_ref[...])
    acc = _dot(_silu(gu[:, :ed]) * gu[:, ed:], sd_ref[...])
    if extra:
        acc = acc + ex_ref[...]
    slot = lax.broadcasted_iota(jnp.int32, (cap, TM), 0).astype(F32)
    for g0 in range(0, N_EXPERTS, MOE_EGROUP):
        pw = jnp.concatenate([jnp.where(slot == rt_ref[0, e:e + 1, :], gt_ref[0, e:e + 1, :], 0.0).astype(MXU_DT)
                              for e in range(g0, g0 + MOE_EGROUP)], axis=0)
        yg = y_ref[0, g0:g0 + MOE_EGROUP].reshape(MOE_EGROUP * cap, D_MODEL)
        acc = acc + _dot_tn(pw, yg)
    z = DEEPNORM_ALPHA * h_ref[...] + mod_ref[0][5:6] * acc
    o_ref[...] = _ln_rows(z, lng_ref[...], lnb_ref[...])


def _overflow_kernel(tile_ref, exp_ref, nr_ref, n_ref, u_ref, gate_ref, rank_ref, wgu_ref, wd_ref, zero_ref, o_ref):
    del zero_ref
    s = pl.program_id(0)
    tile = tile_ref[s]
    e = exp_ref[s]
    ed = EXPERT_DIM
    cap = MOE_CAP
    active = s < n_ref[0]
    first = jnp.logical_or(s == 0, tile_ref[jnp.maximum(s - 1, 0)] != tile)

    @pl.when(jnp.logical_and(active, first))
    def _():
        o_ref[...] = jnp.zeros_like(o_ref)

    @pl.when(active)
    def _():
        wgu = wgu_ref[0, 0].astype(MXU_DT)
        wd = wd_ref[0, 0].astype(MXU_DT)
        rank = rank_ref[0, pl.ds(e, 1), :]
        gate = gate_ref[0, pl.ds(e, 1), :]

        def one_round(r, carry):
            slot = lax.broadcasted_iota(jnp.int32, (cap, TM), 0).astype(F32) + (r * cap).astype(F32)
            hit = slot == rank
            x = _dot(jnp.where(hit, 1.0, 0.0), u_ref[...])
            gu = _dot(x, wgu)
            y = _dot(_silu(gu[:, :ed]) * gu[:, ed:], wd)
            o_ref[...] += _dot_tn(jnp.where(hit, gate, 0.0), y)
            return carry

        lax.fori_loop(1, nr_ref[s], one_round, 0)


def _moe_layer(h, mods, nct, layer, router_w, router_b, w_gu, w_down, sh_gu, sh_down, ln_g, ln_b):
    t, d = h.shape
    nt = t // TM
    ne, cap = N_EXPERTS, MOE_CAP
    rw = jnp.concatenate([router_w.T, jnp.zeros((LANES - ne, d), F32)], axis=0)
    rb = jnp.concatenate([router_b, jnp.zeros((LANES - ne,), F32)])[:, None]
    per_tile = pl.BlockSpec((1, ne, TM), lambda i: (i, 0, 0))
    slots = pl.BlockSpec((1, ne, cap, d), lambda i: (i, 0, 0, 0))
    u, gates, ranks, xs = pl.pallas_call(
        _router_kernel, grid=(nt,),
        in_specs=[_row_spec(d), _mod_spec(nct), _full_spec((LANES, d)), _full_spec((LANES, 1))],
        out_specs=[_row_spec(d), per_tile, per_tile, slots],
        out_shape=[jax.ShapeDtypeStruct((t, d), MXU_DT), jax.ShapeDtypeStruct((nt, ne, TM), F32),
                   jax.ShapeDtypeStruct((nt, ne, TM), F32), jax.ShapeDtypeStruct((nt, ne, cap, d), MXU_DT)],
        compiler_params=_cparams("arbitrary"), name="moe_router",
    )(h, mods, rw, rb)

    run = max(g for g in range(1, MOE_RUN + 1) if nt % g == 0)
    ys = pl.pallas_call(
        _expert_kernel, grid=(ne, nt // run),
        in_specs=[pl.BlockSpec((run, 1, cap, d), lambda e, c: (c, e, 0, 0)),
                  pl.BlockSpec((1, 1, d, 2 * EXPERT_DIM), lambda e, c: (layer, e, 0, 0)),
                  pl.BlockSpec((1, 1, EXPERT_DIM, d), lambda e, c: (layer, e, 0, 0))],
        out_specs=pl.BlockSpec((run, 1, cap, d), lambda e, c: (c, e, 0, 0)),
        out_shape=jax.ShapeDtypeStruct((nt, ne, cap, d), MXU_DT),
        scratch_shapes=[pltpu.VMEM((d, 2 * EXPERT_DIM), MXU_DT), pltpu.VMEM((EXPERT_DIM, d), MXU_DT)],
        compiler_params=_cparams("arbitrary", "arbitrary"), name="moe_experts",
    )(xs, w_gu, w_down)

    sgu, sd = sh_gu.astype(MXU_DT), sh_down.astype(MXU_DT)
    base_specs = [_row_spec(d), per_tile, per_tile, slots, _full_spec(sgu.shape), _full_spec(sd.shape),
                  _row_spec(d), _mod_spec(nct), _full_spec((1, d)), _full_spec((1, d))]
    base_args = (u, gates, ranks, ys, sgu, sd, h, mods, ln_g[None], ln_b[None])

    def combine(*extra):
        return pl.pallas_call(
            functools.partial(_combine_kernel, extra=bool(extra)), grid=(nt,),
            in_specs=base_specs + [_row_spec(d)] * len(extra), out_specs=_row_spec(d),
            out_shape=jax.ShapeDtypeStruct((t, d), F32), compiler_params=_cparams("arbitrary"), name="moe_combine",
        )(*base_args, *extra)

    count = (jnp.max(ranks, axis=-1).astype(jnp.int32) + 1).reshape(-1)
    over = count > cap
    n_over = jnp.sum(over.astype(jnp.int32))

    def with_overflow(size):
        def run():
            idx = jnp.nonzero(over, size=size, fill_value=0)[0].astype(jnp.int32)
            idx = jnp.where(jnp.arange(size) < n_over, idx, idx[jnp.maximum(n_over - 1, 0)])
            tiles, exps = idx // ne, idx % ne
            rounds = (count[idx] + cap - 1) // cap
            grid_spec = pltpu.PrefetchScalarGridSpec(
                num_scalar_prefetch=4, grid=(size,),
                in_specs=[pl.BlockSpec((TM, d), lambda s, tl, ex, nr, n: (tl[s], 0)),
                          pl.BlockSpec((1, ne, TM), lambda s, tl, ex, nr, n: (tl[s], 0, 0)),
                          pl.BlockSpec((1, ne, TM), lambda s, tl, ex, nr, n: (tl[s], 0, 0)),
                          pl.BlockSpec((1, 1, d, 2 * EXPERT_DIM), lambda s, tl, ex, nr, n: (layer, ex[s], 0, 0)),
                          pl.BlockSpec((1, 1, EXPERT_DIM, d), lambda s, tl, ex, nr, n: (layer, ex[s], 0, 0)),
                          pl.BlockSpec(memory_space=pl.ANY)],
                out_specs=pl.BlockSpec((TM, d), lambda s, tl, ex, nr, n: (tl[s], 0)))
            extra = pl.pallas_call(
                _overflow_kernel, grid_spec=grid_spec, out_shape=jax.ShapeDtypeStruct((t, d), F32),
                input_output_aliases={9: 0}, compiler_params=_cparams("arbitrary"), name="moe_overflow",
            )(tiles, exps, rounds, n_over[None], u, gates, ranks, w_gu, w_down, jnp.zeros((t, d), F32))
            return combine(extra)
        return run

    sizes = sorted({min(MOE_OVER_STEPS, nt * ne), nt * ne})
    branch = sum((n_over > sz).astype(jnp.int32) for sz in [0] + sizes[:-1])
    return lax.switch(branch, [combine] + [with_overflow(sz) for sz in sizes])


def kernel(x, c, ctx, c_ctx, ada_w, ada_b, post_ln_g, post_ln_b, lru_w_in, lru_conv_w, lru_conv_b, lru_gate_w, lru_gate_b, lru_lambda, lru_w_out, rwkv_mu, rwkv_w_in, rwkv_w0, rwkv_w_l1, rwkv_w_l2, rwkv_a0, rwkv_a_l1, rwkv_a_l2, rwkv_g_l1, rwkv_g_l2, rwkv_k_k, rwkv_k_a, rwkv_r_k, rwkv_ln_g, rwkv_ln_b, rwkv_w_out, ret_w_in, ret_decay, ret_gn_g, ret_gn_b, ret_w_out, hgrn_w_in, hgrn_b_f, hgrn_lb, hgrn_norm_g, hgrn_w_out, moe_router, moe_bias, moe_w_gu, moe_w_down, moe_sh_gu, moe_sh_down):
    assert x.shape[0] == 1 and ctx.shape[0] == 1
    n_ctx, n_lat, d = ctx.shape[1], x.shape[1], x.shape[2]
    assert n_ctx % TM == 0 and n_lat % TM == 0 and d == D_MODEL
    nct = n_ctx // TM
    rows = n_lat // GRID_W
    pos_row = jnp.repeat(jnp.arange(rows, dtype=F32), GRID_W)
    pos_col = jnp.tile(jnp.arange(GRID_W, dtype=F32), rows)
    n_freq = RET_QK // 4
    freqs = ROPE_BASE ** (-jnp.arange(n_freq, dtype=F32) / n_freq)
    ang = jnp.concatenate([pos_row[:, None] * freqs, pos_col[:, None] * freqs], axis=-1)
    ang = jnp.concatenate([ang, ang], axis=-1)
    rope_cos = jnp.concatenate([jnp.ones((n_ctx, RET_QK), F32), jnp.cos(ang)], axis=0)
    rope_sin = jnp.concatenate([jnp.zeros((n_ctx, RET_QK), F32), jnp.sin(ang)], axis=0)
    lb_cum = jnp.cumsum(jax.nn.softmax(hgrn_lb.astype(F32), axis=0), axis=0)

    cond = jnp.concatenate([c_ctx[None], c, jnp.zeros((6, d), F32)], axis=0)
    mods_all = _ada_mods(cond, ada_w, ada_b)
    h = jnp.concatenate([ctx[0], x[0]], axis=0)
    for i in range(DEPTH):
        kind, j = i % N_MIXERS, i // N_MIXERS
        mods = mods_all[i]
        lng, lnb = post_ln_g[i, 0], post_ln_b[i, 0]
        if kind == 0:
            h = _lru_mixer(h, mods, nct, lru_w_in[j], lru_conv_w[j], lru_conv_b[j], lru_gate_w[j], lru_gate_b[j],
                           lru_lambda[j], lru_w_out[j], lng, lnb)
        elif kind == 1:
            h = _rwkv_mixer(h, mods, nct, rwkv_mu[j], rwkv_w_in[j], rwkv_w0[j], rwkv_w_l1[j], rwkv_w_l2[j], rwkv_a0[j],
                            rwkv_a_l1[j], rwkv_a_l2[j], rwkv_g_l1[j], rwkv_g_l2[j], rwkv_k_k[j], rwkv_k_a[j],
                            rwkv_r_k[j], rwkv_ln_g[j], rwkv_ln_b[j], rwkv_w_out[j], lng, lnb)
        elif kind == 2:
            h = _ret_mixer(h, mods, nct, rope_cos, rope_sin, ret_w_in[j], ret_decay[j], ret_gn_g[j], ret_gn_b[j],
                           ret_w_out[j], lng, lnb)
        else:
            h = _hgrn_mixer(h, mods, nct, lb_cum[i] - lb_cum[0], hgrn_w_in[j], hgrn_b_f[j], hgrn_norm_g[j],
                            hgrn_w_out[j], lng, lnb)
        h = _moe_layer(h, mods, nct, i, moe_router[i], moe_bias[i], moe_w_gu, moe_w_down, moe_sh_gu[i],
                       moe_sh_down[i], post_ln_g[i, 1], post_ln_b[i, 1])
    return h[n_ctx:][None]
```

```python
import math
import functools
import jax
import jax.numpy as jnp
from jax import lax
from jax.experimental import pallas as pl
from jax.experimental.pallas import tpu as pltpu

F32 = jnp.float32
MXU_DT = jnp.bfloat16
ACT_DT = jnp.bfloat16
LANES = 128
TM = 256
VMEM_LIMIT = 56 * 2 ** 20

D_MODEL = 1024
DEPTH = 4
GRID_W = 64
N_MIXERS = 4
DEEPNORM_ALPHA = (2.0 * DEPTH) ** 0.25
LN_EPS = 1e-5
LRU_WIDTH = D_MODEL
LRU_BLOCKS = 16
LRU_BLOCK = LRU_WIDTH // LRU_BLOCKS
LRU_C = 8.0
RWKV_HEAD = 64
RWKV_HEADS = D_MODEL // RWKV_HEAD
RWKV_DECAY_SCALE = math.exp(-0.5)
RWKV_GN_EPS = 64e-5
RWKV_CHUNK = 64
RET_HEADS = 4
RET_QK = D_MODEL // RET_HEADS
RET_V = 2 * RET_QK
RET_CHUNK = 128
ROPE_BASE = 10000.0
HGRN_HEADS = 8
HGRN_HEAD = D_MODEL // HGRN_HEADS
HGRN_BLOCK = 16
N_EXPERTS = 64
TOP_K = 8
N_GROUPS = 8
TOPK_GROUPS = 4
EXPERT_DIM = 256
ROUTED_SCALE = 2.5
MOE_OVER_STEPS = 256
MOE_CAP = 64
MOE_EGROUP = 8
MOE_RUN = 13


def _cparams(*sem):
    return pltpu.CompilerParams(dimension_semantics=sem, vmem_limit_bytes=VMEM_LIMIT)


def _dot(a, b):
    return jnp.dot(a.astype(MXU_DT), b.astype(MXU_DT), preferred_element_type=F32)


def _dot_nt(a, b):
    return lax.dot_general(a.astype(MXU_DT), b.astype(MXU_DT), (((1,), (1,)), ((), ())), preferred_element_type=F32)


def _dot_tn(a, b):
    return lax.dot_general(a.astype(MXU_DT), b.astype(MXU_DT), (((0,), (0,)), ((), ())), preferred_element_type=F32)


def _split(x, n):
    parts = []
    for _ in range(n):
        p = x.astype(MXU_DT)
        parts.append(p)
        x = x - p.astype(F32)
    return parts


def _dot_sel(sel, x, n):
    return sum(jnp.dot(sel.astype(MXU_DT), p, preferred_element_type=F32) for p in _split(x, n))


def _dot_xsel(x, sel, n):
    return sum(jnp.dot(p, sel.astype(MXU_DT), preferred_element_type=F32) for p in _split(x, n))


def _modulate(h, m, shift_idx):
    return h * (1.0 + m[shift_idx + 1:shift_idx + 2]) + m[shift_idx:shift_idx + 1]


def _ln_rows(z, g, b):
    mu = jnp.mean(z, axis=-1, keepdims=True)
    zc = z - mu
    var = jnp.mean(zc * zc, axis=-1, keepdims=True)
    return zc * lax.rsqrt(var + LN_EPS) * g + b


def _silu(x):
    return x * jax.nn.sigmoid(x)


def _shift_down(x, first_row):
    rows = lax.broadcasted_iota(jnp.int32, (x.shape[0], 1), 0)
    return jnp.where(rows == 0, first_row, pltpu.roll(x, 1, 0))


def _shift_up(x, last_row):
    n = x.shape[0]
    rows = lax.broadcasted_iota(jnp.int32, (n, 1), 0)
    return jnp.where(rows == n - 1, last_row, pltpu.roll(x, n - 1, 0))


def _tile_of(g, nct, nt, reverse):
    if not reverse:
        return g
    return jnp.where(g < nct, nct - 1 - g, nt - 1 - (g - nct))


def _halo_flags(t, nct, nt):
    prev_ok = jnp.logical_and(t != 0, t != nct).astype(F32)
    next_ok = jnp.logical_and(t != nct - 1, t != nt - 1).astype(F32)
    return prev_ok, next_ok


def _ada_kernel(s_ref, w_ref, b_ref, o_ref):
    o_ref[0] = _dot(_silu(s_ref[...]), w_ref[0]) + b_ref[0]


def _ada_mods(cond, ada_w, ada_b):
    nl, d, n6 = ada_w.shape
    out = pl.pallas_call(
        _ada_kernel, grid=(nl, n6 // d),
        in_specs=[pl.BlockSpec((8, d), lambda l, j: (0, 0)),
                  pl.BlockSpec((1, d, d), lambda l, j: (l, 0, j)),
                  pl.BlockSpec((1, 1, d), lambda l, j: (l, 0, j))],
        out_specs=pl.BlockSpec((1, 8, d), lambda l, j: (l, 0, j)),
        out_shape=jax.ShapeDtypeStruct((nl, 8, n6), F32),
        compiler_params=_cparams("arbitrary", "arbitrary"), name="ada_mods",
    )(cond, ada_w, ada_b.reshape(nl, 1, n6))
    return out[:, :2].reshape(nl, 2, 6, d)


def _row_spec(width, tm=TM):
    return pl.BlockSpec((tm, width), lambda i: (i, 0))


def _full_spec(shape):
    nd = len(shape)
    return pl.BlockSpec(tuple(shape), lambda *_: (0,) * nd)


def _mod_spec(nct):
    return pl.BlockSpec((1, 6, D_MODEL), lambda i: (jnp.minimum(i // nct, 1), 0, 0))


def _out_ln_kernel(p_ref, w_ref, h_ref, mod_ref, lng_ref, lnb_ref, o_ref):
    y = _dot(p_ref[...], w_ref[...])
    z = DEEPNORM_ALPHA * h_ref[...] + mod_ref[0][2:3] * y
    o_ref[...] = _ln_rows(z, lng_ref[...], lnb_ref[...])


def _out_ln(p, w_out, h, mods, ln_g, ln_b, nct):
    t, din = p.shape
    d = D_MODEL
    return pl.pallas_call(
        _out_ln_kernel, grid=(t // TM,),
        in_specs=[_row_spec(din), _full_spec((din, d)), _row_spec(d), _mod_spec(nct),
                  _full_spec((1, d)), _full_spec((1, d))],
        out_specs=_row_spec(d), out_shape=jax.ShapeDtypeStruct((t, d), F32),
        compiler_params=_cparams("arbitrary"), name="out_ln",
    )(p, w_out.astype(MXU_DT), h, mods, ln_g[None], ln_b[None])


def _lru_in_kernel(h_ref, mod_ref, w_ref, g_ref, x_ref):
    u = _modulate(h_ref[...], mod_ref[0], 0)
    z = _dot(u, w_ref[...])
    g_ref[...] = jax.nn.gelu(z[:, :LRU_WIDTH], approximate=True).astype(g_ref.dtype)
    x_ref[...] = z[:, LRU_WIDTH:]


def _lru_scan_kernel(x_ref, xp_ref, xn_ref, cw_ref, cb_ref, gw_ref, gb_ref, lam_ref, *rest, nct, nt, reverse, final):
    if final:
        hf_ref, g_ref, o_ref, a_s, b_s, h_s, st_s = rest
    else:
        o_ref, a_s, b_s, h_s, st_s = rest
    g = pl.program_id(0)
    t = _tile_of(g, nct, nt, reverse)
    prev_ok, next_ok = _halo_flags(t, nct, nt)

    @pl.when(g == 0)
    def _():
        st_s[...] = jnp.zeros_like(st_s)

    x = x_ref[...]
    xm1 = _shift_down(x, xp_ref[7:8, :] * prev_ok)
    n0 = xn_ref[0:1, :] * next_ok
    n1 = xn_ref[1:2, :] * next_ok
    xp1 = _shift_up(x, n0)
    xp2 = _shift_up(xp1, n1)
    cw = cw_ref[...]
    xc = cw[0:1] * xm1 + cw[1:2] * x + cw[2:3] * xp1 + cw[3:4] * xp2 + cb_ref[...]
    gates = jax.nn.sigmoid(_dot(xc, gw_ref[...]) + gb_ref[...])
    lam = lam_ref[...]
    softplus = jnp.maximum(-lam, 0.0) + jnp.log(1.0 + jnp.exp(-jnp.abs(lam)))
    log_a = -LRU_C * gates[:, :LRU_WIDTH] * softplus
    a_s[...] = jnp.exp(log_a)
    b_s[...] = jnp.sqrt(1.0 - jnp.exp(2.0 * log_a)) * (gates[:, LRU_WIDTH:] * xc)

    def row(r, hcur):
        tt = (TM - 1 - r) if reverse else r
        hcur = a_s[pl.ds(tt, 1), :] * hcur + b_s[pl.ds(tt, 1), :]
        h_s[pl.ds(tt, 1), :] = hcur
        return hcur

    st_s[...] = lax.fori_loop(0, TM, row, st_s[...], unroll=8)
    if final:
        o_ref[...] = g_ref[...] * (hf_ref[...] + h_s[...])
    else:
        o_ref[...] = h_s[...]


def _lru_mixer(h, mods, nct, w_in, conv_w, conv_b, gate_w, gate_b, lam, w_out, ln_g, ln_b):
    t, d = h.shape
    nt = t // TM
    w = LRU_WIDTH
    gelu, rnn = pl.pallas_call(
        _lru_in_kernel, grid=(nt,),
        in_specs=[_row_spec(d), _mod_spec(nct), _full_spec((d, 2 * w))],
        out_specs=[_row_spec(w), _row_spec(w)],
        out_shape=[jax.ShapeDtypeStruct((t, w), ACT_DT), jax.ShapeDtypeStruct((t, w), F32)],
        compiler_params=_cparams("arbitrary"), name="lru_in",
    )(h, mods, w_in.astype(MXU_DT))
    eye = jnp.eye(LRU_BLOCKS, dtype=F32)
    gw = jnp.einsum('dgnij,nm->dgnimj', gate_w, eye).reshape(2, 2, w, w)
    gw = jnp.concatenate([gw[:, 0], gw[:, 1]], axis=-1).astype(MXU_DT)
    gb = gate_b.reshape(2, 1, 2 * w)
    hf = None
    for d_ in range(2):
        reverse = d_ == 1
        final = d_ == 1
        tile = lambda g: _tile_of(g, nct, nt, reverse)
        ins = [pl.BlockSpec((TM, w), lambda g: (tile(g), 0)),
               pl.BlockSpec((8, w), lambda g: (jnp.maximum(tile(g) * (TM // 8) - 1, 0), 0)),
               pl.BlockSpec((8, w), lambda g: (jnp.minimum((tile(g) + 1) * (TM // 8), t // 8 - 1), 0)),
               _full_spec((4, w)), _full_spec((1, w)), _full_spec((w, 2 * w)), _full_spec((1, 2 * w)),
               _full_spec((1, w))]
        args = [rnn, rnn, rnn, conv_w, conv_b[None], gw[d_], gb[d_], lam[d_][None]]
        if final:
            ins += [pl.BlockSpec((TM, w), lambda g: (tile(g), 0))] * 2
            args += [hf, gelu]
        out = pl.pallas_call(
            functools.partial(_lru_scan_kernel, nct=nct, nt=nt, reverse=reverse, final=final),
            grid=(nt,), in_specs=ins,
            out_specs=pl.BlockSpec((TM, w), lambda g: (tile(g), 0)),
            out_shape=jax.ShapeDtypeStruct((t, w), F32),
            scratch_shapes=[pltpu.VMEM((TM, w), F32)] * 3 + [pltpu.VMEM((1, w), F32)],
            compiler_params=_cparams("arbitrary"), name="lru_scan_%d" % d_,
        )(*args)
        hf = out
    return _out_ln(hf, w_out, h, mods, ln_g, ln_b, nct)


def _seg_sum(x, e_ref, et_ref):
    s = _dot_xsel(x, e_ref[...], 2)
    return _dot_xsel(s, et_ref[...], 2)


def _rwkv_prep_kernel(h_ref, hp_ref, hn_ref, mod_ref, mu_ref, win_ref, wl1_ref, wl2_ref, w0_ref, al1_ref, al2_ref,
                      a0_ref, gl1_ref, gl2_ref, kk_ref, ka_ref, rk_ref, e_ref, et_ref,
                      r_o, v_o, kk_o, g_o, bv_o, lw0_o, lw1_o, kt0_o, kt1_o, ab0_o, ab1_o, *, nct, nt):
    i = pl.program_id(0)
    prev_ok, next_ok = _halo_flags(i, nct, nt)
    m = mod_ref[0]
    u = _modulate(h_ref[...], m, 0)
    up = _modulate(hp_ref[7:8, :], m, 0) * prev_ok
    un = _modulate(hn_ref[0:1, :], m, 0) * next_ok
    lane = lax.broadcasted_iota(jnp.int32, (1, D_MODEL), 1)
    sh = jnp.where(lane < D_MODEL // 2, _shift_down(u, up), _shift_up(u, un))
    dx = sh - u
    mu = mu_ref[...]
    xm = [u + dx * mu[c:c + 1] for c in range(6)]
    r = _dot(xm[0], win_ref[0])
    k = _dot(xm[1], win_ref[1])
    v = _dot(xm[2], win_ref[2])
    t1 = jnp.tanh(_dot(xm[3], wl1_ref[...]))
    t2 = _dot(xm[4], al1_ref[...])
    g = _dot(jax.nn.sigmoid(_dot(xm[5], gl1_ref[...])), gl2_ref[...])
    kk = k * kk_ref[...]
    kk = kk * lax.rsqrt(_seg_sum(kk * kk, e_ref, et_ref) + 1e-12)
    ktsum = None
    for z, (lw_o, kt_o, ab_o) in enumerate(((lw0_o, kt0_o, ab0_o), (lw1_o, kt1_o, ab1_o))):
        d_w = w0_ref[z:z + 1, :] + _dot(t1, wl2_ref[z])
        lw_o[...] = -RWKV_DECAY_SCALE * jax.nn.sigmoid(d_w)
        a = jax.nn.sigmoid(a0_ref[z:z + 1, :] + _dot(t2, al2_ref[z]))
        kt = k * (1.0 + (a - 1.0) * ka_ref[...])
        kt_o[...] = kt.astype(kt_o.dtype)
        ab_o[...] = (kk * a).astype(ab_o.dtype)
        ktsum = kt if ktsum is None else ktsum + kt
    r_o[...] = r.astype(r_o.dtype)
    v_o[...] = v.astype(v_o.dtype)
    kk_o[...] = kk.astype(kk_o.dtype)
    g_o[...] = g.astype(g_o.dtype)
    bv_o[...] = (_seg_sum(r * ktsum * rk_ref[...], e_ref, et_ref) * v).astype(bv_o.dtype)


def _rwkv_scan_kernel(r_ref, v_ref, kk_ref, lw_ref, kt_ref, ab_ref, o_ref, s_ref, *, reverse):
    c = RWKV_CHUNK

    @pl.when(pl.program_id(0) == 0)
    def _():
        s_ref[...] = jnp.zeros_like(s_ref)

    ri = lax.broadcasted_iota(jnp.int32, (c, c), 0)
    ci = lax.broadcasted_iota(jnp.int32, (c, c), 1)
    incl = (ci >= ri) if reverse else (ci <= ri)
    ri2 = lax.broadcasted_iota(jnp.int32, (c, 2 * c), 0)
    ci2 = jnp.bitwise_and(lax.broadcasted_iota(jnp.int32, (c, 2 * c), 1), c - 1)
    incl2 = (ci2 >= ri2) if reverse else (ci2 <= ri2)
    strict2 = (ci2 > ri2) if reverse else (ci2 < ri2)
    lw = lw_ref[...]
    cl = _dot_sel(jnp.where(incl, 1.0, 0.0), lw, 3)
    tot = cl[0:1, :] if reverse else cl[c - 1:c, :]
    e_in = jnp.exp(cl)
    e_out = jnp.exp(-cl)
    e_end = jnp.exp(tot - cl)
    kk = kk_ref[...].astype(F32)
    kt = kt_ref[...].astype(F32)
    ab = ab_ref[...].astype(F32)
    kap = kk * jnp.exp(cl - lw)
    rh = r_ref[...].astype(F32) * e_in
    kh = kt * e_out
    bh = ab * e_out
    kb = kt * e_end
    bb = ab * e_end
    e_tot = jnp.exp(tot)
    vv = v_ref[...].astype(F32)
    lane_a = lax.broadcasted_iota(jnp.int32, (1, LANES), 1) < RWKV_HEAD
    bi = lax.broadcasted_iota(jnp.int32, (LANES, LANES), 0) < RWKV_HEAD
    bj = lax.broadcasted_iota(jnp.int32, (LANES, LANES), 1) < RWKV_HEAD
    blockdiag = bi == bj

    def stack2(x):
        return jnp.concatenate([jnp.where(lane_a, x, 0.0), jnp.where(lane_a, 0.0, x)], axis=0)

    pairs = range(D_MODEL // LANES)
    sls = [slice(p * LANES, (p + 1) * LANES) for p in pairs]
    s = [s_ref[p] for p in pairs]
    xq = [jnp.concatenate([kap[:, sl], rh[:, sl]], axis=0) for sl in sls]
    yk = [jnp.concatenate([stack2(kh[:, sl]), stack2(bh[:, sl])], axis=0) for sl in sls]
    gm = [_dot_nt(xq[p], yk[p]) for p in pairs]
    xs = [_dot_nt(xq[p], s[p]) for p in pairs]
    l_kk = [jnp.where(strict2, g[:c, :2 * c], 0.0) for g in gm]
    l_bk = [jnp.where(strict2, g[:c, 2 * c:], 0.0) for g in gm]
    a_rk = [jnp.where(incl2, g[c:, :2 * c], 0.0) for g in gm]
    a_rb = [jnp.where(incl2, g[c:, 2 * c:], 0.0) for g in gm]
    v2 = [stack2(vv[:, sl]) for sl in sls]
    x = [xs[p][:c] + _dot(l_kk[p], v2[p]) for p in pairs]
    lp = [_dot(l_bk[p], stack2(l_bk[p])) for p in pairs]
    x = [x[p] - _dot(l_bk[p], stack2(x[p])) for p in pairs]
    for it in range(5):
        x = [x[p] + _dot(lp[p], stack2(x[p])) for p in pairs]
        if it < 4:
            lp = [_dot(lp[p], stack2(lp[p])) for p in pairs]
    o = [xs[p][c:] + _dot(jnp.concatenate([a_rk[p], -a_rb[p]], axis=1), jnp.concatenate([v2[p], stack2(x[p])], axis=0))
         for p in pairs]
    upd = [_dot_tn(jnp.concatenate([vv[:, sls[p]], -x[p]], axis=0),
                   jnp.concatenate([kb[:, sls[p]], bb[:, sls[p]]], axis=0)) for p in pairs]
    for p in pairs:
        o_ref[:, sls[p]] = o[p]
        s_ref[p] = s[p] * e_tot[:, sls[p]] + jnp.where(blockdiag, upd[p], 0.0)


def _rwkv_out_kernel(of_ref, ob_ref, bv_ref, g_ref, lg_ref, lb_ref, e_ref, et_ref, w_ref, h_ref, mod_ref, lng_ref,
                     lnb_ref, o_ref):
    o = of_ref[...] + ob_ref[...]
    inv = 1.0 / RWKV_HEAD
    oc = o - _seg_sum(o, e_ref, et_ref) * inv
    var = _seg_sum(oc * oc, e_ref, et_ref) * inv
    y = oc * lax.rsqrt(var + RWKV_GN_EPS) * lg_ref[...] + lb_ref[...] + bv_ref[...]
    yo = _dot(y * g_ref[...], w_ref[...])
    z = DEEPNORM_ALPHA * h_ref[...] + mod_ref[0][2:3] * yo
    o_ref[...] = _ln_rows(z, lng_ref[...], lnb_ref[...])


def _rwkv_mixer(h, mods, nct, mu, w_in, w0, w_l1, w_l2, a0, a_l1, a_l2, g_l1, g_l2, k_k, k_a, r_k, gn_g, gn_b, w_out,
                ln_g, ln_b):
    t, d = h.shape
    nt = t // TM
    bf = MXU_DT
    lw_ = w_l1.shape[-1]
    la_ = a_l1.shape[-1]
    zw = jnp.zeros((lw_, d), F32)
    za = jnp.zeros((la_, d), F32)
    wl1 = jnp.concatenate([w_l1[0], w_l1[1]], axis=1).astype(bf)
    wl2 = jnp.stack([jnp.concatenate([w_l2[0], zw], 0), jnp.concatenate([zw, w_l2[1]], 0)]).astype(bf)
    al1 = jnp.concatenate([a_l1[0], a_l1[1]], axis=1).astype(bf)
    al2 = jnp.stack([jnp.concatenate([a_l2[0], za], 0), jnp.concatenate([za, a_l2[1]], 0)]).astype(bf)
    head_of = jnp.arange(d) // RWKV_HEAD
    e = (head_of[:, None] == jnp.arange(LANES)[None, :]).astype(bf)
    et = e.T
    halo_p = pl.BlockSpec((8, d), lambda i: (jnp.maximum(i * (TM // 8) - 1, 0), 0))
    halo_n = pl.BlockSpec((8, d), lambda i: (jnp.minimum((i + 1) * (TM // 8), t // 8 - 1), 0))
    args = [h, h, h, mods, mu, w_in.astype(bf), wl1, wl2, w0, al1, al2, a0, g_l1.astype(bf), g_l2.astype(bf),
            k_k[None], k_a[None], r_k.reshape(1, d), e, et]
    ins = [_row_spec(d), halo_p, halo_n, _mod_spec(nct)] + [_full_spec(a.shape) for a in args[4:]]
    outs = pl.pallas_call(
        functools.partial(_rwkv_prep_kernel, nct=nct, nt=nt), grid=(nt,), in_specs=ins,
        out_specs=[_row_spec(d)] * 11,
        out_shape=[jax.ShapeDtypeStruct((t, d), dt) for dt in [ACT_DT] * 5 + [F32] * 2 + [ACT_DT] * 4],
        compiler_params=_cparams("arbitrary"), name="rwkv_prep",
    )(*args)
    r, v, kk, g, bv, lw0, lw1, kt0, kt1, ab0, ab1 = outs
    c = RWKV_CHUNK
    ncc, nc = nct * (TM // c), t // c
    o_dir = []
    for d_, (lw, kt, ab) in enumerate(((lw0, kt0, ab0), (lw1, kt1, ab1))):
        reverse = d_ == 1
        spec = pl.BlockSpec((c, d), lambda g_, reverse=reverse: (_tile_of(g_, ncc, nc, reverse), 0))
        o_dir.append(pl.pallas_call(
            functools.partial(_rwkv_scan_kernel, reverse=reverse), grid=(nc,), in_specs=[spec] * 6, out_specs=spec,
            out_shape=jax.ShapeDtypeStruct((t, d), F32),
            scratch_shapes=[pltpu.VMEM((d // LANES, LANES, LANES), F32)],
            compiler_params=_cparams("arbitrary"), name="rwkv_scan_%d" % d_,
        )(r, v, kk, lw, kt, ab))
    args = [o_dir[0], o_dir[1], bv, g, gn_g[None], gn_b[None], e, et, w_out.astype(bf), h, mods, ln_g[None], ln_b[None]]
    ins = [_row_spec(d)] * 4 + [_full_spec(a.shape) for a in args[4:9]] + [_row_spec(d), _mod_spec(nct),
                                                                          _full_spec((1, d)), _full_spec((1, d))]
    return pl.pallas_call(
        _rwkv_out_kernel, grid=(nt,), in_specs=ins, out_specs=_row_spec(d),
        out_shape=jax.ShapeDtypeStruct((t, d), F32), compiler_params=_cparams("arbitrary"), name="rwkv_out",
    )(*args)


def _ret_in_kernel(h_ref, mod_ref, w_ref, cos_ref, sin_ref, q_o, k_o, v_o, g_o):
    d = D_MODEL
    u = _modulate(h_ref[...], mod_ref[0], 0).astype(MXU_DT)
    q = _dot(u, w_ref[:, 0:d])
    k = _dot(u, w_ref[:, d:2 * d]) * (RET_QK ** -0.5)
    v_o[...] = _dot(u, w_ref[:, 2 * d:4 * d]).astype(v_o.dtype)
    g_o[...] = _silu(_dot(u, w_ref[:, 4 * d:6 * d])).astype(g_o.dtype)
    cos = cos_ref[...]
    sin = sin_ref[...]
    half = RET_QK // 2
    for z, z_o in ((q, q_o), (k, k_o)):
        for hh in range(RET_HEADS):
            lo = z[:, hh * RET_QK:hh * RET_QK + half]
            hi = z[:, hh * RET_QK + half:(hh + 1) * RET_QK]
            zh = jnp.concatenate([lo, hi], axis=1)
            rot = jnp.concatenate([-hi, lo], axis=1)
            z_o[:, hh * RET_QK:(hh + 1) * RET_QK] = (zh * cos + rot * sin).astype(z_o.dtype)


def _ret_scan_kernel(qf_ref, kf_ref, vf_ref, qb_ref, kb_ref, vb_ref, inner_ref, qd_ref, kd_ref, bd_ref, of_ref, ob_ref,
                     r_ref):
    @pl.when(pl.program_id(0) == 0)
    def _():
        r_ref[...] = jnp.zeros_like(r_ref)

    refs = ((qf_ref, kf_ref, vf_ref, of_ref), (qb_ref, kb_ref, vb_ref, ob_ref))
    cells = [(d, hh) for d in range(2) for hh in range(RET_HEADS)]
    qs = lambda hh: slice(hh * RET_QK, (hh + 1) * RET_QK)
    vs = lambda hh: slice(hh * RET_V, (hh + 1) * RET_V)
    q = {(d, hh): refs[d][0][:, qs(hh)] for d, hh in cells}
    k = {(d, hh): refs[d][1][:, qs(hh)] for d, hh in cells}
    v = {(d, hh): refs[d][2][:, vs(hh)] for d, hh in cells}
    state = {c: r_ref[c[0], c[1]] for c in cells}
    scores = {c: _dot_nt(q[c], k[c]) * inner_ref[c[0], c[1]] for c in cells}
    carry_in = {c: _dot(q[c], state[c]) * qd_ref[c[0], c[1]] for c in cells}
    upd = {c: _dot_tn(k[c] * kd_ref[c[0], c[1]], v[c]) for c in cells}
    for c in cells:
        refs[c[0]][3][:, vs(c[1])] = _dot(scores[c], v[c]) + carry_in[c]
        r_ref[c[0], c[1]] = state[c] * bd_ref[c[0], c[1]] + upd[c]


def _ret_out_kernel(of_ref, ob_ref, g_ref, gg_ref, gb_ref, w_ref, h_ref, mod_ref, lng_ref, lnb_ref, o_ref):
    parts = []
    for hh in range(RET_HEADS):
        sl = slice(hh * RET_V, (hh + 1) * RET_V)
        o = of_ref[:, sl] + ob_ref[:, sl]
        mu = jnp.mean(o, axis=-1, keepdims=True)
        oc = o - mu
        var = jnp.mean(oc * oc, axis=-1, keepdims=True)
        y = oc * lax.rsqrt(var + LN_EPS) * gg_ref[:, sl] + gb_ref[:, sl]
        parts.append((g_ref[:, sl] * y).astype(MXU_DT))
    yo = _dot(jnp.concatenate(parts, axis=1), w_ref[...])
    z = DEEPNORM_ALPHA * h_ref[...] + mod_ref[0][2:3] * yo
    o_ref[...] = _ln_rows(z, lng_ref[...], lnb_ref[...])


def _ret_mixer(h, mods, nct, rope_cos, rope_sin, w_in, decay_logit, gn_g, gn_b, w_out, ln_g, ln_b):
    t, d = h.shape
    nt = t // TM
    hv = RET_HEADS * RET_V
    q, k, v, sg = pl.pallas_call(
        _ret_in_kernel, grid=(nt,),
        in_specs=[_row_spec(d), _mod_spec(nct), _full_spec(w_in.shape), _row_spec(RET_QK), _row_spec(RET_QK)],
        out_specs=[_row_spec(d), _row_spec(d), _row_spec(hv), _row_spec(hv)],
        out_shape=[jax.ShapeDtypeStruct((t, w), ACT_DT) for w in (d, d, hv, hv)],
        compiler_params=_cparams("arbitrary"), name="ret_in",
    )(h, mods, w_in.astype(MXU_DT), rope_cos, rope_sin)
    c = RET_CHUNK
    ncc, nc = nct * (TM // c), t // c
    log_gamma = jax.nn.log_sigmoid(decay_logit.astype(F32))
    pos = jnp.arange(c, dtype=F32)
    tabs = []
    for d_ in range(2):
        lg = log_gamma[d_][:, None, None]
        p = (c - 1.0 - pos) if d_ == 1 else pos
        rel = p[:, None] - p[None, :]
        tabs.append((jnp.where(rel >= 0, jnp.exp(jnp.maximum(rel, 0.0) * lg), 0.0),
                     jnp.exp((p + 1.0) * log_gamma[d_][:, None])[:, :, None],
                     jnp.exp((c - 1.0 - p) * log_gamma[d_][:, None])[:, :, None],
                     jnp.exp(c * log_gamma[d_])[:, None, None]))
    inner, q_dec, k_dec, blk_dec = (jnp.stack(z) for z in zip(*tabs))
    cs = lambda w, reverse: pl.BlockSpec((c, w), lambda g_: (_tile_of(g_, ncc, nc, reverse), 0))
    o_dir = pl.pallas_call(
        _ret_scan_kernel, grid=(nc,),
        in_specs=[cs(d, False), cs(d, False), cs(hv, False), cs(d, True), cs(d, True), cs(hv, True),
                  _full_spec(inner.shape), _full_spec(q_dec.shape), _full_spec(k_dec.shape), _full_spec(blk_dec.shape)],
        out_specs=[cs(hv, False), cs(hv, True)], out_shape=[jax.ShapeDtypeStruct((t, hv), F32)] * 2,
        scratch_shapes=[pltpu.VMEM((2, RET_HEADS, RET_QK, RET_V), F32)],
        compiler_params=_cparams("arbitrary"), name="ret_scan",
    )(q, k, v, q, k, v, inner, q_dec, k_dec, blk_dec)
    return pl.pallas_call(
        _ret_out_kernel, grid=(nt,),
        in_specs=[_row_spec(hv)] * 3 + [_full_spec((1, hv)), _full_spec((1, hv)), _full_spec((hv, d)), _row_spec(d),
                                        _mod_spec(nct), _full_spec((1, d)), _full_spec((1, d))],
        out_specs=_row_spec(d), out_shape=jax.ShapeDtypeStruct((t, d), F32),
        compiler_params=_cparams("arbitrary"), name="ret_out",
    )(o_dir[0], o_dir[1], sg, gn_g[None], gn_b[None], w_out.astype(MXU_DT), h, mods, ln_g[None], ln_b[None])


def _hgrn_in_kernel(h_ref, mod_ref, w_ref, lb_ref, bf_ref, q_o, v_o, g_o, f0_o, f1_o):
    d = D_MODEL
    u = _modulate(h_ref[...], mod_ref[0], 0).astype(MXU_DT)
    lb = lb_ref[...]
    q_o[...] = _silu(_dot(u, w_ref[:, 0:d])).astype(q_o.dtype)
    f0_o[...] = lb + (1.0 - lb) * jax.nn.sigmoid(_dot(u, w_ref[:, d:2 * d]) + bf_ref[0:1, :])
    f1_o[...] = lb + (1.0 - lb) * jax.nn.sigmoid(_dot(u, w_ref[:, 2 * d:3 * d]) + bf_ref[1:2, :])
    v_o[...] = _dot(u, w_ref[:, 3 * d:4 * d]).astype(v_o.dtype)
    g_o[...] = _silu(_dot(u, w_ref[:, 4 * d:5 * d])).astype(g_o.dtype)


def _hgrn_scan_kernel(qf_ref, vf_ref, ff_ref, qb_ref, vb_ref, fb_ref, of_ref, ob_ref, s_ref, b_s, rb_s, rk_s, rv_s):
    hb = HGRN_BLOCK
    nb = TM // hb
    half = hb // 2
    dirs = ((qf_ref, vf_ref, ff_ref, of_ref, False), (qb_ref, vb_ref, fb_ref, ob_ref, True))

    @pl.when(pl.program_id(0) == 0)
    def _():
        s_ref[...] = jnp.zeros_like(s_ref)

    ri = lax.broadcasted_iota(jnp.int32, (TM, TM), 0)
    ci = lax.broadcasted_iota(jnp.int32, (TM, TM), 1)
    same_block = (ri // hb) == (ci // hb)
    ti = lax.broadcasted_iota(jnp.int32, (half, 1), 0)
    rowi = lax.broadcasted_iota(jnp.int32, (hb, 1), 0)
    heads = range(HGRN_HEADS)
    sls = [slice(hh * HGRN_HEAD, (hh + 1) * HGRN_HEAD) for hh in heads]
    cells = [(d, hh) for d in range(2) for hh in heads]

    for d, (q_ref, v_ref, f_ref, o_ref, reverse) in enumerate(dirs):
        tri = jnp.where(jnp.logical_and(same_block, (ci >= ri) if reverse else (ci <= ri)), 1.0, 0.0)
        b_s[d] = _dot_sel(tri, jnp.log(f_ref[...]), 3)

    def block(bi, par):
        pre = []
        for d, (q_ref, v_ref, f_ref, o_ref, reverse) in enumerate(dirs):
            blk = (nb - 1 - bi) if reverse else bi
            r0 = pl.multiple_of(blk * hb, hb)
            kx = 1.0 - f_ref[pl.ds(r0, hb), :]
            q = q_ref[pl.ds(r0, hb), :].astype(F32)
            v = v_ref[pl.ds(r0, hb), :].astype(F32)
            b = b_s[d, pl.ds(r0, hb), :]
            rb_s[d, par] = b
            rk_s[d, par] = kx
            rv_s[d, par] = v
            tot = b[0:1, :] if reverse else b[hb - 1:hb, :]
            first = (rowi >= half) if reverse else (rowi < half)
            beta = b[half:half + 1, :] if reverse else b[half - 1:half, :]
            pre.append(dict(
                r0=r0, q=q, v=v, b=b, qe=q * jnp.exp(b), kb=kx * jnp.exp(tot - b), e_tot=jnp.exp(tot),
                k_first=kx * jnp.exp(jnp.where(first, beta - b, -jnp.inf)),
                q_second=q * jnp.exp(jnp.where(first, -jnp.inf, b - beta)),
                causal=[(ti <= si) if reverse else (ti >= si) for si in range(half)]))
        s = {c: s_ref[c[0], c[1]] for c in cells}
        m_first = {(d, hh): _dot_tn(pre[d]['k_first'][:, sls[hh]], pre[d]['v'][:, sls[hh]]) for d, hh in cells}
        o = {(d, hh): _dot_nt(pre[d]['qe'][:, sls[hh]], s[d, hh]) + _dot(pre[d]['q_second'][:, sls[hh]], m_first[d, hh])
             for d, hh in cells}
        upd = {(d, hh): _dot_tn(pre[d]['v'][:, sls[hh]], pre[d]['kb'][:, sls[hh]]) for d, hh in cells}
        for d, hh in cells:
            sl = sls[hh]
            p = pre[d]
            parts = []
            for lo in (0, half):
                bt = p['b'][lo:lo + half, sl]
                qt = p['q'][lo:lo + half, sl]
                acc = jnp.zeros((half, HGRN_HEAD), F32)
                for si in range(half):
                    row = slice(lo + si, lo + si + 1)
                    dec = jnp.exp(jnp.where(p['causal'][si], bt - rb_s[d, par, row, sl], -jnp.inf))
                    sc = jnp.sum(qt * rk_s[d, par, row, sl] * dec, axis=-1, keepdims=True)
                    acc = acc + sc * rv_s[d, par, row, sl]
                parts.append(acc)
            dirs[d][3][pl.ds(p['r0'], hb), sl] = o[d, hh] + jnp.concatenate(parts, axis=0)
            s_ref[d, hh] = s[d, hh] * p['e_tot'][:, sl] + upd[d, hh]

    def two_blocks(bj, carry):
        block(2 * bj, 0)
        block(2 * bj + 1, 1)
        return carry

    lax.fori_loop(0, nb // 2, two_blocks, 0)


def _hgrn_out_kernel(of_ref, ob_ref, g_ref, ng_ref, w_ref, h_ref, mod_ref, lng_ref, lnb_ref, o_ref):
    parts = []
    for hh in range(HGRN_HEADS):
        sl = slice(hh * HGRN_HEAD, (hh + 1) * HGRN_HEAD)
        o = of_ref[:, sl] + ob_ref[:, sl]
        y = o * lax.rsqrt(jnp.mean(o * o, axis=-1, keepdims=True) + LN_EPS) * ng_ref[...]
        parts.append((y * g_ref[:, sl]).astype(MXU_DT))
    yo = _dot(jnp.concatenate(parts, axis=1), w_ref[...])
    z = DEEPNORM_ALPHA * h_ref[...] + mod_ref[0][2:3] * yo
    o_ref[...] = _ln_rows(z, lng_ref[...], lnb_ref[...])


def _hgrn_mixer(h, mods, nct, lb, w_in, b_f, norm_g, w_out, ln_g, ln_b):
    t, d = h.shape
    nt = t // TM
    q, v, sg, f0, f1 = pl.pallas_call(
        _hgrn_in_kernel, grid=(nt,),
        in_specs=[_row_spec(d), _mod_spec(nct), _full_spec(w_in.shape), _full_spec((1, d)), _full_spec((2, d))],
        out_specs=[_row_spec(d)] * 5,
        out_shape=[jax.ShapeDtypeStruct((t, d), dt) for dt in [ACT_DT] * 3 + [F32] * 2],
        compiler_params=_cparams("arbitrary"), name="hgrn_in",
    )(h, mods, w_in.astype(MXU_DT), lb[None], b_f)
    fwd = pl.BlockSpec((TM, d), lambda g_: (g_, 0))
    bwd = pl.BlockSpec((TM, d), lambda g_: (_tile_of(g_, nct, nt, True), 0))
    o_dir = pl.pallas_call(
        _hgrn_scan_kernel, grid=(nt,), in_specs=[fwd] * 3 + [bwd] * 3, out_specs=[fwd, bwd],
        out_shape=[jax.ShapeDtypeStruct((t, d), F32)] * 2,
        scratch_shapes=[pltpu.VMEM((2, HGRN_HEADS, HGRN_HEAD, HGRN_HEAD), F32)] + [pltpu.VMEM((2, TM, d), F32)]
        + [pltpu.VMEM((2, 2, HGRN_BLOCK, d), F32)] * 3,
        compiler_params=_cparams("arbitrary"), name="hgrn_scan",
    )(q, v, f0, q, v, f1)
    return pl.pallas_call(
        _hgrn_out_kernel, grid=(nt,),
        in_specs=[_row_spec(d)] * 3 + [_full_spec((1, HGRN_HEAD)), _full_spec((d, d)), _row_spec(d), _mod_spec(nct),
                                       _full_spec((1, d)), _full_spec((1, d))],
        out_specs=_row_spec(d), out_shape=jax.ShapeDtypeStruct((t, d), F32),
        compiler_params=_cparams("arbitrary"), name="hgrn_out",
    )(o_dir[0], o_dir[1], sg, norm_g[None], w_out.astype(MXU_DT), h, mods, ln_g[None], ln_b[None])


def _router_kernel(h_ref, mod_ref, rw_ref, rb_ref, u_o, gate_o, rank_o, x_o):
    u = _modulate(h_ref[...], mod_ref[0], 3)
    u_o[...] = u.astype(u_o.dtype)
    w_hi, w_lo = _split(rw_ref[...], 2)
    u_hi, u_lo = _split(u, 2)
    nt_dims = (((1,), (1,)), ((), ()))
    logits = (lax.dot_general(w_hi, u_hi, nt_dims, preferred_element_type=F32)
              + lax.dot_general(w_hi, u_lo, nt_dims, preferred_element_type=F32)
              + lax.dot_general(w_lo, u_hi, nt_dims, preferred_element_type=F32))
    ne, gs = N_EXPERTS, N_EXPERTS // N_GROUPS
    neg = -jnp.inf
    scores = jax.nn.sigmoid(logits[:ne])
    choice = scores + rb_ref[:ne]
    c3 = choice.reshape(N_GROUPS, gs, TM)
    mi = lax.broadcasted_iota(jnp.int32, c3.shape, 1).astype(F32)
    m1 = jnp.max(c3, axis=1, keepdims=True)
    i1 = jnp.min(jnp.where(c3 == m1, mi, float(gs)), axis=1, keepdims=True)
    m2 = jnp.max(jnp.where(mi == i1, neg, c3), axis=1, keepdims=True)
    gscore = m1 + m2
    gi = lax.broadcasted_iota(jnp.int32, gscore.shape, 0).astype(F32)
    gsel = jnp.zeros(gscore.shape, F32)
    for _ in range(TOPK_GROUPS):
        gm = jnp.max(gscore, axis=0, keepdims=True)
        pick = gi == jnp.min(jnp.where(gscore == gm, gi, float(N_GROUPS)), axis=0, keepdims=True)
        gsel = jnp.where(pick, 1.0, gsel)
        gscore = jnp.where(pick, neg, gscore)
    emask = jnp.broadcast_to(gsel, c3.shape).reshape(ne, TM)
    masked = jnp.where(emask > 0.5, choice, neg)
    ei = lax.broadcasted_iota(jnp.int32, masked.shape, 0).astype(F32)
    chosen = jnp.zeros(masked.shape, F32)
    for _ in range(TOP_K):
        em = jnp.max(masked, axis=0, keepdims=True)
        pick = ei == jnp.min(jnp.where(masked == em, ei, float(ne)), axis=0, keepdims=True)
        chosen = jnp.where(pick, 1.0, chosen)
        masked = jnp.where(pick, neg, masked)
    top_w = scores * chosen
    gates = ROUTED_SCALE * top_w / jnp.sum(top_w, axis=0, keepdims=True)
    ti = lax.broadcasted_iota(jnp.int32, (TM, TM), 0)
    tj = lax.broadcasted_iota(jnp.int32, (TM, TM), 1)
    before = jnp.where(ti < tj, 1.0, 0.0).astype(MXU_DT)
    prefix = jnp.dot(chosen.astype(MXU_DT), before, preferred_element_type=F32)
    rank = jnp.where(chosen > 0.5, prefix, -1.0)
    gate_o[0] = gates
    rank_o[0] = rank
    cap = MOE_CAP
    slot = lax.broadcasted_iota(jnp.int32, (cap, TM), 0).astype(F32)
    ub = u.astype(MXU_DT)
    for g0 in range(0, ne, MOE_EGROUP):
        onehot = jnp.concatenate([jnp.where(slot == rank[e:e + 1, :], 1.0, 0.0).astype(MXU_DT)
                                  for e in range(g0, g0 + MOE_EGROUP)], axis=0)
        xg = jnp.dot(onehot, ub, preferred_element_type=F32)
        x_o[0, g0:g0 + MOE_EGROUP] = xg.reshape(MOE_EGROUP, cap, D_MODEL).astype(x_o.dtype)


def _expert_kernel(x_ref, wgu_ref, wd_ref, y_ref, wgu_s, wd_s):
    @pl.when(pl.program_id(1) == 0)
    def _():
        wgu_s[...] = wgu_ref[0, 0].astype(MXU_DT)
        wd_s[...] = wd_ref[0, 0].astype(MXU_DT)

    g = x_ref.shape[0]
    ed = EXPERT_DIM
    x = x_ref[...].reshape(g * MOE_CAP, D_MODEL)
    gu = _dot(x, wgu_s[...])
    y = _dot(_silu(gu[:, :ed]) * gu[:, ed:], wd_s[...])
    y_ref[...] = y.reshape(g, 1, MOE_CAP, D_MODEL).astype(y_ref.dtype)


def _combine_kernel(u_ref, gt_ref, rt_ref, y_ref, sgu_ref, sd_ref, h_ref, mod_ref, lng_ref, lnb_ref, *rest, extra):
    if extra:
        ex_ref, o_ref = rest
    else:
        (o_ref,) = rest
    ed = EXPERT_DIM
    cap = MOE_CAP
    gu = _dot(u_ref[...], sgu_ref[...])
    acc = _dot(_silu(gu[:, :ed]) * gu[:, ed:], sd_ref[...])
    if extra:
        acc = acc + ex_ref[...]
    slot = lax.broadcasted_iota(jnp.int32, (cap, TM), 0).astype(F32)
    for g0 in range(0, N_EXPERTS, MOE_EGROUP):
        pw = jnp.concatenate([jnp.where(slot == rt_ref[0, e:e + 1, :], gt_ref[0, e:e + 1, :], 0.0).astype(MXU_DT)
                              for e in range(g0, g0 + MOE_EGROUP)], axis=0)
        yg = y_ref[0, g0:g0 + MOE_EGROUP].reshape(MOE_EGROUP * cap, D_MODEL)
        acc = acc + _dot_tn(pw, yg)
    z = DEEPNORM_ALPHA * h_ref[...] + mod_ref[0][5:6] * acc
    o_ref[...] = _ln_rows(z, lng_ref[...], lnb_ref[...])


def _overflow_kernel(tile_ref, exp_ref, nr_ref, n_ref, u_ref, gate_ref, rank_ref, wgu_ref, wd_ref, zero_ref, o_ref):
    del zero_ref
    s = pl.program_id(0)
    tile = tile_ref[s]
    e = exp_ref[s]
    ed = EXPERT_DIM
    cap = MOE_CAP
    active = s < n_ref[0]
    first = jnp.logical_or(s == 0, tile_ref[jnp.maximum(s - 1, 0)] != tile)

    @pl.when(jnp.logical_and(active, first))
    def _():
        o_ref[...] = jnp.zeros_like(o_ref)

    @pl.when(active)
    def _():
        wgu = wgu_ref[0, 0].astype(MXU_DT)
        wd = wd_ref[0, 0].astype(MXU_DT)
        rank = rank_ref[0, pl.ds(e, 1), :]
        gate = gate_ref[0, pl.ds(e, 1), :]

        def one_round(r, carry):
            slot = lax.broadcasted_iota(jnp.int32, (cap, TM), 0).astype(F32) + (r * cap).astype(F32)
            hit = slot == rank
            x = _dot(jnp.where(hit, 1.0, 0.0), u_ref[...])
            gu = _dot(x, wgu)
            y = _dot(_silu(gu[:, :ed]) * gu[:, ed:], wd)
            o_ref[...] += _dot_tn(jnp.where(hit, gate, 0.0), y)
            return carry

        lax.fori_loop(1, nr_ref[s], one_round, 0)


def _moe_layer(h, mods, nct, layer, router_w, router_b, w_gu, w_down, sh_gu, sh_down, ln_g, ln_b):
    t, d = h.shape
    nt = t // TM
    ne, cap = N_EXPERTS, MOE_CAP
    rw = jnp.concatenate([router_w.T, jnp.zeros((LANES - ne, d), F32)], axis=0)
    rb = jnp.concatenate([router_b, jnp.zeros((LANES - ne,), F32)])[:, None]
    per_tile = pl.BlockSpec((1, ne, TM), lambda i: (i, 0, 0))
    slots = pl.BlockSpec((1, ne, cap, d), lambda i: (i, 0, 0, 0))
    u, gates, ranks, xs = pl.pallas_call(
        _router_kernel, grid=(nt,),
        in_specs=[_row_spec(d), _mod_spec(nct), _full_spec((LANES, d)), _full_spec((LANES, 1))],
        out_specs=[_row_spec(d), per_tile, per_tile, slots],
        out_shape=[jax.ShapeDtypeStruct((t, d), MXU_DT), jax.ShapeDtypeStruct((nt, ne, TM), F32),
                   jax.ShapeDtypeStruct((nt, ne, TM), F32), jax.ShapeDtypeStruct((nt, ne, cap, d), MXU_DT)],
        compiler_params=_cparams("arbitrary"), name="moe_router",
    )(h, mods, rw, rb)

    run = max(g for g in range(1, MOE_RUN + 1) if nt % g == 0)
    ys = pl.pallas_call(
        _expert_kernel, grid=(ne, nt // run),
        in_specs=[pl.BlockSpec((run, 1, cap, d), lambda e, c: (c, e, 0, 0)),
                  pl.BlockSpec((1, 1, d, 2 * EXPERT_DIM), lambda e, c: (layer, e, 0, 0)),
                  pl.BlockSpec((1, 1, EXPERT_DIM, d), lambda e, c: (layer, e, 0, 0))],
        out_specs=pl.BlockSpec((run, 1, cap, d), lambda e, c: (c, e, 0, 0)),
        out_shape=jax.ShapeDtypeStruct((nt, ne, cap, d), MXU_DT),
        scratch_shapes=[pltpu.VMEM((d, 2 * EXPERT_DIM), MXU_DT), pltpu.VMEM((EXPERT_DIM, d), MXU_DT)],
        compiler_params=_cparams("arbitrary", "arbitrary"), name="moe_experts",
    )(xs, w_gu, w_down)

    sgu, sd = sh_gu.astype(MXU_DT), sh_down.astype(MXU_DT)
    base_specs = [_row_spec(d), per_tile, per_tile, slots, _full_spec(sgu.shape), _full_spec(sd.shape),
                  _row_spec(d), _mod_spec(nct), _full_spec((1, d)), _full_spec((1, d))]
    base_args = (u, gates, ranks, ys, sgu, sd, h, mods, ln_g[None], ln_b[None])

    def combine(*extra):
        return pl.pallas_call(
            functools.partial(_combine_kernel, extra=bool(extra)), grid=(nt,),
            in_specs=base_specs + [_row_spec(d)] * len(extra), out_specs=_row_spec(d),
            out_shape=jax.ShapeDtypeStruct((t, d), F32), compiler_params=_cparams("arbitrary"), name="moe_combine",
        )(*base_args, *extra)

    count = (jnp.max(ranks, axis=-1).astype(jnp.int32) + 1).reshape(-1)
    over = count > cap
    n_over = jnp.sum(over.astype(jnp.int32))

    def with_overflow(size):
        def run():
            idx = jnp.nonzero(over, size=size, fill_value=0)[0].astype(jnp.int32)
            idx = jnp.where(jnp.arange(size) < n_over, idx, idx[jnp.maximum(n_over - 1, 0)])
            tiles, exps = idx // ne, idx % ne
            rounds = (count[idx] + cap - 1) // cap
            grid_spec = pltpu.PrefetchScalarGridSpec(
                num_scalar_prefetch=4, grid=(size,),
                in_specs=[pl.BlockSpec((TM, d), lambda s, tl, ex, nr, n: (tl[s], 0)),
                          pl.BlockSpec((1, ne, TM), lambda s, tl, ex, nr, n: (tl[s], 0, 0)),
                          pl.BlockSpec((1, ne, TM), lambda s, tl, ex, nr, n: (tl[s], 0, 0)),
                          pl.BlockSpec((1, 1, d, 2 * EXPERT_DIM), lambda s, tl, ex, nr, n: (layer, ex[s], 0, 0)),
                          pl.BlockSpec((1, 1, EXPERT_DIM, d), lambda s, tl, ex, nr, n: (layer, ex[s], 0, 0)),
                          pl.BlockSpec(memory_space=pl.ANY)],
                out_specs=pl.BlockSpec((TM, d), lambda s, tl, ex, nr, n: (tl[s], 0)))
            extra = pl.pallas_call(
                _overflow_kernel, grid_spec=grid_spec, out_shape=jax.ShapeDtypeStruct((t, d), F32),
                input_output_aliases={9: 0}, compiler_params=_cparams("arbitrary"), name="moe_overflow",
            )(tiles, exps, rounds, n_over[None], u, gates, ranks, w_gu, w_down, jnp.zeros((t, d), F32))
            return combine(extra)
        return run

    sizes = sorted({min(MOE_OVER_STEPS, nt * ne), nt * ne})
    branch = sum((n_over > sz).astype(jnp.int32) for sz in [0] + sizes[:-1])
    return lax.switch(branch, [combine] + [with_overflow(sz) for sz in sizes])


def kernel(x, c, ctx, c_ctx, ada_w, ada_b, post_ln_g, post_ln_b, lru_w_in, lru_conv_w, lru_conv_b, lru_gate_w, lru_gate_b, lru_lambda, lru_w_out, rwkv_mu, rwkv_w_in, rwkv_w0, rwkv_w_l1, rwkv_w_l2, rwkv_a0, rwkv_a_l1, rwkv_a_l2, rwkv_g_l1, rwkv_g_l2, rwkv_k_k, rwkv_k_a, rwkv_r_k, rwkv_ln_g, rwkv_ln_b, rwkv_w_out, ret_w_in, ret_decay, ret_gn_g, ret_gn_b, ret_w_out, hgrn_w_in, hgrn_b_f, hgrn_lb, hgrn_norm_g, hgrn_w_out, moe_router, moe_bias, moe_w_gu, moe_w_down, moe_sh_gu, moe_sh_down):
    assert x.shape[0] == 1 and ctx.shape[0] == 1
    n_ctx, n_lat, d = ctx.shape[1], x.shape[1], x.shape[2]
    assert n_ctx % TM == 0 and n_lat % TM == 0 and d == D_MODEL
    nct = n_ctx // TM
    rows = n_lat // GRID_W
    pos_row = jnp.repeat(jnp.arange(rows, dtype=F32), GRID_W)
    pos_col = jnp.tile(jnp.arange(GRID_W, dtype=F32), rows)
    n_freq = RET_QK // 4
    freqs = ROPE_BASE ** (-jnp.arange(n_freq, dtype=F32) / n_freq)
    ang = jnp.concatenate([pos_row[:, None] * freqs, pos_col[:, None] * freqs], axis=-1)
    ang = jnp.concatenate([ang, ang], axis=-1)
    rope_cos = jnp.concatenate([jnp.ones((n_ctx, RET_QK), F32), jnp.cos(ang)], axis=0)
    rope_sin = jnp.concatenate([jnp.zeros((n_ctx, RET_QK), F32), jnp.sin(ang)], axis=0)
    lb_cum = jnp.cumsum(jax.nn.softmax(hgrn_lb.astype(F32), axis=0), axis=0)

    cond = jnp.concatenate([c_ctx[None], c, jnp.zeros((6, d), F32)], axis=0)
    mods_all = _ada_mods(cond, ada_w, ada_b)
    h = jnp.concatenate([ctx[0], x[0]], axis=0)
    for i in range(DEPTH):
        kind, j = i % N_MIXERS, i // N_MIXERS
        mods = mods_all[i]
        lng, lnb = post_ln_g[i, 0], post_ln_b[i, 0]
        if kind == 0:
            h = _lru_mixer(h, mods, nct, lru_w_in[j], lru_conv_w[j], lru_conv_b[j], lru_gate_w[j], lru_gate_b[j],
                           lru_lambda[j], lru_w_out[j], lng, lnb)
        elif kind == 1:
            h = _rwkv_mixer(h, mods, nct, rwkv_mu[j], rwkv_w_in[j], rwkv_w0[j], rwkv_w_l1[j], rwkv_w_l2[j], rwkv_a0[j],
                            rwkv_a_l1[j], rwkv_a_l2[j], rwkv_g_l1[j], rwkv_g_l2[j], rwkv_k_k[j], rwkv_k_a[j],
                            rwkv_r_k[j], rwkv_ln_g[j], rwkv_ln_b[j], rwkv_w_out[j], lng, lnb)
        elif kind == 2:
            h = _ret_mixer(h, mods, nct, rope_cos, rope_sin, ret_w_in[j], ret_decay[j], ret_gn_g[j], ret_gn_b[j],
                           ret_w_out[j], lng, lnb)
        else:
            h = _hgrn_mixer(h, mods, nct, lb_cum[i] - lb_cum[0], hgrn_w_in[j], hgrn_b_f[j], hgrn_norm_g[j],
                            hgrn_w_out[j], lng, lnb)
        h = _moe_layer(h, mods, nct, i, moe_router[i], moe_bias[i], moe_w_gu, moe_w_down, moe_sh_gu[i],
                       moe_sh_down[i], post_ln_g[i, 1], post_ln_b[i, 1])
    return h[n_ctx:][None]
```

```python
import math
import functools
import jax
import jax.numpy as jnp
from jax import lax
from jax.experimental import pallas as pl
from jax.experimental.pallas import tpu as pltpu

F32 = jnp.float32
MXU_DT = jnp.bfloat16
ACT_DT = jnp.bfloat16
LANES = 128
TM = 256
VMEM_LIMIT = 56 * 2 ** 20

D_MODEL = 1024
DEPTH = 4
GRID_W = 64
N_MIXERS = 4
DEEPNORM_ALPHA = (2.0 * DEPTH) ** 0.25
LN_EPS = 1e-5
LRU_WIDTH = D_MODEL
LRU_BLOCKS = 16
LRU_BLOCK = LRU_WIDTH // LRU_BLOCKS
LRU_C = 8.0
RWKV_HEAD = 64
RWKV_HEADS = D_MODEL // RWKV_HEAD
RWKV_DECAY_SCALE = math.exp(-0.5)
RWKV_GN_EPS = 64e-5
RWKV_CHUNK = 64
RET_HEADS = 4
RET_QK = D_MODEL // RET_HEADS
RET_V = 2 * RET_QK
RET_CHUNK = 128
ROPE_BASE = 10000.0
HGRN_HEADS = 8
HGRN_HEAD = D_MODEL // HGRN_HEADS
HGRN_BLOCK = 16
N_EXPERTS = 64
TOP_K = 8
N_GROUPS = 8
TOPK_GROUPS = 4
EXPERT_DIM = 256
ROUTED_SCALE = 2.5
MOE_OVER_STEPS = 256
MOE_CAP = 64
MOE_EGROUP = 8
MOE_RUN = 13


def _cparams(*sem):
    return pltpu.CompilerParams(dimension_semantics=sem, vmem_limit_bytes=VMEM_LIMIT)


def _dot(a, b):
    return jnp.dot(a.astype(MXU_DT), b.astype(MXU_DT), preferred_element_type=F32)


def _dot_nt(a, b):
    return lax.dot_general(a.astype(MXU_DT), b.astype(MXU_DT), (((1,), (1,)), ((), ())), preferred_element_type=F32)


def _dot_tn(a, b):
    return lax.dot_general(a.astype(MXU_DT), b.astype(MXU_DT), (((0,), (0,)), ((), ())), preferred_element_type=F32)


def _split(x, n):
    parts = []
    for _ in range(n):
        p = x.astype(MXU_DT)
        parts.append(p)
        x = x - p.astype(F32)
    return parts


def _dot_sel(sel, x, n):
    return sum(jnp.dot(sel.astype(MXU_DT), p, preferred_element_type=F32) for p in _split(x, n))


def _dot_xsel(x, sel, n):
    return sum(jnp.dot(p, sel.astype(MXU_DT), preferred_element_type=F32) for p in _split(x, n))


def _modulate(h, m, shift_idx):
    return h * (1.0 + m[shift_idx + 1:shift_idx + 2]) + m[shift_idx:shift_idx + 1]


def _ln_rows(z, g, b):
    mu = jnp.mean(z, axis=-1, keepdims=True)
    zc = z - mu
    var = jnp.mean(zc * zc, axis=-1, keepdims=True)
    return zc * lax.rsqrt(var + LN_EPS) * g + b


def _silu(x):
    return x * jax.nn.sigmoid(x)


def _shift_down(x, first_row):
    rows = lax.broadcasted_iota(jnp.int32, (x.shape[0], 1), 0)
    return jnp.where(rows == 0, first_row, pltpu.roll(x, 1, 0))


def _shift_up(x, last_row):
    n = x.shape[0]
    rows = lax.broadcasted_iota(jnp.int32, (n, 1), 0)
    return jnp.where(rows == n - 1, last_row, pltpu.roll(x, n - 1, 0))


def _tile_of(g, nct, nt, reverse):
    if not reverse:
        return g
    return jnp.where(g < nct, nct - 1 - g, nt - 1 - (g - nct))


def _halo_flags(t, nct, nt):
    prev_ok = jnp.logical_and(t != 0, t != nct).astype(F32)
    next_ok = jnp.logical_and(t != nct - 1, t != nt - 1).astype(F32)
    return prev_ok, next_ok


def _ada_kernel(s_ref, w_ref, b_ref, o_ref):
    o_ref[0] = _dot(_silu(s_ref[...]), w_ref[0]) + b_ref[0]


def _ada_mods(cond, ada_w, ada_b):
    nl, d, n6 = ada_w.shape
    out = pl.pallas_call(
        _ada_kernel, grid=(nl, n6 // d),
        in_specs=[pl.BlockSpec((8, d), lambda l, j: (0, 0)),
                  pl.BlockSpec((1, d, d), lambda l, j: (l, 0, j)),
                  pl.BlockSpec((1, 1, d), lambda l, j: (l, 0, j))],
        out_specs=pl.BlockSpec((1, 8, d), lambda l, j: (l, 0, j)),
        out_shape=jax.ShapeDtypeStruct((nl, 8, n6), F32),
        compiler_params=_cparams("arbitrary", "arbitrary"), name="ada_mods",
    )(cond, ada_w, ada_b.reshape(nl, 1, n6))
    return out[:, :2].reshape(nl, 2, 6, d)


def _row_spec(width, tm=TM):
    return pl.BlockSpec((tm, width), lambda i: (i, 0))


def _full_spec(shape):
    nd = len(shape)
    return pl.BlockSpec(tuple(shape), lambda *_: (0,) * nd)


def _mod_spec(nct):
    return pl.BlockSpec((1, 6, D_MODEL), lambda i: (jnp.minimum(i // nct, 1), 0, 0))


def _lru_out_kernel(g_ref, hf_ref, hb_ref, w_ref, h_ref, mod_ref, lng_ref, lnb_ref, o_ref):
    y = _dot(g_ref[...] * (hf_ref[...] + hb_ref[...]), w_ref[...])
    z = DEEPNORM_ALPHA * h_ref[...] + mod_ref[0][2:3] * y
    o_ref[...] = _ln_rows(z, lng_ref[...], lnb_ref[...])


def _lru_in_kernel(h_ref, mod_ref, w_ref, g_ref, x_ref):
    u = _modulate(h_ref[...], mod_ref[0], 0)
    z = _dot(u, w_ref[...])
    g_ref[...] = jax.nn.gelu(z[:, :LRU_WIDTH], approximate=True).astype(g_ref.dtype)
    x_ref[...] = z[:, LRU_WIDTH:]


def _lru_scan_kernel(xf_ref, xfp_ref, xfn_ref, xb_ref, xbp_ref, xbn_ref, cw_ref, cb_ref, gw_ref, gb_ref, lam_ref,
                     hf_o, hb_o, a_s, b_s, st_s, *, nct, nt):
    g = pl.program_id(0)

    @pl.when(g == 0)
    def _():
        st_s[...] = jnp.zeros_like(st_s)

    cw = cw_ref[...]
    for d, (x_ref, xp_ref, xn_ref) in enumerate(((xf_ref, xfp_ref, xfn_ref), (xb_ref, xbp_ref, xbn_ref))):
        prev_ok, next_ok = _halo_flags(_tile_of(g, nct, nt, d == 1), nct, nt)
        x = x_ref[...]
        xm1 = _shift_down(x, xp_ref[7:8, :] * prev_ok)
        xp1 = _shift_up(x, xn_ref[0:1, :] * next_ok)
        xp2 = _shift_up(xp1, xn_ref[1:2, :] * next_ok)
        xc = cw[0:1] * xm1 + cw[1:2] * x + cw[2:3] * xp1 + cw[3:4] * xp2 + cb_ref[...]
        gates = jax.nn.sigmoid(_dot(xc, gw_ref[d]) + gb_ref[d])
        lam = lam_ref[d:d + 1, :]
        softplus = jnp.maximum(-lam, 0.0) + jnp.log(1.0 + jnp.exp(-jnp.abs(lam)))
        log_a = -LRU_C * gates[:, :LRU_WIDTH] * softplus
        a_s[d] = jnp.exp(log_a)
        b_s[d] = jnp.sqrt(1.0 - jnp.exp(2.0 * log_a)) * (gates[:, LRU_WIDTH:] * xc)

    def row(r, carry):
        hf, hb = carry
        rb = TM - 1 - r
        hf = a_s[0, pl.ds(r, 1), :] * hf + b_s[0, pl.ds(r, 1), :]
        hb = a_s[1, pl.ds(rb, 1), :] * hb + b_s[1, pl.ds(rb, 1), :]
        hf_o[pl.ds(r, 1), :] = hf
        hb_o[pl.ds(rb, 1), :] = hb
        return hf, hb

    hf, hb = lax.fori_loop(0, TM, row, (st_s[0], st_s[1]), unroll=8)
    st_s[0] = hf
    st_s[1] = hb


def _lru_mixer(h, mods, nct, w_in, conv_w, conv_b, gate_w, gate_b, lam, w_out, ln_g, ln_b):
    t, d = h.shape
    nt = t // TM
    w = LRU_WIDTH
    gelu, rnn = pl.pallas_call(
        _lru_in_kernel, grid=(nt,),
        in_specs=[_row_spec(d), _mod_spec(nct), _full_spec((d, 2 * w))],
        out_specs=[_row_spec(w), _row_spec(w)],
        out_shape=[jax.ShapeDtypeStruct((t, w), ACT_DT), jax.ShapeDtypeStruct((t, w), F32)],
        compiler_params=_cparams("arbitrary"), name="lru_in",
    )(h, mods, w_in.astype(MXU_DT))
    eye = jnp.eye(LRU_BLOCKS, dtype=F32)
    gw = jnp.einsum('dgnij,nm->dgnimj', gate_w, eye).reshape(2, 2, w, w)
    gw = jnp.concatenate([gw[:, 0], gw[:, 1]], axis=-1).astype(MXU_DT)
    gb = gate_b.reshape(2, 1, 2 * w)
    def tile_specs(reverse):
        tile = lambda g: _tile_of(g, nct, nt, reverse)
        return [pl.BlockSpec((TM, w), lambda g: (tile(g), 0)),
                pl.BlockSpec((8, w), lambda g: (jnp.maximum(tile(g) * (TM // 8) - 1, 0), 0)),
                pl.BlockSpec((8, w), lambda g: (jnp.minimum((tile(g) + 1) * (TM // 8), t // 8 - 1), 0))]

    fwd, bwd = tile_specs(False), tile_specs(True)
    hf, hb = pl.pallas_call(
        functools.partial(_lru_scan_kernel, nct=nct, nt=nt), grid=(nt,),
        in_specs=fwd + bwd + [_full_spec((4, w)), _full_spec((1, w)), _full_spec((2, w, 2 * w)),
                              _full_spec((2, 1, 2 * w)), _full_spec((2, w))],
        out_specs=[fwd[0], bwd[0]], out_shape=[jax.ShapeDtypeStruct((t, w), F32)] * 2,
        scratch_shapes=[pltpu.VMEM((2, TM, w), F32)] * 2 + [pltpu.VMEM((2, 1, w), F32)],
        compiler_params=_cparams("arbitrary"), name="lru_scan",
    )(rnn, rnn, rnn, rnn, rnn, rnn, conv_w, conv_b[None], gw, gb, lam)
    return pl.pallas_call(
        _lru_out_kernel, grid=(nt,),
        in_specs=[_row_spec(w)] * 3 + [_full_spec((w, d)), _row_spec(d), _mod_spec(nct), _full_spec((1, d)),
                                       _full_spec((1, d))],
        out_specs=_row_spec(d), out_shape=jax.ShapeDtypeStruct((t, d), F32),
        compiler_params=_cparams("arbitrary"), name="lru_out",
    )(gelu, hf, hb, w_out.astype(MXU_DT), h, mods, ln_g[None], ln_b[None])


def _seg_sum(x, e_ref, et_ref):
    s = _dot_xsel(x, e_ref[...], 2)
    return _dot_xsel(s, et_ref[...], 2)


def _rwkv_prep_kernel(h_ref, hp_ref, hn_ref, mod_ref, mu_ref, win_ref, wl1_ref, wl2_ref, w0_ref, al1_ref, al2_ref,
                      a0_ref, gl1_ref, gl2_ref, kk_ref, ka_ref, rk_ref, e_ref, et_ref,
                      r_o, v_o, kk_o, g_o, bv_o, lw0_o, lw1_o, kt0_o, kt1_o, ab0_o, ab1_o, *, nct, nt):
    i = pl.program_id(0)
    prev_ok, next_ok = _halo_flags(i, nct, nt)
    m = mod_ref[0]
    u = _modulate(h_ref[...], m, 0)
    up = _modulate(hp_ref[7:8, :], m, 0) * prev_ok
    un = _modulate(hn_ref[0:1, :], m, 0) * next_ok
    lane = lax.broadcasted_iota(jnp.int32, (1, D_MODEL), 1)
    sh = jnp.where(lane < D_MODEL // 2, _shift_down(u, up), _shift_up(u, un))
    dx = sh - u
    mu = mu_ref[...]
    xm = [u + dx * mu[c:c + 1] for c in range(6)]
    r = _dot(xm[0], win_ref[0])
    k = _dot(xm[1], win_ref[1])
    v = _dot(xm[2], win_ref[2])
    t1 = jnp.tanh(_dot(xm[3], wl1_ref[...]))
    t2 = _dot(xm[4], al1_ref[...])
    g = _dot(jax.nn.sigmoid(_dot(xm[5], gl1_ref[...])), gl2_ref[...])
    kk = k * kk_ref[...]
    kk = kk * lax.rsqrt(_seg_sum(kk * kk, e_ref, et_ref) + 1e-12)
    ktsum = None
    for z, (lw_o, kt_o, ab_o) in enumerate(((lw0_o, kt0_o, ab0_o), (lw1_o, kt1_o, ab1_o))):
        d_w = w0_ref[z:z + 1, :] + _dot(t1, wl2_ref[z])
        lw_o[...] = -RWKV_DECAY_SCALE * jax.nn.sigmoid(d_w)
        a = jax.nn.sigmoid(a0_ref[z:z + 1, :] + _dot(t2, al2_ref[z]))
        kt = k * (1.0 + (a - 1.0) * ka_ref[...])
        kt_o[...] = kt.astype(kt_o.dtype)
        ab_o[...] = (kk * a).astype(ab_o.dtype)
        ktsum = kt if ktsum is None else ktsum + kt
    r_o[...] = r.astype(r_o.dtype)
    v_o[...] = v.astype(v_o.dtype)
    kk_o[...] = kk.astype(kk_o.dtype)
    g_o[...] = g.astype(g_o.dtype)
    bv_o[...] = (_seg_sum(r * ktsum * rk_ref[...], e_ref, et_ref) * v).astype(bv_o.dtype)


def _rwkv_scan_kernel(r_ref, v_ref, kk_ref, lw_ref, kt_ref, ab_ref, o_ref, s_ref, *, reverse):
    c = RWKV_CHUNK

    @pl.when(pl.program_id(0) == 0)
    def _():
        s_ref[...] = jnp.zeros_like(s_ref)

    ri = lax.broadcasted_iota(jnp.int32, (c, c), 0)
    ci = lax.broadcasted_iota(jnp.int32, (c, c), 1)
    incl = (ci >= ri) if reverse else (ci <= ri)
    ri2 = lax.broadcasted_iota(jnp.int32, (c, 2 * c), 0)
    ci2 = jnp.bitwise_and(lax.broadcasted_iota(jnp.int32, (c, 2 * c), 1), c - 1)
    incl2 = (ci2 >= ri2) if reverse else (ci2 <= ri2)
    strict2 = (ci2 > ri2) if reverse else (ci2 < ri2)
    lw = lw_ref[...]
    cl = _dot_sel(jnp.where(incl, 1.0, 0.0), lw, 3)
    tot = cl[0:1, :] if reverse else cl[c - 1:c, :]
    e_in = jnp.exp(cl)
    e_out = jnp.exp(-cl)
    e_end = jnp.exp(tot - cl)
    kk = kk_ref[...].astype(F32)
    kt = kt_ref[...].astype(F32)
    ab = ab_ref[...].astype(F32)
    kap = kk * jnp.exp(cl - lw)
    rh = r_ref[...].astype(F32) * e_in
    kh = kt * e_out
    bh = ab * e_out
    kb = kt * e_end
    bb = ab * e_end
    e_tot = jnp.exp(tot)
    vv = v_ref[...].astype(F32)
    lane_a = lax.broadcasted_iota(jnp.int32, (1, LANES), 1) < RWKV_HEAD
    bi = lax.broadcasted_iota(jnp.int32, (LANES, LANES), 0) < RWKV_HEAD
    bj = lax.broadcasted_iota(jnp.int32, (LANES, LANES), 1) < RWKV_HEAD
    blockdiag = bi == bj

    def stack2(x):
        return jnp.concatenate([jnp.where(lane_a, x, 0.0), jnp.where(lane_a, 0.0, x)], axis=0)

    pairs = range(D_MODEL // LANES)
    sls = [slice(p * LANES, (p + 1) * LANES) for p in pairs]
    s = [s_ref[p] for p in pairs]
    xq = [jnp.concatenate([kap[:, sl], rh[:, sl]], axis=0) for sl in sls]
    yk = [jnp.concatenate([stack2(kh[:, sl]), stack2(bh[:, sl])], axis=0) for sl in sls]
    gm = [_dot_nt(xq[p], yk[p]) for p in pairs]
    xs = [_dot_nt(xq[p], s[p]) for p in pairs]
    l_kk = [jnp.where(strict2, g[:c, :2 * c], 0.0) for g in gm]
    l_bk = [jnp.where(strict2, g[:c, 2 * c:], 0.0) for g in gm]
    a_rk = [jnp.where(incl2, g[c:, :2 * c], 0.0) for g in gm]
    a_rb = [jnp.where(incl2, g[c:, 2 * c:], 0.0) for g in gm]
    v2 = [stack2(vv[:, sl]) for sl in sls]
    x = [xs[p][:c] + _dot(l_kk[p], v2[p]) for p in pairs]
    lp = [_dot(l_bk[p], stack2(l_bk[p])) for p in pairs]
    x = [x[p] - _dot(l_bk[p], stack2(x[p])) for p in pairs]
    for it in range(5):
        x = [x[p] + _dot(lp[p], stack2(x[p])) for p in pairs]
        if it < 4:
            lp = [_dot(lp[p], stack2(lp[p])) for p in pairs]
    o = [xs[p][c:] + _dot(jnp.concatenate([a_rk[p], -a_rb[p]], axis=1), jnp.concatenate([v2[p], stack2(x[p])], axis=0))
         for p in pairs]
    upd = [_dot_tn(jnp.concatenate([vv[:, sls[p]], -x[p]], axis=0),
                   jnp.concatenate([kb[:, sls[p]], bb[:, sls[p]]], axis=0)) for p in pairs]
    for p in pairs:
        o_ref[:, sls[p]] = o[p]
        s_ref[p] = s[p] * e_tot[:, sls[p]] + jnp.where(blockdiag, upd[p], 0.0)


def _rwkv_out_kernel(of_ref, ob_ref, bv_ref, g_ref, lg_ref, lb_ref, e_ref, et_ref, w_ref, h_ref, mod_ref, lng_ref,
                     lnb_ref, o_ref):
    o = of_ref[...] + ob_ref[...]
    inv = 1.0 / RWKV_HEAD
    oc = o - _seg_sum(o, e_ref, et_ref) * inv
    var = _seg_sum(oc * oc, e_ref, et_ref) * inv
    y = oc * lax.rsqrt(var + RWKV_GN_EPS) * lg_ref[...] + lb_ref[...] + bv_ref[...]
    yo = _dot(y * g_ref[...], w_ref[...])
    z = DEEPNORM_ALPHA * h_ref[...] + mod_ref[0][2:3] * yo
    o_ref[...] = _ln_rows(z, lng_ref[...], lnb_ref[...])


def _rwkv_mixer(h, mods, nct, mu, w_in, w0, w_l1, w_l2, a0, a_l1, a_l2, g_l1, g_l2, k_k, k_a, r_k, gn_g, gn_b, w_out,
                ln_g, ln_b):
    t, d = h.shape
    nt = t // TM
    bf = MXU_DT
    lw_ = w_l1.shape[-1]
    la_ = a_l1.shape[-1]
    zw = jnp.zeros((lw_, d), F32)
    za = jnp.zeros((la_, d), F32)
    wl1 = jnp.concatenate([w_l1[0], w_l1[1]], axis=1).astype(bf)
    wl2 = jnp.stack([jnp.concatenate([w_l2[0], zw], 0), jnp.concatenate([zw, w_l2[1]], 0)]).astype(bf)
    al1 = jnp.concatenate([a_l1[0], a_l1[1]], axis=1).astype(bf)
    al2 = jnp.stack([jnp.concatenate([a_l2[0], za], 0), jnp.concatenate([za, a_l2[1]], 0)]).astype(bf)
    head_of = jnp.arange(d) // RWKV_HEAD
    e = (head_of[:, None] == jnp.arange(LANES)[None, :]).astype(bf)
    et = e.T
    halo_p = pl.BlockSpec((8, d), lambda i: (jnp.maximum(i * (TM // 8) - 1, 0), 0))
    halo_n = pl.BlockSpec((8, d), lambda i: (jnp.minimum((i + 1) * (TM // 8), t // 8 - 1), 0))
    args = [h, h, h, mods, mu, w_in.astype(bf), wl1, wl2, w0, al1, al2, a0, g_l1.astype(bf), g_l2.astype(bf),
            k_k[None], k_a[None], r_k.reshape(1, d), e, et]
    ins = [_row_spec(d), halo_p, halo_n, _mod_spec(nct)] + [_full_spec(a.shape) for a in args[4:]]
    outs = pl.pallas_call(
        functools.partial(_rwkv_prep_kernel, nct=nct, nt=nt), grid=(nt,), in_specs=ins,
        out_specs=[_row_spec(d)] * 11,
        out_shape=[jax.ShapeDtypeStruct((t, d), dt) for dt in [ACT_DT] * 5 + [F32] * 2 + [ACT_DT] * 4],
        compiler_params=_cparams("arbitrary"), name="rwkv_prep",
    )(*args)
    r, v, kk, g, bv, lw0, lw1, kt0, kt1, ab0, ab1 = outs
    c = RWKV_CHUNK
    ncc, nc = nct * (TM // c), t // c
    o_dir = []
    for d_, (lw, kt, ab) in enumerate(((lw0, kt0, ab0), (lw1, kt1, ab1))):
        reverse = d_ == 1
        spec = pl.BlockSpec((c, d), lambda g_, reverse=reverse: (_tile_of(g_, ncc, nc, reverse), 0))
        o_dir.append(pl.pallas_call(
            functools.partial(_rwkv_scan_kernel, reverse=reverse), grid=(nc,), in_specs=[spec] * 6, out_specs=spec,
            out_shape=jax.ShapeDtypeStruct((t, d), F32),
            scratch_shapes=[pltpu.VMEM((d // LANES, LANES, LANES), F32)],
            compiler_params=_cparams("arbitrary"), name="rwkv_scan_%d" % d_,
        )(r, v, kk, lw, kt, ab))
    args = [o_dir[0], o_dir[1], bv, g, gn_g[None], gn_b[None], e, et, w_out.astype(bf), h, mods, ln_g[None], ln_b[None]]
    ins = [_row_spec(d)] * 4 + [_full_spec(a.shape) for a in args[4:9]] + [_row_spec(d), _mod_spec(nct),
                                                                          _full_spec((1, d)), _full_spec((1, d))]
    return pl.pallas_call(
        _rwkv_out_kernel, grid=(nt,), in_specs=ins, out_specs=_row_spec(d),
        out_shape=jax.ShapeDtypeStruct((t, d), F32), compiler_params=_cparams("arbitrary"), name="rwkv_out",
    )(*args)


def _ret_in_kernel(h_ref, mod_ref, w_ref, cos_ref, sin_ref, q_o, k_o, v_o, g_o):
    d = D_MODEL
    u = _modulate(h_ref[...], mod_ref[0], 0).astype(MXU_DT)
    q = _dot(u, w_ref[:, 0:d])
    k = _dot(u, w_ref[:, d:2 * d]) * (RET_QK ** -0.5)
    v_o[...] = _dot(u, w_ref[:, 2 * d:4 * d]).astype(v_o.dtype)
    g_o[...] = _silu(_dot(u, w_ref[:, 4 * d:6 * d])).astype(g_o.dtype)
    cos = cos_ref[...]
    sin = sin_ref[...]
    half = RET_QK // 2
    for z, z_o in ((q, q_o), (k, k_o)):
        for hh in range(RET_HEADS):
            lo = z[:, hh * RET_QK:hh * RET_QK + half]
            hi = z[:, hh * RET_QK + half:(hh + 1) * RET_QK]
            zh = jnp.concatenate([lo, hi], axis=1)
            rot = jnp.concatenate([-hi, lo], axis=1)
            z_o[:, hh * RET_QK:(hh + 1) * RET_QK] = (zh * cos + rot * sin).astype(z_o.dtype)


def _ret_scan_kernel(qf_ref, kf_ref, vf_ref, qb_ref, kb_ref, vb_ref, inner_ref, qd_ref, kd_ref, bd_ref, of_ref, ob_ref,
                     r_ref):
    @pl.when(pl.program_id(0) == 0)
    def _():
        r_ref[...] = jnp.zeros_like(r_ref)

    refs = ((qf_ref, kf_ref, vf_ref, of_ref), (qb_ref, kb_ref, vb_ref, ob_ref))
    cells = [(d, hh) for d in range(2) for hh in range(RET_HEADS)]
    qs = lambda hh: slice(hh * RET_QK, (hh + 1) * RET_QK)
    vs = lambda hh: slice(hh * RET_V, (hh + 1) * RET_V)
    q = {(d, hh): refs[d][0][:, qs(hh)] for d, hh in cells}
    k = {(d, hh): refs[d][1][:, qs(hh)] for d, hh in cells}
    v = {(d, hh): refs[d][2][:, vs(hh)] for d, hh in cells}
    state = {c: r_ref[c[0], c[1]] for c in cells}
    scores = {c: _dot_nt(q[c], k[c]) * inner_ref[c[0], c[1]] for c in cells}
    carry_in = {c: _dot(q[c], state[c]) * qd_ref[c[0], c[1]] for c in cells}
    upd = {c: _dot_tn(k[c] * kd_ref[c[0], c[1]], v[c]) for c in cells}
    for c in cells:
        refs[c[0]][3][:, vs(c[1])] = _dot(scores[c], v[c]) + carry_in[c]
        r_ref[c[0], c[1]] = state[c] * bd_ref[c[0], c[1]] + upd[c]


def _ret_out_kernel(of_ref, ob_ref, g_ref, gg_ref, gb_ref, w_ref, h_ref, mod_ref, lng_ref, lnb_ref, o_ref):
    parts = []
    for hh in range(RET_HEADS):
        sl = slice(hh * RET_V, (hh + 1) * RET_V)
        o = of_ref[:, sl] + ob_ref[:, sl]
        mu = jnp.mean(o, axis=-1, keepdims=True)
        oc = o - mu
        var = jnp.mean(oc * oc, axis=-1, keepdims=True)
        y = oc * lax.rsqrt(var + LN_EPS) * gg_ref[:, sl] + gb_ref[:, sl]
        parts.append((g_ref[:, sl] * y).astype(MXU_DT))
    yo = _dot(jnp.concatenate(parts, axis=1), w_ref[...])
    z = DEEPNORM_ALPHA * h_ref[...] + mod_ref[0][2:3] * yo
    o_ref[...] = _ln_rows(z, lng_ref[...], lnb_ref[...])


def _ret_mixer(h, mods, nct, rope_cos, rope_sin, w_in, decay_logit, gn_g, gn_b, w_out, ln_g, ln_b):
    t, d = h.shape
    nt = t // TM
    hv = RET_HEADS * RET_V
    q, k, v, sg = pl.pallas_call(
        _ret_in_kernel, grid=(nt,),
        in_specs=[_row_spec(d), _mod_spec(nct), _full_spec(w_in.shape), _row_spec(RET_QK), _row_spec(RET_QK)],
        out_specs=[_row_spec(d), _row_spec(d), _row_spec(hv), _row_spec(hv)],
        out_shape=[jax.ShapeDtypeStruct((t, w), ACT_DT) for w in (d, d, hv, hv)],
        compiler_params=_cparams("arbitrary"), name="ret_in",
    )(h, mods, w_in.astype(MXU_DT), rope_cos, rope_sin)
    c = RET_CHUNK
    ncc, nc = nct * (TM // c), t // c
    log_gamma = jax.nn.log_sigmoid(decay_logit.astype(F32))
    pos = jnp.arange(c, dtype=F32)
    tabs = []
    for d_ in range(2):
        lg = log_gamma[d_][:, None, None]
        p = (c - 1.0 - pos) if d_ == 1 else pos
        rel = p[:, None] - p[None, :]
        tabs.append((jnp.where(rel >= 0, jnp.exp(jnp.maximum(rel, 0.0) * lg), 0.0),
                     jnp.exp((p + 1.0) * log_gamma[d_][:, None])[:, :, None],
                     jnp.exp((c - 1.0 - p) * log_gamma[d_][:, None])[:, :, None],
                     jnp.exp(c * log_gamma[d_])[:, None, None]))
    inner, q_dec, k_dec, blk_dec = (jnp.stack(z) for z in zip(*tabs))
    cs = lambda w, reverse: pl.BlockSpec((c, w), lambda g_: (_tile_of(g_, ncc, nc, reverse), 0))
    o_dir = pl.pallas_call(
        _ret_scan_kernel, grid=(nc,),
        in_specs=[cs(d, False), cs(d, False), cs(hv, False), cs(d, True), cs(d, True), cs(hv, True),
                  _full_spec(inner.shape), _full_spec(q_dec.shape), _full_spec(k_dec.shape), _full_spec(blk_dec.shape)],
        out_specs=[cs(hv, False), cs(hv, True)], out_shape=[jax.ShapeDtypeStruct((t, hv), F32)] * 2,
        scratch_shapes=[pltpu.VMEM((2, RET_HEADS, RET_QK, RET_V), F32)],
        compiler_params=_cparams("arbitrary"), name="ret_scan",
    )(q, k, v, q, k, v, inner, q_dec, k_dec, blk_dec)
    return pl.pallas_call(
        _ret_out_kernel, grid=(nt,),
        in_specs=[_row_spec(hv)] * 3 + [_full_spec((1, hv)), _full_spec((1, hv)), _full_spec((hv, d)), _row_spec(d),
                                        _mod_spec(nct), _full_spec((1, d)), _full_spec((1, d))],
        out_specs=_row_spec(d), out_shape=jax.ShapeDtypeStruct((t, d), F32),
        compiler_params=_cparams("arbitrary"), name="ret_out",
    )(o_dir[0], o_dir[1], sg, gn_g[None], gn_b[None], w_out.astype(MXU_DT), h, mods, ln_g[None], ln_b[None])


def _hgrn_in_kernel(h_ref, mod_ref, w_ref, lb_ref, bf_ref, q_o, v_o, g_o, f0_o, f1_o):
    d = D_MODEL
    u = _modulate(h_ref[...], mod_ref[0], 0).astype(MXU_DT)
    lb = lb_ref[...]
    q_o[...] = _silu(_dot(u, w_ref[:, 0:d])).astype(q_o.dtype)
    f0_o[...] = lb + (1.0 - lb) * jax.nn.sigmoid(_dot(u, w_ref[:, d:2 * d]) + bf_ref[0:1, :])
    f1_o[...] = lb + (1.0 - lb) * jax.nn.sigmoid(_dot(u, w_ref[:, 2 * d:3 * d]) + bf_ref[1:2, :])
    v_o[...] = _dot(u, w_ref[:, 3 * d:4 * d]).astype(v_o.dtype)
    g_o[...] = _silu(_dot(u, w_ref[:, 4 * d:5 * d])).astype(g_o.dtype)


def _hgrn_scan_kernel(qf_ref, vf_ref, ff_ref, qb_ref, vb_ref, fb_ref, of_ref, ob_ref, s_ref, b_s, rb_s, rk_s, rv_s):
    hb = HGRN_BLOCK
    nb = TM // hb
    half = hb // 2
    dirs = ((qf_ref, vf_ref, ff_ref, of_ref, False), (qb_ref, vb_ref, fb_ref, ob_ref, True))

    @pl.when(pl.program_id(0) == 0)
    def _():
        s_ref[...] = jnp.zeros_like(s_ref)

    ri = lax.broadcasted_iota(jnp.int32, (TM, TM), 0)
    ci = lax.broadcasted_iota(jnp.int32, (TM, TM), 1)
    same_block = (ri // hb) == (ci // hb)
    ti = lax.broadcasted_iota(jnp.int32, (half, 1), 0)
    rowi = lax.broadcasted_iota(jnp.int32, (hb, 1), 0)
    heads = range(HGRN_HEADS)
    sls = [slice(hh * HGRN_HEAD, (hh + 1) * HGRN_HEAD) for hh in heads]
    cells = [(d, hh) for d in range(2) for hh in heads]

    for d, (q_ref, v_ref, f_ref, o_ref, reverse) in enumerate(dirs):
        tri = jnp.where(jnp.logical_and(same_block, (ci >= ri) if reverse else (ci <= ri)), 1.0, 0.0)
        b_s[d] = _dot_sel(tri, jnp.log(f_ref[...]), 3)

    def block(bi, par):
        pre = []
        for d, (q_ref, v_ref, f_ref, o_ref, reverse) in enumerate(dirs):
            blk = (nb - 1 - bi) if reverse else bi
            r0 = pl.multiple_of(blk * hb, hb)
            kx = 1.0 - f_ref[pl.ds(r0, hb), :]
            q = q_ref[pl.ds(r0, hb), :].astype(F32)
            v = v_ref[pl.ds(r0, hb), :].astype(F32)
            b = b_s[d, pl.ds(r0, hb), :]
            rb_s[d, par] = b
            rk_s[d, par] = kx
            rv_s[d, par] = v
            tot = b[0:1, :] if reverse else b[hb - 1:hb, :]
            first = (rowi >= half) if reverse else (rowi < half)
            beta = b[half:half + 1, :] if reverse else b[half - 1:half, :]
            pre.append(dict(
                r0=r0, q=q, v=v, b=b, qe=q * jnp.exp(b), kb=kx * jnp.exp(tot - b), e_tot=jnp.exp(tot),
                k_first=kx * jnp.exp(jnp.where(first, beta - b, -jnp.inf)),
                q_second=q * jnp.exp(jnp.where(first, -jnp.inf, b - beta)),
                causal=[(ti <= si) if reverse else (ti >= si) for si in range(half)]))
        s = {c: s_ref[c[0], c[1]] for c in cells}
        m_first = {(d, hh): _dot_tn(pre[d]['k_first'][:, sls[hh]], pre[d]['v'][:, sls[hh]]) for d, hh in cells}
        o = {(d, hh): _dot_nt(pre[d]['qe'][:, sls[hh]], s[d, hh]) + _dot(pre[d]['q_second'][:, sls[hh]], m_first[d, hh])
             for d, hh in cells}
        upd = {(d, hh): _dot_tn(pre[d]['v'][:, sls[hh]], pre[d]['kb'][:, sls[hh]]) for d, hh in cells}
        for d, hh in cells:
            sl = sls[hh]
            p = pre[d]
            parts = []
            for lo in (0, half):
                bt = p['b'][lo:lo + half, sl]
                qt = p['q'][lo:lo + half, sl]
                acc = jnp.zeros((half, HGRN_HEAD), F32)
                for si in range(half):
                    row = slice(lo + si, lo + si + 1)
                    dec = jnp.exp(jnp.where(p['causal'][si], bt - rb_s[d, par, row, sl], -jnp.inf))
                    sc = jnp.sum(qt * rk_s[d, par, row, sl] * dec, axis=-1, keepdims=True)
                    acc = acc + sc * rv_s[d, par, row, sl]
                parts.append(acc)
            dirs[d][3][pl.ds(p['r0'], hb), sl] = o[d, hh] + jnp.concatenate(parts, axis=0)
            s_ref[d, hh] = s[d, hh] * p['e_tot'][:, sl] + upd[d, hh]

    def two_blocks(bj, carry):
        block(2 * bj, 0)
        block(2 * bj + 1, 1)
        return carry

    lax.fori_loop(0, nb // 2, two_blocks, 0)


def _hgrn_out_kernel(of_ref, ob_ref, g_ref, ng_ref, w_ref, h_ref, mod_ref, lng_ref, lnb_ref, o_ref):
    parts = []
    for hh in range(HGRN_HEADS):
        sl = slice(hh * HGRN_HEAD, (hh + 1) * HGRN_HEAD)
        o = of_ref[:, sl] + ob_ref[:, sl]
        y = o * lax.rsqrt(jnp.mean(o * o, axis=-1, keepdims=True) + LN_EPS) * ng_ref[...]
        parts.append((y * g_ref[:, sl]).astype(MXU_DT))
    yo = _dot(jnp.concatenate(parts, axis=1), w_ref[...])
    z = DEEPNORM_ALPHA * h_ref[...] + mod_ref[0][2:3] * yo
    o_ref[...] = _ln_rows(z, lng_ref[...], lnb_ref[...])


def _hgrn_mixer(h, mods, nct, lb, w_in, b_f, norm_g, w_out, ln_g, ln_b):
    t, d = h.shape
    nt = t // TM
    q, v, sg, f0, f1 = pl.pallas_call(
        _hgrn_in_kernel, grid=(nt,),
        in_specs=[_row_spec(d), _mod_spec(nct), _full_spec(w_in.shape), _full_spec((1, d)), _full_spec((2, d))],
        out_specs=[_row_spec(d)] * 5,
        out_shape=[jax.ShapeDtypeStruct((t, d), dt) for dt in [ACT_DT] * 3 + [F32] * 2],
        compiler_params=_cparams("arbitrary"), name="hgrn_in",
    )(h, mods, w_in.astype(MXU_DT), lb[None], b_f)
    fwd = pl.BlockSpec((TM, d), lambda g_: (g_, 0))
    bwd = pl.BlockSpec((TM, d), lambda g_: (_tile_of(g_, nct, nt, True), 0))
    o_dir = pl.pallas_call(
        _hgrn_scan_kernel, grid=(nt,), in_specs=[fwd] * 3 + [bwd] * 3, out_specs=[fwd, bwd],
        out_shape=[jax.ShapeDtypeStruct((t, d), F32)] * 2,
        scratch_shapes=[pltpu.VMEM((2, HGRN_HEADS, HGRN_HEAD, HGRN_HEAD), F32)] + [pltpu.VMEM((2, TM, d), F32)]
        + [pltpu.VMEM((2, 2, HGRN_BLOCK, d), F32)] * 3,
        compiler_params=_cparams("arbitrary"), name="hgrn_scan",
    )(q, v, f0, q, v, f1)
    return pl.pallas_call(
        _hgrn_out_kernel, grid=(nt,),
        in_specs=[_row_spec(d)] * 3 + [_full_spec((1, HGRN_HEAD)), _full_spec((d, d)), _row_spec(d), _mod_spec(nct),
                                       _full_spec((1, d)), _full_spec((1, d))],
        out_specs=_row_spec(d), out_shape=jax.ShapeDtypeStruct((t, d), F32),
        compiler_params=_cparams("arbitrary"), name="hgrn_out",
    )(o_dir[0], o_dir[1], sg, norm_g[None], w_out.astype(MXU_DT), h, mods, ln_g[None], ln_b[None])


def _router_kernel(h_ref, mod_ref, rw_ref, rb_ref, u_o, gate_o, rank_o, x_o):
    u = _modulate(h_ref[...], mod_ref[0], 3)
    u_o[...] = u.astype(u_o.dtype)
    w_hi, w_lo = _split(rw_ref[...], 2)
    u_hi, u_lo = _split(u, 2)
    nt_dims = (((1,), (1,)), ((), ()))
    logits = (lax.dot_general(w_hi, u_hi, nt_dims, preferred_element_type=F32)
              + lax.dot_general(w_hi, u_lo, nt_dims, preferred_element_type=F32)
              + lax.dot_general(w_lo, u_hi, nt_dims, preferred_element_type=F32))
    ne, gs = N_EXPERTS, N_EXPERTS // N_GROUPS
    neg = -jnp.inf
    scores = jax.nn.sigmoid(logits[:ne])
    choice = scores + rb_ref[:ne]
    c3 = choice.reshape(N_GROUPS, gs, TM)
    mi = lax.broadcasted_iota(jnp.int32, c3.shape, 1).astype(F32)
    m1 = jnp.max(c3, axis=1, keepdims=True)
    i1 = jnp.min(jnp.where(c3 == m1, mi, float(gs)), axis=1, keepdims=True)
    m2 = jnp.max(jnp.where(mi == i1, neg, c3), axis=1, keepdims=True)
    gscore = m1 + m2
    gi = lax.broadcasted_iota(jnp.int32, gscore.shape, 0).astype(F32)
    gsel = jnp.zeros(gscore.shape, F32)
    for _ in range(TOPK_GROUPS):
        gm = jnp.max(gscore, axis=0, keepdims=True)
        pick = gi == jnp.min(jnp.where(gscore == gm, gi, float(N_GROUPS)), axis=0, keepdims=True)
        gsel = jnp.where(pick, 1.0, gsel)
        gscore = jnp.where(pick, neg, gscore)
    emask = jnp.broadcast_to(gsel, c3.shape).reshape(ne, TM)
    masked = jnp.where(emask > 0.5, choice, neg)
    ei = lax.broadcasted_iota(jnp.int32, masked.shape, 0).astype(F32)
    chosen = jnp.zeros(masked.shape, F32)
    for _ in range(TOP_K):
        em = jnp.max(masked, axis=0, keepdims=True)
        pick = ei == jnp.min(jnp.where(masked == em, ei, float(ne)), axis=0, keepdims=True)
        chosen = jnp.where(pick, 1.0, chosen)
        masked = jnp.where(pick, neg, masked)
    top_w = scores * chosen
    gates = ROUTED_SCALE * top_w / jnp.sum(top_w, axis=0, keepdims=True)
    ti = lax.broadcasted_iota(jnp.int32, (TM, TM), 0)
    tj = lax.broadcasted_iota(jnp.int32, (TM, TM), 1)
    before = jnp.where(ti < tj, 1.0, 0.0).astype(MXU_DT)
    prefix = jnp.dot(chosen.astype(MXU_DT), before, preferred_element_type=F32)
    rank = jnp.where(chosen > 0.5, prefix, -1.0)
    gate_o[0] = gates
    rank_o[0] = rank
    cap = MOE_CAP
    slot = lax.broadcasted_iota(jnp.int32, (cap, TM), 0).astype(F32)
    ub = u.astype(MXU_DT)
    for g0 in range(0, ne, MOE_EGROUP):
        onehot = jnp.concatenate([jnp.where(slot == rank[e:e + 1, :], 1.0, 0.0).astype(MXU_DT)
                                  for e in range(g0, g0 + MOE_EGROUP)], axis=0)
        xg = jnp.dot(onehot, ub, preferred_element_type=F32)
        x_o[0, g0:g0 + MOE_EGROUP] = xg.reshape(MOE_EGROUP, cap, D_MODEL).astype(x_o.dtype)


def _expert_kernel(x_ref, wgu_ref, wd_ref, y_ref, wgu_b, wd_b):
    @pl.when(pl.program_id(1) == 0)
    def _():
        wgu_b[0] = wgu_ref[0, 0].astype(wgu_b.dtype)
        wd_b[0] = wd_ref[0, 0].astype(wd_b.dtype)

    g = x_ref.shape[0]
    ed = EXPERT_DIM
    x = x_ref[...].reshape(g * MOE_CAP, D_MODEL)
    gu = _dot(x, wgu_b[0])
    y = _dot(_silu(gu[:, :ed]) * gu[:, ed:], wd_b[0])
    y_ref[...] = y.reshape(g, 1, MOE_CAP, D_MODEL).astype(y_ref.dtype)


def _combine_kernel(u_ref, gt_ref, rt_ref, y_ref, sgu_ref, sd_ref, h_ref, mod_ref, lng_ref, lnb_ref, *rest, extra):
    if extra:
        ex_ref, o_ref = rest
    else:
        (o_ref,) = rest
    ed = EXPERT_DIM
    cap = MOE_CAP
    gu = _dot(u_ref[...], sgu_ref[...])
    acc = _dot(_silu(gu[:, :ed]) * gu[:, ed:], sd_ref[...])
    if extra:
        acc = acc + ex_ref[...]
    slot = lax.broadcasted_iota(jnp.int32, (cap, TM), 0).astype(F32)
    for g0 in range(0, N_EXPERTS, MOE_EGROUP):
        pw = jnp.concatenate([jnp.where(slot == rt_ref[0, e:e + 1, :], gt_ref[0, e:e + 1, :], 0.0).astype(MXU_DT)
                              for e in range(g0, g0 + MOE_EGROUP)], axis=0)
        yg = y_ref[0, g0:g0 + MOE_EGROUP].reshape(MOE_EGROUP * cap, D_MODEL)
        acc = acc + _dot_tn(pw, yg)
    z = DEEPNORM_ALPHA * h_ref[...] + mod_ref[0][5:6] * acc
    o_ref[...] = _ln_rows(z, lng_ref[...], lnb_ref[...])


def _overflow_kernel(tile_ref, exp_ref, nr_ref, n_ref, u_ref, gate_ref, rank_ref, wgu_ref, wd_ref, zero_ref, o_ref):
    del zero_ref
    s = pl.program_id(0)
    tile = tile_ref[s]
    e = exp_ref[s]
    ed = EXPERT_DIM
    cap = MOE_CAP
    active = s < n_ref[0]
    first = jnp.logical_or(s == 0, tile_ref[jnp.maximum(s - 1, 0)] != tile)

    @pl.when(jnp.logical_and(active, first))
    def _():
        o_ref[...] = jnp.zeros_like(o_ref)

    @pl.when(active)
    def _():
        wgu = wgu_ref[0]
        wd = wd_ref[0]
        rank = rank_ref[0, pl.ds(e, 1), :]
        gate = gate_ref[0, pl.ds(e, 1), :]

        def one_round(r, carry):
            slot = lax.broadcasted_iota(jnp.int32, (cap, TM), 0).astype(F32) + (r * cap).astype(F32)
            hit = slot == rank
            x = _dot(jnp.where(hit, 1.0, 0.0), u_ref[...])
            gu = _dot(x, wgu)
            y = _dot(_silu(gu[:, :ed]) * gu[:, ed:], wd)
            o_ref[...] += _dot_tn(jnp.where(hit, gate, 0.0), y)
            return carry

        lax.fori_loop(1, nr_ref[s], one_round, 0)


def _moe_layer(h, mods, nct, layer, router_w, router_b, w_gu, w_down, sh_gu, sh_down, ln_g, ln_b):
    t, d = h.shape
    nt = t // TM
    ne, cap = N_EXPERTS, MOE_CAP
    rw = jnp.concatenate([router_w.T, jnp.zeros((LANES - ne, d), F32)], axis=0)
    rb = jnp.concatenate([router_b, jnp.zeros((LANES - ne,), F32)])[:, None]
    per_tile = pl.BlockSpec((1, ne, TM), lambda i: (i, 0, 0))
    slots = pl.BlockSpec((1, ne, cap, d), lambda i: (i, 0, 0, 0))
    u, gates, ranks, xs = pl.pallas_call(
        _router_kernel, grid=(nt,),
        in_specs=[_row_spec(d), _mod_spec(nct), _full_spec((LANES, d)), _full_spec((LANES, 1))],
        out_specs=[_row_spec(d), per_tile, per_tile, slots],
        out_shape=[jax.ShapeDtypeStruct((t, d), MXU_DT), jax.ShapeDtypeStruct((nt, ne, TM), F32),
                   jax.ShapeDtypeStruct((nt, ne, TM), F32), jax.ShapeDtypeStruct((nt, ne, cap, d), MXU_DT)],
        compiler_params=_cparams("arbitrary"), name="moe_router",
    )(h, mods, rw, rb)

    run = max(g for g in range(1, MOE_RUN + 1) if nt % g == 0)
    ys, wgu_b, wd_b = pl.pallas_call(
        _expert_kernel, grid=(ne, nt // run),
        in_specs=[pl.BlockSpec((run, 1, cap, d), lambda e, c: (c, e, 0, 0)),
                  pl.BlockSpec((1, 1, d, 2 * EXPERT_DIM), lambda e, c: (layer, e, 0, 0)),
                  pl.BlockSpec((1, 1, EXPERT_DIM, d), lambda e, c: (layer, e, 0, 0))],
        out_specs=[pl.BlockSpec((run, 1, cap, d), lambda e, c: (c, e, 0, 0)),
                   pl.BlockSpec((1, d, 2 * EXPERT_DIM), lambda e, c: (e, 0, 0)),
                   pl.BlockSpec((1, EXPERT_DIM, d), lambda e, c: (e, 0, 0))],
        out_shape=[jax.ShapeDtypeStruct((nt, ne, cap, d), MXU_DT),
                   jax.ShapeDtypeStruct((ne, d, 2 * EXPERT_DIM), MXU_DT), jax.ShapeDtypeStruct((ne, EXPERT_DIM, d), MXU_DT)],
        compiler_params=_cparams("arbitrary", "arbitrary"), name="moe_experts",
    )(xs, w_gu, w_down)

    sgu, sd = sh_gu.astype(MXU_DT), sh_down.astype(MXU_DT)
    base_specs = [_row_spec(d), per_tile, per_tile, slots, _full_spec(sgu.shape), _full_spec(sd.shape),
                  _row_spec(d), _mod_spec(nct), _full_spec((1, d)), _full_spec((1, d))]
    base_args = (u, gates, ranks, ys, sgu, sd, h, mods, ln_g[None], ln_b[None])

    def combine(*extra):
        return pl.pallas_call(
            functools.partial(_combine_kernel, extra=bool(extra)), grid=(nt,),
            in_specs=base_specs + [_row_spec(d)] * len(extra), out_specs=_row_spec(d),
            out_shape=jax.ShapeDtypeStruct((t, d), F32), compiler_params=_cparams("arbitrary"), name="moe_combine",
        )(*base_args, *extra)

    count = (jnp.max(ranks, axis=-1).astype(jnp.int32) + 1).reshape(-1)
    over = count > cap
    n_over = jnp.sum(over.astype(jnp.int32))

    def with_overflow(size):
        def run():
            idx = jnp.nonzero(over, size=size, fill_value=0)[0].astype(jnp.int32)
            idx = jnp.where(jnp.arange(size) < n_over, idx, idx[jnp.maximum(n_over - 1, 0)])
            tiles, exps = idx // ne, idx % ne
            rounds = (count[idx] + cap - 1) // cap
            grid_spec = pltpu.PrefetchScalarGridSpec(
                num_scalar_prefetch=4, grid=(size,),
                in_specs=[pl.BlockSpec((TM, d), lambda s, tl, ex, nr, n: (tl[s], 0)),
                          pl.BlockSpec((1, ne, TM), lambda s, tl, ex, nr, n: (tl[s], 0, 0)),
                          pl.BlockSpec((1, ne, TM), lambda s, tl, ex, nr, n: (tl[s], 0, 0)),
                          pl.BlockSpec((1, d, 2 * EXPERT_DIM), lambda s, tl, ex, nr, n: (ex[s], 0, 0)),
                          pl.BlockSpec((1, EXPERT_DIM, d), lambda s, tl, ex, nr, n: (ex[s], 0, 0)),
                          pl.BlockSpec(memory_space=pl.ANY)],
                out_specs=pl.BlockSpec((TM, d), lambda s, tl, ex, nr, n: (tl[s], 0)))
            extra = pl.pallas_call(
                _overflow_kernel, grid_spec=grid_spec, out_shape=jax.ShapeDtypeStruct((t, d), F32),
                input_output_aliases={9: 0}, compiler_params=_cparams("arbitrary"), name="moe_overflow",
            )(tiles, exps, rounds, n_over[None], u, gates, ranks, wgu_b, wd_b, jnp.zeros((t, d), F32))
            return combine(extra)
        return run

    sizes = sorted({min(MOE_OVER_STEPS, nt * ne), nt * ne})
    branch = sum((n_over > sz).astype(jnp.int32) for sz in [0] + sizes[:-1])
    return lax.switch(branch, [combine] + [with_overflow(sz) for sz in sizes])


def kernel(x, c, ctx, c_ctx, ada_w, ada_b, post_ln_g, post_ln_b, lru_w_in, lru_conv_w, lru_conv_b, lru_gate_w, lru_gate_b, lru_lambda, lru_w_out, rwkv_mu, rwkv_w_in, rwkv_w0, rwkv_w_l1, rwkv_w_l2, rwkv_a0, rwkv_a_l1, rwkv_a_l2, rwkv_g_l1, rwkv_g_l2, rwkv_k_k, rwkv_k_a, rwkv_r_k, rwkv_ln_g, rwkv_ln_b, rwkv_w_out, ret_w_in, ret_decay, ret_gn_g, ret_gn_b, ret_w_out, hgrn_w_in, hgrn_b_f, hgrn_lb, hgrn_norm_g, hgrn_w_out, moe_router, moe_bias, moe_w_gu, moe_w_down, moe_sh_gu, moe_sh_down):
    assert x.shape[0] == 1 and ctx.shape[0] == 1
    n_ctx, n_lat, d = ctx.shape[1], x.shape[1], x.shape[2]
    assert n_ctx % TM == 0 and n_lat % TM == 0 and d == D_MODEL
    nct = n_ctx // TM
    rows = n_lat // GRID_W
    pos_row = jnp.repeat(jnp.arange(rows, dtype=F32), GRID_W)
    pos_col = jnp.tile(jnp.arange(GRID_W, dtype=F32), rows)
    n_freq = RET_QK // 4
    freqs = ROPE_BASE ** (-jnp.arange(n_freq, dtype=F32) / n_freq)
    ang = jnp.concatenate([pos_row[:, None] * freqs, pos_col[:, None] * freqs], axis=-1)
    ang = jnp.concatenate([ang, ang], axis=-1)
    rope_cos = jnp.concatenate([jnp.ones((n_ctx, RET_QK), F32), jnp.cos(ang)], axis=0)
    rope_sin = jnp.concatenate([jnp.zeros((n_ctx, RET_QK), F32), jnp.sin(ang)], axis=0)
    lb_cum = jnp.cumsum(jax.nn.softmax(hgrn_lb.astype(F32), axis=0), axis=0)

    cond = jnp.concatenate([c_ctx[None], c, jnp.zeros((6, d), F32)], axis=0)
    mods_all = _ada_mods(cond, ada_w, ada_b)
    h = jnp.concatenate([ctx[0], x[0]], axis=0)
    for i in range(DEPTH):
        kind, j = i % N_MIXERS, i // N_MIXERS
        mods = mods_all[i]
        lng, lnb = post_ln_g[i, 0], post_ln_b[i, 0]
        if kind == 0:
            h = _lru_mixer(h, mods, nct, lru_w_in[j], lru_conv_w[j], lru_conv_b[j], lru_gate_w[j], lru_gate_b[j],
                           lru_lambda[j], lru_w_out[j], lng, lnb)
        elif kind == 1:
            h = _rwkv_mixer(h, mods, nct, rwkv_mu[j], rwkv_w_in[j], rwkv_w0[j], rwkv_w_l1[j], rwkv_w_l2[j], rwkv_a0[j],
                            rwkv_a_l1[j], rwkv_a_l2[j], rwkv_g_l1[j], rwkv_g_l2[j], rwkv_k_k[j], rwkv_k_a[j],
                            rwkv_r_k[j], rwkv_ln_g[j], rwkv_ln_b[j], rwkv_w_out[j], lng, lnb)
        elif kind == 2:
            h = _ret_mixer(h, mods, nct, rope_cos, rope_sin, ret_w_in[j], ret_decay[j], ret_gn_g[j], ret_gn_b[j],
                           ret_w_out[j], lng, lnb)
        else:
            h = _hgrn_mixer(h, mods, nct, lb_cum[i] - lb_cum[0], hgrn_w_in[j], hgrn_b_f[j], hgrn_norm_g[j],
                            hgrn_w_out[j], lng, lnb)
        h = _moe_layer(h, mods, nct, i, moe_router[i], moe_bias[i], moe_w_gu, moe_w_down, moe_sh_gu[i],
                       moe_sh_down[i], post_ln_g[i, 1], post_ln_b[i, 1])
    return h[n_ctx:][None]
```

```python
import math
import functools
import jax
import jax.numpy as jnp
from jax import lax
from jax.experimental import pallas as pl
from jax.experimental.pallas import tpu as pltpu

F32 = jnp.float32
MXU_DT = jnp.bfloat16
ACT_DT = jnp.bfloat16
LANES = 128
TM = 256
VMEM_LIMIT = 56 * 2 ** 20

D_MODEL = 1024
DEPTH = 4
GRID_W = 64
N_MIXERS = 4
DEEPNORM_ALPHA = (2.0 * DEPTH) ** 0.25
LN_EPS = 1e-5
LRU_WIDTH = D_MODEL
LRU_BLOCKS = 16
LRU_BLOCK = LRU_WIDTH // LRU_BLOCKS
LRU_C = 8.0
RWKV_HEAD = 64
RWKV_HEADS = D_MODEL // RWKV_HEAD
RWKV_DECAY_SCALE = math.exp(-0.5)
RWKV_GN_EPS = 64e-5
RWKV_CHUNK = 64
RET_HEADS = 4
RET_QK = D_MODEL // RET_HEADS
RET_V = 2 * RET_QK
RET_CHUNK = 128
ROPE_BASE = 10000.0
HGRN_HEADS = 8
HGRN_HEAD = D_MODEL // HGRN_HEADS
HGRN_BLOCK = 16
N_EXPERTS = 64
TOP_K = 8
N_GROUPS = 8
TOPK_GROUPS = 4
EXPERT_DIM = 256
ROUTED_SCALE = 2.5
MOE_OVER_STEPS = 256
MOE_CAP = 64
MOE_EGROUP = 8
MOE_RUN = 65
MOE_CHUNK = 13


def _cparams(*sem):
    return pltpu.CompilerParams(dimension_semantics=sem, vmem_limit_bytes=VMEM_LIMIT)


def _dot(a, b):
    return jnp.dot(a.astype(MXU_DT), b.astype(MXU_DT), preferred_element_type=F32)


def _dot_nt(a, b):
    return lax.dot_general(a.astype(MXU_DT), b.astype(MXU_DT), (((1,), (1,)), ((), ())), preferred_element_type=F32)


def _dot_tn(a, b):
    return lax.dot_general(a.astype(MXU_DT), b.astype(MXU_DT), (((0,), (0,)), ((), ())), preferred_element_type=F32)


def _split(x, n):
    parts = []
    for _ in range(n):
        p = x.astype(MXU_DT)
        parts.append(p)
        x = x - p.astype(F32)
    return parts


def _dot_sel(sel, x, n):
    return sum(jnp.dot(sel.astype(MXU_DT), p, preferred_element_type=F32) for p in _split(x, n))


def _dot_xsel(x, sel, n):
    return sum(jnp.dot(p, sel.astype(MXU_DT), preferred_element_type=F32) for p in _split(x, n))


def _modulate(h, m, shift_idx):
    return h * (1.0 + m[shift_idx + 1:shift_idx + 2]) + m[shift_idx:shift_idx + 1]


def _ln_rows(z, g, b):
    mu = jnp.mean(z, axis=-1, keepdims=True)
    zc = z - mu
    var = jnp.mean(zc * zc, axis=-1, keepdims=True)
    return zc * lax.rsqrt(var + LN_EPS) * g + b


def _silu(x):
    return x * jax.nn.sigmoid(x)


def _shift_down(x, first_row):
    rows = lax.broadcasted_iota(jnp.int32, (x.shape[0], 1), 0)
    return jnp.where(rows == 0, first_row, pltpu.roll(x, 1, 0))


def _shift_up(x, last_row):
    n = x.shape[0]
    rows = lax.broadcasted_iota(jnp.int32, (n, 1), 0)
    return jnp.where(rows == n - 1, last_row, pltpu.roll(x, n - 1, 0))


def _tile_of(g, nct, nt, reverse):
    if not reverse:
        return g
    return jnp.where(g < nct, nct - 1 - g, nt - 1 - (g - nct))


def _halo_flags(t, nct, nt):
    prev_ok = jnp.logical_and(t != 0, t != nct).astype(F32)
    next_ok = jnp.logical_and(t != nct - 1, t != nt - 1).astype(F32)
    return prev_ok, next_ok


def _ada_kernel(s_ref, w_ref, b_ref, o_ref):
    o_ref[0] = _dot(_silu(s_ref[...]), w_ref[0]) + b_ref[0]


def _ada_mods(cond, ada_w, ada_b):
    nl, d, n6 = ada_w.shape
    out = pl.pallas_call(
        _ada_kernel, grid=(nl, n6 // d),
        in_specs=[pl.BlockSpec((8, d), lambda l, j: (0, 0)),
                  pl.BlockSpec((1, d, d), lambda l, j: (l, 0, j)),
                  pl.BlockSpec((1, 1, d), lambda l, j: (l, 0, j))],
        out_specs=pl.BlockSpec((1, 8, d), lambda l, j: (l, 0, j)),
        out_shape=jax.ShapeDtypeStruct((nl, 8, n6), F32),
        compiler_params=_cparams("arbitrary", "arbitrary"), name="ada_mods",
    )(cond, ada_w, ada_b.reshape(nl, 1, n6))
    return out[:, :2].reshape(nl, 2, 6, d)


def _row_spec(width, tm=TM):
    return pl.BlockSpec((tm, width), lambda i: (i, 0))


def _full_spec(shape):
    nd = len(shape)
    return pl.BlockSpec(tuple(shape), lambda *_: (0,) * nd)


def _mod_spec(nct):
    return pl.BlockSpec((1, 6, D_MODEL), lambda i: (jnp.minimum(i // nct, 1), 0, 0))


def _lru_out_kernel(g_ref, hf_ref, hb_ref, w_ref, h_ref, mod_ref, lng_ref, lnb_ref, o_ref):
    y = _dot(g_ref[...] * (hf_ref[...] + hb_ref[...]), w_ref[...])
    z = DEEPNORM_ALPHA * h_ref[...] + mod_ref[0][2:3] * y
    o_ref[...] = _ln_rows(z, lng_ref[...], lnb_ref[...])


def _lru_in_kernel(h_ref, mod_ref, w_ref, g_ref, x_ref):
    u = _modulate(h_ref[...], mod_ref[0], 0)
    z = _dot(u, w_ref[...])
    g_ref[...] = jax.nn.gelu(z[:, :LRU_WIDTH], approximate=True).astype(g_ref.dtype)
    x_ref[...] = z[:, LRU_WIDTH:]


def _lru_scan_kernel(xf_ref, xfp_ref, xfn_ref, xb_ref, xbp_ref, xbn_ref, cw_ref, cb_ref, gw_ref, gb_ref, lam_ref,
                     hf_o, hb_o, a_s, b_s, st_s, *, nct, nt):
    g = pl.program_id(0)

    @pl.when(g == 0)
    def _():
        st_s[...] = jnp.zeros_like(st_s)

    cw = cw_ref[...]
    for d, (x_ref, xp_ref, xn_ref) in enumerate(((xf_ref, xfp_ref, xfn_ref), (xb_ref, xbp_ref, xbn_ref))):
        prev_ok, next_ok = _halo_flags(_tile_of(g, nct, nt, d == 1), nct, nt)
        x = x_ref[...]
        xm1 = _shift_down(x, xp_ref[7:8, :] * prev_ok)
        xp1 = _shift_up(x, xn_ref[0:1, :] * next_ok)
        xp2 = _shift_up(xp1, xn_ref[1:2, :] * next_ok)
        xc = cw[0:1] * xm1 + cw[1:2] * x + cw[2:3] * xp1 + cw[3:4] * xp2 + cb_ref[...]
        gates = jax.nn.sigmoid(_dot(xc, gw_ref[d]) + gb_ref[d])
        lam = lam_ref[d:d + 1, :]
        softplus = jnp.maximum(-lam, 0.0) + jnp.log(1.0 + jnp.exp(-jnp.abs(lam)))
        log_a = -LRU_C * gates[:, :LRU_WIDTH] * softplus
        a_s[d] = jnp.exp(log_a)
        b_s[d] = jnp.sqrt(1.0 - jnp.exp(2.0 * log_a)) * (gates[:, LRU_WIDTH:] * xc)

    def row(r, carry):
        hf, hb = carry
        rb = TM - 1 - r
        hf = a_s[0, pl.ds(r, 1), :] * hf + b_s[0, pl.ds(r, 1), :]
        hb = a_s[1, pl.ds(rb, 1), :] * hb + b_s[1, pl.ds(rb, 1), :]
        hf_o[pl.ds(r, 1), :] = hf
        hb_o[pl.ds(rb, 1), :] = hb
        return hf, hb

    hf, hb = lax.fori_loop(0, TM, row, (st_s[0], st_s[1]), unroll=8)
    st_s[0] = hf
    st_s[1] = hb


def _lru_mixer(h, mods, nct, w_in, conv_w, conv_b, gate_w, gate_b, lam, w_out, ln_g, ln_b):
    t, d = h.shape
    nt = t // TM
    w = LRU_WIDTH
    gelu, rnn = pl.pallas_call(
        _lru_in_kernel, grid=(nt,),
        in_specs=[_row_spec(d), _mod_spec(nct), _full_spec((d, 2 * w))],
        out_specs=[_row_spec(w), _row_spec(w)],
        out_shape=[jax.ShapeDtypeStruct((t, w), ACT_DT), jax.ShapeDtypeStruct((t, w), F32)],
        compiler_params=_cparams("arbitrary"), name="lru_in",
    )(h, mods, w_in.astype(MXU_DT))
    eye = jnp.eye(LRU_BLOCKS, dtype=F32)
    gw = jnp.einsum('dgnij,nm->dgnimj', gate_w, eye).reshape(2, 2, w, w)
    gw = jnp.concatenate([gw[:, 0], gw[:, 1]], axis=-1).astype(MXU_DT)
    gb = gate_b.reshape(2, 1, 2 * w)
    def tile_specs(reverse):
        tile = lambda g: _tile_of(g, nct, nt, reverse)
        return [pl.BlockSpec((TM, w), lambda g: (tile(g), 0)),
                pl.BlockSpec((8, w), lambda g: (jnp.maximum(tile(g) * (TM // 8) - 1, 0), 0)),
                pl.BlockSpec((8, w), lambda g: (jnp.minimum((tile(g) + 1) * (TM // 8), t // 8 - 1), 0))]

    fwd, bwd = tile_specs(False), tile_specs(True)
    hf, hb = pl.pallas_call(
        functools.partial(_lru_scan_kernel, nct=nct, nt=nt), grid=(nt,),
        in_specs=fwd + bwd + [_full_spec((4, w)), _full_spec((1, w)), _full_spec((2, w, 2 * w)),
                              _full_spec((2, 1, 2 * w)), _full_spec((2, w))],
        out_specs=[fwd[0], bwd[0]], out_shape=[jax.ShapeDtypeStruct((t, w), F32)] * 2,
        scratch_shapes=[pltpu.VMEM((2, TM, w), F32)] * 2 + [pltpu.VMEM((2, 1, w), F32)],
        compiler_params=_cparams("arbitrary"), name="lru_scan",
    )(rnn, rnn, rnn, rnn, rnn, rnn, conv_w, conv_b[None], gw, gb, lam)
    return pl.pallas_call(
        _lru_out_kernel, grid=(nt,),
        in_specs=[_row_spec(w)] * 3 + [_full_spec((w, d)), _row_spec(d), _mod_spec(nct), _full_spec((1, d)),
                                       _full_spec((1, d))],
        out_specs=_row_spec(d), out_shape=jax.ShapeDtypeStruct((t, d), F32),
        compiler_params=_cparams("arbitrary"), name="lru_out",
    )(gelu, hf, hb, w_out.astype(MXU_DT), h, mods, ln_g[None], ln_b[None])


def _seg_sum(x, e_ref, et_ref):
    s = _dot_xsel(x, e_ref[...], 2)
    return _dot_xsel(s, et_ref[...], 2)


def _rwkv_prep_kernel(h_ref, hp_ref, hn_ref, mod_ref, mu_ref, win_ref, wl1_ref, wl2_ref, w0_ref, al1_ref, al2_ref,
                      a0_ref, gl1_ref, gl2_ref, kk_ref, ka_ref, rk_ref, e_ref, et_ref,
                      r_o, v_o, kk_o, g_o, bv_o, lw0_o, lw1_o, kt0_o, kt1_o, ab0_o, ab1_o, *, nct, nt):
    i = pl.program_id(0)
    prev_ok, next_ok = _halo_flags(i, nct, nt)
    m = mod_ref[0]
    u = _modulate(h_ref[...], m, 0)
    up = _modulate(hp_ref[7:8, :], m, 0) * prev_ok
    un = _modulate(hn_ref[0:1, :], m, 0) * next_ok
    lane = lax.broadcasted_iota(jnp.int32, (1, D_MODEL), 1)
    sh = jnp.where(lane < D_MODEL // 2, _shift_down(u, up), _shift_up(u, un))
    dx = sh - u
    mu = mu_ref[...]
    xm = [u + dx * mu[c:c + 1] for c in range(6)]
    r = _dot(xm[0], win_ref[0])
    k = _dot(xm[1], win_ref[1])
    v = _dot(xm[2], win_ref[2])
    t1 = jnp.tanh(_dot(xm[3], wl1_ref[...]))
    t2 = _dot(xm[4], al1_ref[...])
    g = _dot(jax.nn.sigmoid(_dot(xm[5], gl1_ref[...])), gl2_ref[...])
    kk = k * kk_ref[...]
    kk = kk * lax.rsqrt(_seg_sum(kk * kk, e_ref, et_ref) + 1e-12)
    ktsum = None
    for z, (lw_o, kt_o, ab_o) in enumerate(((lw0_o, kt0_o, ab0_o), (lw1_o, kt1_o, ab1_o))):
        d_w = w0_ref[z:z + 1, :] + _dot(t1, wl2_ref[z])
        lw_o[...] = -RWKV_DECAY_SCALE * jax.nn.sigmoid(d_w)
        a = jax.nn.sigmoid(a0_ref[z:z + 1, :] + _dot(t2, al2_ref[z]))
        kt = k * (1.0 + (a - 1.0) * ka_ref[...])
        kt_o[...] = kt.astype(kt_o.dtype)
        ab_o[...] = (kk * a).astype(ab_o.dtype)
        ktsum = kt if ktsum is None else ktsum + kt
    r_o[...] = r.astype(r_o.dtype)
    v_o[...] = v.astype(v_o.dtype)
    kk_o[...] = kk.astype(kk_o.dtype)
    g_o[...] = g.astype(g_o.dtype)
    bv_o[...] = (_seg_sum(r * ktsum * rk_ref[...], e_ref, et_ref) * v).astype(bv_o.dtype)


def _rwkv_scan_kernel(r_ref, v_ref, kk_ref, lw_ref, kt_ref, ab_ref, o_ref, s_ref, *, reverse):
    c = RWKV_CHUNK

    @pl.when(pl.program_id(0) == 0)
    def _():
        s_ref[...] = jnp.zeros_like(s_ref)

    ri = lax.broadcasted_iota(jnp.int32, (c, c), 0)
    ci = lax.broadcasted_iota(jnp.int32, (c, c), 1)
    incl = (ci >= ri) if reverse else (ci <= ri)
    ri2 = lax.broadcasted_iota(jnp.int32, (c, 2 * c), 0)
    ci2 = jnp.bitwise_and(lax.broadcasted_iota(jnp.int32, (c, 2 * c), 1), c - 1)
    incl2 = (ci2 >= ri2) if reverse else (ci2 <= ri2)
    strict2 = (ci2 > ri2) if reverse else (ci2 < ri2)
    lw = lw_ref[...]
    cl = _dot_sel(jnp.where(incl, 1.0, 0.0), lw, 3)
    tot = cl[0:1, :] if reverse else cl[c - 1:c, :]
    e_in = jnp.exp(cl)
    e_out = jnp.exp(-cl)
    e_end = jnp.exp(tot - cl)
    kk = kk_ref[...].astype(F32)
    kt = kt_ref[...].astype(F32)
    ab = ab_ref[...].astype(F32)
    kap = kk * jnp.exp(cl - lw)
    rh = r_ref[...].astype(F32) * e_in
    kh = kt * e_out
    bh = ab * e_out
    kb = kt * e_end
    bb = ab * e_end
    e_tot = jnp.exp(tot)
    vv = v_ref[...].astype(F32)
    lane_a = lax.broadcasted_iota(jnp.int32, (1, LANES), 1) < RWKV_HEAD
    bi = lax.broadcasted_iota(jnp.int32, (LANES, LANES), 0) < RWKV_HEAD
    bj = lax.broadcasted_iota(jnp.int32, (LANES, LANES), 1) < RWKV_HEAD
    blockdiag = bi == bj

    def stack2(x):
        return jnp.concatenate([jnp.where(lane_a, x, 0.0), jnp.where(lane_a, 0.0, x)], axis=0)

    pairs = range(D_MODEL // LANES)
    sls = [slice(p * LANES, (p + 1) * LANES) for p in pairs]
    s = [s_ref[p] for p in pairs]
    xq = [jnp.concatenate([kap[:, sl], rh[:, sl]], axis=0) for sl in sls]
    yk = [jnp.concatenate([stack2(kh[:, sl]), stack2(bh[:, sl])], axis=0) for sl in sls]
    gm = [_dot_nt(xq[p], yk[p]) for p in pairs]
    xs = [_dot_nt(xq[p], s[p]) for p in pairs]
    l_kk = [jnp.where(strict2, g[:c, :2 * c], 0.0) for g in gm]
    l_bk = [jnp.where(strict2, g[:c, 2 * c:], 0.0) for g in gm]
    a_rk = [jnp.where(incl2, g[c:, :2 * c], 0.0) for g in gm]
    a_rb = [jnp.where(incl2, g[c:, 2 * c:], 0.0) for g in gm]
    v2 = [stack2(vv[:, sl]) for sl in sls]
    x = [xs[p][:c] + _dot(l_kk[p], v2[p]) for p in pairs]
    lp = [_dot(l_bk[p], stack2(l_bk[p])) for p in pairs]
    x = [x[p] - _dot(l_bk[p], stack2(x[p])) for p in pairs]
    for it in range(5):
        x = [x[p] + _dot(lp[p], stack2(x[p])) for p in pairs]
        if it < 4:
            lp = [_dot(lp[p], stack2(lp[p])) for p in pairs]
    o = [xs[p][c:] + _dot(jnp.concatenate([a_rk[p], -a_rb[p]], axis=1), jnp.concatenate([v2[p], stack2(x[p])], axis=0))
         for p in pairs]
    upd = [_dot_tn(jnp.concatenate([vv[:, sls[p]], -x[p]], axis=0),
                   jnp.concatenate([kb[:, sls[p]], bb[:, sls[p]]], axis=0)) for p in pairs]
    for p in pairs:
        o_ref[:, sls[p]] = o[p]
        s_ref[p] = s[p] * e_tot[:, sls[p]] + jnp.where(blockdiag, upd[p], 0.0)


def _rwkv_out_kernel(of_ref, ob_ref, bv_ref, g_ref, lg_ref, lb_ref, e_ref, et_ref, w_ref, h_ref, mod_ref, lng_ref,
                     lnb_ref, o_ref):
    o = of_ref[...] + ob_ref[...]
    inv = 1.0 / RWKV_HEAD
    oc = o - _seg_sum(o, e_ref, et_ref) * inv
    var = _seg_sum(oc * oc, e_ref, et_ref) * inv
    y = oc * lax.rsqrt(var + RWKV_GN_EPS) * lg_ref[...] + lb_ref[...] + bv_ref[...]
    yo = _dot(y * g_ref[...], w_ref[...])
    z = DEEPNORM_ALPHA * h_ref[...] + mod_ref[0][2:3] * yo
    o_ref[...] = _ln_rows(z, lng_ref[...], lnb_ref[...])


def _rwkv_mixer(h, mods, nct, mu, w_in, w0, w_l1, w_l2, a0, a_l1, a_l2, g_l1, g_l2, k_k, k_a, r_k, gn_g, gn_b, w_out,
                ln_g, ln_b):
    t, d = h.shape
    nt = t // TM
    bf = MXU_DT
    lw_ = w_l1.shape[-1]
    la_ = a_l1.shape[-1]
    zw = jnp.zeros((lw_, d), F32)
    za = jnp.zeros((la_, d), F32)
    wl1 = jnp.concatenate([w_l1[0], w_l1[1]], axis=1).astype(bf)
    wl2 = jnp.stack([jnp.concatenate([w_l2[0], zw], 0), jnp.concatenate([zw, w_l2[1]], 0)]).astype(bf)
    al1 = jnp.concatenate([a_l1[0], a_l1[1]], axis=1).astype(bf)
    al2 = jnp.stack([jnp.concatenate([a_l2[0], za], 0), jnp.concatenate([za, a_l2[1]], 0)]).astype(bf)
    head_of = jnp.arange(d) // RWKV_HEAD
    e = (head_of[:, None] == jnp.arange(LANES)[None, :]).astype(bf)
    et = e.T
    halo_p = pl.BlockSpec((8, d), lambda i: (jnp.maximum(i * (TM // 8) - 1, 0), 0))
    halo_n = pl.BlockSpec((8, d), lambda i: (jnp.minimum((i + 1) * (TM // 8), t // 8 - 1), 0))
    args = [h, h, h, mods, mu, w_in.astype(bf), wl1, wl2, w0, al1, al2, a0, g_l1.astype(bf), g_l2.astype(bf),
            k_k[None], k_a[None], r_k.reshape(1, d), e, et]
    ins = [_row_spec(d), halo_p, halo_n, _mod_spec(nct)] + [_full_spec(a.shape) for a in args[4:]]
    outs = pl.pallas_call(
        functools.partial(_rwkv_prep_kernel, nct=nct, nt=nt), grid=(nt,), in_specs=ins,
        out_specs=[_row_spec(d)] * 11,
        out_shape=[jax.ShapeDtypeStruct((t, d), dt) for dt in [ACT_DT] * 5 + [F32] * 2 + [ACT_DT] * 4],
        compiler_params=_cparams("arbitrary"), name="rwkv_prep",
    )(*args)
    r, v, kk, g, bv, lw0, lw1, kt0, kt1, ab0, ab1 = outs
    c = RWKV_CHUNK
    ncc, nc = nct * (TM // c), t // c
    o_dir = []
    for d_, (lw, kt, ab) in enumerate(((lw0, kt0, ab0), (lw1, kt1, ab1))):
        reverse = d_ == 1
        spec = pl.BlockSpec((c, d), lambda g_, reverse=reverse: (_tile_of(g_, ncc, nc, reverse), 0))
        o_dir.append(pl.pallas_call(
            functools.partial(_rwkv_scan_kernel, reverse=reverse), grid=(nc,), in_specs=[spec] * 6, out_specs=spec,
            out_shape=jax.ShapeDtypeStruct((t, d), F32),
            scratch_shapes=[pltpu.VMEM((d // LANES, LANES, LANES), F32)],
            compiler_params=_cparams("arbitrary"), name="rwkv_scan_%d" % d_,
        )(r, v, kk, lw, kt, ab))
    args = [o_dir[0], o_dir[1], bv, g, gn_g[None], gn_b[None], e, et, w_out.astype(bf), h, mods, ln_g[None], ln_b[None]]
    ins = [_row_spec(d)] * 4 + [_full_spec(a.shape) for a in args[4:9]] + [_row_spec(d), _mod_spec(nct),
                                                                          _full_spec((1, d)), _full_spec((1, d))]
    return pl.pallas_call(
        _rwkv_out_kernel, grid=(nt,), in_specs=ins, out_specs=_row_spec(d),
        out_shape=jax.ShapeDtypeStruct((t, d), F32), compiler_params=_cparams("arbitrary"), name="rwkv_out",
    )(*args)


def _ret_in_kernel(h_ref, mod_ref, w_ref, cos_ref, sin_ref, q_o, k_o, v_o, g_o):
    d = D_MODEL
    u = _modulate(h_ref[...], mod_ref[0], 0).astype(MXU_DT)
    q = _dot(u, w_ref[:, 0:d])
    k = _dot(u, w_ref[:, d:2 * d]) * (RET_QK ** -0.5)
    v_o[...] = _dot(u, w_ref[:, 2 * d:4 * d]).astype(v_o.dtype)
    g_o[...] = _silu(_dot(u, w_ref[:, 4 * d:6 * d])).astype(g_o.dtype)
    cos = cos_ref[...]
    sin = sin_ref[...]
    half = RET_QK // 2
    for z, z_o in ((q, q_o), (k, k_o)):
        for hh in range(RET_HEADS):
            lo = z[:, hh * RET_QK:hh * RET_QK + half]
            hi = z[:, hh * RET_QK + half:(hh + 1) * RET_QK]
            zh = jnp.concatenate([lo, hi], axis=1)
            rot = jnp.concatenate([-hi, lo], axis=1)
            z_o[:, hh * RET_QK:(hh + 1) * RET_QK] = (zh * cos + rot * sin).astype(z_o.dtype)


def _ret_scan_kernel(qf_ref, kf_ref, vf_ref, qb_ref, kb_ref, vb_ref, inner_ref, qd_ref, kd_ref, bd_ref, of_ref, ob_ref,
                     r_ref):
    @pl.when(pl.program_id(0) == 0)
    def _():
        r_ref[...] = jnp.zeros_like(r_ref)

    refs = ((qf_ref, kf_ref, vf_ref, of_ref), (qb_ref, kb_ref, vb_ref, ob_ref))
    cells = [(d, hh) for d in range(2) for hh in range(RET_HEADS)]
    qs = lambda hh: slice(hh * RET_QK, (hh + 1) * RET_QK)
    vs = lambda hh: slice(hh * RET_V, (hh + 1) * RET_V)
    q = {(d, hh): refs[d][0][:, qs(hh)] for d, hh in cells}
    k = {(d, hh): refs[d][1][:, qs(hh)] for d, hh in cells}
    v = {(d, hh): refs[d][2][:, vs(hh)] for d, hh in cells}
    state = {c: r_ref[c[0], c[1]] for c in cells}
    scores = {c: _dot_nt(q[c], k[c]) * inner_ref[c[0], c[1]] for c in cells}
    carry_in = {c: _dot(q[c], state[c]) * qd_ref[c[0], c[1]] for c in cells}
    upd = {c: _dot_tn(k[c] * kd_ref[c[0], c[1]], v[c]) for c in cells}
    for c in cells:
        refs[c[0]][3][:, vs(c[1])] = _dot(scores[c], v[c]) + carry_in[c]
        r_ref[c[0], c[1]] = state[c] * bd_ref[c[0], c[1]] + upd[c]


def _ret_out_kernel(of_ref, ob_ref, g_ref, gg_ref, gb_ref, w_ref, h_ref, mod_ref, lng_ref, lnb_ref, o_ref):
    parts = []
    for hh in range(RET_HEADS):
        sl = slice(hh * RET_V, (hh + 1) * RET_V)
        o = of_ref[:, sl] + ob_ref[:, sl]
        mu = jnp.mean(o, axis=-1, keepdims=True)
        oc = o - mu
        var = jnp.mean(oc * oc, axis=-1, keepdims=True)
        y = oc * lax.rsqrt(var + LN_EPS) * gg_ref[:, sl] + gb_ref[:, sl]
        parts.append((g_ref[:, sl] * y).astype(MXU_DT))
    yo = _dot(jnp.concatenate(parts, axis=1), w_ref[...])
    z = DEEPNORM_ALPHA * h_ref[...] + mod_ref[0][2:3] * yo
    o_ref[...] = _ln_rows(z, lng_ref[...], lnb_ref[...])


def _ret_mixer(h, mods, nct, rope_cos, rope_sin, w_in, decay_logit, gn_g, gn_b, w_out, ln_g, ln_b):
    t, d = h.shape
    nt = t // TM
    hv = RET_HEADS * RET_V
    q, k, v, sg = pl.pallas_call(
        _ret_in_kernel, grid=(nt,),
        in_specs=[_row_spec(d), _mod_spec(nct), _full_spec(w_in.shape), _row_spec(RET_QK), _row_spec(RET_QK)],
        out_specs=[_row_spec(d), _row_spec(d), _row_spec(hv), _row_spec(hv)],
        out_shape=[jax.ShapeDtypeStruct((t, w), ACT_DT) for w in (d, d, hv, hv)],
        compiler_params=_cparams("arbitrary"), name="ret_in",
    )(h, mods, w_in.astype(MXU_DT), rope_cos, rope_sin)
    c = RET_CHUNK
    ncc, nc = nct * (TM // c), t // c
    log_gamma = jax.nn.log_sigmoid(decay_logit.astype(F32))
    pos = jnp.arange(c, dtype=F32)
    tabs = []
    for d_ in range(2):
        lg = log_gamma[d_][:, None, None]
        p = (c - 1.0 - pos) if d_ == 1 else pos
        rel = p[:, None] - p[None, :]
        tabs.append((jnp.where(rel >= 0, jnp.exp(jnp.maximum(rel, 0.0) * lg), 0.0),
                     jnp.exp((p + 1.0) * log_gamma[d_][:, None])[:, :, None],
                     jnp.exp((c - 1.0 - p) * log_gamma[d_][:, None])[:, :, None],
                     jnp.exp(c * log_gamma[d_])[:, None, None]))
    inner, q_dec, k_dec, blk_dec = (jnp.stack(z) for z in zip(*tabs))
    cs = lambda w, reverse: pl.BlockSpec((c, w), lambda g_: (_tile_of(g_, ncc, nc, reverse), 0))
    o_dir = pl.pallas_call(
        _ret_scan_kernel, grid=(nc,),
        in_specs=[cs(d, False), cs(d, False), cs(hv, False), cs(d, True), cs(d, True), cs(hv, True),
                  _full_spec(inner.shape), _full_spec(q_dec.shape), _full_spec(k_dec.shape), _full_spec(blk_dec.shape)],
        out_specs=[cs(hv, False), cs(hv, True)], out_shape=[jax.ShapeDtypeStruct((t, hv), F32)] * 2,
        scratch_shapes=[pltpu.VMEM((2, RET_HEADS, RET_QK, RET_V), F32)],
        compiler_params=_cparams("arbitrary"), name="ret_scan",
    )(q, k, v, q, k, v, inner, q_dec, k_dec, blk_dec)
    return pl.pallas_call(
        _ret_out_kernel, grid=(nt,),
        in_specs=[_row_spec(hv)] * 3 + [_full_spec((1, hv)), _full_spec((1, hv)), _full_spec((hv, d)), _row_spec(d),
                                        _mod_spec(nct), _full_spec((1, d)), _full_spec((1, d))],
        out_specs=_row_spec(d), out_shape=jax.ShapeDtypeStruct((t, d), F32),
        compiler_params=_cparams("arbitrary"), name="ret_out",
    )(o_dir[0], o_dir[1], sg, gn_g[None], gn_b[None], w_out.astype(MXU_DT), h, mods, ln_g[None], ln_b[None])


def _hgrn_in_kernel(h_ref, mod_ref, w_ref, lb_ref, bf_ref, q_o, v_o, g_o, f0_o, f1_o):
    d = D_MODEL
    u = _modulate(h_ref[...], mod_ref[0], 0).astype(MXU_DT)
    lb = lb_ref[...]
    q_o[...] = _silu(_dot(u, w_ref[:, 0:d])).astype(q_o.dtype)
    f0_o[...] = lb + (1.0 - lb) * jax.nn.sigmoid(_dot(u, w_ref[:, d:2 * d]) + bf_ref[0:1, :])
    f1_o[...] = lb + (1.0 - lb) * jax.nn.sigmoid(_dot(u, w_ref[:, 2 * d:3 * d]) + bf_ref[1:2, :])
    v_o[...] = _dot(u, w_ref[:, 3 * d:4 * d]).astype(v_o.dtype)
    g_o[...] = _silu(_dot(u, w_ref[:, 4 * d:5 * d])).astype(g_o.dtype)


def _hgrn_scan_kernel(qf_ref, vf_ref, ff_ref, qb_ref, vb_ref, fb_ref, of_ref, ob_ref, s_ref, b_s, rb_s, rk_s, rv_s):
    hb = HGRN_BLOCK
    nb = TM // hb
    half = hb // 2
    dirs = ((qf_ref, vf_ref, ff_ref, of_ref, False), (qb_ref, vb_ref, fb_ref, ob_ref, True))

    @pl.when(pl.program_id(0) == 0)
    def _():
        s_ref[...] = jnp.zeros_like(s_ref)

    ri = lax.broadcasted_iota(jnp.int32, (TM, TM), 0)
    ci = lax.broadcasted_iota(jnp.int32, (TM, TM), 1)
    same_block = (ri // hb) == (ci // hb)
    ti = lax.broadcasted_iota(jnp.int32, (half, 1), 0)
    rowi = lax.broadcasted_iota(jnp.int32, (hb, 1), 0)
    heads = range(HGRN_HEADS)
    sls = [slice(hh * HGRN_HEAD, (hh + 1) * HGRN_HEAD) for hh in heads]
    cells = [(d, hh) for d in range(2) for hh in heads]

    for d, (q_ref, v_ref, f_ref, o_ref, reverse) in enumerate(dirs):
        tri = jnp.where(jnp.logical_and(same_block, (ci >= ri) if reverse else (ci <= ri)), 1.0, 0.0)
        b_s[d] = _dot_sel(tri, jnp.log(f_ref[...]), 3)

    def block(bi, par):
        pre = []
        for d, (q_ref, v_ref, f_ref, o_ref, reverse) in enumerate(dirs):
            blk = (nb - 1 - bi) if reverse else bi
            r0 = pl.multiple_of(blk * hb, hb)
            kx = 1.0 - f_ref[pl.ds(r0, hb), :]
            q = q_ref[pl.ds(r0, hb), :].astype(F32)
            v = v_ref[pl.ds(r0, hb), :].astype(F32)
            b = b_s[d, pl.ds(r0, hb), :]
            rb_s[d, par] = b
            rk_s[d, par] = kx
            rv_s[d, par] = v
            tot = b[0:1, :] if reverse else b[hb - 1:hb, :]
            first = (rowi >= half) if reverse else (rowi < half)
            beta = b[half:half + 1, :] if reverse else b[half - 1:half, :]
            pre.append(dict(
                r0=r0, q=q, v=v, b=b, qe=q * jnp.exp(b), kb=kx * jnp.exp(tot - b), e_tot=jnp.exp(tot),
                k_first=kx * jnp.exp(jnp.where(first, beta - b, -jnp.inf)),
                q_second=q * jnp.exp(jnp.where(first, -jnp.inf, b - beta)),
                causal=[(ti <= si) if reverse else (ti >= si) for si in range(half)]))
        s = {c: s_ref[c[0], c[1]] for c in cells}
        m_first = {(d, hh): _dot_tn(pre[d]['k_first'][:, sls[hh]], pre[d]['v'][:, sls[hh]]) for d, hh in cells}
        o = {(d, hh): _dot_nt(pre[d]['qe'][:, sls[hh]], s[d, hh]) + _dot(pre[d]['q_second'][:, sls[hh]], m_first[d, hh])
             for d, hh in cells}
        upd = {(d, hh): _dot_tn(pre[d]['v'][:, sls[hh]], pre[d]['kb'][:, sls[hh]]) for d, hh in cells}
        for d, hh in cells:
            sl = sls[hh]
            p = pre[d]
            parts = []
            for lo in (0, half):
                bt = p['b'][lo:lo + half, sl]
                qt = p['q'][lo:lo + half, sl]
                acc = jnp.zeros((half, HGRN_HEAD), F32)
                for si in range(half):
                    row = slice(lo + si, lo + si + 1)
                    dec = jnp.exp(jnp.where(p['causal'][si], bt - rb_s[d, par, row, sl], -jnp.inf))
                    sc = jnp.sum(qt * rk_s[d, par, row, sl] * dec, axis=-1, keepdims=True)
                    acc = acc + sc * rv_s[d, par, row, sl]
                parts.append(acc)
            dirs[d][3][pl.ds(p['r0'], hb), sl] = o[d, hh] + jnp.concatenate(parts, axis=0)
            s_ref[d, hh] = s[d, hh] * p['e_tot'][:, sl] + upd[d, hh]

    def two_blocks(bj, carry):
        block(2 * bj, 0)
        block(2 * bj + 1, 1)
        return carry

    lax.fori_loop(0, nb // 2, two_blocks, 0)


def _hgrn_out_kernel(of_ref, ob_ref, g_ref, ng_ref, w_ref, h_ref, mod_ref, lng_ref, lnb_ref, o_ref):
    parts = []
    for hh in range(HGRN_HEADS):
        sl = slice(hh * HGRN_HEAD, (hh + 1) * HGRN_HEAD)
        o = of_ref[:, sl] + ob_ref[:, sl]
        y = o * lax.rsqrt(jnp.mean(o * o, axis=-1, keepdims=True) + LN_EPS) * ng_ref[...]
        parts.append((y * g_ref[:, sl]).astype(MXU_DT))
    yo = _dot(jnp.concatenate(parts, axis=1), w_ref[...])
    z = DEEPNORM_ALPHA * h_ref[...] + mod_ref[0][2:3] * yo
    o_ref[...] = _ln_rows(z, lng_ref[...], lnb_ref[...])


def _hgrn_mixer(h, mods, nct, lb, w_in, b_f, norm_g, w_out, ln_g, ln_b):
    t, d = h.shape
    nt = t // TM
    q, v, sg, f0, f1 = pl.pallas_call(
        _hgrn_in_kernel, grid=(nt,),
        in_specs=[_row_spec(d), _mod_spec(nct), _full_spec(w_in.shape), _full_spec((1, d)), _full_spec((2, d))],
        out_specs=[_row_spec(d)] * 5,
        out_shape=[jax.ShapeDtypeStruct((t, d), dt) for dt in [ACT_DT] * 3 + [F32] * 2],
        compiler_params=_cparams("arbitrary"), name="hgrn_in",
    )(h, mods, w_in.astype(MXU_DT), lb[None], b_f)
    fwd = pl.BlockSpec((TM, d), lambda g_: (g_, 0))
    bwd = pl.BlockSpec((TM, d), lambda g_: (_tile_of(g_, nct, nt, True), 0))
    o_dir = pl.pallas_call(
        _hgrn_scan_kernel, grid=(nt,), in_specs=[fwd] * 3 + [bwd] * 3, out_specs=[fwd, bwd],
        out_shape=[jax.ShapeDtypeStruct((t, d), F32)] * 2,
        scratch_shapes=[pltpu.VMEM((2, HGRN_HEADS, HGRN_HEAD, HGRN_HEAD), F32)] + [pltpu.VMEM((2, TM, d), F32)]
        + [pltpu.VMEM((2, 2, HGRN_BLOCK, d), F32)] * 3,
        compiler_params=_cparams("arbitrary"), name="hgrn_scan",
    )(q, v, f0, q, v, f1)
    return pl.pallas_call(
        _hgrn_out_kernel, grid=(nt,),
        in_specs=[_row_spec(d)] * 3 + [_full_spec((1, HGRN_HEAD)), _full_spec((d, d)), _row_spec(d), _mod_spec(nct),
                                       _full_spec((1, d)), _full_spec((1, d))],
        out_specs=_row_spec(d), out_shape=jax.ShapeDtypeStruct((t, d), F32),
        compiler_params=_cparams("arbitrary"), name="hgrn_out",
    )(o_dir[0], o_dir[1], sg, norm_g[None], w_out.astype(MXU_DT), h, mods, ln_g[None], ln_b[None])


def _router_kernel(h_ref, mod_ref, rw_ref, rb_ref, u_o, gate_o, rank_o, x_o):
    u = _modulate(h_ref[...], mod_ref[0], 3)
    u_o[...] = u.astype(u_o.dtype)
    w_hi, w_lo = _split(rw_ref[...], 2)
    u_hi, u_lo = _split(u, 2)
    nt_dims = (((1,), (1,)), ((), ()))
    logits = (lax.dot_general(w_hi, u_hi, nt_dims, preferred_element_type=F32)
              + lax.dot_general(w_hi, u_lo, nt_dims, preferred_element_type=F32)
              + lax.dot_general(w_lo, u_hi, nt_dims, preferred_element_type=F32))
    ne, gs = N_EXPERTS, N_EXPERTS // N_GROUPS
    neg = -jnp.inf
    scores = jax.nn.sigmoid(logits[:ne])
    choice = scores + rb_ref[:ne]
    c3 = choice.reshape(N_GROUPS, gs, TM)
    mi = lax.broadcasted_iota(jnp.int32, c3.shape, 1).astype(F32)
    m1 = jnp.max(c3, axis=1, keepdims=True)
    i1 = jnp.min(jnp.where(c3 == m1, mi, float(gs)), axis=1, keepdims=True)
    m2 = jnp.max(jnp.where(mi == i1, neg, c3), axis=1, keepdims=True)
    gscore = m1 + m2
    gi = lax.broadcasted_iota(jnp.int32, gscore.shape, 0).astype(F32)
    gsel = jnp.zeros(gscore.shape, F32)
    for _ in range(TOPK_GROUPS):
        gm = jnp.max(gscore, axis=0, keepdims=True)
        pick = gi == jnp.min(jnp.where(gscore == gm, gi, float(N_GROUPS)), axis=0, keepdims=True)
        gsel = jnp.where(pick, 1.0, gsel)
        gscore = jnp.where(pick, neg, gscore)
    emask = jnp.broadcast_to(gsel, c3.shape).reshape(ne, TM)
    masked = jnp.where(emask > 0.5, choice, neg)
    ei = lax.broadcasted_iota(jnp.int32, masked.shape, 0).astype(F32)
    chosen = jnp.zeros(masked.shape, F32)
    for _ in range(TOP_K):
        em = jnp.max(masked, axis=0, keepdims=True)
        pick = ei == jnp.min(jnp.where(masked == em, ei, float(ne)), axis=0, keepdims=True)
        chosen = jnp.where(pick, 1.0, chosen)
        masked = jnp.where(pick, neg, masked)
    top_w = scores * chosen
    gates = ROUTED_SCALE * top_w / jnp.sum(top_w, axis=0, keepdims=True)
    ti = lax.broadcasted_iota(jnp.int32, (TM, TM), 0)
    tj = lax.broadcasted_iota(jnp.int32, (TM, TM), 1)
    before = jnp.where(ti < tj, 1.0, 0.0).astype(MXU_DT)
    prefix = jnp.dot(chosen.astype(MXU_DT), before, preferred_element_type=F32)
    rank = jnp.where(chosen > 0.5, prefix, -1.0)
    gate_o[0] = gates
    rank_o[0] = rank
    cap = MOE_CAP
    slot = lax.broadcasted_iota(jnp.int32, (cap, TM), 0).astype(F32)
    ub = u.astype(MXU_DT)
    for g0 in range(0, ne, MOE_EGROUP):
        onehot = jnp.concatenate([jnp.where(slot == rank[e:e + 1, :], 1.0, 0.0).astype(MXU_DT)
                                  for e in range(g0, g0 + MOE_EGROUP)], axis=0)
        xg = jnp.dot(onehot, ub, preferred_element_type=F32)
        x_o[0, g0:g0 + MOE_EGROUP] = xg.reshape(MOE_EGROUP, cap, D_MODEL).astype(x_o.dtype)


def _expert_kernel(x_ref, wgu_ref, wd_ref, y_ref, wgu_b, wd_b):
    @pl.when(pl.program_id(1) == 0)
    def _():
        wgu_b[0] = wgu_ref[0, 0].astype(wgu_b.dtype)
        wd_b[0] = wd_ref[0, 0].astype(wd_b.dtype)

    g = x_ref.shape[0]
    ch = max(c for c in range(1, MOE_CHUNK + 1) if g % c == 0)
    ed = EXPERT_DIM

    def chunk(ci, carry):
        t0 = ci * ch
        x = x_ref[pl.ds(t0, ch)].reshape(ch * MOE_CAP, D_MODEL)
        gu = _dot(x, wgu_b[0])
        y = _dot(_silu(gu[:, :ed]) * gu[:, ed:], wd_b[0])
        y_ref[pl.ds(t0, ch)] = y.reshape(ch, 1, MOE_CAP, D_MODEL).astype(y_ref.dtype)
        return carry

    lax.fori_loop(0, g // ch, chunk, 0)


def _combine_kernel(u_ref, gt_ref, rt_ref, y_ref, sgu_ref, sd_ref, h_ref, mod_ref, lng_ref, lnb_ref, *rest, extra):
    if extra:
        ex_ref, o_ref = rest
    else:
        (o_ref,) = rest
    ed = EXPERT_DIM
    cap = MOE_CAP
    gu = _dot(u_ref[...], sgu_ref[...])
    acc = _dot(_silu(gu[:, :ed]) * gu[:, ed:], sd_ref[...])
    if extra:
        acc = acc + ex_ref[...]
    slot = lax.broadcasted_iota(jnp.int32, (cap, TM), 0).astype(F32)
    for g0 in range(0, N_EXPERTS, MOE_EGROUP):
        pw = jnp.concatenate([jnp.where(slot == rt_ref[0, e:e + 1, :], gt_ref[0, e:e + 1, :], 0.0).astype(MXU_DT)
                              for e in range(g0, g0 + MOE_EGROUP)], axis=0)
        yg = y_ref[0, g0:g0 + MOE_EGROUP].reshape(MOE_EGROUP * cap, D_MODEL)
        acc = acc + _dot_tn(pw, yg)
    z = DEEPNORM_ALPHA * h_ref[...] + mod_ref[0][5:6] * acc
    o_ref[...] = _ln_rows(z, lng_ref[...], lnb_ref[...])


def _overflow_kernel(tile_ref, exp_ref, nr_ref, n_ref, u_ref, gate_ref, rank_ref, wgu_ref, wd_ref, zero_ref, o_ref):
    del zero_ref
    s = pl.program_id(0)
    tile = tile_ref[s]
    e = exp_ref[s]
    ed = EXPERT_DIM
    cap = MOE_CAP
    active = s < n_ref[0]
    first = jnp.logical_or(s == 0, tile_ref[jnp.maximum(s - 1, 0)] != tile)

    @pl.when(jnp.logical_and(active, first))
    def _():
        o_ref[...] = jnp.zeros_like(o_ref)

    @pl.when(active)
    def _():
        wgu = wgu_ref[0]
        wd = wd_ref[0]
        rank = rank_ref[0, pl.ds(e, 1), :]
        gate = gate_ref[0, pl.ds(e, 1), :]

        def one_round(r, carry):
            slot = lax.broadcasted_iota(jnp.int32, (cap, TM), 0).astype(F32) + (r * cap).astype(F32)
            hit = slot == rank
            x = _dot(jnp.where(hit, 1.0, 0.0), u_ref[...])
            gu = _dot(x, wgu)
            y = _dot(_silu(gu[:, :ed]) * gu[:, ed:], wd)
            o_ref[...] += _dot_tn(jnp.where(hit, gate, 0.0), y)
            return carry

        lax.fori_loop(1, nr_ref[s], one_round, 0)


def _moe_layer(h, mods, nct, layer, router_w, router_b, w_gu, w_down, sh_gu, sh_down, ln_g, ln_b):
    t, d = h.shape
    nt = t // TM
    ne, cap = N_EXPERTS, MOE_CAP
    rw = jnp.concatenate([router_w.T, jnp.zeros((LANES - ne, d), F32)], axis=0)
    rb = jnp.concatenate([router_b, jnp.zeros((LANES - ne,), F32)])[:, None]
    per_tile = pl.BlockSpec((1, ne, TM), lambda i: (i, 0, 0))
    slots = pl.BlockSpec((1, ne, cap, d), lambda i: (i, 0, 0, 0))
    u, gates, ranks, xs = pl.pallas_call(
        _router_kernel, grid=(nt,),
        in_specs=[_row_spec(d), _mod_spec(nct), _full_spec((LANES, d)), _full_spec((LANES, 1))],
        out_specs=[_row_spec(d), per_tile, per_tile, slots],
        out_shape=[jax.ShapeDtypeStruct((t, d), MXU_DT), jax.ShapeDtypeStruct((nt, ne, TM), F32),
                   jax.ShapeDtypeStruct((nt, ne, TM), F32), jax.ShapeDtypeStruct((nt, ne, cap, d), MXU_DT)],
        compiler_params=_cparams("arbitrary"), name="moe_router",
    )(h, mods, rw, rb)

    run = max(g for g in range(1, MOE_RUN + 1) if nt % g == 0)
    ys, wgu_b, wd_b = pl.pallas_call(
        _expert_kernel, grid=(ne, nt // run),
        in_specs=[pl.BlockSpec((run, 1, cap, d), lambda e, c: (c, e, 0, 0)),
                  pl.BlockSpec((1, 1, d, 2 * EXPERT_DIM), lambda e, c: (layer, e, 0, 0)),
                  pl.BlockSpec((1, 1, EXPERT_DIM, d), lambda e, c: (layer, e, 0, 0))],
        out_specs=[pl.BlockSpec((run, 1, cap, d), lambda e, c: (c, e, 0, 0)),
                   pl.BlockSpec((1, d, 2 * EXPERT_DIM), lambda e, c: (e, 0, 0)),
                   pl.BlockSpec((1, EXPERT_DIM, d), lambda e, c: (e, 0, 0))],
        out_shape=[jax.ShapeDtypeStruct((nt, ne, cap, d), MXU_DT),
                   jax.ShapeDtypeStruct((ne, d, 2 * EXPERT_DIM), MXU_DT), jax.ShapeDtypeStruct((ne, EXPERT_DIM, d), MXU_DT)],
        compiler_params=_cparams("arbitrary", "arbitrary"), name="moe_experts",
    )(xs, w_gu, w_down)

    sgu, sd = sh_gu.astype(MXU_DT), sh_down.astype(MXU_DT)
    base_specs = [_row_spec(d), per_tile, per_tile, slots, _full_spec(sgu.shape), _full_spec(sd.shape),
                  _row_spec(d), _mod_spec(nct), _full_spec((1, d)), _full_spec((1, d))]
    base_args = (u, gates, ranks, ys, sgu, sd, h, mods, ln_g[None], ln_b[None])

    def combine(*extra):
        return pl.pallas_call(
            functools.partial(_combine_kernel, extra=bool(extra)), grid=(nt,),
            in_specs=base_specs + [_row_spec(d)] * len(extra), out_specs=_row_spec(d),
            out_shape=jax.ShapeDtypeStruct((t, d), F32), compiler_params=_cparams("arbitrary"), name="moe_combine",
        )(*base_args, *extra)

    count = (jnp.max(ranks, axis=-1).astype(jnp.int32) + 1).reshape(-1)
    over = count > cap
    n_over = jnp.sum(over.astype(jnp.int32))

    def with_overflow(size):
        def run():
            idx = jnp.nonzero(over, size=size, fill_value=0)[0].astype(jnp.int32)
            idx = jnp.where(jnp.arange(size) < n_over, idx, idx[jnp.maximum(n_over - 1, 0)])
            tiles, exps = idx // ne, idx % ne
            rounds = (count[idx] + cap - 1) // cap
            grid_spec = pltpu.PrefetchScalarGridSpec(
                num_scalar_prefetch=4, grid=(size,),
                in_specs=[pl.BlockSpec((TM, d), lambda s, tl, ex, nr, n: (tl[s], 0)),
                          pl.BlockSpec((1, ne, TM), lambda s, tl, ex, nr, n: (tl[s], 0, 0)),
                          pl.BlockSpec((1, ne, TM), lambda s, tl, ex, nr, n: (tl[s], 0, 0)),
                          pl.BlockSpec((1, d, 2 * EXPERT_DIM), lambda s, tl, ex, nr, n: (ex[s], 0, 0)),
                          pl.BlockSpec((1, EXPERT_DIM, d), lambda s, tl, ex, nr, n: (ex[s], 0, 0)),
                          pl.BlockSpec(memory_space=pl.ANY)],
                out_specs=pl.BlockSpec((TM, d), lambda s, tl, ex, nr, n: (tl[s], 0)))
            extra = pl.pallas_call(
                _overflow_kernel, grid_spec=grid_spec, out_shape=jax.ShapeDtypeStruct((t, d), F32),
                input_output_aliases={9: 0}, compiler_params=_cparams("arbitrary"), name="moe_overflow",
            )(tiles, exps, rounds, n_over[None], u, gates, ranks, wgu_b, wd_b, jnp.zeros((t, d), F32))
            return combine(extra)
        return run

    sizes = sorted({min(MOE_OVER_STEPS, nt * ne), nt * ne})
    branch = sum((n_over > sz).astype(jnp.int32) for sz in [0] + sizes[:-1])
    return lax.switch(branch, [combine] + [with_overflow(sz) for sz in sizes])


def kernel(x, c, ctx, c_ctx, ada_w, ada_b, post_ln_g, post_ln_b, lru_w_in, lru_conv_w, lru_conv_b, lru_gate_w, lru_gate_b, lru_lambda, lru_w_out, rwkv_mu, rwkv_w_in, rwkv_w0, rwkv_w_l1, rwkv_w_l2, rwkv_a0, rwkv_a_l1, rwkv_a_l2, rwkv_g_l1, rwkv_g_l2, rwkv_k_k, rwkv_k_a, rwkv_r_k, rwkv_ln_g, rwkv_ln_b, rwkv_w_out, ret_w_in, ret_decay, ret_gn_g, ret_gn_b, ret_w_out, hgrn_w_in, hgrn_b_f, hgrn_lb, hgrn_norm_g, hgrn_w_out, moe_router, moe_bias, moe_w_gu, moe_w_down, moe_sh_gu, moe_sh_down):
    assert x.shape[0] == 1 and ctx.shape[0] == 1
    n_ctx, n_lat, d = ctx.shape[1], x.shape[1], x.shape[2]
    assert n_ctx % TM == 0 and n_lat % TM == 0 and d == D_MODEL
    nct = n_ctx // TM
    rows = n_lat // GRID_W
    pos_row = jnp.repeat(jnp.arange(rows, dtype=F32), GRID_W)
    pos_col = jnp.tile(jnp.arange(GRID_W, dtype=F32), rows)
    n_freq = RET_QK // 4
    freqs = ROPE_BASE ** (-jnp.arange(n_freq, dtype=F32) / n_freq)
    ang = jnp.concatenate([pos_row[:, None] * freqs, pos_col[:, None] * freqs], axis=-1)
    ang = jnp.concatenate([ang, ang], axis=-1)
    rope_cos = jnp.concatenate([jnp.ones((n_ctx, RET_QK), F32), jnp.cos(ang)], axis=0)
    rope_sin = jnp.concatenate([jnp.zeros((n_ctx, RET_QK), F32), jnp.sin(ang)], axis=0)
    lb_cum = jnp.cumsum(jax.nn.softmax(hgrn_lb.astype(F32), axis=0), axis=0)

    cond = jnp.concatenate([c_ctx[None], c, jnp.zeros((6, d), F32)], axis=0)
    mods_all = _ada_mods(cond, ada_w, ada_b)
    h = jnp.concatenate([ctx[0], x[0]], axis=0)
    for i in range(DEPTH):
        kind, j = i % N_MIXERS, i // N_MIXERS
        mods = mods_all[i]
        lng, lnb = post_ln_g[i, 0], post_ln_b[i, 0]
        if kind == 0:
            h = _lru_mixer(h, mods, nct, lru_w_in[j], lru_conv_w[j], lru_conv_b[j], lru_gate_w[j], lru_gate_b[j],
                           lru_lambda[j], lru_w_out[j], lng, lnb)
        elif kind == 1:
            h = _rwkv_mixer(h, mods, nct, rwkv_mu[j], rwkv_w_in[j], rwkv_w0[j], rwkv_w_l1[j], rwkv_w_l2[j], rwkv_a0[j],
                            rwkv_a_l1[j], rwkv_a_l2[j], rwkv_g_l1[j], rwkv_g_l2[j], rwkv_k_k[j], rwkv_k_a[j],
                            rwkv_r_k[j], rwkv_ln_g[j], rwkv_ln_b[j], rwkv_w_out[j], lng, lnb)
        elif kind == 2:
            h = _ret_mixer(h, mods, nct, rope_cos, rope_sin, ret_w_in[j], ret_decay[j], ret_gn_g[j], ret_gn_b[j],
                           ret_w_out[j], lng, lnb)
        else:
            h = _hgrn_mixer(h, mods, nct, lb_cum[i] - lb_cum[0], hgrn_w_in[j], hgrn_b_f[j], hgrn_norm_g[j],
                            hgrn_w_out[j], lng, lnb)
        h = _moe_layer(h, mods, nct, i, moe_router[i], moe_bias[i], moe_w_gu, moe_w_down, moe_sh_gu[i],
                       moe_sh_down[i], post_ln_g[i, 1], post_ln_b[i, 1])
    return h[n_ctx:][None]
```

```python
import math
import functools
import jax
import jax.numpy as jnp
from jax import lax
from jax.experimental import pallas as pl
from jax.experimental.pallas import tpu as pltpu

F32 = jnp.float32
MXU_DT = jnp.bfloat16
ACT_DT = jnp.bfloat16
LANES = 128
TM = 256
VMEM_LIMIT = 56 * 2 ** 20

D_MODEL = 1024
DEPTH = 4
GRID_W = 64
N_MIXERS = 4
DEEPNORM_ALPHA = (2.0 * DEPTH) ** 0.25
LN_EPS = 1e-5
LRU_WIDTH = D_MODEL
LRU_BLOCKS = 16
LRU_BLOCK = LRU_WIDTH // LRU_BLOCKS
LRU_C = 8.0
RWKV_HEAD = 64
RWKV_HEADS = D_MODEL // RWKV_HEAD
RWKV_DECAY_SCALE = math.exp(-0.5)
RWKV_GN_EPS = 64e-5
RWKV_CHUNK = 64
RET_HEADS = 4
RET_QK = D_MODEL // RET_HEADS
RET_V = 2 * RET_QK
RET_CHUNK = 128
ROPE_BASE = 10000.0
HGRN_HEADS = 8
HGRN_HEAD = D_MODEL // HGRN_HEADS
HGRN_BLOCK = 16
N_EXPERTS = 64
TOP_K = 8
N_GROUPS = 8
TOPK_GROUPS = 4
EXPERT_DIM = 256
ROUTED_SCALE = 2.5
MOE_OVER_STEPS = 256
MOE_CAP = 64
MOE_EGROUP = 8
MOE_RUN = 65
MOE_CHUNK = 13


def _cparams(*sem):
    return pltpu.CompilerParams(dimension_semantics=sem, vmem_limit_bytes=VMEM_LIMIT)


def _dot(a, b):
    return jnp.dot(a.astype(MXU_DT), b.astype(MXU_DT), preferred_element_type=F32)


def _dot_nt(a, b):
    return lax.dot_general(a.astype(MXU_DT), b.astype(MXU_DT), (((1,), (1,)), ((), ())), preferred_element_type=F32)


def _dot_tn(a, b):
    return lax.dot_general(a.astype(MXU_DT), b.astype(MXU_DT), (((0,), (0,)), ((), ())), preferred_element_type=F32)


def _split(x, n):
    parts = []
    for _ in range(n):
        p = x.astype(MXU_DT)
        parts.append(p)
        x = x - p.astype(F32)
    return parts


def _dot_sel(sel, x, n):
    return sum(jnp.dot(sel.astype(MXU_DT), p, preferred_element_type=F32) for p in _split(x, n))


def _dot_xsel(x, sel, n):
    return sum(jnp.dot(p, sel.astype(MXU_DT), preferred_element_type=F32) for p in _split(x, n))


def _modulate(h, m, shift_idx):
    return h * (1.0 + m[shift_idx + 1:shift_idx + 2]) + m[shift_idx:shift_idx + 1]


def _ln_rows(z, g, b):
    mu = jnp.mean(z, axis=-1, keepdims=True)
    zc = z - mu
    var = jnp.mean(zc * zc, axis=-1, keepdims=True)
    return zc * lax.rsqrt(var + LN_EPS) * g + b


def _silu(x):
    return x * jax.nn.sigmoid(x)


def _shift_down(x, first_row):
    rows = lax.broadcasted_iota(jnp.int32, (x.shape[0], 1), 0)
    return jnp.where(rows == 0, first_row, pltpu.roll(x, 1, 0))


def _shift_up(x, last_row):
    n = x.shape[0]
    rows = lax.broadcasted_iota(jnp.int32, (n, 1), 0)
    return jnp.where(rows == n - 1, last_row, pltpu.roll(x, n - 1, 0))


def _tile_of(g, nct, nt, reverse):
    if not reverse:
        return g
    return jnp.where(g < nct, nct - 1 - g, nt - 1 - (g - nct))


def _halo_flags(t, nct, nt):
    prev_ok = jnp.logical_and(t != 0, t != nct).astype(F32)
    next_ok = jnp.logical_and(t != nct - 1, t != nt - 1).astype(F32)
    return prev_ok, next_ok


def _ada_kernel(s_ref, w_ref, b_ref, o_ref):
    o_ref[0] = _dot(_silu(s_ref[...]), w_ref[0]) + b_ref[0]


def _ada_mods(cond, ada_w, ada_b):
    nl, d, n6 = ada_w.shape
    out = pl.pallas_call(
        _ada_kernel, grid=(nl, n6 // d),
        in_specs=[pl.BlockSpec((8, d), lambda l, j: (0, 0)),
                  pl.BlockSpec((1, d, d), lambda l, j: (l, 0, j)),
                  pl.BlockSpec((1, 1, d), lambda l, j: (l, 0, j))],
        out_specs=pl.BlockSpec((1, 8, d), lambda l, j: (l, 0, j)),
        out_shape=jax.ShapeDtypeStruct((nl, 8, n6), F32),
        compiler_params=_cparams("arbitrary", "arbitrary"), name="ada_mods",
    )(cond, ada_w, ada_b.reshape(nl, 1, n6))
    return out[:, :2].reshape(nl, 2, 6, d)


def _row_spec(width, tm=TM):
    return pl.BlockSpec((tm, width), lambda i: (i, 0))


def _full_spec(shape):
    nd = len(shape)
    return pl.BlockSpec(tuple(shape), lambda *_: (0,) * nd)


def _mod_spec(nct):
    return pl.BlockSpec((1, 6, D_MODEL), lambda i: (jnp.minimum(i // nct, 1), 0, 0))


def _lru_out_kernel(g_ref, hf_ref, hb_ref, w_ref, h_ref, mod_ref, lng_ref, lnb_ref, o_ref):
    y = _dot(g_ref[...] * (hf_ref[...] + hb_ref[...]), w_ref[...])
    z = DEEPNORM_ALPHA * h_ref[...] + mod_ref[0][2:3] * y
    o_ref[...] = _ln_rows(z, lng_ref[...], lnb_ref[...])


def _lru_in_kernel(h_ref, mod_ref, w_ref, g_ref, x_ref):
    u = _modulate(h_ref[...], mod_ref[0], 0)
    z = _dot(u, w_ref[...])
    g_ref[...] = jax.nn.gelu(z[:, :LRU_WIDTH], approximate=True).astype(g_ref.dtype)
    x_ref[...] = z[:, LRU_WIDTH:]


def _lru_scan_kernel(xf_ref, xfp_ref, xfn_ref, xb_ref, xbp_ref, xbn_ref, cw_ref, cb_ref, gw_ref, gb_ref, lam_ref,
                     hf_o, hb_o, a_s, b_s, st_s, *, nct, nt):
    g = pl.program_id(0)

    @pl.when(g == 0)
    def _():
        st_s[...] = jnp.zeros_like(st_s)

    cw = cw_ref[...]
    for d, (x_ref, xp_ref, xn_ref) in enumerate(((xf_ref, xfp_ref, xfn_ref), (xb_ref, xbp_ref, xbn_ref))):
        prev_ok, next_ok = _halo_flags(_tile_of(g, nct, nt, d == 1), nct, nt)
        x = x_ref[...]
        xm1 = _shift_down(x, xp_ref[7:8, :] * prev_ok)
        xp1 = _shift_up(x, xn_ref[0:1, :] * next_ok)
        xp2 = _shift_up(xp1, xn_ref[1:2, :] * next_ok)
        xc = cw[0:1] * xm1 + cw[1:2] * x + cw[2:3] * xp1 + cw[3:4] * xp2 + cb_ref[...]
        gates = jax.nn.sigmoid(_dot(xc, gw_ref[d]) + gb_ref[d])
        lam = lam_ref[d:d + 1, :]
        softplus = jnp.maximum(-lam, 0.0) + jnp.log(1.0 + jnp.exp(-jnp.abs(lam)))
        log_a = -LRU_C * gates[:, :LRU_WIDTH] * softplus
        a_s[d] = jnp.exp(log_a)
        b_s[d] = jnp.sqrt(1.0 - jnp.exp(2.0 * log_a)) * (gates[:, LRU_WIDTH:] * xc)

    def row(r, carry):
        hf, hb = carry
        rb = TM - 1 - r
        hf = a_s[0, pl.ds(r, 1), :] * hf + b_s[0, pl.ds(r, 1), :]
        hb = a_s[1, pl.ds(rb, 1), :] * hb + b_s[1, pl.ds(rb, 1), :]
        hf_o[pl.ds(r, 1), :] = hf
        hb_o[pl.ds(rb, 1), :] = hb
        return hf, hb

    hf, hb = lax.fori_loop(0, TM, row, (st_s[0], st_s[1]), unroll=8)
    st_s[0] = hf
    st_s[1] = hb


def _lru_mixer(h, mods, nct, w_in, conv_w, conv_b, gate_w, gate_b, lam, w_out, ln_g, ln_b):
    t, d = h.shape
    nt = t // TM
    w = LRU_WIDTH
    gelu, rnn = pl.pallas_call(
        _lru_in_kernel, grid=(nt,),
        in_specs=[_row_spec(d), _mod_spec(nct), _full_spec((d, 2 * w))],
        out_specs=[_row_spec(w), _row_spec(w)],
        out_shape=[jax.ShapeDtypeStruct((t, w), ACT_DT), jax.ShapeDtypeStruct((t, w), F32)],
        compiler_params=_cparams("arbitrary"), name="lru_in",
    )(h, mods, w_in.astype(MXU_DT))
    eye = jnp.eye(LRU_BLOCKS, dtype=F32)
    gw = jnp.einsum('dgnij,nm->dgnimj', gate_w, eye).reshape(2, 2, w, w)
    gw = jnp.concatenate([gw[:, 0], gw[:, 1]], axis=-1).astype(MXU_DT)
    gb = gate_b.reshape(2, 1, 2 * w)
    def tile_specs(reverse):
        tile = lambda g: _tile_of(g, nct, nt, reverse)
        return [pl.BlockSpec((TM, w), lambda g: (tile(g), 0)),
                pl.BlockSpec((8, w), lambda g: (jnp.maximum(tile(g) * (TM // 8) - 1, 0), 0)),
                pl.BlockSpec((8, w), lambda g: (jnp.minimum((tile(g) + 1) * (TM // 8), t // 8 - 1), 0))]

    fwd, bwd = tile_specs(False), tile_specs(True)
    hf, hb = pl.pallas_call(
        functools.partial(_lru_scan_kernel, nct=nct, nt=nt), grid=(nt,),
        in_specs=fwd + bwd + [_full_spec((4, w)), _full_spec((1, w)), _full_spec((2, w, 2 * w)),
                              _full_spec((2, 1, 2 * w)), _full_spec((2, w))],
        out_specs=[fwd[0], bwd[0]], out_shape=[jax.ShapeDtypeStruct((t, w), F32)] * 2,
        scratch_shapes=[pltpu.VMEM((2, TM, w), F32)] * 2 + [pltpu.VMEM((2, 1, w), F32)],
        compiler_params=_cparams("arbitrary"), name="lru_scan",
    )(rnn, rnn, rnn, rnn, rnn, rnn, conv_w, conv_b[None], gw, gb, lam)
    return pl.pallas_call(
        _lru_out_kernel, grid=(nt,),
        in_specs=[_row_spec(w)] * 3 + [_full_spec((w, d)), _row_spec(d), _mod_spec(nct), _full_spec((1, d)),
                                       _full_spec((1, d))],
        out_specs=_row_spec(d), out_shape=jax.ShapeDtypeStruct((t, d), F32),
        compiler_params=_cparams("arbitrary"), name="lru_out",
    )(gelu, hf, hb, w_out.astype(MXU_DT), h, mods, ln_g[None], ln_b[None])


def _seg_sum(x, e_ref, et_ref):
    s = _dot_xsel(x, e_ref[...], 2)
    return _dot_xsel(s, et_ref[...], 2)


def _rwkv_prep_kernel(h_ref, hp_ref, hn_ref, mod_ref, mu_ref, win_ref, wl1_ref, wl2_ref, w0_ref, al1_ref, al2_ref,
                      a0_ref, gl1_ref, gl2_ref, kk_ref, ka_ref, rk_ref, e_ref, et_ref,
                      r_o, v_o, kk_o, g_o, bv_o, lw0_o, lw1_o, kt0_o, kt1_o, ab0_o, ab1_o, *, nct, nt):
    i = pl.program_id(0)
    prev_ok, next_ok = _halo_flags(i, nct, nt)
    m = mod_ref[0]
    u = _modulate(h_ref[...], m, 0)
    up = _modulate(hp_ref[7:8, :], m, 0) * prev_ok
    un = _modulate(hn_ref[0:1, :], m, 0) * next_ok
    lane = lax.broadcasted_iota(jnp.int32, (1, D_MODEL), 1)
    sh = jnp.where(lane < D_MODEL // 2, _shift_down(u, up), _shift_up(u, un))
    dx = sh - u
    mu = mu_ref[...]
    xm = [u + dx * mu[c:c + 1] for c in range(6)]
    r = _dot(xm[0], win_ref[0])
    k = _dot(xm[1], win_ref[1])
    v = _dot(xm[2], win_ref[2])
    t1 = jnp.tanh(_dot(xm[3], wl1_ref[...]))
    t2 = _dot(xm[4], al1_ref[...])
    g = _dot(jax.nn.sigmoid(_dot(xm[5], gl1_ref[...])), gl2_ref[...])
    kk = k * kk_ref[...]
    kk = kk * lax.rsqrt(_seg_sum(kk * kk, e_ref, et_ref) + 1e-12)
    ktsum = None
    for z, (lw_o, kt_o, ab_o) in enumerate(((lw0_o, kt0_o, ab0_o), (lw1_o, kt1_o, ab1_o))):
        d_w = w0_ref[z:z + 1, :] + _dot(t1, wl2_ref[z])
        lw_o[...] = -RWKV_DECAY_SCALE * jax.nn.sigmoid(d_w)
        a = jax.nn.sigmoid(a0_ref[z:z + 1, :] + _dot(t2, al2_ref[z]))
        kt = k * (1.0 + (a - 1.0) * ka_ref[...])
        kt_o[...] = kt.astype(kt_o.dtype)
        ab_o[...] = (kk * a).astype(ab_o.dtype)
        ktsum = kt if ktsum is None else ktsum + kt
    r_o[...] = r.astype(r_o.dtype)
    v_o[...] = v.astype(v_o.dtype)
    kk_o[...] = kk.astype(kk_o.dtype)
    g_o[...] = g.astype(g_o.dtype)
    bv_o[...] = (_seg_sum(r * ktsum * rk_ref[...], e_ref, et_ref) * v).astype(bv_o.dtype)


def _rwkv_scan_kernel(r_ref, v_ref, kk_ref, lw_ref, kt_ref, ab_ref, o_ref, s_ref, *, reverse):
    c = RWKV_CHUNK

    @pl.when(pl.program_id(0) == 0)
    def _():
        s_ref[...] = jnp.zeros_like(s_ref)

    ri = lax.broadcasted_iota(jnp.int32, (c, c), 0)
    ci = lax.broadcasted_iota(jnp.int32, (c, c), 1)
    incl = (ci >= ri) if reverse else (ci <= ri)
    ri2 = lax.broadcasted_iota(jnp.int32, (c, 2 * c), 0)
    ci2 = jnp.bitwise_and(lax.broadcasted_iota(jnp.int32, (c, 2 * c), 1), c - 1)
    incl2 = (ci2 >= ri2) if reverse else (ci2 <= ri2)
    strict2 = (ci2 > ri2) if reverse else (ci2 < ri2)
    lane_a = lax.broadcasted_iota(jnp.int32, (1, LANES), 1) < RWKV_HEAD
    bi = lax.broadcasted_iota(jnp.int32, (LANES, LANES), 0) < RWKV_HEAD
    bj = lax.broadcasted_iota(jnp.int32, (LANES, LANES), 1) < RWKV_HEAD
    blockdiag = bi == bj

    def stack2(x):
        return jnp.concatenate([jnp.where(lane_a, x, 0.0), jnp.where(lane_a, 0.0, x)], axis=0)

    nch = r_ref.shape[0] // c

    def one_chunk(ci, carry):
        rows = pl.ds(pl.multiple_of(((nch - 1 - ci) if reverse else ci) * c, c), c)
        lw = lw_ref[rows, :]
        cl = _dot_sel(jnp.where(incl, 1.0, 0.0), lw, 3)
        tot = cl[0:1, :] if reverse else cl[c - 1:c, :]
        e_in = jnp.exp(cl)
        e_out = jnp.exp(-cl)
        e_end = jnp.exp(tot - cl)
        kk = kk_ref[rows, :].astype(F32)
        kt = kt_ref[rows, :].astype(F32)
        ab = ab_ref[rows, :].astype(F32)
        kap = kk * jnp.exp(cl - lw)
        rh = r_ref[rows, :].astype(F32) * e_in
        kh = kt * e_out
        bh = ab * e_out
        kb = kt * e_end
        bb = ab * e_end
        e_tot = jnp.exp(tot)
        vv = v_ref[rows, :].astype(F32)
        pairs = range(D_MODEL // LANES)
        sls = [slice(p * LANES, (p + 1) * LANES) for p in pairs]
        s = [s_ref[p] for p in pairs]
        xq = [jnp.concatenate([kap[:, sl], rh[:, sl]], axis=0) for sl in sls]
        yk = [jnp.concatenate([stack2(kh[:, sl]), stack2(bh[:, sl])], axis=0) for sl in sls]
        gm = [_dot_nt(xq[p], yk[p]) for p in pairs]
        xs = [_dot_nt(xq[p], s[p]) for p in pairs]
        l_kk = [jnp.where(strict2, g[:c, :2 * c], 0.0) for g in gm]
        l_bk = [jnp.where(strict2, g[:c, 2 * c:], 0.0) for g in gm]
        a_rk = [jnp.where(incl2, g[c:, :2 * c], 0.0) for g in gm]
        a_rb = [jnp.where(incl2, g[c:, 2 * c:], 0.0) for g in gm]
        v2 = [stack2(vv[:, sl]) for sl in sls]
        x = [xs[p][:c] + _dot(l_kk[p], v2[p]) for p in pairs]
        lp = [_dot(l_bk[p], stack2(l_bk[p])) for p in pairs]
        x = [x[p] - _dot(l_bk[p], stack2(x[p])) for p in pairs]
        for it in range(5):
            x = [x[p] + _dot(lp[p], stack2(x[p])) for p in pairs]
            if it < 4:
                lp = [_dot(lp[p], stack2(lp[p])) for p in pairs]
        o = [xs[p][c:] + _dot(jnp.concatenate([a_rk[p], -a_rb[p]], axis=1),
                              jnp.concatenate([v2[p], stack2(x[p])], axis=0)) for p in pairs]
        upd = [_dot_tn(jnp.concatenate([vv[:, sls[p]], -x[p]], axis=0),
                       jnp.concatenate([kb[:, sls[p]], bb[:, sls[p]]], axis=0)) for p in pairs]
        for p in pairs:
            o_ref[rows, sls[p]] = o[p]
            s_ref[p] = s[p] * e_tot[:, sls[p]] + jnp.where(blockdiag, upd[p], 0.0)
        return carry

    lax.fori_loop(0, nch, one_chunk, 0)


def _rwkv_out_kernel(of_ref, ob_ref, bv_ref, g_ref, lg_ref, lb_ref, e_ref, et_ref, w_ref, h_ref, mod_ref, lng_ref,
                     lnb_ref, o_ref):
    o = of_ref[...] + ob_ref[...]
    inv = 1.0 / RWKV_HEAD
    oc = o - _seg_sum(o, e_ref, et_ref) * inv
    var = _seg_sum(oc * oc, e_ref, et_ref) * inv
    y = oc * lax.rsqrt(var + RWKV_GN_EPS) * lg_ref[...] + lb_ref[...] + bv_ref[...]
    yo = _dot(y * g_ref[...], w_ref[...])
    z = DEEPNORM_ALPHA * h_ref[...] + mod_ref[0][2:3] * yo
    o_ref[...] = _ln_rows(z, lng_ref[...], lnb_ref[...])


def _rwkv_mixer(h, mods, nct, mu, w_in, w0, w_l1, w_l2, a0, a_l1, a_l2, g_l1, g_l2, k_k, k_a, r_k, gn_g, gn_b, w_out,
                ln_g, ln_b):
    t, d = h.shape
    nt = t // TM
    bf = MXU_DT
    lw_ = w_l1.shape[-1]
    la_ = a_l1.shape[-1]
    zw = jnp.zeros((lw_, d), F32)
    za = jnp.zeros((la_, d), F32)
    wl1 = jnp.concatenate([w_l1[0], w_l1[1]], axis=1).astype(bf)
    wl2 = jnp.stack([jnp.concatenate([w_l2[0], zw], 0), jnp.concatenate([zw, w_l2[1]], 0)]).astype(bf)
    al1 = jnp.concatenate([a_l1[0], a_l1[1]], axis=1).astype(bf)
    al2 = jnp.stack([jnp.concatenate([a_l2[0], za], 0), jnp.concatenate([za, a_l2[1]], 0)]).astype(bf)
    head_of = jnp.arange(d) // RWKV_HEAD
    e = (head_of[:, None] == jnp.arange(LANES)[None, :]).astype(bf)
    et = e.T
    halo_p = pl.BlockSpec((8, d), lambda i: (jnp.maximum(i * (TM // 8) - 1, 0), 0))
    halo_n = pl.BlockSpec((8, d), lambda i: (jnp.minimum((i + 1) * (TM // 8), t // 8 - 1), 0))
    args = [h, h, h, mods, mu, w_in.astype(bf), wl1, wl2, w0, al1, al2, a0, g_l1.astype(bf), g_l2.astype(bf),
            k_k[None], k_a[None], r_k.reshape(1, d), e, et]
    ins = [_row_spec(d), halo_p, halo_n, _mod_spec(nct)] + [_full_spec(a.shape) for a in args[4:]]
    outs = pl.pallas_call(
        functools.partial(_rwkv_prep_kernel, nct=nct, nt=nt), grid=(nt,), in_specs=ins,
        out_specs=[_row_spec(d)] * 11,
        out_shape=[jax.ShapeDtypeStruct((t, d), dt) for dt in [ACT_DT] * 5 + [F32] * 2 + [ACT_DT] * 4],
        compiler_params=_cparams("arbitrary"), name="rwkv_prep",
    )(*args)
    r, v, kk, g, bv, lw0, lw1, kt0, kt1, ab0, ab1 = outs
    c = RWKV_CHUNK
    ncc, nc = nct * (TM // c), t // c
    o_dir = []
    for d_, (lw, kt, ab) in enumerate(((lw0, kt0, ab0), (lw1, kt1, ab1))):
        reverse = d_ == 1
        spec = pl.BlockSpec((TM, d), lambda g_, reverse=reverse: (_tile_of(g_, nct, nt, reverse), 0))
        o_dir.append(pl.pallas_call(
            functools.partial(_rwkv_scan_kernel, reverse=reverse), grid=(nt,), in_specs=[spec] * 6, out_specs=spec,
            out_shape=jax.ShapeDtypeStruct((t, d), F32),
            scratch_shapes=[pltpu.VMEM((d // LANES, LANES, LANES), F32)],
            compiler_params=_cparams("arbitrary"), name="rwkv_scan_%d" % d_,
        )(r, v, kk, lw, kt, ab))
    args = [o_dir[0], o_dir[1], bv, g, gn_g[None], gn_b[None], e, et, w_out.astype(bf), h, mods, ln_g[None], ln_b[None]]
    ins = [_row_spec(d)] * 4 + [_full_spec(a.shape) for a in args[4:9]] + [_row_spec(d), _mod_spec(nct),
                                                                          _full_spec((1, d)), _full_spec((1, d))]
    return pl.pallas_call(
        _rwkv_out_kernel, grid=(nt,), in_specs=ins, out_specs=_row_spec(d),
        out_shape=jax.ShapeDtypeStruct((t, d), F32), compiler_params=_cparams("arbitrary"), name="rwkv_out",
    )(*args)


def _ret_in_kernel(h_ref, mod_ref, w_ref, cos_ref, sin_ref, q_o, k_o, v_o, g_o):
    d = D_MODEL
    u = _modulate(h_ref[...], mod_ref[0], 0).astype(MXU_DT)
    q = _dot(u, w_ref[:, 0:d])
    k = _dot(u, w_ref[:, d:2 * d]) * (RET_QK ** -0.5)
    v_o[...] = _dot(u, w_ref[:, 2 * d:4 * d]).astype(v_o.dtype)
    g_o[...] = _silu(_dot(u, w_ref[:, 4 * d:6 * d])).astype(g_o.dtype)
    cos = cos_ref[...]
    sin = sin_ref[...]
    half = RET_QK // 2
    for z, z_o in ((q, q_o), (k, k_o)):
        for hh in range(RET_HEADS):
            lo = z[:, hh * RET_QK:hh * RET_QK + half]
            hi = z[:, hh * RET_QK + half:(hh + 1) * RET_QK]
            zh = jnp.concatenate([lo, hi], axis=1)
            rot = jnp.concatenate([-hi, lo], axis=1)
            z_o[:, hh * RET_QK:(hh + 1) * RET_QK] = (zh * cos + rot * sin).astype(z_o.dtype)


def _ret_scan_kernel(qf_ref, kf_ref, vf_ref, qb_ref, kb_ref, vb_ref, inner_ref, qd_ref, kd_ref, bd_ref, of_ref, ob_ref,
                     r_ref):
    @pl.when(pl.program_id(0) == 0)
    def _():
        r_ref[...] = jnp.zeros_like(r_ref)

    refs = ((qf_ref, kf_ref, vf_ref, of_ref), (qb_ref, kb_ref, vb_ref, ob_ref))
    cells = [(d, hh) for d in range(2) for hh in range(RET_HEADS)]
    qs = lambda hh: slice(hh * RET_QK, (hh + 1) * RET_QK)
    vs = lambda hh: slice(hh * RET_V, (hh + 1) * RET_V)
    q = {(d, hh): refs[d][0][:, qs(hh)] for d, hh in cells}
    k = {(d, hh): refs[d][1][:, qs(hh)] for d, hh in cells}
    v = {(d, hh): refs[d][2][:, vs(hh)] for d, hh in cells}
    state = {c: r_ref[c[0], c[1]] for c in cells}
    scores = {c: _dot_nt(q[c], k[c]) * inner_ref[c[0], c[1]] for c in cells}
    carry_in = {c: _dot(q[c], state[c]) * qd_ref[c[0], c[1]] for c in cells}
    upd = {c: _dot_tn(k[c] * kd_ref[c[0], c[1]], v[c]) for c in cells}
    for c in cells:
        refs[c[0]][3][:, vs(c[1])] = _dot(scores[c], v[c]) + carry_in[c]
        r_ref[c[0], c[1]] = state[c] * bd_ref[c[0], c[1]] + upd[c]


def _ret_out_kernel(of_ref, ob_ref, g_ref, gg_ref, gb_ref, w_ref, h_ref, mod_ref, lng_ref, lnb_ref, o_ref):
    parts = []
    for hh in range(RET_HEADS):
        sl = slice(hh * RET_V, (hh + 1) * RET_V)
        o = of_ref[:, sl] + ob_ref[:, sl]
        mu = jnp.mean(o, axis=-1, keepdims=True)
        oc = o - mu
        var = jnp.mean(oc * oc, axis=-1, keepdims=True)
        y = oc * lax.rsqrt(var + LN_EPS) * gg_ref[:, sl] + gb_ref[:, sl]
        parts.append((g_ref[:, sl] * y).astype(MXU_DT))
    yo = _dot(jnp.concatenate(parts, axis=1), w_ref[...])
    z = DEEPNORM_ALPHA * h_ref[...] + mod_ref[0][2:3] * yo
    o_ref[...] = _ln_rows(z, lng_ref[...], lnb_ref[...])


def _ret_mixer(h, mods, nct, rope_cos, rope_sin, w_in, decay_logit, gn_g, gn_b, w_out, ln_g, ln_b):
    t, d = h.shape
    nt = t // TM
    hv = RET_HEADS * RET_V
    q, k, v, sg = pl.pallas_call(
        _ret_in_kernel, grid=(nt,),
        in_specs=[_row_spec(d), _mod_spec(nct), _full_spec(w_in.shape), _row_spec(RET_QK), _row_spec(RET_QK)],
        out_specs=[_row_spec(d), _row_spec(d), _row_spec(hv), _row_spec(hv)],
        out_shape=[jax.ShapeDtypeStruct((t, w), ACT_DT) for w in (d, d, hv, hv)],
        compiler_params=_cparams("arbitrary"), name="ret_in",
    )(h, mods, w_in.astype(MXU_DT), rope_cos, rope_sin)
    c = RET_CHUNK
    ncc, nc = nct * (TM // c), t // c
    log_gamma = jax.nn.log_sigmoid(decay_logit.astype(F32))
    pos = jnp.arange(c, dtype=F32)
    tabs = []
    for d_ in range(2):
        lg = log_gamma[d_][:, None, None]
        p = (c - 1.0 - pos) if d_ == 1 else pos
        rel = p[:, None] - p[None, :]
        tabs.append((jnp.where(rel >= 0, jnp.exp(jnp.maximum(rel, 0.0) * lg), 0.0),
                     jnp.exp((p + 1.0) * log_gamma[d_][:, None])[:, :, None],
                     jnp.exp((c - 1.0 - p) * log_gamma[d_][:, None])[:, :, None],
                     jnp.exp(c * log_gamma[d_])[:, None, None]))
    inner, q_dec, k_dec, blk_dec = (jnp.stack(z) for z in zip(*tabs))
    cs = lambda w, reverse: pl.BlockSpec((c, w), lambda g_: (_tile_of(g_, ncc, nc, reverse), 0))
    o_dir = pl.pallas_call(
        _ret_scan_kernel, grid=(nc,),
        in_specs=[cs(d, False), cs(d, False), cs(hv, False), cs(d, True), cs(d, True), cs(hv, True),
                  _full_spec(inner.shape), _full_spec(q_dec.shape), _full_spec(k_dec.shape), _full_spec(blk_dec.shape)],
        out_specs=[cs(hv, False), cs(hv, True)], out_shape=[jax.ShapeDtypeStruct((t, hv), F32)] * 2,
        scratch_shapes=[pltpu.VMEM((2, RET_HEADS, RET_QK, RET_V), F32)],
        compiler_params=_cparams("arbitrary"), name="ret_scan",
    )(q, k, v, q, k, v, inner, q_dec, k_dec, blk_dec)
    return pl.pallas_call(
        _ret_out_kernel, grid=(nt,),
        in_specs=[_row_spec(hv)] * 3 + [_full_spec((1, hv)), _full_spec((1, hv)), _full_spec((hv, d)), _row_spec(d),
                                        _mod_spec(nct), _full_spec((1, d)), _full_spec((1, d))],
        out_specs=_row_spec(d), out_shape=jax.ShapeDtypeStruct((t, d), F32),
        compiler_params=_cparams("arbitrary"), name="ret_out",
    )(o_dir[0], o_dir[1], sg, gn_g[None], gn_b[None], w_out.astype(MXU_DT), h, mods, ln_g[None], ln_b[None])


def _hgrn_in_kernel(h_ref, mod_ref, w_ref, lb_ref, bf_ref, q_o, v_o, g_o, f0_o, f1_o):
    d = D_MODEL
    u = _modulate(h_ref[...], mod_ref[0], 0).astype(MXU_DT)
    lb = lb_ref[...]
    q_o[...] = _silu(_dot(u, w_ref[:, 0:d])).astype(q_o.dtype)
    f0_o[...] = lb + (1.0 - lb) * jax.nn.sigmoid(_dot(u, w_ref[:, d:2 * d]) + bf_ref[0:1, :])
    f1_o[...] = lb + (1.0 - lb) * jax.nn.sigmoid(_dot(u, w_ref[:, 2 * d:3 * d]) + bf_ref[1:2, :])
    v_o[...] = _dot(u, w_ref[:, 3 * d:4 * d]).astype(v_o.dtype)
    g_o[...] = _silu(_dot(u, w_ref[:, 4 * d:5 * d])).astype(g_o.dtype)


def _hgrn_scan_kernel(qf_ref, vf_ref, ff_ref, qb_ref, vb_ref, fb_ref, of_ref, ob_ref, s_ref, b_s, rb_s, rk_s, rv_s):
    hb = HGRN_BLOCK
    nb = TM // hb
    half = hb // 2
    dirs = ((qf_ref, vf_ref, ff_ref, of_ref, False), (qb_ref, vb_ref, fb_ref, ob_ref, True))

    @pl.when(pl.program_id(0) == 0)
    def _():
        s_ref[...] = jnp.zeros_like(s_ref)

    span = 4 * hb
    ri = lax.broadcasted_iota(jnp.int32, (span, span), 0)
    ci = lax.broadcasted_iota(jnp.int32, (span, span), 1)
    same_block = (ri // hb) == (ci // hb)
    ti = lax.broadcasted_iota(jnp.int32, (half, 1), 0)
    rowi = lax.broadcasted_iota(jnp.int32, (hb, 1), 0)
    heads = range(HGRN_HEADS)
    sls = [slice(hh * HGRN_HEAD, (hh + 1) * HGRN_HEAD) for hh in heads]
    cells = [(d, hh) for d in range(2) for hh in heads]

    for d, (q_ref, v_ref, f_ref, o_ref, reverse) in enumerate(dirs):
        tri = jnp.where(jnp.logical_and(same_block, (ci >= ri) if reverse else (ci <= ri)), 1.0, 0.0)
        for r0 in range(0, TM, span):
            b_s[d, r0:r0 + span] = _dot_sel(tri, jnp.log(f_ref[r0:r0 + span, :]), 3)

    def block(bi, par):
        pre = []
        for d, (q_ref, v_ref, f_ref, o_ref, reverse) in enumerate(dirs):
            blk = (nb - 1 - bi) if reverse else bi
            r0 = pl.multiple_of(blk * hb, hb)
            kx = 1.0 - f_ref[pl.ds(r0, hb), :]
            q = q_ref[pl.ds(r0, hb), :].astype(F32)
            v = v_ref[pl.ds(r0, hb), :].astype(F32)
            b = b_s[d, pl.ds(r0, hb), :]
            rb_s[d, par] = b
            rk_s[d, par] = kx
            rv_s[d, par] = v
            tot = b[0:1, :] if reverse else b[hb - 1:hb, :]
            first = (rowi >= half) if reverse else (rowi < half)
            beta = b[half:half + 1, :] if reverse else b[half - 1:half, :]
            pre.append(dict(
                r0=r0, q=q, v=v, b=b, qe=q * jnp.exp(b), kb=kx * jnp.exp(tot - b), e_tot=jnp.exp(tot),
                k_first=kx * jnp.exp(jnp.where(first, beta - b, -jnp.inf)),
                q_second=q * jnp.exp(jnp.where(first, -jnp.inf, b - beta)),
                causal=[(ti <= si) if reverse else (ti >= si) for si in range(half)]))
        s = {c: s_ref[c[0], c[1]] for c in cells}
        m_first = {(d, hh): _dot_tn(pre[d]['k_first'][:, sls[hh]], pre[d]['v'][:, sls[hh]]) for d, hh in cells}
        o = {(d, hh): _dot_nt(pre[d]['qe'][:, sls[hh]], s[d, hh]) + _dot(pre[d]['q_second'][:, sls[hh]], m_first[d, hh])
             for d, hh in cells}
        upd = {(d, hh): _dot_tn(pre[d]['v'][:, sls[hh]], pre[d]['kb'][:, sls[hh]]) for d, hh in cells}
        for d, hh in cells:
            sl = sls[hh]
            p = pre[d]
            parts = []
            for lo in (0, half):
                bt = p['b'][lo:lo + half, sl]
                qt = p['q'][lo:lo + half, sl]
                acc = jnp.zeros((half, HGRN_HEAD), F32)
                for si in range(half):
                    row = slice(lo + si, lo + si + 1)
                    dec = jnp.exp(jnp.where(p['causal'][si], bt - rb_s[d, par, row, sl], -jnp.inf))
                    sc = jnp.sum(qt * rk_s[d, par, row, sl] * dec, axis=-1, keepdims=True)
                    acc = acc + sc * rv_s[d, par, row, sl]
                parts.append(acc)
            dirs[d][3][pl.ds(p['r0'], hb), sl] = o[d, hh] + jnp.concatenate(parts, axis=0)
            s_ref[d, hh] = s[d, hh] * p['e_tot'][:, sl] + upd[d, hh]

    def two_blocks(bj, carry):
        block(2 * bj, 0)
        block(2 * bj + 1, 1)
        return carry

    lax.fori_loop(0, nb // 2, two_blocks, 0)


def _hgrn_out_kernel(of_ref, ob_ref, g_ref, ng_ref, w_ref, h_ref, mod_ref, lng_ref, lnb_ref, o_ref):
    parts = []
    for hh in range(HGRN_HEADS):
        sl = slice(hh * HGRN_HEAD, (hh + 1) * HGRN_HEAD)
        o = of_ref[:, sl] + ob_ref[:, sl]
        y = o * lax.rsqrt(jnp.mean(o * o, axis=-1, keepdims=True) + LN_EPS) * ng_ref[...]
        parts.append((y * g_ref[:, sl]).astype(MXU_DT))
    yo = _dot(jnp.concatenate(parts, axis=1), w_ref[...])
    z = DEEPNORM_ALPHA * h_ref[...] + mod_ref[0][2:3] * yo
    o_ref[...] = _ln_rows(z, lng_ref[...], lnb_ref[...])


def _hgrn_mixer(h, mods, nct, lb, w_in, b_f, norm_g, w_out, ln_g, ln_b):
    t, d = h.shape
    nt = t // TM
    q, v, sg, f0, f1 = pl.pallas_call(
        _hgrn_in_kernel, grid=(nt,),
        in_specs=[_row_spec(d), _mod_spec(nct), _full_spec(w_in.shape), _full_spec((1, d)), _full_spec((2, d))],
        out_specs=[_row_spec(d)] * 5,
        out_shape=[jax.ShapeDtypeStruct((t, d), dt) for dt in [ACT_DT] * 3 + [F32] * 2],
        compiler_params=_cparams("arbitrary"), name="hgrn_in",
    )(h, mods, w_in.astype(MXU_DT), lb[None], b_f)
    fwd = pl.BlockSpec((TM, d), lambda g_: (g_, 0))
    bwd = pl.BlockSpec((TM, d), lambda g_: (_tile_of(g_, nct, nt, True), 0))
    o_dir = pl.pallas_call(
        _hgrn_scan_kernel, grid=(nt,), in_specs=[fwd] * 3 + [bwd] * 3, out_specs=[fwd, bwd],
        out_shape=[jax.ShapeDtypeStruct((t, d), F32)] * 2,
        scratch_shapes=[pltpu.VMEM((2, HGRN_HEADS, HGRN_HEAD, HGRN_HEAD), F32)] + [pltpu.VMEM((2, TM, d), F32)]
        + [pltpu.VMEM((2, 2, HGRN_BLOCK, d), F32)] * 3,
        compiler_params=_cparams("arbitrary"), name="hgrn_scan",
    )(q, v, f0, q, v, f1)
    return pl.pallas_call(
        _hgrn_out_kernel, grid=(nt,),
        in_specs=[_row_spec(d)] * 3 + [_full_spec((1, HGRN_HEAD)), _full_spec((d, d)), _row_spec(d), _mod_spec(nct),
                                       _full_spec((1, d)), _full_spec((1, d))],
        out_specs=_row_spec(d), out_shape=jax.ShapeDtypeStruct((t, d), F32),
        compiler_params=_cparams("arbitrary"), name="hgrn_out",
    )(o_dir[0], o_dir[1], sg, norm_g[None], w_out.astype(MXU_DT), h, mods, ln_g[None], ln_b[None])


def _router_kernel(h_ref, mod_ref, rw_ref, rb_ref, u_o, gate_o, rank_o, x_o):
    u = _modulate(h_ref[...], mod_ref[0], 3)
    u_o[...] = u.astype(u_o.dtype)
    w_hi, w_lo = _split(rw_ref[...], 2)
    u_hi, u_lo = _split(u, 2)
    nt_dims = (((1,), (1,)), ((), ()))
    logits = (lax.dot_general(w_hi, u_hi, nt_dims, preferred_element_type=F32)
              + lax.dot_general(w_hi, u_lo, nt_dims, preferred_element_type=F32)
              + lax.dot_general(w_lo, u_hi, nt_dims, preferred_element_type=F32))
    ne, gs = N_EXPERTS, N_EXPERTS // N_GROUPS
    neg = -jnp.inf
    scores = jax.nn.sigmoid(logits[:ne])
    choice = scores + rb_ref[:ne]
    c3 = choice.reshape(N_GROUPS, gs, TM)
    mi = lax.broadcasted_iota(jnp.int32, c3.shape, 1).astype(F32)
    m1 = jnp.max(c3, axis=1, keepdims=True)
    i1 = jnp.min(jnp.where(c3 == m1, mi, float(gs)), axis=1, keepdims=True)
    m2 = jnp.max(jnp.where(mi == i1, neg, c3), axis=1, keepdims=True)
    gscore = m1 + m2
    gi = lax.broadcasted_iota(jnp.int32, gscore.shape, 0).astype(F32)
    gsel = jnp.zeros(gscore.shape, F32)
    for _ in range(TOPK_GROUPS):
        gm = jnp.max(gscore, axis=0, keepdims=True)
        pick = gi == jnp.min(jnp.where(gscore == gm, gi, float(N_GROUPS)), axis=0, keepdims=True)
        gsel = jnp.where(pick, 1.0, gsel)
        gscore = jnp.where(pick, neg, gscore)
    emask = jnp.broadcast_to(gsel, c3.shape).reshape(ne, TM)
    masked = jnp.where(emask > 0.5, choice, neg)
    ei = lax.broadcasted_iota(jnp.int32, masked.shape, 0).astype(F32)
    chosen = jnp.zeros(masked.shape, F32)
    for _ in range(TOP_K):
        em = jnp.max(masked, axis=0, keepdims=True)
        pick = ei == jnp.min(jnp.where(masked == em, ei, float(ne)), axis=0, keepdims=True)
        chosen = jnp.where(pick, 1.0, chosen)
        masked = jnp.where(pick, neg, masked)
    top_w = scores * chosen
    gates = ROUTED_SCALE * top_w / jnp.sum(top_w, axis=0, keepdims=True)
    ti = lax.broadcasted_iota(jnp.int32, (TM, TM), 0)
    tj = lax.broadcasted_iota(jnp.int32, (TM, TM), 1)
    before = jnp.where(ti < tj, 1.0, 0.0).astype(MXU_DT)
    prefix = jnp.dot(chosen.astype(MXU_DT), before, preferred_element_type=F32)
    rank = jnp.where(chosen > 0.5, prefix, -1.0)
    gate_o[0] = gates
    rank_o[0] = rank
    cap = MOE_CAP
    slot = lax.broadcasted_iota(jnp.int32, (cap, TM), 0).astype(F32)
    ub = u.astype(MXU_DT)
    for g0 in range(0, ne, MOE_EGROUP):
        onehot = jnp.concatenate([jnp.where(slot == rank[e:e + 1, :], 1.0, 0.0).astype(MXU_DT)
                                  for e in range(g0, g0 + MOE_EGROUP)], axis=0)
        xg = jnp.dot(onehot, ub, preferred_element_type=F32)
        x_o[0, g0:g0 + MOE_EGROUP] = xg.reshape(MOE_EGROUP, cap, D_MODEL).astype(x_o.dtype)


def _expert_kernel(x_ref, wgu_ref, wd_ref, y_ref, wgu_b, wd_b):
    @pl.when(pl.program_id(1) == 0)
    def _():
        wgu_b[0] = wgu_ref[0, 0].astype(wgu_b.dtype)
        wd_b[0] = wd_ref[0, 0].astype(wd_b.dtype)

    g = x_ref.shape[0]
    ch = max(c for c in range(1, MOE_CHUNK + 1) if g % c == 0)
    ed = EXPERT_DIM

    def chunk(ci, carry):
        t0 = ci * ch
        x = x_ref[pl.ds(t0, ch)].reshape(ch * MOE_CAP, D_MODEL)
        gu = _dot(x, wgu_b[0])
        y = _dot(_silu(gu[:, :ed]) * gu[:, ed:], wd_b[0])
        y_ref[pl.ds(t0, ch)] = y.reshape(ch, 1, MOE_CAP, D_MODEL).astype(y_ref.dtype)
        return carry

    lax.fori_loop(0, g // ch, chunk, 0)


def _combine_kernel(u_ref, gt_ref, rt_ref, y_ref, sgu_ref, sd_ref, h_ref, mod_ref, lng_ref, lnb_ref, *rest, extra):
    if extra:
        ex_ref, o_ref = rest
    else:
        (o_ref,) = rest
    ed = EXPERT_DIM
    cap = MOE_CAP
    gu = _dot(u_ref[...], sgu_ref[...])
    acc = _dot(_silu(gu[:, :ed]) * gu[:, ed:], sd_ref[...])
    if extra:
        acc = acc + ex_ref[...]
    slot = lax.broadcasted_iota(jnp.int32, (cap, TM), 0).astype(F32)
    for g0 in range(0, N_EXPERTS, MOE_EGROUP):
        pw = jnp.concatenate([jnp.where(slot == rt_ref[0, e:e + 1, :], gt_ref[0, e:e + 1, :], 0.0).astype(MXU_DT)
                              for e in range(g0, g0 + MOE_EGROUP)], axis=0)
        yg = y_ref[0, g0:g0 + MOE_EGROUP].reshape(MOE_EGROUP * cap, D_MODEL)
        acc = acc + _dot_tn(pw, yg)
    z = DEEPNORM_ALPHA * h_ref[...] + mod_ref[0][5:6] * acc
    o_ref[...] = _ln_rows(z, lng_ref[...], lnb_ref[...])


def _overflow_kernel(tile_ref, exp_ref, nr_ref, n_ref, u_ref, gate_ref, rank_ref, wgu_ref, wd_ref, zero_ref, o_ref):
    del zero_ref
    s = pl.program_id(0)
    tile = tile_ref[s]
    e = exp_ref[s]
    ed = EXPERT_DIM
    cap = MOE_CAP
    active = s < n_ref[0]
    first = jnp.logical_or(s == 0, tile_ref[jnp.maximum(s - 1, 0)] != tile)

    @pl.when(jnp.logical_and(active, first))
    def _():
        o_ref[...] = jnp.zeros_like(o_ref)

    @pl.when(active)
    def _():
        wgu = wgu_ref[0]
        wd = wd_ref[0]
        rank = rank_ref[0, pl.ds(e, 1), :]
        gate = gate_ref[0, pl.ds(e, 1), :]

        def one_round(r, carry):
            slot = lax.broadcasted_iota(jnp.int32, (cap, TM), 0).astype(F32) + (r * cap).astype(F32)
            hit = slot == rank
            x = _dot(jnp.where(hit, 1.0, 0.0), u_ref[...])
            gu = _dot(x, wgu)
            y = _dot(_silu(gu[:, :ed]) * gu[:, ed:], wd)
            o_ref[...] += _dot_tn(jnp.where(hit, gate, 0.0), y)
            return carry

        lax.fori_loop(1, nr_ref[s], one_round, 0)


def _moe_layer(h, mods, nct, layer, router_w, router_b, w_gu, w_down, sh_gu, sh_down, ln_g, ln_b):
    t, d = h.shape
    nt = t // TM
    ne, cap = N_EXPERTS, MOE_CAP
    rw = jnp.concatenate([router_w.T, jnp.zeros((LANES - ne, d), F32)], axis=0)
    rb = jnp.concatenate([router_b, jnp.zeros((LANES - ne,), F32)])[:, None]
    per_tile = pl.BlockSpec((1, ne, TM), lambda i: (i, 0, 0))
    slots = pl.BlockSpec((1, ne, cap, d), lambda i: (i, 0, 0, 0))
    u, gates, ranks, xs = pl.pallas_call(
        _router_kernel, grid=(nt,),
        in_specs=[_row_spec(d), _mod_spec(nct), _full_spec((LANES, d)), _full_spec((LANES, 1))],
        out_specs=[_row_spec(d), per_tile, per_tile, slots],
        out_shape=[jax.ShapeDtypeStruct((t, d), MXU_DT), jax.ShapeDtypeStruct((nt, ne, TM), F32),
                   jax.ShapeDtypeStruct((nt, ne, TM), F32), jax.ShapeDtypeStruct((nt, ne, cap, d), MXU_DT)],
        compiler_params=_cparams("arbitrary"), name="moe_router",
    )(h, mods, rw, rb)

    run = max(g for g in range(1, MOE_RUN + 1) if nt % g == 0)
    ys, wgu_b, wd_b = pl.pallas_call(
        _expert_kernel, grid=(ne, nt // run),
        in_specs=[pl.BlockSpec((run, 1, cap, d), lambda e, c: (c, e, 0, 0)),
                  pl.BlockSpec((1, 1, d, 2 * EXPERT_DIM), lambda e, c: (layer, e, 0, 0)),
                  pl.BlockSpec((1, 1, EXPERT_DIM, d), lambda e, c: (layer, e, 0, 0))],
        out_specs=[pl.BlockSpec((run, 1, cap, d), lambda e, c: (c, e, 0, 0)),
                   pl.BlockSpec((1, d, 2 * EXPERT_DIM), lambda e, c: (e, 0, 0)),
                   pl.BlockSpec((1, EXPERT_DIM, d), lambda e, c: (e, 0, 0))],
        out_shape=[jax.ShapeDtypeStruct((nt, ne, cap, d), MXU_DT),
                   jax.ShapeDtypeStruct((ne, d, 2 * EXPERT_DIM), MXU_DT), jax.ShapeDtypeStruct((ne, EXPERT_DIM, d), MXU_DT)],
        compiler_params=_cparams("arbitrary", "arbitrary"), name="moe_experts",
    )(xs, w_gu, w_down)

    sgu, sd = sh_gu.astype(MXU_DT), sh_down.astype(MXU_DT)
    base_specs = [_row_spec(d), per_tile, per_tile, slots, _full_spec(sgu.shape), _full_spec(sd.shape),
                  _row_spec(d), _mod_spec(nct), _full_spec((1, d)), _full_spec((1, d))]
    base_args = (u, gates, ranks, ys, sgu, sd, h, mods, ln_g[None], ln_b[None])

    def combine(*extra):
        return pl.pallas_call(
            functools.partial(_combine_kernel, extra=bool(extra)), grid=(nt,),
            in_specs=base_specs + [_row_spec(d)] * len(extra), out_specs=_row_spec(d),
            out_shape=jax.ShapeDtypeStruct((t, d), F32), compiler_params=_cparams("arbitrary"), name="moe_combine",
        )(*base_args, *extra)

    count = (jnp.max(ranks, axis=-1).astype(jnp.int32) + 1).reshape(-1)
    over = count > cap
    n_over = jnp.sum(over.astype(jnp.int32))

    def with_overflow(size):
        def run():
            idx = jnp.nonzero(over, size=size, fill_value=0)[0].astype(jnp.int32)
            idx = jnp.where(jnp.arange(size) < n_over, idx, idx[jnp.maximum(n_over - 1, 0)])
            tiles, exps = idx // ne, idx % ne
            rounds = (count[idx] + cap - 1) // cap
            grid_spec = pltpu.PrefetchScalarGridSpec(
                num_scalar_prefetch=4, grid=(size,),
                in_specs=[pl.BlockSpec((TM, d), lambda s, tl, ex, nr, n: (tl[s], 0)),
                          pl.BlockSpec((1, ne, TM), lambda s, tl, ex, nr, n: (tl[s], 0, 0)),
                          pl.BlockSpec((1, ne, TM), lambda s, tl, ex, nr, n: (tl[s], 0, 0)),
                          pl.BlockSpec((1, d, 2 * EXPERT_DIM), lambda s, tl, ex, nr, n: (ex[s], 0, 0)),
                          pl.BlockSpec((1, EXPERT_DIM, d), lambda s, tl, ex, nr, n: (ex[s], 0, 0)),
                          pl.BlockSpec(memory_space=pl.ANY)],
                out_specs=pl.BlockSpec((TM, d), lambda s, tl, ex, nr, n: (tl[s], 0)))
            extra = pl.pallas_call(
                _overflow_kernel, grid_spec=grid_spec, out_shape=jax.ShapeDtypeStruct((t, d), F32),
                input_output_aliases={9: 0}, compiler_params=_cparams("arbitrary"), name="moe_overflow",
            )(tiles, exps, rounds, n_over[None], u, gates, ranks, wgu_b, wd_b, jnp.zeros((t, d), F32))
            return combine(extra)
        return run

    sizes = sorted({min(MOE_OVER_STEPS, nt * ne), nt * ne})
    branch = sum((n_over > sz).astype(jnp.int32) for sz in [0] + sizes[:-1])
    return lax.switch(branch, [combine] + [with_overflow(sz) for sz in sizes])


def kernel(x, c, ctx, c_ctx, ada_w, ada_b, post_ln_g, post_ln_b, lru_w_in, lru_conv_w, lru_conv_b, lru_gate_w, lru_gate_b, lru_lambda, lru_w_out, rwkv_mu, rwkv_w_in, rwkv_w0, rwkv_w_l1, rwkv_w_l2, rwkv_a0, rwkv_a_l1, rwkv_a_l2, rwkv_g_l1, rwkv_g_l2, rwkv_k_k, rwkv_k_a, rwkv_r_k, rwkv_ln_g, rwkv_ln_b, rwkv_w_out, ret_w_in, ret_decay, ret_gn_g, ret_gn_b, ret_w_out, hgrn_w_in, hgrn_b_f, hgrn_lb, hgrn_norm_g, hgrn_w_out, moe_router, moe_bias, moe_w_gu, moe_w_down, moe_sh_gu, moe_sh_down):
    assert x.shape[0] == 1 and ctx.shape[0] == 1
    n_ctx, n_lat, d = ctx.shape[1], x.shape[1], x.shape[2]
    assert n_ctx % TM == 0 and n_lat % TM == 0 and d == D_MODEL
    nct = n_ctx // TM
    rows = n_lat // GRID_W
    pos_row = jnp.repeat(jnp.arange(rows, dtype=F32), GRID_W)
    pos_col = jnp.tile(jnp.arange(GRID_W, dtype=F32), rows)
    n_freq = RET_QK // 4
    freqs = ROPE_BASE ** (-jnp.arange(n_freq, dtype=F32) / n_freq)
    ang = jnp.concatenate([pos_row[:, None] * freqs, pos_col[:, None] * freqs], axis=-1)
    ang = jnp.concatenate([ang, ang], axis=-1)
    rope_cos = jnp.concatenate([jnp.ones((n_ctx, RET_QK), F32), jnp.cos(ang)], axis=0)
    rope_sin = jnp.concatenate([jnp.zeros((n_ctx, RET_QK), F32), jnp.sin(ang)], axis=0)
    lb_cum = jnp.cumsum(jax.nn.softmax(hgrn_lb.astype(F32), axis=0), axis=0)

    cond = jnp.concatenate([c_ctx[None], c, jnp.zeros((6, d), F32)], axis=0)
    mods_all = _ada_mods(cond, ada_w, ada_b)
    h = jnp.concatenate([ctx[0], x[0]], axis=0)
    for i in range(DEPTH):
        kind, j = i % N_MIXERS, i // N_MIXERS
        mods = mods_all[i]
        lng, lnb = post_ln_g[i, 0], post_ln_b[i, 0]
        if kind == 0:
            h = _lru_mixer(h, mods, nct, lru_w_in[j], lru_conv_w[j], lru_conv_b[j], lru_gate_w[j], lru_gate_b[j],
                           lru_lambda[j], lru_w_out[j], lng, lnb)
        elif kind == 1:
            h = _rwkv_mixer(h, mods, nct, rwkv_mu[j], rwkv_w_in[j], rwkv_w0[j], rwkv_w_l1[j], rwkv_w_l2[j], rwkv_a0[j],
                            rwkv_a_l1[j], rwkv_a_l2[j], rwkv_g_l1[j], rwkv_g_l2[j], rwkv_k_k[j], rwkv_k_a[j],
                            rwkv_r_k[j], rwkv_ln_g[j], rwkv_ln_b[j], rwkv_w_out[j], lng, lnb)
        elif kind == 2:
            h = _ret_mixer(h, mods, nct, rope_cos, rope_sin, ret_w_in[j], ret_decay[j], ret_gn_g[j], ret_gn_b[j],
                           ret_w_out[j], lng, lnb)
        else:
            h = _hgrn_mixer(h, mods, nct, lb_cum[i] - lb_cum[0], hgrn_w_in[j], hgrn_b_f[j], hgrn_norm_g[j],
                            hgrn_w_out[j], lng, lnb)
        h = _moe_layer(h, mods, nct, i, moe_router[i], moe_bias[i], moe_w_gu, moe_w_down, moe_sh_gu[i],
                       moe_sh_down[i], post_ln_g[i, 1], post_ln_b[i, 1])
    return h[n_ctx:][None]
```

```python
import math
import functools
import jax
import jax.numpy as jnp
from jax import lax
from jax.experimental import pallas as pl
from jax.experimental.pallas import tpu as pltpu

F32 = jnp.float32
MXU_DT = jnp.bfloat16
ACT_DT = jnp.bfloat16
LANES = 128
TM = 256
VMEM_LIMIT = 56 * 2 ** 20

D_MODEL = 1024
DEPTH = 4
GRID_W = 64
N_MIXERS = 4
DEEPNORM_ALPHA = (2.0 * DEPTH) ** 0.25
LN_EPS = 1e-5
LRU_WIDTH = D_MODEL
LRU_BLOCKS = 16
LRU_BLOCK = LRU_WIDTH // LRU_BLOCKS
LRU_C = 8.0
RWKV_HEAD = 64
RWKV_HEADS = D_MODEL // RWKV_HEAD
RWKV_DECAY_SCALE = math.exp(-0.5)
RWKV_GN_EPS = 64e-5
RWKV_CHUNK = 64
RET_HEADS = 4
RET_QK = D_MODEL // RET_HEADS
RET_V = 2 * RET_QK
RET_CHUNK = 128
ROPE_BASE = 10000.0
HGRN_HEADS = 8
HGRN_HEAD = D_MODEL // HGRN_HEADS
HGRN_BLOCK = 16
HGRN_GROUP = 16
HGRN_UNROLL = 4
N_EXPERTS = 64
TOP_K = 8
N_GROUPS = 8
TOPK_GROUPS = 4
EXPERT_DIM = 256
ROUTED_SCALE = 2.5
MOE_OVER_STEPS = 256
MOE_CAP = 64
MOE_EGROUP = 8
MOE_RUN = 65
MOE_CHUNK = 13


def _cparams(*sem):
    return pltpu.CompilerParams(dimension_semantics=sem, vmem_limit_bytes=VMEM_LIMIT)


def _dot(a, b):
    return jnp.dot(a.astype(MXU_DT), b.astype(MXU_DT), preferred_element_type=F32)


def _dot_nt(a, b):
    return lax.dot_general(a.astype(MXU_DT), b.astype(MXU_DT), (((1,), (1,)), ((), ())), preferred_element_type=F32)


def _dot_tn(a, b):
    return lax.dot_general(a.astype(MXU_DT), b.astype(MXU_DT), (((0,), (0,)), ((), ())), preferred_element_type=F32)


def _split(x, n):
    parts = []
    for _ in range(n):
        p = x.astype(MXU_DT)
        parts.append(p)
        x = x - p.astype(F32)
    return parts


def _dot_sel(sel, x, n):
    return sum(jnp.dot(sel.astype(MXU_DT), p, preferred_element_type=F32) for p in _split(x, n))


def _dot_xsel(x, sel, n):
    return sum(jnp.dot(p, sel.astype(MXU_DT), preferred_element_type=F32) for p in _split(x, n))


def _modulate(h, m, shift_idx):
    return h * (1.0 + m[shift_idx + 1:shift_idx + 2]) + m[shift_idx:shift_idx + 1]


def _ln_rows(z, g, b):
    mu = jnp.mean(z, axis=-1, keepdims=True)
    zc = z - mu
    var = jnp.mean(zc * zc, axis=-1, keepdims=True)
    return zc * lax.rsqrt(var + LN_EPS) * g + b


def _silu(x):
    return x * jax.nn.sigmoid(x)


def _shift_down(x, first_row):
    rows = lax.broadcasted_iota(jnp.int32, (x.shape[0], 1), 0)
    return jnp.where(rows == 0, first_row, pltpu.roll(x, 1, 0))


def _shift_up(x, last_row):
    n = x.shape[0]
    rows = lax.broadcasted_iota(jnp.int32, (n, 1), 0)
    return jnp.where(rows == n - 1, last_row, pltpu.roll(x, n - 1, 0))


def _tile_of(g, nct, nt, reverse):
    if not reverse:
        return g
    return jnp.where(g < nct, nct - 1 - g, nt - 1 - (g - nct))


def _halo_flags(t, nct, nt):
    prev_ok = jnp.logical_and(t != 0, t != nct).astype(F32)
    next_ok = jnp.logical_and(t != nct - 1, t != nt - 1).astype(F32)
    return prev_ok, next_ok


def _ada_kernel(s_ref, w_ref, b_ref, o_ref):
    o_ref[0] = _dot(_silu(s_ref[...]), w_ref[0]) + b_ref[0]


def _ada_mods(cond, ada_w, ada_b):
    nl, d, n6 = ada_w.shape
    out = pl.pallas_call(
        _ada_kernel, grid=(nl, n6 // d),
        in_specs=[pl.BlockSpec((8, d), lambda l, j: (0, 0)),
                  pl.BlockSpec((1, d, d), lambda l, j: (l, 0, j)),
                  pl.BlockSpec((1, 1, d), lambda l, j: (l, 0, j))],
        out_specs=pl.BlockSpec((1, 8, d), lambda l, j: (l, 0, j)),
        out_shape=jax.ShapeDtypeStruct((nl, 8, n6), F32),
        compiler_params=_cparams("arbitrary", "arbitrary"), name="ada_mods",
    )(cond, ada_w, ada_b.reshape(nl, 1, n6))
    return out[:, :2].reshape(nl, 2, 6, d)


def _row_spec(width, tm=TM):
    return pl.BlockSpec((tm, width), lambda i: (i, 0))


def _full_spec(shape):
    nd = len(shape)
    return pl.BlockSpec(tuple(shape), lambda *_: (0,) * nd)


def _mod_spec(nct):
    return pl.BlockSpec((1, 6, D_MODEL), lambda i: (jnp.minimum(i // nct, 1), 0, 0))


def _lru_out_kernel(g_ref, hf_ref, hb_ref, w_ref, h_ref, mod_ref, lng_ref, lnb_ref, o_ref):
    y = _dot(g_ref[...] * (hf_ref[...] + hb_ref[...]), w_ref[...])
    z = DEEPNORM_ALPHA * h_ref[...] + mod_ref[0][2:3] * y
    o_ref[...] = _ln_rows(z, lng_ref[...], lnb_ref[...])

def _lru_in_kernel(h_ref, mod_ref, w_ref, g_ref, x_ref):
    u = _modulate(h_ref[...], mod_ref[0], 0)
    z = _dot(u, w_ref[...])
    g_ref[...] = jax.nn.gelu(z[:, :LRU_WIDTH], approximate=True).astype(g_ref.dtype)
    x_ref[...] = z[:, LRU_WIDTH:]


def _lru_scan_kernel(xf_ref, xfp_ref, xfn_ref, xb_ref, xbp_ref, xbn_ref, cw_ref, cb_ref, gw_ref, gb_ref, lam_ref,
                     hf_o, hb_o, a_s, b_s, st_s, *, nct, nt):
    g = pl.program_id(0)

    @pl.when(g == 0)
    def _():
        st_s[...] = jnp.zeros_like(st_s)

    cw = cw_ref[...]
    for d, (x_ref, xp_ref, xn_ref) in enumerate(((xf_ref, xfp_ref, xfn_ref), (xb_ref, xbp_ref, xbn_ref))):
        prev_ok, next_ok = _halo_flags(_tile_of(g, nct, nt, d == 1), nct, nt)
        x = x_ref[...]
        xm1 = _shift_down(x, xp_ref[7:8, :] * prev_ok)
        xp1 = _shift_up(x, xn_ref[0:1, :] * next_ok)
        xp2 = _shift_up(xp1, xn_ref[1:2, :] * next_ok)
        xc = cw[0:1] * xm1 + cw[1:2] * x + cw[2:3] * xp1 + cw[3:4] * xp2 + cb_ref[...]
        gates = jax.nn.sigmoid(_dot(xc, gw_ref[d]) + gb_ref[d])
        lam = lam_ref[d:d + 1, :]
        softplus = jnp.maximum(-lam, 0.0) + jnp.log(1.0 + jnp.exp(-jnp.abs(lam)))
        log_a = -LRU_C * gates[:, :LRU_WIDTH] * softplus
        a_s[d] = jnp.exp(log_a)
        b_s[d] = jnp.sqrt(1.0 - jnp.exp(2.0 * log_a)) * (gates[:, LRU_WIDTH:] * xc)

    def row(r, carry):
        hf, hb = carry
        rb = TM - 1 - r
        hf = a_s[0, pl.ds(r, 1), :] * hf + b_s[0, pl.ds(r, 1), :]
        hb = a_s[1, pl.ds(rb, 1), :] * hb + b_s[1, pl.ds(rb, 1), :]
        hf_o[pl.ds(r, 1), :] = hf
        hb_o[pl.ds(rb, 1), :] = hb
        return hf, hb

    hf, hb = lax.fori_loop(0, TM, row, (st_s[0], st_s[1]), unroll=8)
    st_s[0] = hf
    st_s[1] = hb


def _lru_mixer(h, mods, nct, w_in, conv_w, conv_b, gate_w, gate_b, lam, w_out, ln_g, ln_b):
    t, d = h.shape
    nt = t // TM
    w = LRU_WIDTH
    gelu, rnn = pl.pallas_call(
        _lru_in_kernel, grid=(nt,),
        in_specs=[_row_spec(d), _mod_spec(nct), _full_spec((d, 2 * w))],
        out_specs=[_row_spec(w), _row_spec(w)],
        out_shape=[jax.ShapeDtypeStruct((t, w), ACT_DT), jax.ShapeDtypeStruct((t, w), F32)],
        compiler_params=_cparams("arbitrary"), name="lru_in",
    )(h, mods, w_in.astype(MXU_DT))
    eye = jnp.eye(LRU_BLOCKS, dtype=F32)
    gw = jnp.einsum('dgnij,nm->dgnimj', gate_w, eye).reshape(2, 2, w, w)
    gw = jnp.concatenate([gw[:, 0], gw[:, 1]], axis=-1).astype(MXU_DT)
    gb = gate_b.reshape(2, 1, 2 * w)
    def tile_specs(reverse):
        tile = lambda g: _tile_of(g, nct, nt, reverse)
        return [pl.BlockSpec((TM, w), lambda g: (tile(g), 0)),
                pl.BlockSpec((8, w), lambda g: (jnp.maximum(tile(g) * (TM // 8) - 1, 0), 0)),
                pl.BlockSpec((8, w), lambda g: (jnp.minimum((tile(g) + 1) * (TM // 8), t // 8 - 1), 0))]

    fwd, bwd = tile_specs(False), tile_specs(True)
    hf, hb = pl.pallas_call(
        functools.partial(_lru_scan_kernel, nct=nct, nt=nt), grid=(nt,),
        in_specs=fwd + bwd + [_full_spec((4, w)), _full_spec((1, w)), _full_spec((2, w, 2 * w)),
                              _full_spec((2, 1, 2 * w)), _full_spec((2, w))],
        out_specs=[fwd[0], bwd[0]], out_shape=[jax.ShapeDtypeStruct((t, w), F32)] * 2,
        scratch_shapes=[pltpu.VMEM((2, TM, w), F32)] * 2 + [pltpu.VMEM((2, 1, w), F32)],
        compiler_params=_cparams("arbitrary"), name="lru_scan",
    )(rnn, rnn, rnn, rnn, rnn, rnn, conv_w, conv_b[None], gw, gb, lam)
    return pl.pallas_call(
        _lru_out_kernel, grid=(nt,),
        in_specs=[_row_spec(w)] * 3 + [_full_spec((w, d)), _row_spec(d), _mod_spec(nct), _full_spec((1, d)),
                                       _full_spec((1, d))],
        out_specs=_row_spec(d), out_shape=jax.ShapeDtypeStruct((t, d), F32),
        compiler_params=_cparams("arbitrary"), name="lru_out",
    )(gelu, hf, hb, w_out.astype(MXU_DT), h, mods, ln_g[None], ln_b[None])


def _seg_sum(x, e_ref, et_ref):
    s = _dot_xsel(x, e_ref[...], 2)
    return _dot_xsel(s, et_ref[...], 2)


def _rwkv_prep_kernel(h_ref, hp_ref, hn_ref, mod_ref, mu_ref, win_ref, wl1_ref, wl2_ref, w0_ref, al1_ref, al2_ref,
                      a0_ref, gl1_ref, gl2_ref, kk_ref, ka_ref, rk_ref, e_ref, et_ref,
                      r_o, v_o, kk_o, g_o, bv_o, lw0_o, lw1_o, kt0_o, kt1_o, ab0_o, ab1_o, *, nct, nt):
    i = pl.program_id(0)
    prev_ok, next_ok = _halo_flags(i, nct, nt)
    m = mod_ref[0]
    u = _modulate(h_ref[...], m, 0)
    up = _modulate(hp_ref[7:8, :], m, 0) * prev_ok
    un = _modulate(hn_ref[0:1, :], m, 0) * next_ok
    lane = lax.broadcasted_iota(jnp.int32, (1, D_MODEL), 1)
    sh = jnp.where(lane < D_MODEL // 2, _shift_down(u, up), _shift_up(u, un))
    dx = sh - u
    mu = mu_ref[...]
    xm = [u + dx * mu[c:c + 1] for c in range(6)]
    r = _dot(xm[0], win_ref[0])
    k = _dot(xm[1], win_ref[1])
    v = _dot(xm[2], win_ref[2])
    t1 = jnp.tanh(_dot(xm[3], wl1_ref[...]))
    t2 = _dot(xm[4], al1_ref[...])
    g = _dot(jax.nn.sigmoid(_dot(xm[5], gl1_ref[...])), gl2_ref[...])
    kk = k * kk_ref[...]
    kk = kk * lax.rsqrt(_seg_sum(kk * kk, e_ref, et_ref) + 1e-12)
    ktsum = None
    for z, (lw_o, kt_o, ab_o) in enumerate(((lw0_o, kt0_o, ab0_o), (lw1_o, kt1_o, ab1_o))):
        d_w = w0_ref[z:z + 1, :] + _dot(t1, wl2_ref[z])
        lw_o[...] = -RWKV_DECAY_SCALE * jax.nn.sigmoid(d_w)
        a = jax.nn.sigmoid(a0_ref[z:z + 1, :] + _dot(t2, al2_ref[z]))
        kt = k * (1.0 + (a - 1.0) * ka_ref[...])
        kt_o[...] = kt.astype(kt_o.dtype)
        ab_o[...] = (kk * a).astype(ab_o.dtype)
        ktsum = kt if ktsum is None else ktsum + kt
    r_o[...] = r.astype(r_o.dtype)
    v_o[...] = v.astype(v_o.dtype)
    kk_o[...] = kk.astype(kk_o.dtype)
    g_o[...] = g.astype(g_o.dtype)
    bv_o[...] = (_seg_sum(r * ktsum * rk_ref[...], e_ref, et_ref) * v).astype(bv_o.dtype)


def _rwkv_scan_kernel(r_ref, v_ref, kk_ref, lw_ref, kt_ref, ab_ref, o_ref, s_ref, *, reverse):
    c = RWKV_CHUNK

    @pl.when(pl.program_id(0) == 0)
    def _():
        s_ref[...] = jnp.zeros_like(s_ref)

    ri = lax.broadcasted_iota(jnp.int32, (c, c), 0)
    ci = lax.broadcasted_iota(jnp.int32, (c, c), 1)
    incl = (ci >= ri) if reverse else (ci <= ri)
    ri2 = lax.broadcasted_iota(jnp.int32, (c, 2 * c), 0)
    ci2 = jnp.bitwise_and(lax.broadcasted_iota(jnp.int32, (c, 2 * c), 1), c - 1)
    incl2 = (ci2 >= ri2) if reverse else (ci2 <= ri2)
    strict2 = (ci2 > ri2) if reverse else (ci2 < ri2)
    lane_a = lax.broadcasted_iota(jnp.int32, (1, LANES), 1) < RWKV_HEAD
    bi = lax.broadcasted_iota(jnp.int32, (LANES, LANES), 0) < RWKV_HEAD
    bj = lax.broadcasted_iota(jnp.int32, (LANES, LANES), 1) < RWKV_HEAD
    blockdiag = bi == bj

    def stack2(x):
        return jnp.concatenate([jnp.where(lane_a, x, 0.0), jnp.where(lane_a, 0.0, x)], axis=0)

    nch = r_ref.shape[0] // c

    def one_chunk(ci, carry):
        rows = pl.ds(pl.multiple_of(((nch - 1 - ci) if reverse else ci) * c, c), c)
        lw = lw_ref[rows, :]
        cl = _dot_sel(jnp.where(incl, 1.0, 0.0), lw, 3)
        tot = cl[0:1, :] if reverse else cl[c - 1:c, :]
        e_in = jnp.exp(cl)
        e_out = jnp.exp(-cl)
        e_end = jnp.exp(tot - cl)
        kk = kk_ref[rows, :].astype(F32)
        kt = kt_ref[rows, :].astype(F32)
        ab = ab_ref[rows, :].astype(F32)
        kap = kk * jnp.exp(cl - lw)
        rh = r_ref[rows, :].astype(F32) * e_in
        kh = kt * e_out
        bh = ab * e_out
        kb = kt * e_end
        bb = ab * e_end
        e_tot = jnp.exp(tot)
        vv = v_ref[rows, :].astype(F32)
        pairs = range(D_MODEL // LANES)
        sls = [slice(p * LANES, (p + 1) * LANES) for p in pairs]
        s = [s_ref[p] for p in pairs]
        xq = [jnp.concatenate([kap[:, sl], rh[:, sl]], axis=0) for sl in sls]
        yk = [jnp.concatenate([stack2(kh[:, sl]), stack2(bh[:, sl])], axis=0) for sl in sls]
        gm = [_dot_nt(xq[p], yk[p]) for p in pairs]
        xs = [_dot_nt(xq[p], s[p]) for p in pairs]
        l_kk = [jnp.where(strict2, g[:c, :2 * c], 0.0) for g in gm]
        l_bk = [jnp.where(strict2, g[:c, 2 * c:], 0.0) for g in gm]
        a_rk = [jnp.where(incl2, g[c:, :2 * c], 0.0) for g in gm]
        a_rb = [jnp.where(incl2, g[c:, 2 * c:], 0.0) for g in gm]
        v2 = [stack2(vv[:, sl]) for sl in sls]
        x = [xs[p][:c] + _dot(l_kk[p], v2[p]) for p in pairs]
        lp = [_dot(l_bk[p], stack2(l_bk[p])) for p in pairs]
        x = [x[p] - _dot(l_bk[p], stack2(x[p])) for p in pairs]
        for it in range(5):
            x = [x[p] + _dot(lp[p], stack2(x[p])) for p in pairs]
            if it < 4:
                lp = [_dot(lp[p], stack2(lp[p])) for p in pairs]
        o = [xs[p][c:] + _dot(jnp.concatenate([a_rk[p], -a_rb[p]], axis=1),
                              jnp.concatenate([v2[p], stack2(x[p])], axis=0)) for p in pairs]
        upd = [_dot_tn(jnp.concatenate([vv[:, sls[p]], -x[p]], axis=0),
                       jnp.concatenate([kb[:, sls[p]], bb[:, sls[p]]], axis=0)) for p in pairs]
        for p in pairs:
            o_ref[rows, sls[p]] = o[p]
            s_ref[p] = s[p] * e_tot[:, sls[p]] + jnp.where(blockdiag, upd[p], 0.0)
        return carry

    lax.fori_loop(0, nch, one_chunk, 0)


def _rwkv_out_kernel(of_ref, ob_ref, bv_ref, g_ref, lg_ref, lb_ref, e_ref, et_ref, w_ref, h_ref, mod_ref, lng_ref,
                     lnb_ref, o_ref):
    o = of_ref[...] + ob_ref[...]
    inv = 1.0 / RWKV_HEAD
    oc = o - _seg_sum(o, e_ref, et_ref) * inv
    var = _seg_sum(oc * oc, e_ref, et_ref) * inv
    y = oc * lax.rsqrt(var + RWKV_GN_EPS) * lg_ref[...] + lb_ref[...] + bv_ref[...]
    yo = _dot(y * g_ref[...], w_ref[...])
    z = DEEPNORM_ALPHA * h_ref[...] + mod_ref[0][2:3] * yo
    o_ref[...] = _ln_rows(z, lng_ref[...], lnb_ref[...])


def _rwkv_mixer(h, mods, nct, mu, w_in, w0, w_l1, w_l2, a0, a_l1, a_l2, g_l1, g_l2, k_k, k_a, r_k, gn_g, gn_b, w_out,
                ln_g, ln_b):
    t, d = h.shape
    nt = t // TM
    bf = MXU_DT
    lw_ = w_l1.shape[-1]
    la_ = a_l1.shape[-1]
    zw = jnp.zeros((lw_, d), F32)
    za = jnp.zeros((la_, d), F32)
    wl1 = jnp.concatenate([w_l1[0], w_l1[1]], axis=1).astype(bf)
    wl2 = jnp.stack([jnp.concatenate([w_l2[0], zw], 0), jnp.concatenate([zw, w_l2[1]], 0)]).astype(bf)
    al1 = jnp.concatenate([a_l1[0], a_l1[1]], axis=1).astype(bf)
    al2 = jnp.stack([jnp.concatenate([a_l2[0], za], 0), jnp.concatenate([za, a_l2[1]], 0)]).astype(bf)
    head_of = jnp.arange(d) // RWKV_HEAD
    e = (head_of[:, None] == jnp.arange(LANES)[None, :]).astype(bf)
    et = e.T
    halo_p = pl.BlockSpec((8, d), lambda i: (jnp.maximum(i * (TM // 8) - 1, 0), 0))
    halo_n = pl.BlockSpec((8, d), lambda i: (jnp.minimum((i + 1) * (TM // 8), t // 8 - 1), 0))
    args = [h, h, h, mods, mu, w_in.astype(bf), wl1, wl2, w0, al1, al2, a0, g_l1.astype(bf), g_l2.astype(bf),
            k_k[None], k_a[None], r_k.reshape(1, d), e, et]
    ins = [_row_spec(d), halo_p, halo_n, _mod_spec(nct)] + [_full_spec(a.shape) for a in args[4:]]
    outs = pl.pallas_call(
        functools.partial(_rwkv_prep_kernel, nct=nct, nt=nt), grid=(nt,), in_specs=ins,
        out_specs=[_row_spec(d)] * 11,
        out_shape=[jax.ShapeDtypeStruct((t, d), dt) for dt in [ACT_DT] * 5 + [F32] * 2 + [ACT_DT] * 4],
        compiler_params=_cparams("arbitrary"), name="rwkv_prep",
    )(*args)
    r, v, kk, g, bv, lw0, lw1, kt0, kt1, ab0, ab1 = outs
    c = RWKV_CHUNK
    ncc, nc = nct * (TM // c), t // c
    o_dir = []
    for d_, (lw, kt, ab) in enumerate(((lw0, kt0, ab0), (lw1, kt1, ab1))):
        reverse = d_ == 1
        spec = pl.BlockSpec((TM, d), lambda g_, reverse=reverse: (_tile_of(g_, nct, nt, reverse), 0))
        o_dir.append(pl.pallas_call(
            functools.partial(_rwkv_scan_kernel, reverse=reverse), grid=(nt,), in_specs=[spec] * 6, out_specs=spec,
            out_shape=jax.ShapeDtypeStruct((t, d), F32),
            scratch_shapes=[pltpu.VMEM((d // LANES, LANES, LANES), F32)],
            compiler_params=_cparams("arbitrary"), name="rwkv_scan_%d" % d_,
        )(r, v, kk, lw, kt, ab))
    args = [o_dir[0], o_dir[1], bv, g, gn_g[None], gn_b[None], e, et, w_out.astype(bf), h, mods, ln_g[None], ln_b[None]]
    ins = [_row_spec(d)] * 4 + [_full_spec(a.shape) for a in args[4:9]] + [_row_spec(d), _mod_spec(nct),
                                                                          _full_spec((1, d)), _full_spec((1, d))]
    return pl.pallas_call(
        _rwkv_out_kernel, grid=(nt,), in_specs=ins, out_specs=_row_spec(d),
        out_shape=jax.ShapeDtypeStruct((t, d), F32), compiler_params=_cparams("arbitrary"), name="rwkv_out",
    )(*args)


def _ret_in_kernel(h_ref, mod_ref, w_ref, cos_ref, sin_ref, q_o, k_o, v_o, g_o):
    d = D_MODEL
    u = _modulate(h_ref[...], mod_ref[0], 0).astype(MXU_DT)
    q = _dot(u, w_ref[:, 0:d])
    k = _dot(u, w_ref[:, d:2 * d]) * (RET_QK ** -0.5)
    v_o[...] = _dot(u, w_ref[:, 2 * d:4 * d]).astype(v_o.dtype)
    g_o[...] = _silu(_dot(u, w_ref[:, 4 * d:6 * d])).astype(g_o.dtype)
    cos = cos_ref[...]
    sin = sin_ref[...]
    half = RET_QK // 2
    for z, z_o in ((q, q_o), (k, k_o)):
        for hh in range(RET_HEADS):
            lo = z[:, hh * RET_QK:hh * RET_QK + half]
            hi = z[:, hh * RET_QK + half:(hh + 1) * RET_QK]
            zh = jnp.concatenate([lo, hi], axis=1)
            rot = jnp.concatenate([-hi, lo], axis=1)
            z_o[:, hh * RET_QK:(hh + 1) * RET_QK] = (zh * cos + rot * sin).astype(z_o.dtype)


def _ret_scan_kernel(qf_ref, kf_ref, vf_ref, qb_ref, kb_ref, vb_ref, inner_ref, qd_ref, kd_ref, bd_ref, of_ref, ob_ref,
                     r_ref):
    @pl.when(pl.program_id(0) == 0)
    def _():
        r_ref[...] = jnp.zeros_like(r_ref)

    refs = ((qf_ref, kf_ref, vf_ref, of_ref), (qb_ref, kb_ref, vb_ref, ob_ref))
    cells = [(d, hh) for d in range(2) for hh in range(RET_HEADS)]
    qs = lambda hh: slice(hh * RET_QK, (hh + 1) * RET_QK)
    vs = lambda hh: slice(hh * RET_V, (hh + 1) * RET_V)
    q = {(d, hh): refs[d][0][:, qs(hh)] for d, hh in cells}
    k = {(d, hh): refs[d][1][:, qs(hh)] for d, hh in cells}
    v = {(d, hh): refs[d][2][:, vs(hh)] for d, hh in cells}
    state = {c: r_ref[c[0], c[1]] for c in cells}
    scores = {c: _dot_nt(q[c], k[c]) * inner_ref[c[0], c[1]] for c in cells}
    carry_in = {c: _dot(q[c], state[c]) * qd_ref[c[0], c[1]] for c in cells}
    upd = {c: _dot_tn(k[c] * kd_ref[c[0], c[1]], v[c]) for c in cells}
    for c in cells:
        refs[c[0]][3][:, vs(c[1])] = _dot(scores[c], v[c]) + carry_in[c]
        r_ref[c[0], c[1]] = state[c] * bd_ref[c[0], c[1]] + upd[c]


def _ret_out_kernel(of_ref, ob_ref, g_ref, gg_ref, gb_ref, w_ref, h_ref, mod_ref, lng_ref, lnb_ref, o_ref):
    parts = []
    for hh in range(RET_HEADS):
        sl = slice(hh * RET_V, (hh + 1) * RET_V)
        o = of_ref[:, sl] + ob_ref[:, sl]
        mu = jnp.mean(o, axis=-1, keepdims=True)
        oc = o - mu
        var = jnp.mean(oc * oc, axis=-1, keepdims=True)
        y = oc * lax.rsqrt(var + LN_EPS) * gg_ref[:, sl] + gb_ref[:, sl]
        parts.append((g_ref[:, sl] * y).astype(MXU_DT))
    yo = _dot(jnp.concatenate(parts, axis=1), w_ref[...])
    z = DEEPNORM_ALPHA * h_ref[...] + mod_ref[0][2:3] * yo
    o_ref[...] = _ln_rows(z, lng_ref[...], lnb_ref[...])


def _ret_mixer(h, mods, nct, rope_cos, rope_sin, w_in, decay_logit, gn_g, gn_b, w_out, ln_g, ln_b):
    t, d = h.shape
    nt = t // TM
    hv = RET_HEADS * RET_V
    q, k, v, sg = pl.pallas_call(
        _ret_in_kernel, grid=(nt,),
        in_specs=[_row_spec(d), _mod_spec(nct), _full_spec(w_in.shape), _row_spec(RET_QK), _row_spec(RET_QK)],
        out_specs=[_row_spec(d), _row_spec(d), _row_spec(hv), _row_spec(hv)],
        out_shape=[jax.ShapeDtypeStruct((t, w), ACT_DT) for w in (d, d, hv, hv)],
        compiler_params=_cparams("arbitrary"), name="ret_in",
    )(h, mods, w_in.astype(MXU_DT), rope_cos, rope_sin)
    c = RET_CHUNK
    ncc, nc = nct * (TM // c), t // c
    log_gamma = jax.nn.log_sigmoid(decay_logit.astype(F32))
    pos = jnp.arange(c, dtype=F32)
    tabs = []
    for d_ in range(2):
        lg = log_gamma[d_][:, None, None]
        p = (c - 1.0 - pos) if d_ == 1 else pos
        rel = p[:, None] - p[None, :]
        tabs.append((jnp.where(rel >= 0, jnp.exp(jnp.maximum(rel, 0.0) * lg), 0.0),
                     jnp.exp((p + 1.0) * log_gamma[d_][:, None])[:, :, None],
                     jnp.exp((c - 1.0 - p) * log_gamma[d_][:, None])[:, :, None],
                     jnp.exp(c * log_gamma[d_])[:, None, None]))
    inner, q_dec, k_dec, blk_dec = (jnp.stack(z) for z in zip(*tabs))
    cs = lambda w, reverse: pl.BlockSpec((c, w), lambda g_: (_tile_of(g_, ncc, nc, reverse), 0))
    o_dir = pl.pallas_call(
        _ret_scan_kernel, grid=(nc,),
        in_specs=[cs(d, False), cs(d, False), cs(hv, False), cs(d, True), cs(d, True), cs(hv, True),
                  _full_spec(inner.shape), _full_spec(q_dec.shape), _full_spec(k_dec.shape), _full_spec(blk_dec.shape)],
        out_specs=[cs(hv, False), cs(hv, True)], out_shape=[jax.ShapeDtypeStruct((t, hv), F32)] * 2,
        scratch_shapes=[pltpu.VMEM((2, RET_HEADS, RET_QK, RET_V), F32)],
        compiler_params=_cparams("arbitrary"), name="ret_scan",
    )(q, k, v, q, k, v, inner, q_dec, k_dec, blk_dec)
    return pl.pallas_call(
        _ret_out_kernel, grid=(nt,),
        in_specs=[_row_spec(hv)] * 3 + [_full_spec((1, hv)), _full_spec((1, hv)), _full_spec((hv, d)), _row_spec(d),
                                        _mod_spec(nct), _full_spec((1, d)), _full_spec((1, d))],
        out_specs=_row_spec(d), out_shape=jax.ShapeDtypeStruct((t, d), F32),
        compiler_params=_cparams("arbitrary"), name="ret_out",
    )(o_dir[0], o_dir[1], sg, gn_g[None], gn_b[None], w_out.astype(MXU_DT), h, mods, ln_g[None], ln_b[None])


def _hgrn_in_kernel(h_ref, mod_ref, w_ref, lb_ref, bf_ref, q_o, v_o, g_o, f0_o, f1_o):
    d = D_MODEL
    u = _modulate(h_ref[...], mod_ref[0], 0).astype(MXU_DT)
    lb = lb_ref[...]
    q_o[...] = _silu(_dot(u, w_ref[:, 0:d])).astype(q_o.dtype)
    f0_o[...] = lb + (1.0 - lb) * jax.nn.sigmoid(_dot(u, w_ref[:, d:2 * d]) + bf_ref[0:1, :])
    f1_o[...] = lb + (1.0 - lb) * jax.nn.sigmoid(_dot(u, w_ref[:, 2 * d:3 * d]) + bf_ref[1:2, :])
    v_o[...] = _dot(u, w_ref[:, 3 * d:4 * d]).astype(v_o.dtype)
    g_o[...] = _silu(_dot(u, w_ref[:, 4 * d:5 * d])).astype(g_o.dtype)


def _hgrn_scan_kernel(qf_ref, vf_ref, ff_ref, qb_ref, vb_ref, fb_ref, of_ref, ob_ref, s_ref, b_s, rb_s, rk_s, rv_s):
    hb = HGRN_BLOCK
    nb = TM // hb
    half = hb // 2
    dirs = ((qf_ref, vf_ref, ff_ref, of_ref, False), (qb_ref, vb_ref, fb_ref, ob_ref, True))

    @pl.when(pl.program_id(0) == 0)
    def _():
        s_ref[...] = jnp.zeros_like(s_ref)

    span = 4 * hb
    ri = lax.broadcasted_iota(jnp.int32, (span, span), 0)
    ci = lax.broadcasted_iota(jnp.int32, (span, span), 1)
    same_block = (ri // hb) == (ci // hb)
    ti = lax.broadcasted_iota(jnp.int32, (half, 1), 0)
    rowi = lax.broadcasted_iota(jnp.int32, (hb, 1), 0)
    heads = range(HGRN_HEADS)
    sls = [slice(hh * HGRN_HEAD, (hh + 1) * HGRN_HEAD) for hh in heads]
    cells = [(d, hh) for d in range(2) for hh in heads]

    for d, (q_ref, v_ref, f_ref, o_ref, reverse) in enumerate(dirs):
        tri = jnp.where(jnp.logical_and(same_block, (ci >= ri) if reverse else (ci <= ri)), 1.0, 0.0)
        for r0 in range(0, TM, span):
            b_s[d, r0:r0 + span] = _dot_sel(tri, jnp.log(f_ref[r0:r0 + span, :]), 3)

    def block(bi, par):
        pre = []
        for d, (q_ref, v_ref, f_ref, o_ref, reverse) in enumerate(dirs):
            blk = (nb - 1 - bi) if reverse else bi
            r0 = pl.multiple_of(blk * hb, hb)
            kx = 1.0 - f_ref[pl.ds(r0, hb), :]
            q = q_ref[pl.ds(r0, hb), :].astype(F32)
            v = v_ref[pl.ds(r0, hb), :].astype(F32)
            b = b_s[d, pl.ds(r0, hb), :]
            rb_s[d, par] = b
            rk_s[d, par] = kx
            rv_s[d, par] = v
            tot = b[0:1, :] if reverse else b[hb - 1:hb, :]
            first = (rowi >= half) if reverse else (rowi < half)
            beta = b[half:half + 1, :] if reverse else b[half - 1:half, :]
            pre.append(dict(
                r0=r0, q=q, v=v, b=b, qe=q * jnp.exp(b), kb=kx * jnp.exp(tot - b), e_tot=jnp.exp(tot),
                k_first=kx * jnp.exp(jnp.where(first, beta - b, -jnp.inf)),
                q_second=q * jnp.exp(jnp.where(first, -jnp.inf, b - beta)),
                causal=[(ti <= si) if reverse else (ti >= si) for si in range(half)]))
        for g0 in range(0, len(cells), HGRN_GROUP):
            grp = cells[g0:g0 + HGRN_GROUP]
            s = {c: s_ref[c[0], c[1]] for c in grp}
            m_first = {(d, hh): _dot_tn(pre[d]['k_first'][:, sls[hh]], pre[d]['v'][:, sls[hh]]) for d, hh in grp}
            o = {(d, hh): _dot_nt(pre[d]['qe'][:, sls[hh]], s[d, hh]) + _dot(pre[d]['q_second'][:, sls[hh]], m_first[d, hh])
                 for d, hh in grp}
            upd = {(d, hh): _dot_tn(pre[d]['v'][:, sls[hh]], pre[d]['kb'][:, sls[hh]]) for d, hh in grp}
            for d, hh in grp:
                sl = sls[hh]
                p = pre[d]
                parts = []
                for lo in (0, half):
                    bt = p['b'][lo:lo + half, sl]
                    qt = p['q'][lo:lo + half, sl]
                    acc = jnp.zeros((half, HGRN_HEAD), F32)
                    for si in range(half):
                        row = slice(lo + si, lo + si + 1)
                        dec = jnp.exp(jnp.where(p['causal'][si], bt - rb_s[d, par, row, sl], -jnp.inf))
                        sc = jnp.sum(qt * rk_s[d, par, row, sl] * dec, axis=-1, keepdims=True)
                        acc = acc + sc * rv_s[d, par, row, sl]
                    parts.append(acc)
                dirs[d][3][pl.ds(p['r0'], hb), sl] = o[d, hh] + jnp.concatenate(parts, axis=0)
                s_ref[d, hh] = s[d, hh] * p['e_tot'][:, sl] + upd[d, hh]

    def trip(bj, carry):
        for par in range(HGRN_UNROLL):
            block(HGRN_UNROLL * bj + par, par)
        return carry

    lax.fori_loop(0, nb // HGRN_UNROLL, trip, 0)


def _hgrn_out_kernel(of_ref, ob_ref, g_ref, ng_ref, w_ref, h_ref, mod_ref, lng_ref, lnb_ref, o_ref):
    parts = []
    for hh in range(HGRN_HEADS):
        sl = slice(hh * HGRN_HEAD, (hh + 1) * HGRN_HEAD)
        o = of_ref[:, sl] + ob_ref[:, sl]
        y = o * lax.rsqrt(jnp.mean(o * o, axis=-1, keepdims=True) + LN_EPS) * ng_ref[...]
        parts.append((y * g_ref[:, sl]).astype(MXU_DT))
    yo = _dot(jnp.concatenate(parts, axis=1), w_ref[...])
    z = DEEPNORM_ALPHA * h_ref[...] + mod_ref[0][2:3] * yo
    o_ref[...] = _ln_rows(z, lng_ref[...], lnb_ref[...])


def _hgrn_mixer(h, mods, nct, lb, w_in, b_f, norm_g, w_out, ln_g, ln_b):
    t, d = h.shape
    nt = t // TM
    q, v, sg, f0, f1 = pl.pallas_call(
        _hgrn_in_kernel, grid=(nt,),
        in_specs=[_row_spec(d), _mod_spec(nct), _full_spec(w_in.shape), _full_spec((1, d)), _full_spec((2, d))],
        out_specs=[_row_spec(d)] * 5,
        out_shape=[jax.ShapeDtypeStruct((t, d), dt) for dt in [ACT_DT] * 3 + [F32] * 2],
        compiler_params=_cparams("arbitrary"), name="hgrn_in",
    )(h, mods, w_in.astype(MXU_DT), lb[None], b_f)
    fwd = pl.BlockSpec((TM, d), lambda g_: (g_, 0))
    bwd = pl.BlockSpec((TM, d), lambda g_: (_tile_of(g_, nct, nt, True), 0))
    o_dir = pl.pallas_call(
        _hgrn_scan_kernel, grid=(nt,), in_specs=[fwd] * 3 + [bwd] * 3, out_specs=[fwd, bwd],
        out_shape=[jax.ShapeDtypeStruct((t, d), F32)] * 2,
        scratch_shapes=[pltpu.VMEM((2, HGRN_HEADS, HGRN_HEAD, HGRN_HEAD), F32)] + [pltpu.VMEM((2, TM, d), F32)]
        + [pltpu.VMEM((2, HGRN_UNROLL, HGRN_BLOCK, d), F32)] * 3,
        compiler_params=_cparams("arbitrary"), name="hgrn_scan",
    )(q, v, f0, q, v, f1)
    return pl.pallas_call(
        _hgrn_out_kernel, grid=(nt,),
        in_specs=[_row_spec(d)] * 3 + [_full_spec((1, HGRN_HEAD)), _full_spec((d, d)), _row_spec(d), _mod_spec(nct),
                                       _full_spec((1, d)), _full_spec((1, d))],
        out_specs=_row_spec(d), out_shape=jax.ShapeDtypeStruct((t, d), F32),
        compiler_params=_cparams("arbitrary"), name="hgrn_out",
    )(o_dir[0], o_dir[1], sg, norm_g[None], w_out.astype(MXU_DT), h, mods, ln_g[None], ln_b[None])


def _router_kernel(h_ref, mod_ref, rw_ref, rb_ref, u_o, gate_o, rank_o, x_o):
    u = _modulate(h_ref[...], mod_ref[0], 3)
    u_o[...] = u.astype(u_o.dtype)
    w_hi, w_lo = _split(rw_ref[...], 2)
    u_hi, u_lo = _split(u, 2)
    nt_dims = (((1,), (1,)), ((), ()))
    logits = (lax.dot_general(w_hi, u_hi, nt_dims, preferred_element_type=F32)
              + lax.dot_general(w_hi, u_lo, nt_dims, preferred_element_type=F32)
              + lax.dot_general(w_lo, u_hi, nt_dims, preferred_element_type=F32))
    ne, gs = N_EXPERTS, N_EXPERTS // N_GROUPS
    neg = -jnp.inf
    scores = jax.nn.sigmoid(logits[:ne])
    choice = scores + rb_ref[:ne]
    c3 = choice.reshape(N_GROUPS, gs, TM)
    mi = lax.broadcasted_iota(jnp.int32, c3.shape, 1).astype(F32)
    m1 = jnp.max(c3, axis=1, keepdims=True)
    i1 = jnp.min(jnp.where(c3 == m1, mi, float(gs)), axis=1, keepdims=True)
    m2 = jnp.max(jnp.where(mi == i1, neg, c3), axis=1, keepdims=True)
    gscore = m1 + m2
    gi = lax.broadcasted_iota(jnp.int32, gscore.shape, 0).astype(F32)
    gsel = jnp.zeros(gscore.shape, F32)
    for _ in range(TOPK_GROUPS):
        gm = jnp.max(gscore, axis=0, keepdims=True)
        pick = gi == jnp.min(jnp.where(gscore == gm, gi, float(N_GROUPS)), axis=0, keepdims=True)
        gsel = jnp.where(pick, 1.0, gsel)
        gscore = jnp.where(pick, neg, gscore)
    emask = jnp.broadcast_to(gsel, c3.shape).reshape(ne, TM)
    masked = jnp.where(emask > 0.5, choice, neg)
    ei = lax.broadcasted_iota(jnp.int32, masked.shape, 0).astype(F32)
    chosen = jnp.zeros(masked.shape, F32)
    for _ in range(TOP_K):
        em = jnp.max(masked, axis=0, keepdims=True)
        pick = ei == jnp.min(jnp.where(masked == em, ei, float(ne)), axis=0, keepdims=True)
        chosen = jnp.where(pick, 1.0, chosen)
        masked = jnp.where(pick, neg, masked)
    top_w = scores * chosen
    gates = ROUTED_SCALE * top_w / jnp.sum(top_w, axis=0, keepdims=True)
    ti = lax.broadcasted_iota(jnp.int32, (TM, TM), 0)
    tj = lax.broadcasted_iota(jnp.int32, (TM, TM), 1)
    before = jnp.where(ti < tj, 1.0, 0.0).astype(MXU_DT)
    prefix = jnp.dot(chosen.astype(MXU_DT), before, preferred_element_type=F32)
    rank = jnp.where(chosen > 0.5, prefix, -1.0)
    gate_o[0] = gates
    rank_o[0] = rank
    cap = MOE_CAP
    slot = lax.broadcasted_iota(jnp.int32, (cap, TM), 0).astype(F32)
    ub = u.astype(MXU_DT)
    for g0 in range(0, ne, MOE_EGROUP):
        onehot = jnp.concatenate([jnp.where(slot == rank[e:e + 1, :], 1.0, 0.0).astype(MXU_DT)
                                  for e in range(g0, g0 + MOE_EGROUP)], axis=0)
        xg = jnp.dot(onehot, ub, preferred_element_type=F32)
        x_o[0, g0:g0 + MOE_EGROUP] = xg.reshape(MOE_EGROUP, cap, D_MODEL).astype(x_o.dtype)


def _expert_kernel(x_ref, wgu_ref, wd_ref, y_ref, wgu_b, wd_b):
    @pl.when(pl.program_id(1) == 0)
    def _():
        wgu_b[0] = wgu_ref[0, 0].astype(wgu_b.dtype)
        wd_b[0] = wd_ref[0, 0].astype(wd_b.dtype)

    g = x_ref.shape[0]
    ch = max(c for c in range(1, MOE_CHUNK + 1) if g % c == 0)
    ed = EXPERT_DIM

    def chunk(ci, carry):
        t0 = ci * ch
        x = x_ref[pl.ds(t0, ch)].reshape(ch * MOE_CAP, D_MODEL)
        gu = _dot(x, wgu_b[0])
        y = _dot(_silu(gu[:, :ed]) * gu[:, ed:], wd_b[0])
        y_ref[pl.ds(t0, ch)] = y.reshape(ch, 1, MOE_CAP, D_MODEL).astype(y_ref.dtype)
        return carry

    lax.fori_loop(0, g // ch, chunk, 0)


def _combine_kernel(u_ref, gt_ref, rt_ref, y_ref, sgu_ref, sd_ref, h_ref, mod_ref, lng_ref, lnb_ref, *rest, extra):
    if extra:
        ex_ref, o_ref = rest
    else:
        (o_ref,) = rest
    ed = EXPERT_DIM
    cap = MOE_CAP
    gu = _dot(u_ref[...], sgu_ref[...])
    acc = _dot(_silu(gu[:, :ed]) * gu[:, ed:], sd_ref[...])
    if extra:
        acc = acc + ex_ref[...]
    slot = lax.broadcasted_iota(jnp.int32, (cap, TM), 0).astype(F32)
    for g0 in range(0, N_EXPERTS, MOE_EGROUP):
        pw = jnp.concatenate([jnp.where(slot == rt_ref[0, e:e + 1, :], gt_ref[0, e:e + 1, :], 0.0).astype(MXU_DT)
                              for e in range(g0, g0 + MOE_EGROUP)], axis=0)
        yg = y_ref[0, g0:g0 + MOE_EGROUP].reshape(MOE_EGROUP * cap, D_MODEL)
        acc = acc + _dot_tn(pw, yg)
    z = DEEPNORM_ALPHA * h_ref[...] + mod_ref[0][5:6] * acc
    o_ref[...] = _ln_rows(z, lng_ref[...], lnb_ref[...])


def _overflow_kernel(tile_ref, exp_ref, nr_ref, n_ref, u_ref, gate_ref, rank_ref, wgu_ref, wd_ref, zero_ref, o_ref):
    del zero_ref
    s = pl.program_id(0)
    tile = tile_ref[s]
    e = exp_ref[s]
    ed = EXPERT_DIM
    cap = MOE_CAP
    active = s < n_ref[0]
    first = jnp.logical_or(s == 0, tile_ref[jnp.maximum(s - 1, 0)] != tile)

    @pl.when(jnp.logical_and(active, first))
    def _():
        o_ref[...] = jnp.zeros_like(o_ref)

    @pl.when(active)
    def _():
        wgu = wgu_ref[0]
        wd = wd_ref[0]
        rank = rank_ref[0, pl.ds(e, 1), :]
        gate = gate_ref[0, pl.ds(e, 1), :]

        def one_round(r, carry):
            slot = lax.broadcasted_iota(jnp.int32, (cap, TM), 0).astype(F32) + (r * cap).astype(F32)
            hit = slot == rank
            x = _dot(jnp.where(hit, 1.0, 0.0), u_ref[...])
            gu = _dot(x, wgu)
            y = _dot(_silu(gu[:, :ed]) * gu[:, ed:], wd)
            o_ref[...] += _dot_tn(jnp.where(hit, gate, 0.0), y)
            return carry

        lax.fori_loop(1, nr_ref[s], one_round, 0)


def _moe_layer(h, mods, nct, layer, router_w, router_b, w_gu, w_down, sh_gu, sh_down, ln_g, ln_b, latent_only=False):
    t, d = h.shape
    nt = t // TM
    ne, cap = N_EXPERTS, MOE_CAP
    rw = jnp.concatenate([router_w.T, jnp.zeros((LANES - ne, d), F32)], axis=0)
    rb = jnp.concatenate([router_b, jnp.zeros((LANES - ne,), F32)])[:, None]
    per_tile = pl.BlockSpec((1, ne, TM), lambda i: (i, 0, 0))
    slots = pl.BlockSpec((1, ne, cap, d), lambda i: (i, 0, 0, 0))
    u, gates, ranks, xs = pl.pallas_call(
        _router_kernel, grid=(nt,),
        in_specs=[_row_spec(d), _mod_spec(nct), _full_spec((LANES, d)), _full_spec((LANES, 1))],
        out_specs=[_row_spec(d), per_tile, per_tile, slots],
        out_shape=[jax.ShapeDtypeStruct((t, d), MXU_DT), jax.ShapeDtypeStruct((nt, ne, TM), F32),
                   jax.ShapeDtypeStruct((nt, ne, TM), F32), jax.ShapeDtypeStruct((nt, ne, cap, d), MXU_DT)],
        compiler_params=_cparams("arbitrary"), name="moe_router",
    )(h, mods, rw, rb)

    run = max(g for g in range(1, MOE_RUN + 1) if nt % g == 0)
    ys, wgu_b, wd_b = pl.pallas_call(
        _expert_kernel, grid=(ne, nt // run),
        in_specs=[pl.BlockSpec((run, 1, cap, d), lambda e, c: (c, e, 0, 0)),
                  pl.BlockSpec((1, 1, d, 2 * EXPERT_DIM), lambda e, c: (layer, e, 0, 0)),
                  pl.BlockSpec((1, 1, EXPERT_DIM, d), lambda e, c: (layer, e, 0, 0))],
        out_specs=[pl.BlockSpec((run, 1, cap, d), lambda e, c: (c, e, 0, 0)),
                   pl.BlockSpec((1, d, 2 * EXPERT_DIM), lambda e, c: (e, 0, 0)),
                   pl.BlockSpec((1, EXPERT_DIM, d), lambda e, c: (e, 0, 0))],
        out_shape=[jax.ShapeDtypeStruct((nt, ne, cap, d), MXU_DT),
                   jax.ShapeDtypeStruct((ne, d, 2 * EXPERT_DIM), MXU_DT), jax.ShapeDtypeStruct((ne, EXPERT_DIM, d), MXU_DT)],
        compiler_params=_cparams("arbitrary", "arbitrary"), name="moe_experts",
    )(xs, w_gu, w_down)

    sgu, sd = sh_gu.astype(MXU_DT), sh_down.astype(MXU_DT)
    base_specs = [_row_spec(d), per_tile, per_tile, slots, _full_spec(sgu.shape), _full_spec(sd.shape),
                  _row_spec(d), _mod_spec(nct), _full_spec((1, d)), _full_spec((1, d))]
    base_args = (u, gates, ranks, ys, sgu, sd, h, mods, ln_g[None], ln_b[None])

    skip = nct if latent_only else 0
    out_spec = pl.BlockSpec((TM, d), lambda i: (jnp.maximum(i - skip, 0), 0))

    def combine(*extra):
        return pl.pallas_call(
            functools.partial(_combine_kernel, extra=bool(extra)), grid=(nt,),
            in_specs=base_specs + [_row_spec(d)] * len(extra), out_specs=out_spec,
            out_shape=jax.ShapeDtypeStruct((t - skip * TM, d), F32), compiler_params=_cparams("arbitrary"),
            name="moe_combine",
        )(*base_args, *extra)

    count = (jnp.max(ranks, axis=-1).astype(jnp.int32) + 1).reshape(-1)
    over = count > cap
    n_over = jnp.sum(over.astype(jnp.int32))

    def with_overflow(size):
        def run():
            idx = jnp.nonzero(over, size=size, fill_value=0)[0].astype(jnp.int32)
            idx = jnp.where(jnp.arange(size) < n_over, idx, idx[jnp.maximum(n_over - 1, 0)])
            tiles, exps = idx // ne, idx % ne
            rounds = (count[idx] + cap - 1) // cap
            grid_spec = pltpu.PrefetchScalarGridSpec(
                num_scalar_prefetch=4, grid=(size,),
                in_specs=[pl.BlockSpec((TM, d), lambda s, tl, ex, nr, n: (tl[s], 0)),
                          pl.BlockSpec((1, ne, TM), lambda s, tl, ex, nr, n: (tl[s], 0, 0)),
                          pl.BlockSpec((1, ne, TM), lambda s, tl, ex, nr, n: (tl[s], 0, 0)),
                          pl.BlockSpec((1, d, 2 * EXPERT_DIM), lambda s, tl, ex, nr, n: (ex[s], 0, 0)),
                          pl.BlockSpec((1, EXPERT_DIM, d), lambda s, tl, ex, nr, n: (ex[s], 0, 0)),
                          pl.BlockSpec(memory_space=pl.ANY)],
                out_specs=pl.BlockSpec((TM, d), lambda s, tl, ex, nr, n: (tl[s], 0)))
            extra = pl.pallas_call(
                _overflow_kernel, grid_spec=grid_spec, out_shape=jax.ShapeDtypeStruct((t, d), F32),
                input_output_aliases={9: 0}, compiler_params=_cparams("arbitrary"), name="moe_overflow",
            )(tiles, exps, rounds, n_over[None], u, gates, ranks, wgu_b, wd_b, jnp.zeros((t, d), F32))
            return combine(extra)
        return run

    sizes = sorted({min(MOE_OVER_STEPS, nt * ne), nt * ne})
    branch = sum((n_over > sz).astype(jnp.int32) for sz in [0] + sizes[:-1])
    return lax.switch(branch, [combine] + [with_overflow(sz) for sz in sizes])


def kernel(x, c, ctx, c_ctx, ada_w, ada_b, post_ln_g, post_ln_b, lru_w_in, lru_conv_w, lru_conv_b, lru_gate_w, lru_gate_b, lru_lambda, lru_w_out, rwkv_mu, rwkv_w_in, rwkv_w0, rwkv_w_l1, rwkv_w_l2, rwkv_a0, rwkv_a_l1, rwkv_a_l2, rwkv_g_l1, rwkv_g_l2, rwkv_k_k, rwkv_k_a, rwkv_r_k, rwkv_ln_g, rwkv_ln_b, rwkv_w_out, ret_w_in, ret_decay, ret_gn_g, ret_gn_b, ret_w_out, hgrn_w_in, hgrn_b_f, hgrn_lb, hgrn_norm_g, hgrn_w_out, moe_router, moe_bias, moe_w_gu, moe_w_down, moe_sh_gu, moe_sh_down):
    assert x.shape[0] == 1 and ctx.shape[0] == 1
    n_ctx, n_lat, d = ctx.shape[1], x.shape[1], x.shape[2]
    assert n_ctx % TM == 0 and n_lat % TM == 0 and d == D_MODEL
    nct = n_ctx // TM
    rows = n_lat // GRID_W
    pos_row = jnp.repeat(jnp.arange(rows, dtype=F32), GRID_W)
    pos_col = jnp.tile(jnp.arange(GRID_W, dtype=F32), rows)
    n_freq = RET_QK // 4
    freqs = ROPE_BASE ** (-jnp.arange(n_freq, dtype=F32) / n_freq)
    ang = jnp.concatenate([pos_row[:, None] * freqs, pos_col[:, None] * freqs], axis=-1)
    ang = jnp.concatenate([ang, ang], axis=-1)
    rope_cos = jnp.concatenate([jnp.ones((n_ctx, RET_QK), F32), jnp.cos(ang)], axis=0)
    rope_sin = jnp.concatenate([jnp.zeros((n_ctx, RET_QK), F32), jnp.sin(ang)], axis=0)
    lb_cum = jnp.cumsum(jax.nn.softmax(hgrn_lb.astype(F32), axis=0), axis=0)

    cond = jnp.concatenate([c_ctx[None], c, jnp.zeros((6, d), F32)], axis=0)
    mods_all = _ada_mods(cond, ada_w, ada_b)
    h = jnp.concatenate([ctx[0], x[0]], axis=0)
    for i in range(DEPTH):
        kind, j = i % N_MIXERS, i // N_MIXERS
        mods = mods_all[i]
        lng, lnb = post_ln_g[i, 0], post_ln_b[i, 0]
        if kind == 0:
            h = _lru_mixer(h, mods, nct, lru_w_in[j], lru_conv_w[j], lru_conv_b[j], lru_gate_w[j], lru_gate_b[j],
                           lru_lambda[j], lru_w_out[j], lng, lnb)
        elif kind == 1:
            h = _rwkv_mixer(h, mods, nct, rwkv_mu[j], rwkv_w_in[j], rwkv_w0[j], rwkv_w_l1[j], rwkv_w_l2[j], rwkv_a0[j],
                            rwkv_a_l1[j], rwkv_a_l2[j], rwkv_g_l1[j], rwkv_g_l2[j], rwkv_k_k[j], rwkv_k_a[j],
                            rwkv_r_k[j], rwkv_ln_g[j], rwkv_ln_b[j], rwkv_w_out[j], lng, lnb)
        elif kind == 2:
            h = _ret_mixer(h, mods, nct, rope_cos, rope_sin, ret_w_in[j], ret_decay[j], ret_gn_g[j], ret_gn_b[j],
                           ret_w_out[j], lng, lnb)
        else:
            h = _hgrn_mixer(h, mods, nct, lb_cum[i] - lb_cum[0], hgrn_w_in[j], hgrn_b_f[j], hgrn_norm_g[j],
                            hgrn_w_out[j], lng, lnb)
        h = _moe_layer(h, mods, nct, i, moe_router[i], moe_bias[i], moe_w_gu, moe_w_down, moe_sh_gu[i],
                       moe_sh_down[i], post_ln_g[i, 1], post_ln_b[i, 1], latent_only=i == DEPTH - 1)
    return h[None]
```

```python
import math
import functools
import jax
import jax.numpy as jnp
from jax import lax
from jax.experimental import pallas as pl
from jax.experimental.pallas import tpu as pltpu

F32 = jnp.float32
MXU_DT = jnp.bfloat16
ACT_DT = jnp.bfloat16
LANES = 128
TM = 256
VMEM_LIMIT = 56 * 2 ** 20

D_MODEL = 1024
DEPTH = 4
GRID_W = 64
N_MIXERS = 4
DEEPNORM_ALPHA = (2.0 * DEPTH) ** 0.25
LN_EPS = 1e-5
LRU_WIDTH = D_MODEL
LRU_BLOCKS = 16
LRU_BLOCK = LRU_WIDTH // LRU_BLOCKS
LRU_C = 8.0
RWKV_HEAD = 64
RWKV_HEADS = D_MODEL // RWKV_HEAD
RWKV_DECAY_SCALE = math.exp(-0.5)
RWKV_GN_EPS = 64e-5
RWKV_CHUNK = 64
RET_HEADS = 4
RET_QK = D_MODEL // RET_HEADS
RET_V = 2 * RET_QK
RET_CHUNK = 128
ROPE_BASE = 10000.0
HGRN_HEADS = 8
HGRN_HEAD = D_MODEL // HGRN_HEADS
HGRN_BLOCK = 16
HGRN_GROUP = 16
HGRN_UNROLL = 4
N_EXPERTS = 64
TOP_K = 8
N_GROUPS = 8
TOPK_GROUPS = 4
EXPERT_DIM = 256
ROUTED_SCALE = 2.5
MOE_OVER_STEPS = (128, 256, 512)
MOE_CAP = 64
MOE_EGROUP = 8
MOE_RUN = 65
MOE_CHUNK = 13


def _cparams(*sem):
    return pltpu.CompilerParams(dimension_semantics=sem, vmem_limit_bytes=VMEM_LIMIT)


def _dot(a, b):
    return jnp.dot(a.astype(MXU_DT), b.astype(MXU_DT), preferred_element_type=F32)


def _dot_nt(a, b):
    return lax.dot_general(a.astype(MXU_DT), b.astype(MXU_DT), (((1,), (1,)), ((), ())), preferred_element_type=F32)


def _dot_tn(a, b):
    return lax.dot_general(a.astype(MXU_DT), b.astype(MXU_DT), (((0,), (0,)), ((), ())), preferred_element_type=F32)


def _split(x, n):
    parts = []
    for _ in range(n):
        p = x.astype(MXU_DT)
        parts.append(p)
        x = x - p.astype(F32)
    return parts


def _dot_sel(sel, x, n):
    return sum(jnp.dot(sel.astype(MXU_DT), p, preferred_element_type=F32) for p in _split(x, n))


def _dot_xsel(x, sel, n):
    return sum(jnp.dot(p, sel.astype(MXU_DT), preferred_element_type=F32) for p in _split(x, n))


def _modulate(h, m, shift_idx):
    return h * (1.0 + m[shift_idx + 1:shift_idx + 2]) + m[shift_idx:shift_idx + 1]


def _ln_rows(z, g, b):
    mu = jnp.mean(z, axis=-1, keepdims=True)
    zc = z - mu
    var = jnp.mean(zc * zc, axis=-1, keepdims=True)
    return zc * lax.rsqrt(var + LN_EPS) * g + b


def _silu(x):
    return x * jax.nn.sigmoid(x)


def _shift_down(x, first_row):
    rows = lax.broadcasted_iota(jnp.int32, (x.shape[0], 1), 0)
    return jnp.where(rows == 0, first_row, pltpu.roll(x, 1, 0))


def _shift_up(x, last_row):
    n = x.shape[0]
    rows = lax.broadcasted_iota(jnp.int32, (n, 1), 0)
    return jnp.where(rows == n - 1, last_row, pltpu.roll(x, n - 1, 0))


def _tile_of(g, nct, nt, reverse):
    if not reverse:
        return g
    return jnp.where(g < nct, nct - 1 - g, nt - 1 - (g - nct))


def _halo_flags(t, nct, nt):
    prev_ok = jnp.logical_and(t != 0, t != nct).astype(F32)
    next_ok = jnp.logical_and(t != nct - 1, t != nt - 1).astype(F32)
    return prev_ok, next_ok


def _ada_kernel(s_ref, w_ref, b_ref, o_ref):
    o_ref[0] = _dot(_silu(s_ref[...]), w_ref[0]) + b_ref[0]


def _ada_mods(cond, ada_w, ada_b):
    nl, d, n6 = ada_w.shape
    out = pl.pallas_call(
        _ada_kernel, grid=(nl, n6 // d),
        in_specs=[pl.BlockSpec((8, d), lambda l, j: (0, 0)),
                  pl.BlockSpec((1, d, d), lambda l, j: (l, 0, j)),
                  pl.BlockSpec((1, 1, d), lambda l, j: (l, 0, j))],
        out_specs=pl.BlockSpec((1, 8, d), lambda l, j: (l, 0, j)),
        out_shape=jax.ShapeDtypeStruct((nl, 8, n6), F32),
        compiler_params=_cparams("arbitrary", "arbitrary"), name="ada_mods",
    )(cond, ada_w, ada_b.reshape(nl, 1, n6))
    return out[:, :2].reshape(nl, 2, 6, d)


def _row_spec(width, tm=TM):
    return pl.BlockSpec((tm, width), lambda i: (i, 0))


def _full_spec(shape):
    nd = len(shape)
    return pl.BlockSpec(tuple(shape), lambda *_: (0,) * nd)


def _mod_spec(nct):
    return pl.BlockSpec((1, 6, D_MODEL), lambda i: (jnp.minimum(i // nct, 1), 0, 0))


def _lru_out_kernel(g_ref, hf_ref, hb_ref, w_ref, h_ref, mod_ref, lng_ref, lnb_ref, o_ref):
    y = _dot(g_ref[...] * (hf_ref[...] + hb_ref[...]), w_ref[...])
    z = DEEPNORM_ALPHA * h_ref[...] + mod_ref[0][2:3] * y
    o_ref[...] = _ln_rows(z, lng_ref[...], lnb_ref[...])

def _lru_in_kernel(h_ref, mod_ref, w_ref, g_ref, x_ref):
    u = _modulate(h_ref[...], mod_ref[0], 0)
    z = _dot(u, w_ref[...])
    g_ref[...] = jax.nn.gelu(z[:, :LRU_WIDTH], approximate=True).astype(g_ref.dtype)
    x_ref[...] = z[:, LRU_WIDTH:]


def _lru_scan_kernel(xf_ref, xfp_ref, xfn_ref, xb_ref, xbp_ref, xbn_ref, cw_ref, cb_ref, gw_ref, gb_ref, lam_ref,
                     hf_o, hb_o, a_s, b_s, st_s, *, nct, nt):
    g = pl.program_id(0)

    @pl.when(g == 0)
    def _():
        st_s[...] = jnp.zeros_like(st_s)

    cw = cw_ref[...]
    for d, (x_ref, xp_ref, xn_ref) in enumerate(((xf_ref, xfp_ref, xfn_ref), (xb_ref, xbp_ref, xbn_ref))):
        prev_ok, next_ok = _halo_flags(_tile_of(g, nct, nt, d == 1), nct, nt)
        x = x_ref[...]
        xm1 = _shift_down(x, xp_ref[7:8, :] * prev_ok)
        xp1 = _shift_up(x, xn_ref[0:1, :] * next_ok)
        xp2 = _shift_up(xp1, xn_ref[1:2, :] * next_ok)
        xc = cw[0:1] * xm1 + cw[1:2] * x + cw[2:3] * xp1 + cw[3:4] * xp2 + cb_ref[...]
        gates = jax.nn.sigmoid(_dot(xc, gw_ref[d]) + gb_ref[d])
        lam = lam_ref[d:d + 1, :]
        softplus = jnp.maximum(-lam, 0.0) + jnp.log(1.0 + jnp.exp(-jnp.abs(lam)))
        log_a = -LRU_C * gates[:, :LRU_WIDTH] * softplus
        a_s[d] = jnp.exp(log_a)
        b_s[d] = jnp.sqrt(1.0 - jnp.exp(2.0 * log_a)) * (gates[:, LRU_WIDTH:] * xc)

    def row(r, carry):
        hf, hb = carry
        rb = TM - 1 - r
        hf = a_s[0, pl.ds(r, 1), :] * hf + b_s[0, pl.ds(r, 1), :]
        hb = a_s[1, pl.ds(rb, 1), :] * hb + b_s[1, pl.ds(rb, 1), :]
        hf_o[pl.ds(r, 1), :] = hf
        hb_o[pl.ds(rb, 1), :] = hb
        return hf, hb

    hf, hb = lax.fori_loop(0, TM, row, (st_s[0], st_s[1]), unroll=8)
    st_s[0] = hf
    st_s[1] = hb


def _lru_mixer(h, mods, nct, w_in, conv_w, conv_b, gate_w, gate_b, lam, w_out, ln_g, ln_b):
    t, d = h.shape
    nt = t // TM
    w = LRU_WIDTH
    gelu, rnn = pl.pallas_call(
        _lru_in_kernel, grid=(nt,),
        in_specs=[_row_spec(d), _mod_spec(nct), _full_spec((d, 2 * w))],
        out_specs=[_row_spec(w), _row_spec(w)],
        out_shape=[jax.ShapeDtypeStruct((t, w), ACT_DT), jax.ShapeDtypeStruct((t, w), F32)],
        compiler_params=_cparams("arbitrary"), name="lru_in",
    )(h, mods, w_in.astype(MXU_DT))
    eye = jnp.eye(LRU_BLOCKS, dtype=F32)
    gw = jnp.einsum('dgnij,nm->dgnimj', gate_w, eye).reshape(2, 2, w, w)
    gw = jnp.concatenate([gw[:, 0], gw[:, 1]], axis=-1).astype(MXU_DT)
    gb = gate_b.reshape(2, 1, 2 * w)
    def tile_specs(reverse):
        tile = lambda g: _tile_of(g, nct, nt, reverse)
        return [pl.BlockSpec((TM, w), lambda g: (tile(g), 0)),
                pl.BlockSpec((8, w), lambda g: (jnp.maximum(tile(g) * (TM // 8) - 1, 0), 0)),
                pl.BlockSpec((8, w), lambda g: (jnp.minimum((tile(g) + 1) * (TM // 8), t // 8 - 1), 0))]

    fwd, bwd = tile_specs(False), tile_specs(True)
    hf, hb = pl.pallas_call(
        functools.partial(_lru_scan_kernel, nct=nct, nt=nt), grid=(nt,),
        in_specs=fwd + bwd + [_full_spec((4, w)), _full_spec((1, w)), _full_spec((2, w, 2 * w)),
                              _full_spec((2, 1, 2 * w)), _full_spec((2, w))],
        out_specs=[fwd[0], bwd[0]], out_shape=[jax.ShapeDtypeStruct((t, w), F32)] * 2,
        scratch_shapes=[pltpu.VMEM((2, TM, w), F32)] * 2 + [pltpu.VMEM((2, 1, w), F32)],
        compiler_params=_cparams("arbitrary"), name="lru_scan",
    )(rnn, rnn, rnn, rnn, rnn, rnn, conv_w, conv_b[None], gw, gb, lam)
    return pl.pallas_call(
        _lru_out_kernel, grid=(nt,),
        in_specs=[_row_spec(w)] * 3 + [_full_spec((w, d)), _row_spec(d), _mod_spec(nct), _full_spec((1, d)),
                                       _full_spec((1, d))],
        out_specs=_row_spec(d), out_shape=jax.ShapeDtypeStruct((t, d), F32),
        compiler_params=_cparams("arbitrary"), name="lru_out",
    )(gelu, hf, hb, w_out.astype(MXU_DT), h, mods, ln_g[None], ln_b[None])


def _seg_sum(x, e_ref, et_ref):
    s = _dot_xsel(x, e_ref[...], 2)
    return _dot_xsel(s, et_ref[...], 2)


def _rwkv_prep_kernel(h_ref, hp_ref, hn_ref, mod_ref, mu_ref, win_ref, wl1_ref, wl2_ref, w0_ref, al1_ref, al2_ref,
                      a0_ref, gl1_ref, gl2_ref, kk_ref, ka_ref, rk_ref, e_ref, et_ref,
                      r_o, v_o, kk_o, g_o, bv_o, lw0_o, lw1_o, kt0_o, kt1_o, ab0_o, ab1_o, *, nct, nt):
    i = pl.program_id(0)
    prev_ok, next_ok = _halo_flags(i, nct, nt)
    m = mod_ref[0]
    u = _modulate(h_ref[...], m, 0)
    up = _modulate(hp_ref[7:8, :], m, 0) * prev_ok
    un = _modulate(hn_ref[0:1, :], m, 0) * next_ok
    lane = lax.broadcasted_iota(jnp.int32, (1, D_MODEL), 1)
    sh = jnp.where(lane < D_MODEL // 2, _shift_down(u, up), _shift_up(u, un))
    dx = sh - u
    mu = mu_ref[...]
    xm = [u + dx * mu[c:c + 1] for c in range(6)]
    r = _dot(xm[0], win_ref[0])
    k = _dot(xm[1], win_ref[1])
    v = _dot(xm[2], win_ref[2])
    t1 = jnp.tanh(_dot(xm[3], wl1_ref[...]))
    t2 = _dot(xm[4], al1_ref[...])
    g = _dot(jax.nn.sigmoid(_dot(xm[5], gl1_ref[...])), gl2_ref[...])
    kk = k * kk_ref[...]
    kk = kk * lax.rsqrt(_seg_sum(kk * kk, e_ref, et_ref) + 1e-12)
    ktsum = None
    for z, (lw_o, kt_o, ab_o) in enumerate(((lw0_o, kt0_o, ab0_o), (lw1_o, kt1_o, ab1_o))):
        d_w = w0_ref[z:z + 1, :] + _dot(t1, wl2_ref[z])
        lw_o[...] = -RWKV_DECAY_SCALE * jax.nn.sigmoid(d_w)
        a = jax.nn.sigmoid(a0_ref[z:z + 1, :] + _dot(t2, al2_ref[z]))
        kt = k * (1.0 + (a - 1.0) * ka_ref[...])
        kt_o[...] = kt.astype(kt_o.dtype)
        ab_o[...] = (kk * a).astype(ab_o.dtype)
        ktsum = kt if ktsum is None else ktsum + kt
    r_o[...] = r.astype(r_o.dtype)
    v_o[...] = v.astype(v_o.dtype)
    kk_o[...] = kk.astype(kk_o.dtype)
    g_o[...] = g.astype(g_o.dtype)
    bv_o[...] = (_seg_sum(r * ktsum * rk_ref[...], e_ref, et_ref) * v).astype(bv_o.dtype)


def _rwkv_scan_kernel(r_ref, v_ref, kk_ref, lw_ref, kt_ref, ab_ref, o_ref, s_ref, *, reverse):
    c = RWKV_CHUNK

    @pl.when(pl.program_id(0) == 0)
    def _():
        s_ref[...] = jnp.zeros_like(s_ref)

    ri = lax.broadcasted_iota(jnp.int32, (c, c), 0)
    ci = lax.broadcasted_iota(jnp.int32, (c, c), 1)
    incl = (ci >= ri) if reverse else (ci <= ri)
    ri2 = lax.broadcasted_iota(jnp.int32, (c, 2 * c), 0)
    ci2 = jnp.bitwise_and(lax.broadcasted_iota(jnp.int32, (c, 2 * c), 1), c - 1)
    incl2 = (ci2 >= ri2) if reverse else (ci2 <= ri2)
    strict2 = (ci2 > ri2) if reverse else (ci2 < ri2)
    lane_a = lax.broadcasted_iota(jnp.int32, (1, LANES), 1) < RWKV_HEAD
    bi = lax.broadcasted_iota(jnp.int32, (LANES, LANES), 0) < RWKV_HEAD
    bj = lax.broadcasted_iota(jnp.int32, (LANES, LANES), 1) < RWKV_HEAD
    blockdiag = bi == bj

    def stack2(x):
        return jnp.concatenate([jnp.where(lane_a, x, 0.0), jnp.where(lane_a, 0.0, x)], axis=0)

    nch = r_ref.shape[0] // c

    def one_chunk(ci, carry):
        rows = pl.ds(pl.multiple_of(((nch - 1 - ci) if reverse else ci) * c, c), c)
        lw = lw_ref[rows, :]
        cl = _dot_sel(jnp.where(incl, 1.0, 0.0), lw, 3)
        tot = cl[0:1, :] if reverse else cl[c - 1:c, :]
        e_in = jnp.exp(cl)
        e_out = jnp.exp(-cl)
        e_end = jnp.exp(tot - cl)
        kk = kk_ref[rows, :].astype(F32)
        kt = kt_ref[rows, :].astype(F32)
        ab = ab_ref[rows, :].astype(F32)
        kap = kk * jnp.exp(cl - lw)
        rh = r_ref[rows, :].astype(F32) * e_in
        kh = kt * e_out
        bh = ab * e_out
        kb = kt * e_end
        bb = ab * e_end
        e_tot = jnp.exp(tot)
        vv = v_ref[rows, :].astype(F32)
        pairs = range(D_MODEL // LANES)
        sls = [slice(p * LANES, (p + 1) * LANES) for p in pairs]
        s = [s_ref[p] for p in pairs]
        xq = [jnp.concatenate([kap[:, sl], rh[:, sl]], axis=0) for sl in sls]
        yk = [jnp.concatenate([stack2(kh[:, sl]), stack2(bh[:, sl])], axis=0) for sl in sls]
        gm = [_dot_nt(xq[p], yk[p]) for p in pairs]
        xs = [_dot_nt(xq[p], s[p]) for p in pairs]
        l_kk = [jnp.where(strict2, g[:c, :2 * c], 0.0) for g in gm]
        l_bk = [jnp.where(strict2, g[:c, 2 * c:], 0.0) for g in gm]
        a_rk = [jnp.where(incl2, g[c:, :2 * c], 0.0) for g in gm]
        a_rb = [jnp.where(incl2, g[c:, 2 * c:], 0.0) for g in gm]
        v2 = [stack2(vv[:, sl]) for sl in sls]
        x = [xs[p][:c] + _dot(l_kk[p], v2[p]) for p in pairs]
        lp = [_dot(l_bk[p], stack2(l_bk[p])) for p in pairs]
        x = [x[p] - _dot(l_bk[p], stack2(x[p])) for p in pairs]
        for it in range(5):
            x = [x[p] + _dot(lp[p], stack2(x[p])) for p in pairs]
            if it < 4:
                lp = [_dot(lp[p], stack2(lp[p])) for p in pairs]
        o = [xs[p][c:] + _dot(jnp.concatenate([a_rk[p], -a_rb[p]], axis=1),
                              jnp.concatenate([v2[p], stack2(x[p])], axis=0)) for p in pairs]
        upd = [_dot_tn(jnp.concatenate([vv[:, sls[p]], -x[p]], axis=0),
                       jnp.concatenate([kb[:, sls[p]], bb[:, sls[p]]], axis=0)) for p in pairs]
        for p in pairs:
            o_ref[rows, sls[p]] = o[p]
            s_ref[p] = s[p] * e_tot[:, sls[p]] + jnp.where(blockdiag, upd[p], 0.0)
        return carry

    lax.fori_loop(0, nch, one_chunk, 0)


def _rwkv_out_kernel(of_ref, ob_ref, bv_ref, g_ref, lg_ref, lb_ref, e_ref, et_ref, w_ref, h_ref, mod_ref, lng_ref,
                     lnb_ref, o_ref):
    o = of_ref[...] + ob_ref[...]
    inv = 1.0 / RWKV_HEAD
    oc = o - _seg_sum(o, e_ref, et_ref) * inv
    var = _seg_sum(oc * oc, e_ref, et_ref) * inv
    y = oc * lax.rsqrt(var + RWKV_GN_EPS) * lg_ref[...] + lb_ref[...] + bv_ref[...]
    yo = _dot(y * g_ref[...], w_ref[...])
    z = DEEPNORM_ALPHA * h_ref[...] + mod_ref[0][2:3] * yo
    o_ref[...] = _ln_rows(z, lng_ref[...], lnb_ref[...])


def _rwkv_mixer(h, mods, nct, mu, w_in, w0, w_l1, w_l2, a0, a_l1, a_l2, g_l1, g_l2, k_k, k_a, r_k, gn_g, gn_b, w_out,
                ln_g, ln_b):
    t, d = h.shape
    nt = t // TM
    bf = MXU_DT
    lw_ = w_l1.shape[-1]
    la_ = a_l1.shape[-1]
    zw = jnp.zeros((lw_, d), F32)
    za = jnp.zeros((la_, d), F32)
    wl1 = jnp.concatenate([w_l1[0], w_l1[1]], axis=1).astype(bf)
    wl2 = jnp.stack([jnp.concatenate([w_l2[0], zw], 0), jnp.concatenate([zw, w_l2[1]], 0)]).astype(bf)
    al1 = jnp.concatenate([a_l1[0], a_l1[1]], axis=1).astype(bf)
    al2 = jnp.stack([jnp.concatenate([a_l2[0], za], 0), jnp.concatenate([za, a_l2[1]], 0)]).astype(bf)
    head_of = jnp.arange(d) // RWKV_HEAD
    e = (head_of[:, None] == jnp.arange(LANES)[None, :]).astype(bf)
    et = e.T
    halo_p = pl.BlockSpec((8, d), lambda i: (jnp.maximum(i * (TM // 8) - 1, 0), 0))
    halo_n = pl.BlockSpec((8, d), lambda i: (jnp.minimum((i + 1) * (TM // 8), t // 8 - 1), 0))
    args = [h, h, h, mods, mu, w_in.astype(bf), wl1, wl2, w0, al1, al2, a0, g_l1.astype(bf), g_l2.astype(bf),
            k_k[None], k_a[None], r_k.reshape(1, d), e, et]
    ins = [_row_spec(d), halo_p, halo_n, _mod_spec(nct)] + [_full_spec(a.shape) for a in args[4:]]
    outs = pl.pallas_call(
        functools.partial(_rwkv_prep_kernel, nct=nct, nt=nt), grid=(nt,), in_specs=ins,
        out_specs=[_row_spec(d)] * 11,
        out_shape=[jax.ShapeDtypeStruct((t, d), dt) for dt in [ACT_DT] * 5 + [F32] * 2 + [ACT_DT] * 4],
        compiler_params=_cparams("arbitrary"), name="rwkv_prep",
    )(*args)
    r, v, kk, g, bv, lw0, lw1, kt0, kt1, ab0, ab1 = outs
    c = RWKV_CHUNK
    ncc, nc = nct * (TM // c), t // c
    o_dir = []
    for d_, (lw, kt, ab) in enumerate(((lw0, kt0, ab0), (lw1, kt1, ab1))):
        reverse = d_ == 1
        spec = pl.BlockSpec((TM, d), lambda g_, reverse=reverse: (_tile_of(g_, nct, nt, reverse), 0))
        o_dir.append(pl.pallas_call(
            functools.partial(_rwkv_scan_kernel, reverse=reverse), grid=(nt,), in_specs=[spec] * 6, out_specs=spec,
            out_shape=jax.ShapeDtypeStruct((t, d), F32),
            scratch_shapes=[pltpu.VMEM((d // LANES, LANES, LANES), F32)],
            compiler_params=_cparams("arbitrary"), name="rwkv_scan_%d" % d_,
        )(r, v, kk, lw, kt, ab))
    args = [o_dir[0], o_dir[1], bv, g, gn_g[None], gn_b[None], e, et, w_out.astype(bf), h, mods, ln_g[None], ln_b[None]]
    ins = [_row_spec(d)] * 4 + [_full_spec(a.shape) for a in args[4:9]] + [_row_spec(d), _mod_spec(nct),
                                                                          _full_spec((1, d)), _full_spec((1, d))]
    return pl.pallas_call(
        _rwkv_out_kernel, grid=(nt,), in_specs=ins, out_specs=_row_spec(d),
        out_shape=jax.ShapeDtypeStruct((t, d), F32), compiler_params=_cparams("arbitrary"), name="rwkv_out",
    )(*args)


def _ret_in_kernel(h_ref, mod_ref, w_ref, cos_ref, sin_ref, q_o, k_o, v_o, g_o):
    d = D_MODEL
    u = _modulate(h_ref[...], mod_ref[0], 0).astype(MXU_DT)
    q = _dot(u, w_ref[:, 0:d])
    k = _dot(u, w_ref[:, d:2 * d]) * (RET_QK ** -0.5)
    v_o[...] = _dot(u, w_ref[:, 2 * d:4 * d]).astype(v_o.dtype)
    g_o[...] = _silu(_dot(u, w_ref[:, 4 * d:6 * d])).astype(g_o.dtype)
    cos = cos_ref[...]
    sin = sin_ref[...]
    half = RET_QK // 2
    for z, z_o in ((q, q_o), (k, k_o)):
        for hh in range(RET_HEADS):
            lo = z[:, hh * RET_QK:hh * RET_QK + half]
            hi = z[:, hh * RET_QK + half:(hh + 1) * RET_QK]
            zh = jnp.concatenate([lo, hi], axis=1)
            rot = jnp.concatenate([-hi, lo], axis=1)
            z_o[:, hh * RET_QK:(hh + 1) * RET_QK] = (zh * cos + rot * sin).astype(z_o.dtype)


def _ret_scan_kernel(qf_ref, kf_ref, vf_ref, qb_ref, kb_ref, vb_ref, inner_ref, qd_ref, kd_ref, bd_ref, of_ref, ob_ref,
                     r_ref):
    @pl.when(pl.program_id(0) == 0)
    def _():
        r_ref[...] = jnp.zeros_like(r_ref)

    refs = ((qf_ref, kf_ref, vf_ref, of_ref), (qb_ref, kb_ref, vb_ref, ob_ref))
    cells = [(d, hh) for d in range(2) for hh in range(RET_HEADS)]
    qs = lambda hh: slice(hh * RET_QK, (hh + 1) * RET_QK)
    vs = lambda hh: slice(hh * RET_V, (hh + 1) * RET_V)
    q = {(d, hh): refs[d][0][:, qs(hh)] for d, hh in cells}
    k = {(d, hh): refs[d][1][:, qs(hh)] for d, hh in cells}
    v = {(d, hh): refs[d][2][:, vs(hh)] for d, hh in cells}
    state = {c: r_ref[c[0], c[1]] for c in cells}
    scores = {c: _dot_nt(q[c], k[c]) * inner_ref[c[0], c[1]] for c in cells}
    carry_in = {c: _dot(q[c], state[c]) * qd_ref[c[0], c[1]] for c in cells}
    upd = {c: _dot_tn(k[c] * kd_ref[c[0], c[1]], v[c]) for c in cells}
    for c in cells:
        refs[c[0]][3][:, vs(c[1])] = _dot(scores[c], v[c]) + carry_in[c]
        r_ref[c[0], c[1]] = state[c] * bd_ref[c[0], c[1]] + upd[c]


def _ret_out_kernel(of_ref, ob_ref, g_ref, gg_ref, gb_ref, w_ref, h_ref, mod_ref, lng_ref, lnb_ref, o_ref):
    parts = []
    for hh in range(RET_HEADS):
        sl = slice(hh * RET_V, (hh + 1) * RET_V)
        o = of_ref[:, sl] + ob_ref[:, sl]
        mu = jnp.mean(o, axis=-1, keepdims=True)
        oc = o - mu
        var = jnp.mean(oc * oc, axis=-1, keepdims=True)
        y = oc * lax.rsqrt(var + LN_EPS) * gg_ref[:, sl] + gb_ref[:, sl]
        parts.append((g_ref[:, sl] * y).astype(MXU_DT))
    yo = _dot(jnp.concatenate(parts, axis=1), w_ref[...])
    z = DEEPNORM_ALPHA * h_ref[...] + mod_ref[0][2:3] * yo
    o_ref[...] = _ln_rows(z, lng_ref[...], lnb_ref[...])


def _ret_mixer(h, mods, nct, rope_cos, rope_sin, w_in, decay_logit, gn_g, gn_b, w_out, ln_g, ln_b):
    t, d = h.shape
    nt = t // TM
    hv = RET_HEADS * RET_V
    q, k, v, sg = pl.pallas_call(
        _ret_in_kernel, grid=(nt,),
        in_specs=[_row_spec(d), _mod_spec(nct), _full_spec(w_in.shape), _row_spec(RET_QK), _row_spec(RET_QK)],
        out_specs=[_row_spec(d), _row_spec(d), _row_spec(hv), _row_spec(hv)],
        out_shape=[jax.ShapeDtypeStruct((t, w), ACT_DT) for w in (d, d, hv, hv)],
        compiler_params=_cparams("arbitrary"), name="ret_in",
    )(h, mods, w_in.astype(MXU_DT), rope_cos, rope_sin)
    c = RET_CHUNK
    ncc, nc = nct * (TM // c), t // c
    log_gamma = jax.nn.log_sigmoid(decay_logit.astype(F32))
    pos = jnp.arange(c, dtype=F32)
    tabs = []
    for d_ in range(2):
        lg = log_gamma[d_][:, None, None]
        p = (c - 1.0 - pos) if d_ == 1 else pos
        rel = p[:, None] - p[None, :]
        tabs.append((jnp.where(rel >= 0, jnp.exp(jnp.maximum(rel, 0.0) * lg), 0.0),
                     jnp.exp((p + 1.0) * log_gamma[d_][:, None])[:, :, None],
                     jnp.exp((c - 1.0 - p) * log_gamma[d_][:, None])[:, :, None],
                     jnp.exp(c * log_gamma[d_])[:, None, None]))
    inner, q_dec, k_dec, blk_dec = (jnp.stack(z) for z in zip(*tabs))
    cs = lambda w, reverse: pl.BlockSpec((c, w), lambda g_: (_tile_of(g_, ncc, nc, reverse), 0))
    o_dir = pl.pallas_call(
        _ret_scan_kernel, grid=(nc,),
        in_specs=[cs(d, False), cs(d, False), cs(hv, False), cs(d, True), cs(d, True), cs(hv, True),
                  _full_spec(inner.shape), _full_spec(q_dec.shape), _full_spec(k_dec.shape), _full_spec(blk_dec.shape)],
        out_specs=[cs(hv, False), cs(hv, True)], out_shape=[jax.ShapeDtypeStruct((t, hv), F32)] * 2,
        scratch_shapes=[pltpu.VMEM((2, RET_HEADS, RET_QK, RET_V), F32)],
        compiler_params=_cparams("arbitrary"), name="ret_scan",
    )(q, k, v, q, k, v, inner, q_dec, k_dec, blk_dec)
    return pl.pallas_call(
        _ret_out_kernel, grid=(nt,),
        in_specs=[_row_spec(hv)] * 3 + [_full_spec((1, hv)), _full_spec((1, hv)), _full_spec((hv, d)), _row_spec(d),
                                        _mod_spec(nct), _full_spec((1, d)), _full_spec((1, d))],
        out_specs=_row_spec(d), out_shape=jax.ShapeDtypeStruct((t, d), F32),
        compiler_params=_cparams("arbitrary"), name="ret_out",
    )(o_dir[0], o_dir[1], sg, gn_g[None], gn_b[None], w_out.astype(MXU_DT), h, mods, ln_g[None], ln_b[None])


def _hgrn_in_kernel(h_ref, mod_ref, w_ref, lb_ref, bf_ref, q_o, v_o, g_o, f0_o, f1_o):
    d = D_MODEL
    u = _modulate(h_ref[...], mod_ref[0], 0).astype(MXU_DT)
    lb = lb_ref[...]
    q_o[...] = _silu(_dot(u, w_ref[:, 0:d])).astype(q_o.dtype)
    f0_o[...] = lb + (1.0 - lb) * jax.nn.sigmoid(_dot(u, w_ref[:, d:2 * d]) + bf_ref[0:1, :])
    f1_o[...] = lb + (1.0 - lb) * jax.nn.sigmoid(_dot(u, w_ref[:, 2 * d:3 * d]) + bf_ref[1:2, :])
    v_o[...] = _dot(u, w_ref[:, 3 * d:4 * d]).astype(v_o.dtype)
    g_o[...] = _silu(_dot(u, w_ref[:, 4 * d:5 * d])).astype(g_o.dtype)


def _hgrn_scan_kernel(qf_ref, vf_ref, ff_ref, qb_ref, vb_ref, fb_ref, of_ref, ob_ref, s_ref, b_s, rb_s, rk_s, rv_s):
    hb = HGRN_BLOCK
    nb = TM // hb
    half = hb // 2
    dirs = ((qf_ref, vf_ref, ff_ref, of_ref, False), (qb_ref, vb_ref, fb_ref, ob_ref, True))

    @pl.when(pl.program_id(0) == 0)
    def _():
        s_ref[...] = jnp.zeros_like(s_ref)

    span = 4 * hb
    ri = lax.broadcasted_iota(jnp.int32, (span, span), 0)
    ci = lax.broadcasted_iota(jnp.int32, (span, span), 1)
    same_block = (ri // hb) == (ci // hb)
    ti = lax.broadcasted_iota(jnp.int32, (half, 1), 0)
    rowi = lax.broadcasted_iota(jnp.int32, (hb, 1), 0)
    heads = range(HGRN_HEADS)
    sls = [slice(hh * HGRN_HEAD, (hh + 1) * HGRN_HEAD) for hh in heads]
    cells = [(d, hh) for d in range(2) for hh in heads]

    for d, (q_ref, v_ref, f_ref, o_ref, reverse) in enumerate(dirs):
        tri = jnp.where(jnp.logical_and(same_block, (ci >= ri) if reverse else (ci <= ri)), 1.0, 0.0)
        for r0 in range(0, TM, span):
            b_s[d, r0:r0 + span] = _dot_sel(tri, jnp.log(f_ref[r0:r0 + span, :]), 3)

    def block(bi, par):
        pre = []
        for d, (q_ref, v_ref, f_ref, o_ref, reverse) in enumerate(dirs):
            blk = (nb - 1 - bi) if reverse else bi
            r0 = pl.multiple_of(blk * hb, hb)
            kx = 1.0 - f_ref[pl.ds(r0, hb), :]
            q = q_ref[pl.ds(r0, hb), :].astype(F32)
            v = v_ref[pl.ds(r0, hb), :].astype(F32)
            b = b_s[d, pl.ds(r0, hb), :]
            rb_s[d, par] = b
            rk_s[d, par] = kx
            rv_s[d, par] = v
            tot = b[0:1, :] if reverse else b[hb - 1:hb, :]
            first = (rowi >= half) if reverse else (rowi < half)
            beta = b[half:half + 1, :] if reverse else b[half - 1:half, :]
            pre.append(dict(
                r0=r0, q=q, v=v, b=b, qe=q * jnp.exp(b), kb=kx * jnp.exp(tot - b), e_tot=jnp.exp(tot),
                k_first=kx * jnp.exp(jnp.where(first, beta - b, -jnp.inf)),
                q_second=q * jnp.exp(jnp.where(first, -jnp.inf, b - beta)),
                causal=[(ti <= si) if reverse else (ti >= si) for si in range(half)]))
        for g0 in range(0, len(cells), HGRN_GROUP):
            grp = cells[g0:g0 + HGRN_GROUP]
            s = {c: s_ref[c[0], c[1]] for c in grp}
            m_first = {(d, hh): _dot_tn(pre[d]['k_first'][:, sls[hh]], pre[d]['v'][:, sls[hh]]) for d, hh in grp}
            o = {(d, hh): _dot_nt(pre[d]['qe'][:, sls[hh]], s[d, hh]) + _dot(pre[d]['q_second'][:, sls[hh]], m_first[d, hh])
                 for d, hh in grp}
            upd = {(d, hh): _dot_tn(pre[d]['v'][:, sls[hh]], pre[d]['kb'][:, sls[hh]]) for d, hh in grp}
            for d, hh in grp:
                sl = sls[hh]
                p = pre[d]
                parts = []
                for lo in (0, half):
                    bt = p['b'][lo:lo + half, sl]
                    qt = p['q'][lo:lo + half, sl]
                    acc = jnp.zeros((half, HGRN_HEAD), F32)
                    for si in range(half):
                        row = slice(lo + si, lo + si + 1)
                        dec = jnp.exp(jnp.where(p['causal'][si], bt - rb_s[d, par, row, sl], -jnp.inf))
                        sc = jnp.sum(qt * rk_s[d, par, row, sl] * dec, axis=-1, keepdims=True)
                        acc = acc + sc * rv_s[d, par, row, sl]
                    parts.append(acc)
                dirs[d][3][pl.ds(p['r0'], hb), sl] = o[d, hh] + jnp.concatenate(parts, axis=0)
                s_ref[d, hh] = s[d, hh] * p['e_tot'][:, sl] + upd[d, hh]

    def trip(bj, carry):
        for par in range(HGRN_UNROLL):
            block(HGRN_UNROLL * bj + par, par)
        return carry

    lax.fori_loop(0, nb // HGRN_UNROLL, trip, 0)


def _hgrn_out_kernel(of_ref, ob_ref, g_ref, ng_ref, w_ref, h_ref, mod_ref, lng_ref, lnb_ref, o_ref):
    parts = []
    for hh in range(HGRN_HEADS):
        sl = slice(hh * HGRN_HEAD, (hh + 1) * HGRN_HEAD)
        o = of_ref[:, sl] + ob_ref[:, sl]
        y = o * lax.rsqrt(jnp.mean(o * o, axis=-1, keepdims=True) + LN_EPS) * ng_ref[...]
        parts.append((y * g_ref[:, sl]).astype(MXU_DT))
    yo = _dot(jnp.concatenate(parts, axis=1), w_ref[...])
    z = DEEPNORM_ALPHA * h_ref[...] + mod_ref[0][2:3] * yo
    o_ref[...] = _ln_rows(z, lng_ref[...], lnb_ref[...])


def _hgrn_mixer(h, mods, nct, lb, w_in, b_f, norm_g, w_out, ln_g, ln_b):
    t, d = h.shape
    nt = t // TM
    q, v, sg, f0, f1 = pl.pallas_call(
        _hgrn_in_kernel, grid=(nt,),
        in_specs=[_row_spec(d), _mod_spec(nct), _full_spec(w_in.shape), _full_spec((1, d)), _full_spec((2, d))],
        out_specs=[_row_spec(d)] * 5,
        out_shape=[jax.ShapeDtypeStruct((t, d), dt) for dt in [ACT_DT] * 3 + [F32] * 2],
        compiler_params=_cparams("arbitrary"), name="hgrn_in",
    )(h, mods, w_in.astype(MXU_DT), lb[None], b_f)
    fwd = pl.BlockSpec((TM, d), lambda g_: (g_, 0))
    bwd = pl.BlockSpec((TM, d), lambda g_: (_tile_of(g_, nct, nt, True), 0))
    o_dir = pl.pallas_call(
        _hgrn_scan_kernel, grid=(nt,), in_specs=[fwd] * 3 + [bwd] * 3, out_specs=[fwd, bwd],
        out_shape=[jax.ShapeDtypeStruct((t, d), F32)] * 2,
        scratch_shapes=[pltpu.VMEM((2, HGRN_HEADS, HGRN_HEAD, HGRN_HEAD), F32)] + [pltpu.VMEM((2, TM, d), F32)]
        + [pltpu.VMEM((2, HGRN_UNROLL, HGRN_BLOCK, d), F32)] * 3,
        compiler_params=_cparams("arbitrary"), name="hgrn_scan",
    )(q, v, f0, q, v, f1)
    return pl.pallas_call(
        _hgrn_out_kernel, grid=(nt,),
        in_specs=[_row_spec(d)] * 3 + [_full_spec((1, HGRN_HEAD)), _full_spec((d, d)), _row_spec(d), _mod_spec(nct),
                                       _full_spec((1, d)), _full_spec((1, d))],
        out_specs=_row_spec(d), out_shape=jax.ShapeDtypeStruct((t, d), F32),
        compiler_params=_cparams("arbitrary"), name="hgrn_out",
    )(o_dir[0], o_dir[1], sg, norm_g[None], w_out.astype(MXU_DT), h, mods, ln_g[None], ln_b[None])


def _router_kernel(h_ref, mod_ref, rw_ref, rb_ref, u_o, gate_o, rank_o, x_o):
    u = _modulate(h_ref[...], mod_ref[0], 3)
    u_o[...] = u.astype(u_o.dtype)
    w_hi, w_lo = _split(rw_ref[...], 2)
    u_hi, u_lo = _split(u, 2)
    nt_dims = (((1,), (1,)), ((), ()))
    logits = (lax.dot_general(w_hi, u_hi, nt_dims, preferred_element_type=F32)
              + lax.dot_general(w_hi, u_lo, nt_dims, preferred_element_type=F32)
              + lax.dot_general(w_lo, u_hi, nt_dims, preferred_element_type=F32))
    ne, gs = N_EXPERTS, N_EXPERTS // N_GROUPS
    neg = -jnp.inf
    scores = jax.nn.sigmoid(logits[:ne])
    choice = scores + rb_ref[:ne]
    c3 = choice.reshape(N_GROUPS, gs, TM)
    mi = lax.broadcasted_iota(jnp.int32, c3.shape, 1).astype(F32)
    m1 = jnp.max(c3, axis=1, keepdims=True)
    i1 = jnp.min(jnp.where(c3 == m1, mi, float(gs)), axis=1, keepdims=True)
    m2 = jnp.max(jnp.where(mi == i1, neg, c3), axis=1, keepdims=True)
    gscore = m1 + m2
    gi = lax.broadcasted_iota(jnp.int32, gscore.shape, 0).astype(F32)
    gsel = jnp.zeros(gscore.shape, F32)
    for _ in range(TOPK_GROUPS):
        gm = jnp.max(gscore, axis=0, keepdims=True)
        pick = gi == jnp.min(jnp.where(gscore == gm, gi, float(N_GROUPS)), axis=0, keepdims=True)
        gsel = jnp.where(pick, 1.0, gsel)
        gscore = jnp.where(pick, neg, gscore)
    emask = jnp.broadcast_to(gsel, c3.shape).reshape(ne, TM)
    masked = jnp.where(emask > 0.5, choice, neg)
    ei = lax.broadcasted_iota(jnp.int32, masked.shape, 0).astype(F32)
    chosen = jnp.zeros(masked.shape, F32)
    for _ in range(TOP_K):
        em = jnp.max(masked, axis=0, keepdims=True)
        pick = ei == jnp.min(jnp.where(masked == em, ei, float(ne)), axis=0, keepdims=True)
        chosen = jnp.where(pick, 1.0, chosen)
        masked = jnp.where(pick, neg, masked)
    top_w = scores * chosen
    gates = ROUTED_SCALE * top_w / jnp.sum(top_w, axis=0, keepdims=True)
    ti = lax.broadcasted_iota(jnp.int32, (TM, TM), 0)
    tj = lax.broadcasted_iota(jnp.int32, (TM, TM), 1)
    before = jnp.where(ti < tj, 1.0, 0.0).astype(MXU_DT)
    prefix = jnp.dot(chosen.astype(MXU_DT), before, preferred_element_type=F32)
    rank = jnp.where(chosen > 0.5, prefix, -1.0)
    gate_o[0] = gates
    rank_o[0] = rank
    cap = MOE_CAP
    slot = lax.broadcasted_iota(jnp.int32, (cap, TM), 0).astype(F32)
    ub = u.astype(MXU_DT)
    for g0 in range(0, ne, MOE_EGROUP):
        onehot = jnp.concatenate([jnp.where(slot == rank[e:e + 1, :], 1.0, 0.0).astype(MXU_DT)
                                  for e in range(g0, g0 + MOE_EGROUP)], axis=0)
        xg = jnp.dot(onehot, ub, preferred_element_type=F32)
        x_o[0, g0:g0 + MOE_EGROUP] = xg.reshape(MOE_EGROUP, cap, D_MODEL).astype(x_o.dtype)


def _expert_kernel(x_ref, wgu_ref, wd_ref, y_ref, wgu_b, wd_b):
    @pl.when(pl.program_id(1) == 0)
    def _():
        wgu_b[0] = wgu_ref[0, 0].astype(wgu_b.dtype)
        wd_b[0] = wd_ref[0, 0].astype(wd_b.dtype)

    g = x_ref.shape[0]
    ch = max(c for c in range(1, MOE_CHUNK + 1) if g % c == 0)
    ed = EXPERT_DIM

    def chunk(ci, carry):
        t0 = ci * ch
        x = x_ref[pl.ds(t0, ch)].reshape(ch * MOE_CAP, D_MODEL)
        gu = _dot(x, wgu_b[0])
        y = _dot(_silu(gu[:, :ed]) * gu[:, ed:], wd_b[0])
        y_ref[pl.ds(t0, ch)] = y.reshape(ch, 1, MOE_CAP, D_MODEL).astype(y_ref.dtype)
        return carry

    lax.fori_loop(0, g // ch, chunk, 0)


def _combine_kernel(u_ref, gt_ref, rt_ref, y_ref, sgu_ref, sd_ref, h_ref, mod_ref, lng_ref, lnb_ref, *rest, extra):
    if extra:
        ex_ref, o_ref = rest
    else:
        (o_ref,) = rest
    ed = EXPERT_DIM
    cap = MOE_CAP
    gu = _dot(u_ref[...], sgu_ref[...])
    acc = _dot(_silu(gu[:, :ed]) * gu[:, ed:], sd_ref[...])
    if extra:
        acc = acc + ex_ref[...]
    slot = lax.broadcasted_iota(jnp.int32, (cap, TM), 0).astype(F32)
    for g0 in range(0, N_EXPERTS, MOE_EGROUP):
        pw = jnp.concatenate([jnp.where(slot == rt_ref[0, e:e + 1, :], gt_ref[0, e:e + 1, :], 0.0).astype(MXU_DT)
                              for e in range(g0, g0 + MOE_EGROUP)], axis=0)
        yg = y_ref[0, g0:g0 + MOE_EGROUP].reshape(MOE_EGROUP * cap, D_MODEL)
        acc = acc + _dot_tn(pw, yg)
    z = DEEPNORM_ALPHA * h_ref[...] + mod_ref[0][5:6] * acc
    o_ref[...] = _ln_rows(z, lng_ref[...], lnb_ref[...])


def _overflow_kernel(tile_ref, exp_ref, nr_ref, n_ref, u_ref, gate_ref, rank_ref, wgu_ref, wd_ref, zero_ref, o_ref):
    del zero_ref
    s = pl.program_id(0)
    tile = tile_ref[s]
    e = exp_ref[s]
    ed = EXPERT_DIM
    cap = MOE_CAP
    active = s < n_ref[0]
    first = jnp.logical_or(s == 0, tile_ref[jnp.maximum(s - 1, 0)] != tile)

    @pl.when(jnp.logical_and(active, first))
    def _():
        o_ref[...] = jnp.zeros_like(o_ref)

    @pl.when(active)
    def _():
        wgu = wgu_ref[0]
        wd = wd_ref[0]
        rank = rank_ref[0, pl.ds(e, 1), :]
        gate = gate_ref[0, pl.ds(e, 1), :]

        def one_round(r, carry):
            slot = lax.broadcasted_iota(jnp.int32, (cap, TM), 0).astype(F32) + (r * cap).astype(F32)
            hit = slot == rank
            x = _dot(jnp.where(hit, 1.0, 0.0), u_ref[...])
            gu = _dot(x, wgu)
            y = _dot(_silu(gu[:, :ed]) * gu[:, ed:], wd)
            o_ref[...] += _dot_tn(jnp.where(hit, gate, 0.0), y)
            return carry

        lax.fori_loop(1, nr_ref[s], one_round, 0)


def _moe_layer(h, mods, nct, layer, router_w, router_b, w_gu, w_down, sh_gu, sh_down, ln_g, ln_b, latent_only=False):
    t, d = h.shape
    nt = t // TM
    ne, cap = N_EXPERTS, MOE_CAP
    rw = jnp.concatenate([router_w.T, jnp.zeros((LANES - ne, d), F32)], axis=0)
    rb = jnp.concatenate([router_b, jnp.zeros((LANES - ne,), F32)])[:, None]
    per_tile = pl.BlockSpec((1, ne, TM), lambda i: (i, 0, 0))
    slots = pl.BlockSpec((1, ne, cap, d), lambda i: (i, 0, 0, 0))
    u, gates, ranks, xs = pl.pallas_call(
        _router_kernel, grid=(nt,),
        in_specs=[_row_spec(d), _mod_spec(nct), _full_spec((LANES, d)), _full_spec((LANES, 1))],
        out_specs=[_row_spec(d), per_tile, per_tile, slots],
        out_shape=[jax.ShapeDtypeStruct((t, d), MXU_DT), jax.ShapeDtypeStruct((nt, ne, TM), F32),
                   jax.ShapeDtypeStruct((nt, ne, TM), F32), jax.ShapeDtypeStruct((nt, ne, cap, d), MXU_DT)],
        compiler_params=_cparams("arbitrary"), name="moe_router",
    )(h, mods, rw, rb)

    run = max(g for g in range(1, MOE_RUN + 1) if nt % g == 0)
    ys, wgu_b, wd_b = pl.pallas_call(
        _expert_kernel, grid=(ne, nt // run),
        in_specs=[pl.BlockSpec((run, 1, cap, d), lambda e, c: (c, e, 0, 0)),
                  pl.BlockSpec((1, 1, d, 2 * EXPERT_DIM), lambda e, c: (layer, e, 0, 0)),
                  pl.BlockSpec((1, 1, EXPERT_DIM, d), lambda e, c: (layer, e, 0, 0))],
        out_specs=[pl.BlockSpec((run, 1, cap, d), lambda e, c: (c, e, 0, 0)),
                   pl.BlockSpec((1, d, 2 * EXPERT_DIM), lambda e, c: (e, 0, 0)),
                   pl.BlockSpec((1, EXPERT_DIM, d), lambda e, c: (e, 0, 0))],
        out_shape=[jax.ShapeDtypeStruct((nt, ne, cap, d), MXU_DT),
                   jax.ShapeDtypeStruct((ne, d, 2 * EXPERT_DIM), MXU_DT), jax.ShapeDtypeStruct((ne, EXPERT_DIM, d), MXU_DT)],
        compiler_params=_cparams("arbitrary", "arbitrary"), name="moe_experts",
    )(xs, w_gu, w_down)

    sgu, sd = sh_gu.astype(MXU_DT), sh_down.astype(MXU_DT)
    base_specs = [_row_spec(d), per_tile, per_tile, slots, _full_spec(sgu.shape), _full_spec(sd.shape),
                  _row_spec(d), _mod_spec(nct), _full_spec((1, d)), _full_spec((1, d))]
    base_args = (u, gates, ranks, ys, sgu, sd, h, mods, ln_g[None], ln_b[None])

    skip = nct if latent_only else 0
    out_spec = pl.BlockSpec((TM, d), lambda i: (jnp.maximum(i - skip, 0), 0))

    def combine(*extra):
        return pl.pallas_call(
            functools.partial(_combine_kernel, extra=bool(extra)), grid=(nt,),
            in_specs=base_specs + [_row_spec(d)] * len(extra), out_specs=out_spec,
            out_shape=jax.ShapeDtypeStruct((t - skip * TM, d), F32), compiler_params=_cparams("arbitrary"),
            name="moe_combine",
        )(*base_args, *extra)

    count = (jnp.max(ranks, axis=-1).astype(jnp.int32) + 1).reshape(-1)
    over = count > cap
    n_over = jnp.sum(over.astype(jnp.int32))

    def with_overflow(size):
        def run():
            idx = jnp.nonzero(over, size=size, fill_value=0)[0].astype(jnp.int32)
            idx = jnp.where(jnp.arange(size) < n_over, idx, idx[jnp.maximum(n_over - 1, 0)])
            tiles, exps = idx // ne, idx % ne
            rounds = (count[idx] + cap - 1) // cap
            grid_spec = pltpu.PrefetchScalarGridSpec(
                num_scalar_prefetch=4, grid=(size,),
                in_specs=[pl.BlockSpec((TM, d), lambda s, tl, ex, nr, n: (tl[s], 0)),
                          pl.BlockSpec((1, ne, TM), lambda s, tl, ex, nr, n: (tl[s], 0, 0)),
                          pl.BlockSpec((1, ne, TM), lambda s, tl, ex, nr, n: (tl[s], 0, 0)),
                          pl.BlockSpec((1, d, 2 * EXPERT_DIM), lambda s, tl, ex, nr, n: (ex[s], 0, 0)),
                          pl.BlockSpec((1, EXPERT_DIM, d), lambda s, tl, ex, nr, n: (ex[s], 0, 0)),
                          pl.BlockSpec(memory_space=pl.ANY)],
                out_specs=pl.BlockSpec((TM, d), lambda s, tl, ex, nr, n: (tl[s], 0)))
            extra = pl.pallas_call(
                _overflow_kernel, grid_spec=grid_spec, out_shape=jax.ShapeDtypeStruct((t, d), F32),
                input_output_aliases={9: 0}, compiler_params=_cparams("arbitrary"), name="moe_overflow",
            )(tiles, exps, rounds, n_over[None], u, gates, ranks, wgu_b, wd_b, jnp.zeros((t, d), F32))
            return combine(extra)
        return run

    sizes = sorted({min(sz, nt * ne) for sz in MOE_OVER_STEPS} | {nt * ne})
    branch = sum((n_over > sz).astype(jnp.int32) for sz in [0] + sizes[:-1])
    return lax.switch(branch, [combine] + [with_overflow(sz) for sz in sizes])


def kernel(x, c, ctx, c_ctx, ada_w, ada_b, post_ln_g, post_ln_b, lru_w_in, lru_conv_w, lru_conv_b, lru_gate_w, lru_gate_b, lru_lambda, lru_w_out, rwkv_mu, rwkv_w_in, rwkv_w0, rwkv_w_l1, rwkv_w_l2, rwkv_a0, rwkv_a_l1, rwkv_a_l2, rwkv_g_l1, rwkv_g_l2, rwkv_k_k, rwkv_k_a, rwkv_r_k, rwkv_ln_g, rwkv_ln_b, rwkv_w_out, ret_w_in, ret_decay, ret_gn_g, ret_gn_b, ret_w_out, hgrn_w_in, hgrn_b_f, hgrn_lb, hgrn_norm_g, hgrn_w_out, moe_router, moe_bias, moe_w_gu, moe_w_down, moe_sh_gu, moe_sh_down):
    assert x.shape[0] == 1 and ctx.shape[0] == 1
    n_ctx, n_lat, d = ctx.shape[1], x.shape[1], x.shape[2]
    assert n_ctx % TM == 0 and n_lat % TM == 0 and d == D_MODEL
    nct = n_ctx // TM
    rows = n_lat // GRID_W
    n_freq = RET_QK // 4
    freqs = ROPE_BASE ** (-jnp.arange(n_freq, dtype=F32) / n_freq)
    ang_row = jnp.arange(rows, dtype=F32)[:, None] * freqs
    ang_col = jnp.arange(GRID_W, dtype=F32)[:, None] * freqs

    def table(fn, ctx_value):
        lat = jnp.concatenate([jnp.repeat(fn(ang_row), GRID_W, axis=0), jnp.tile(fn(ang_col), (rows, 1))], axis=-1)
        return jnp.concatenate([jnp.full((n_ctx, RET_QK), ctx_value, F32), jnp.concatenate([lat, lat], axis=-1)], axis=0)

    rope_cos = table(jnp.cos, 1.0)
    rope_sin = table(jnp.sin, 0.0)
    lb_cum = jnp.cumsum(jax.nn.softmax(hgrn_lb.astype(F32), axis=0), axis=0)

    cond = jnp.concatenate([c_ctx[None], c, jnp.zeros((6, d), F32)], axis=0)
    mods_all = _ada_mods(cond, ada_w, ada_b)
    h = jnp.concatenate([ctx[0], x[0]], axis=0)
    for i in range(DEPTH):
        kind, j = i % N_MIXERS, i // N_MIXERS
        mods = mods_all[i]
        lng, lnb = post_ln_g[i, 0], post_ln_b[i, 0]
        if kind == 0:
            h = _lru_mixer(h, mods, nct, lru_w_in[j], lru_conv_w[j], lru_conv_b[j], lru_gate_w[j], lru_gate_b[j],
                           lru_lambda[j], lru_w_out[j], lng, lnb)
        elif kind == 1:
            h = _rwkv_mixer(h, mods, nct, rwkv_mu[j], rwkv_w_in[j], rwkv_w0[j], rwkv_w_l1[j], rwkv_w_l2[j], rwkv_a0[j],
                            rwkv_a_l1[j], rwkv_a_l2[j], rwkv_g_l1[j], rwkv_g_l2[j], rwkv_k_k[j], rwkv_k_a[j],
                            rwkv_r_k[j], rwkv_ln_g[j], rwkv_ln_b[j], rwkv_w_out[j], lng, lnb)
        elif kind == 2:
            h = _ret_mixer(h, mods, nct, rope_cos, rope_sin, ret_w_in[j], ret_decay[j], ret_gn_g[j], ret_gn_b[j],
                           ret_w_out[j], lng, lnb)
        else:
            h = _hgrn_mixer(h, mods, nct, lb_cum[i] - lb_cum[0], hgrn_w_in[j], hgrn_b_f[j], hgrn_norm_g[j],
                            hgrn_w_out[j], lng, lnb)
        h = _moe_layer(h, mods, nct, i, moe_router[i], moe_bias[i], moe_w_gu, moe_w_down, moe_sh_gu[i],
                       moe_sh_down[i], post_ln_g[i, 1], post_ln_b[i, 1], latent_only=i == DEPTH - 1)
    return h[None]
```

```python
import math
import functools
import jax
import jax.numpy as jnp
from jax import lax
from jax.experimental import pallas as pl
from jax.experimental.pallas import tpu as pltpu

F32 = jnp.float32
MXU_DT = jnp.bfloat16
ACT_DT = jnp.bfloat16
LANES = 128
TM = 256
VMEM_LIMIT = 56 * 2 ** 20

D_MODEL = 1024
DEPTH = 4
GRID_W = 64
N_MIXERS = 4
DEEPNORM_ALPHA = (2.0 * DEPTH) ** 0.25
LN_EPS = 1e-5
LRU_WIDTH = D_MODEL
LRU_BLOCKS = 16
LRU_BLOCK = LRU_WIDTH // LRU_BLOCKS
LRU_C = 8.0
RWKV_HEAD = 64
RWKV_HEADS = D_MODEL // RWKV_HEAD
RWKV_DECAY_SCALE = math.exp(-0.5)
RWKV_GN_EPS = 64e-5
RWKV_CHUNK = 64
RET_HEADS = 4
RET_QK = D_MODEL // RET_HEADS
RET_V = 2 * RET_QK
RET_CHUNK = 128
ROPE_BASE = 10000.0
HGRN_HEADS = 8
HGRN_HEAD = D_MODEL // HGRN_HEADS
HGRN_BLOCK = 16
HGRN_GROUP = 16
HGRN_UNROLL = 4
N_EXPERTS = 64
TOP_K = 8
N_GROUPS = 8
TOPK_GROUPS = 4
EXPERT_DIM = 256
ROUTED_SCALE = 2.5
MOE_OVER_STEPS = (128, 256, 512)
MOE_CAP = 64
MOE_EGROUP = 8
MOE_RUN = 65
MOE_CHUNK = 13


def _cparams(*sem):
    return pltpu.CompilerParams(dimension_semantics=sem, vmem_limit_bytes=VMEM_LIMIT)


def _dot(a, b):
    return jnp.dot(a.astype(MXU_DT), b.astype(MXU_DT), preferred_element_type=F32)


def _dot_nt(a, b):
    return lax.dot_general(a.astype(MXU_DT), b.astype(MXU_DT), (((1,), (1,)), ((), ())), preferred_element_type=F32)


def _dot_tn(a, b):
    return lax.dot_general(a.astype(MXU_DT), b.astype(MXU_DT), (((0,), (0,)), ((), ())), preferred_element_type=F32)


def _split(x, n):
    parts = []
    for _ in range(n):
        p = x.astype(MXU_DT)
        parts.append(p)
        x = x - p.astype(F32)
    return parts


def _dot_sel(sel, x, n):
    return sum(jnp.dot(sel.astype(MXU_DT), p, preferred_element_type=F32) for p in _split(x, n))


def _dot_xsel(x, sel, n):
    return sum(jnp.dot(p, sel.astype(MXU_DT), preferred_element_type=F32) for p in _split(x, n))


def _modulate(h, m, shift_idx):
    return h * (1.0 + m[shift_idx + 1:shift_idx + 2]) + m[shift_idx:shift_idx + 1]


def _ln_rows(z, g, b):
    mu = jnp.mean(z, axis=-1, keepdims=True)
    zc = z - mu
    var = jnp.mean(zc * zc, axis=-1, keepdims=True)
    return zc * lax.rsqrt(var + LN_EPS) * g + b


def _silu(x):
    return x * jax.nn.sigmoid(x)


def _shift_down(x, first_row):
    rows = lax.broadcasted_iota(jnp.int32, (x.shape[0], 1), 0)
    return jnp.where(rows == 0, first_row, pltpu.roll(x, 1, 0))


def _shift_up(x, last_row):
    n = x.shape[0]
    rows = lax.broadcasted_iota(jnp.int32, (n, 1), 0)
    return jnp.where(rows == n - 1, last_row, pltpu.roll(x, n - 1, 0))


def _tile_of(g, nct, nt, reverse):
    if not reverse:
        return g
    return jnp.where(g < nct, nct - 1 - g, nt - 1 - (g - nct))


def _halo_flags(t, nct, nt):
    prev_ok = jnp.logical_and(t != 0, t != nct).astype(F32)
    next_ok = jnp.logical_and(t != nct - 1, t != nt - 1).astype(F32)
    return prev_ok, next_ok


def _ada_kernel(s_ref, w_ref, b_ref, o_ref):
    o_ref[0] = _dot(_silu(s_ref[...]), w_ref[0]) + b_ref[0]


def _ada_mods(cond, ada_w, ada_b):
    nl, d, n6 = ada_w.shape
    out = pl.pallas_call(
        _ada_kernel, grid=(nl, n6 // d),
        in_specs=[pl.BlockSpec((8, d), lambda l, j: (0, 0)),
                  pl.BlockSpec((1, d, d), lambda l, j: (l, 0, j)),
                  pl.BlockSpec((1, 1, d), lambda l, j: (l, 0, j))],
        out_specs=pl.BlockSpec((1, 8, d), lambda l, j: (l, 0, j)),
        out_shape=jax.ShapeDtypeStruct((nl, 8, n6), F32),
        compiler_params=_cparams("arbitrary", "arbitrary"), name="ada_mods",
    )(cond, ada_w, ada_b.reshape(nl, 1, n6))
    return out[:, :2].reshape(nl, 2, 6, d)


def _row_spec(width, tm=TM):
    return pl.BlockSpec((tm, width), lambda i: (i, 0))


def _full_spec(shape):
    nd = len(shape)
    return pl.BlockSpec(tuple(shape), lambda *_: (0,) * nd)


def _mod_spec(nct):
    return pl.BlockSpec((1, 6, D_MODEL), lambda i: (jnp.minimum(i // nct, 1), 0, 0))


def _lru_out_kernel(g_ref, hf_ref, hb_ref, w_ref, h_ref, mod_ref, lng_ref, lnb_ref, o_ref):
    y = _dot(g_ref[...] * (hf_ref[...] + hb_ref[...]), w_ref[...])
    z = DEEPNORM_ALPHA * h_ref[...] + mod_ref[0][2:3] * y
    o_ref[...] = _ln_rows(z, lng_ref[...], lnb_ref[...])

def _lru_in_kernel(h_ref, mod_ref, w_ref, g_ref, x_ref):
    u = _modulate(h_ref[...], mod_ref[0], 0)
    z = _dot(u, w_ref[...])
    g_ref[...] = jax.nn.gelu(z[:, :LRU_WIDTH], approximate=True).astype(g_ref.dtype)
    x_ref[...] = z[:, LRU_WIDTH:]


def _lru_scan_kernel(xf_ref, xfp_ref, xfn_ref, xb_ref, xbp_ref, xbn_ref, cw_ref, cb_ref, gw_ref, gb_ref, lam_ref,
                     hf_o, hb_o, a_s, b_s, st_s, *, nct, nt):
    g = pl.program_id(0)

    @pl.when(g == 0)
    def _():
        st_s[...] = jnp.zeros_like(st_s)

    cw = cw_ref[...]
    for d, (x_ref, xp_ref, xn_ref) in enumerate(((xf_ref, xfp_ref, xfn_ref), (xb_ref, xbp_ref, xbn_ref))):
        prev_ok, next_ok = _halo_flags(_tile_of(g, nct, nt, d == 1), nct, nt)
        x = x_ref[...]
        xm1 = _shift_down(x, xp_ref[7:8, :] * prev_ok)
        xp1 = _shift_up(x, xn_ref[0:1, :] * next_ok)
        xp2 = _shift_up(xp1, xn_ref[1:2, :] * next_ok)
        xc = cw[0:1] * xm1 + cw[1:2] * x + cw[2:3] * xp1 + cw[3:4] * xp2 + cb_ref[...]
        gates = jax.nn.sigmoid(_dot(xc, gw_ref[d]) + gb_ref[d])
        lam = lam_ref[d:d + 1, :]
        softplus = jnp.maximum(-lam, 0.0) + jnp.log(1.0 + jnp.exp(-jnp.abs(lam)))
        log_a = -LRU_C * gates[:, :LRU_WIDTH] * softplus
        a_s[d] = jnp.exp(log_a)
        b_s[d] = jnp.sqrt(1.0 - jnp.exp(2.0 * log_a)) * (gates[:, LRU_WIDTH:] * xc)

    def row(r, carry):
        hf, hb = carry
        rb = TM - 1 - r
        hf = a_s[0, pl.ds(r, 1), :] * hf + b_s[0, pl.ds(r, 1), :]
        hb = a_s[1, pl.ds(rb, 1), :] * hb + b_s[1, pl.ds(rb, 1), :]
        hf_o[pl.ds(r, 1), :] = hf
        hb_o[pl.ds(rb, 1), :] = hb
        return hf, hb

    hf, hb = lax.fori_loop(0, TM, row, (st_s[0], st_s[1]), unroll=8)
    st_s[0] = hf
    st_s[1] = hb


def _lru_mixer(h, mods, nct, w_in, conv_w, conv_b, gate_w, gate_b, lam, w_out, ln_g, ln_b):
    t, d = h.shape
    nt = t // TM
    w = LRU_WIDTH
    gelu, rnn = pl.pallas_call(
        _lru_in_kernel, grid=(nt,),
        in_specs=[_row_spec(d), _mod_spec(nct), _full_spec((d, 2 * w))],
        out_specs=[_row_spec(w), _row_spec(w)],
        out_shape=[jax.ShapeDtypeStruct((t, w), ACT_DT), jax.ShapeDtypeStruct((t, w), F32)],
        compiler_params=_cparams("arbitrary"), name="lru_in",
    )(h, mods, w_in.astype(MXU_DT))
    eye = jnp.eye(LRU_BLOCKS, dtype=F32)
    gw = jnp.einsum('dgnij,nm->dgnimj', gate_w, eye).reshape(2, 2, w, w)
    gw = jnp.concatenate([gw[:, 0], gw[:, 1]], axis=-1).astype(MXU_DT)
    gb = gate_b.reshape(2, 1, 2 * w)
    def tile_specs(reverse):
        tile = lambda g: _tile_of(g, nct, nt, reverse)
        return [pl.BlockSpec((TM, w), lambda g: (tile(g), 0)),
                pl.BlockSpec((8, w), lambda g: (jnp.maximum(tile(g) * (TM // 8) - 1, 0), 0)),
                pl.BlockSpec((8, w), lambda g: (jnp.minimum((tile(g) + 1) * (TM // 8), t // 8 - 1), 0))]

    fwd, bwd = tile_specs(False), tile_specs(True)
    hf, hb = pl.pallas_call(
        functools.partial(_lru_scan_kernel, nct=nct, nt=nt), grid=(nt,),
        in_specs=fwd + bwd + [_full_spec((4, w)), _full_spec((1, w)), _full_spec((2, w, 2 * w)),
                              _full_spec((2, 1, 2 * w)), _full_spec((2, w))],
        out_specs=[fwd[0], bwd[0]], out_shape=[jax.ShapeDtypeStruct((t, w), F32)] * 2,
        scratch_shapes=[pltpu.VMEM((2, TM, w), F32)] * 2 + [pltpu.VMEM((2, 1, w), F32)],
        compiler_params=_cparams("arbitrary"), name="lru_scan",
    )(rnn, rnn, rnn, rnn, rnn, rnn, conv_w, conv_b[None], gw, gb, lam)
    return pl.pallas_call(
        _lru_out_kernel, grid=(nt,),
        in_specs=[_row_spec(w)] * 3 + [_full_spec((w, d)), _row_spec(d), _mod_spec(nct), _full_spec((1, d)),
                                       _full_spec((1, d))],
        out_specs=_row_spec(d), out_shape=jax.ShapeDtypeStruct((t, d), F32),
        compiler_params=_cparams("arbitrary"), name="lru_out",
    )(gelu, hf, hb, w_out.astype(MXU_DT), h, mods, ln_g[None], ln_b[None])


def _seg_sum(x, e_ref, et_ref):
    s = _dot_xsel(x, e_ref[...], 2)
    return _dot_xsel(s, et_ref[...], 2)


def _rwkv_prep_kernel(h_ref, hp_ref, hn_ref, mod_ref, mu_ref, win_ref, wl1_ref, wl2_ref, w0_ref, al1_ref, al2_ref,
                      a0_ref, gl1_ref, gl2_ref, kk_ref, ka_ref, rk_ref, e_ref, et_ref,
                      r_o, v_o, kk_o, g_o, bv_o, lw0_o, lw1_o, kt0_o, kt1_o, ab0_o, ab1_o, *, nct, nt):
    i = pl.program_id(0)
    prev_ok, next_ok = _halo_flags(i, nct, nt)
    m = mod_ref[0]
    u = _modulate(h_ref[...], m, 0)
    up = _modulate(hp_ref[7:8, :], m, 0) * prev_ok
    un = _modulate(hn_ref[0:1, :], m, 0) * next_ok
    lane = lax.broadcasted_iota(jnp.int32, (1, D_MODEL), 1)
    sh = jnp.where(lane < D_MODEL // 2, _shift_down(u, up), _shift_up(u, un))
    dx = sh - u
    mu = mu_ref[...]
    xm = [u + dx * mu[c:c + 1] for c in range(6)]
    r = _dot(xm[0], win_ref[0])
    k = _dot(xm[1], win_ref[1])
    v = _dot(xm[2], win_ref[2])
    t1 = jnp.tanh(_dot(xm[3], wl1_ref[...]))
    t2 = _dot(xm[4], al1_ref[...])
    g = _dot(jax.nn.sigmoid(_dot(xm[5], gl1_ref[...])), gl2_ref[...])
    kk = k * kk_ref[...]
    kk = kk * lax.rsqrt(_seg_sum(kk * kk, e_ref, et_ref) + 1e-12)
    ktsum = None
    for z, (lw_o, kt_o, ab_o) in enumerate(((lw0_o, kt0_o, ab0_o), (lw1_o, kt1_o, ab1_o))):
        d_w = w0_ref[z:z + 1, :] + _dot(t1, wl2_ref[z])
        lw_o[...] = -RWKV_DECAY_SCALE * jax.nn.sigmoid(d_w)
        a = jax.nn.sigmoid(a0_ref[z:z + 1, :] + _dot(t2, al2_ref[z]))
        kt = k * (1.0 + (a - 1.0) * ka_ref[...])
        kt_o[...] = kt.astype(kt_o.dtype)
        ab_o[...] = (kk * a).astype(ab_o.dtype)
        ktsum = kt if ktsum is None else ktsum + kt
    r_o[...] = r.astype(r_o.dtype)
    v_o[...] = v.astype(v_o.dtype)
    kk_o[...] = kk.astype(kk_o.dtype)
    g_o[...] = g.astype(g_o.dtype)
    bv_o[...] = (_seg_sum(r * ktsum * rk_ref[...], e_ref, et_ref) * v).astype(bv_o.dtype)


def _rwkv_scan_kernel(rf_ref, vf_ref, kkf_ref, lwf_ref, ktf_ref, abf_ref, rb_ref, vb_ref, kkb_ref, lwb_ref, ktb_ref,
                      abb_ref, of_ref, ob_ref, s_ref):
    c = RWKV_CHUNK
    dirs = ((rf_ref, vf_ref, kkf_ref, lwf_ref, ktf_ref, abf_ref, of_ref),
            (rb_ref, vb_ref, kkb_ref, lwb_ref, ktb_ref, abb_ref, ob_ref))

    @pl.when(pl.program_id(0) == 0)
    def _():
        s_ref[...] = jnp.zeros_like(s_ref)

    ri = lax.broadcasted_iota(jnp.int32, (c, c), 0)
    ci = lax.broadcasted_iota(jnp.int32, (c, c), 1)
    incl = (ci <= ri, ci >= ri)
    ri2 = lax.broadcasted_iota(jnp.int32, (c, 2 * c), 0)
    ci2 = jnp.bitwise_and(lax.broadcasted_iota(jnp.int32, (c, 2 * c), 1), c - 1)
    incl2 = (ci2 <= ri2, ci2 >= ri2)
    strict2 = (ci2 < ri2, ci2 > ri2)
    lane_a = lax.broadcasted_iota(jnp.int32, (1, LANES), 1) < RWKV_HEAD
    bi = lax.broadcasted_iota(jnp.int32, (LANES, LANES), 0) < RWKV_HEAD
    bj = lax.broadcasted_iota(jnp.int32, (LANES, LANES), 1) < RWKV_HEAD
    blockdiag = bi == bj

    def stack2(x):
        return jnp.concatenate([jnp.where(lane_a, x, 0.0), jnp.where(lane_a, 0.0, x)], axis=0)

    nch = rf_ref.shape[0] // c
    sls = [slice(p * LANES, (p + 1) * LANES) for p in range(D_MODEL // LANES)]
    cells = [(d, p) for d in range(2) for p in range(len(sls))]

    def one_chunk(ci, carry):
        pre = []
        for d, (r_ref, v_ref, kk_ref, lw_ref, kt_ref, ab_ref, o_ref) in enumerate(dirs):
            reverse = d == 1
            rows = pl.ds(pl.multiple_of(((nch - 1 - ci) if reverse else ci) * c, c), c)
            lw = lw_ref[rows, :]
            cl = _dot_sel(jnp.where(incl[d], 1.0, 0.0), lw, 3)
            tot = cl[0:1, :] if reverse else cl[c - 1:c, :]
            e_out = jnp.exp(-cl)
            e_end = jnp.exp(tot - cl)
            kt = kt_ref[rows, :].astype(F32)
            ab = ab_ref[rows, :].astype(F32)
            pre.append(dict(rows=rows, kap=kk_ref[rows, :].astype(F32) * jnp.exp(cl - lw),
                            rh=r_ref[rows, :].astype(F32) * jnp.exp(cl), kh=kt * e_out, bh=ab * e_out,
                            kb=kt * e_end, bb=ab * e_end, e_tot=jnp.exp(tot), vv=v_ref[rows, :].astype(F32)))
        s = {(d, p): s_ref[d, p] for d, p in cells}
        xq = {(d, p): jnp.concatenate([pre[d]['kap'][:, sls[p]], pre[d]['rh'][:, sls[p]]], axis=0) for d, p in cells}
        yk = {(d, p): jnp.concatenate([stack2(pre[d]['kh'][:, sls[p]]), stack2(pre[d]['bh'][:, sls[p]])], axis=0)
              for d, p in cells}
        gm = {k: _dot_nt(xq[k], yk[k]) for k in cells}
        xs = {k: _dot_nt(xq[k], s[k]) for k in cells}
        l_kk = {k: jnp.where(strict2[k[0]], gm[k][:c, :2 * c], 0.0) for k in cells}
        l_bk = {k: jnp.where(strict2[k[0]], gm[k][:c, 2 * c:], 0.0) for k in cells}
        a_rk = {k: jnp.where(incl2[k[0]], gm[k][c:, :2 * c], 0.0) for k in cells}
        a_rb = {k: jnp.where(incl2[k[0]], gm[k][c:, 2 * c:], 0.0) for k in cells}
        v2 = {(d, p): stack2(pre[d]['vv'][:, sls[p]]) for d, p in cells}
        x = {k: xs[k][:c] + _dot(l_kk[k], v2[k]) for k in cells}
        lp = {k: _dot(l_bk[k], stack2(l_bk[k])) for k in cells}
        x = {k: x[k] - _dot(l_bk[k], stack2(x[k])) for k in cells}
        for it in range(5):
            x = {k: x[k] + _dot(lp[k], stack2(x[k])) for k in cells}
            if it < 4:
                lp = {k: _dot(lp[k], stack2(lp[k])) for k in cells}
        o = {k: xs[k][c:] + _dot(jnp.concatenate([a_rk[k], -a_rb[k]], axis=1),
                                 jnp.concatenate([v2[k], stack2(x[k])], axis=0)) for k in cells}
        upd = {(d, p): _dot_tn(jnp.concatenate([pre[d]['vv'][:, sls[p]], -x[d, p]], axis=0),
                               jnp.concatenate([pre[d]['kb'][:, sls[p]], pre[d]['bb'][:, sls[p]]], axis=0))
               for d, p in cells}
        for d, p in cells:
            dirs[d][6][pre[d]['rows'], sls[p]] = o[d, p]
            s_ref[d, p] = s[d, p] * pre[d]['e_tot'][:, sls[p]] + jnp.where(blockdiag, upd[d, p], 0.0)
        return carry

    lax.fori_loop(0, nch, one_chunk, 0)


def _rwkv_out_kernel(of_ref, ob_ref, bv_ref, g_ref, lg_ref, lb_ref, e_ref, et_ref, w_ref, h_ref, mod_ref, lng_ref,
                     lnb_ref, o_ref):
    o = of_ref[...] + ob_ref[...]
    inv = 1.0 / RWKV_HEAD
    oc = o - _seg_sum(o, e_ref, et_ref) * inv
    var = _seg_sum(oc * oc, e_ref, et_ref) * inv
    y = oc * lax.rsqrt(var + RWKV_GN_EPS) * lg_ref[...] + lb_ref[...] + bv_ref[...]
    yo = _dot(y * g_ref[...], w_ref[...])
    z = DEEPNORM_ALPHA * h_ref[...] + mod_ref[0][2:3] * yo
    o_ref[...] = _ln_rows(z, lng_ref[...], lnb_ref[...])


def _rwkv_mixer(h, mods, nct, mu, w_in, w0, w_l1, w_l2, a0, a_l1, a_l2, g_l1, g_l2, k_k, k_a, r_k, gn_g, gn_b, w_out,
                ln_g, ln_b):
    t, d = h.shape
    nt = t // TM
    bf = MXU_DT
    lw_ = w_l1.shape[-1]
    la_ = a_l1.shape[-1]
    zw = jnp.zeros((lw_, d), F32)
    za = jnp.zeros((la_, d), F32)
    wl1 = jnp.concatenate([w_l1[0], w_l1[1]], axis=1).astype(bf)
    wl2 = jnp.stack([jnp.concatenate([w_l2[0], zw], 0), jnp.concatenate([zw, w_l2[1]], 0)]).astype(bf)
    al1 = jnp.concatenate([a_l1[0], a_l1[1]], axis=1).astype(bf)
    al2 = jnp.stack([jnp.concatenate([a_l2[0], za], 0), jnp.concatenate([za, a_l2[1]], 0)]).astype(bf)
    head_of = jnp.arange(d) // RWKV_HEAD
    e = (head_of[:, None] == jnp.arange(LANES)[None, :]).astype(bf)
    et = e.T
    halo_p = pl.BlockSpec((8, d), lambda i: (jnp.maximum(i * (TM // 8) - 1, 0), 0))
    halo_n = pl.BlockSpec((8, d), lambda i: (jnp.minimum((i + 1) * (TM // 8), t // 8 - 1), 0))
    args = [h, h, h, mods, mu, w_in.astype(bf), wl1, wl2, w0, al1, al2, a0, g_l1.astype(bf), g_l2.astype(bf),
            k_k[None], k_a[None], r_k.reshape(1, d), e, et]
    ins = [_row_spec(d), halo_p, halo_n, _mod_spec(nct)] + [_full_spec(a.shape) for a in args[4:]]
    outs = pl.pallas_call(
        functools.partial(_rwkv_prep_kernel, nct=nct, nt=nt), grid=(nt,), in_specs=ins,
        out_specs=[_row_spec(d)] * 11,
        out_shape=[jax.ShapeDtypeStruct((t, d), dt) for dt in [ACT_DT] * 5 + [F32] * 2 + [ACT_DT] * 4],
        compiler_params=_cparams("arbitrary"), name="rwkv_prep",
    )(*args)
    r, v, kk, g, bv, lw0, lw1, kt0, kt1, ab0, ab1 = outs
    fwd = pl.BlockSpec((TM, d), lambda g_: (g_, 0))
    bwd = pl.BlockSpec((TM, d), lambda g_: (_tile_of(g_, nct, nt, True), 0))
    o_dir = pl.pallas_call(
        _rwkv_scan_kernel, grid=(nt,), in_specs=[fwd] * 6 + [bwd] * 6, out_specs=[fwd, bwd],
        out_shape=[jax.ShapeDtypeStruct((t, d), F32)] * 2,
        scratch_shapes=[pltpu.VMEM((2, d // LANES, LANES, LANES), F32)],
        compiler_params=_cparams("arbitrary"), name="rwkv_scan",
    )(r, v, kk, lw0, kt0, ab0, r, v, kk, lw1, kt1, ab1)
    args = [o_dir[0], o_dir[1], bv, g, gn_g[None], gn_b[None], e, et, w_out.astype(bf), h, mods, ln_g[None], ln_b[None]]
    ins = [_row_spec(d)] * 4 + [_full_spec(a.shape) for a in args[4:9]] + [_row_spec(d), _mod_spec(nct),
                                                                          _full_spec((1, d)), _full_spec((1, d))]
    return pl.pallas_call(
        _rwkv_out_kernel, grid=(nt,), in_specs=ins, out_specs=_row_spec(d),
        out_shape=jax.ShapeDtypeStruct((t, d), F32), compiler_params=_cparams("arbitrary"), name="rwkv_out",
    )(*args)


def _ret_in_kernel(h_ref, mod_ref, w_ref, cos_ref, sin_ref, q_o, k_o, v_o, g_o):
    d = D_MODEL
    u = _modulate(h_ref[...], mod_ref[0], 0).astype(MXU_DT)
    q = _dot(u, w_ref[:, 0:d])
    k = _dot(u, w_ref[:, d:2 * d]) * (RET_QK ** -0.5)
    v_o[...] = _dot(u, w_ref[:, 2 * d:4 * d]).astype(v_o.dtype)
    g_o[...] = _silu(_dot(u, w_ref[:, 4 * d:6 * d])).astype(g_o.dtype)
    cos = cos_ref[...]
    sin = sin_ref[...]
    half = RET_QK // 2
    for z, z_o in ((q, q_o), (k, k_o)):
        for hh in range(RET_HEADS):
            lo = z[:, hh * RET_QK:hh * RET_QK + half]
            hi = z[:, hh * RET_QK + half:(hh + 1) * RET_QK]
            zh = jnp.concatenate([lo, hi], axis=1)
            rot = jnp.concatenate([-hi, lo], axis=1)
            z_o[:, hh * RET_QK:(hh + 1) * RET_QK] = (zh * cos + rot * sin).astype(z_o.dtype)


def _ret_scan_kernel(qf_ref, kf_ref, vf_ref, qb_ref, kb_ref, vb_ref, inner_ref, qd_ref, kd_ref, bd_ref, of_ref, ob_ref,
                     r_ref):
    @pl.when(pl.program_id(0) == 0)
    def _():
        r_ref[...] = jnp.zeros_like(r_ref)

    refs = ((qf_ref, kf_ref, vf_ref, of_ref), (qb_ref, kb_ref, vb_ref, ob_ref))
    cells = [(d, hh) for d in range(2) for hh in range(RET_HEADS)]
    qs = lambda hh: slice(hh * RET_QK, (hh + 1) * RET_QK)
    vs = lambda hh: slice(hh * RET_V, (hh + 1) * RET_V)
    q = {(d, hh): refs[d][0][:, qs(hh)] for d, hh in cells}
    k = {(d, hh): refs[d][1][:, qs(hh)] for d, hh in cells}
    v = {(d, hh): refs[d][2][:, vs(hh)] for d, hh in cells}
    state = {c: r_ref[c[0], c[1]] for c in cells}
    scores = {c: _dot_nt(q[c], k[c]) * inner_ref[c[0], c[1]] for c in cells}
    carry_in = {c: _dot(q[c], state[c]) * qd_ref[c[0], c[1]] for c in cells}
    upd = {c: _dot_tn(k[c] * kd_ref[c[0], c[1]], v[c]) for c in cells}
    for c in cells:
        refs[c[0]][3][:, vs(c[1])] = _dot(scores[c], v[c]) + carry_in[c]
        r_ref[c[0], c[1]] = state[c] * bd_ref[c[0], c[1]] + upd[c]


def _ret_out_kernel(of_ref, ob_ref, g_ref, gg_ref, gb_ref, w_ref, h_ref, mod_ref, lng_ref, lnb_ref, o_ref):
    parts = []
    for hh in range(RET_HEADS):
        sl = slice(hh * RET_V, (hh + 1) * RET_V)
        o = of_ref[:, sl] + ob_ref[:, sl]
        mu = jnp.mean(o, axis=-1, keepdims=True)
        oc = o - mu
        var = jnp.mean(oc * oc, axis=-1, keepdims=True)
        y = oc * lax.rsqrt(var + LN_EPS) * gg_ref[:, sl] + gb_ref[:, sl]
        parts.append((g_ref[:, sl] * y).astype(MXU_DT))
    yo = _dot(jnp.concatenate(parts, axis=1), w_ref[...])
    z = DEEPNORM_ALPHA * h_ref[...] + mod_ref[0][2:3] * yo
    o_ref[...] = _ln_rows(z, lng_ref[...], lnb_ref[...])


def _ret_mixer(h, mods, nct, rope_cos, rope_sin, w_in, decay_logit, gn_g, gn_b, w_out, ln_g, ln_b):
    t, d = h.shape
    nt = t // TM
    hv = RET_HEADS * RET_V
    q, k, v, sg = pl.pallas_call(
        _ret_in_kernel, grid=(nt,),
        in_specs=[_row_spec(d), _mod_spec(nct), _full_spec(w_in.shape), _row_spec(RET_QK), _row_spec(RET_QK)],
        out_specs=[_row_spec(d), _row_spec(d), _row_spec(hv), _row_spec(hv)],
        out_shape=[jax.ShapeDtypeStruct((t, w), ACT_DT) for w in (d, d, hv, hv)],
        compiler_params=_cparams("arbitrary"), name="ret_in",
    )(h, mods, w_in.astype(MXU_DT), rope_cos, rope_sin)
    c = RET_CHUNK
    ncc, nc = nct * (TM // c), t // c
    log_gamma = jax.nn.log_sigmoid(decay_logit.astype(F32))
    pos = jnp.arange(c, dtype=F32)
    tabs = []
    for d_ in range(2):
        lg = log_gamma[d_][:, None, None]
        p = (c - 1.0 - pos) if d_ == 1 else pos
        rel = p[:, None] - p[None, :]
        tabs.append((jnp.where(rel >= 0, jnp.exp(jnp.maximum(rel, 0.0) * lg), 0.0),
                     jnp.exp((p + 1.0) * log_gamma[d_][:, None])[:, :, None],
                     jnp.exp((c - 1.0 - p) * log_gamma[d_][:, None])[:, :, None],
                     jnp.exp(c * log_gamma[d_])[:, None, None]))
    inner, q_dec, k_dec, blk_dec = (jnp.stack(z) for z in zip(*tabs))
    cs = lambda w, reverse: pl.BlockSpec((c, w), lambda g_: (_tile_of(g_, ncc, nc, reverse), 0))
    o_dir = pl.pallas_call(
        _ret_scan_kernel, grid=(nc,),
        in_specs=[cs(d, False), cs(d, False), cs(hv, False), cs(d, True), cs(d, True), cs(hv, True),
                  _full_spec(inner.shape), _full_spec(q_dec.shape), _full_spec(k_dec.shape), _full_spec(blk_dec.shape)],
        out_specs=[cs(hv, False), cs(hv, True)], out_shape=[jax.ShapeDtypeStruct((t, hv), F32)] * 2,
        scratch_shapes=[pltpu.VMEM((2, RET_HEADS, RET_QK, RET_V), F32)],
        compiler_params=_cparams("arbitrary"), name="ret_scan",
    )(q, k, v, q, k, v, inner, q_dec, k_dec, blk_dec)
    return pl.pallas_call(
        _ret_out_kernel, grid=(nt,),
        in_specs=[_row_spec(hv)] * 3 + [_full_spec((1, hv)), _full_spec((1, hv)), _full_spec((hv, d)), _row_spec(d),
                                        _mod_spec(nct), _full_spec((1, d)), _full_spec((1, d))],
        out_specs=_row_spec(d), out_shape=jax.ShapeDtypeStruct((t, d), F32),
        compiler_params=_cparams("arbitrary"), name="ret_out",
    )(o_dir[0], o_dir[1], sg, gn_g[None], gn_b[None], w_out.astype(MXU_DT), h, mods, ln_g[None], ln_b[None])


def _hgrn_in_kernel(h_ref, mod_ref, w_ref, lb_ref, bf_ref, q_o, v_o, g_o, f0_o, f1_o):
    d = D_MODEL
    u = _modulate(h_ref[...], mod_ref[0], 0).astype(MXU_DT)
    lb = lb_ref[...]
    q_o[...] = _silu(_dot(u, w_ref[:, 0:d])).astype(q_o.dtype)
    f0_o[...] = lb + (1.0 - lb) * jax.nn.sigmoid(_dot(u, w_ref[:, d:2 * d]) + bf_ref[0:1, :])
    f1_o[...] = lb + (1.0 - lb) * jax.nn.sigmoid(_dot(u, w_ref[:, 2 * d:3 * d]) + bf_ref[1:2, :])
    v_o[...] = _dot(u, w_ref[:, 3 * d:4 * d]).astype(v_o.dtype)
    g_o[...] = _silu(_dot(u, w_ref[:, 4 * d:5 * d])).astype(g_o.dtype)


def _hgrn_scan_kernel(qf_ref, vf_ref, ff_ref, qb_ref, vb_ref, fb_ref, of_ref, ob_ref, s_ref, b_s, rb_s, rk_s, rv_s):
    hb = HGRN_BLOCK
    nb = TM // hb
    half = hb // 2
    dirs = ((qf_ref, vf_ref, ff_ref, of_ref, False), (qb_ref, vb_ref, fb_ref, ob_ref, True))

    @pl.when(pl.program_id(0) == 0)
    def _():
        s_ref[...] = jnp.zeros_like(s_ref)

    span = 4 * hb
    ri = lax.broadcasted_iota(jnp.int32, (span, span), 0)
    ci = lax.broadcasted_iota(jnp.int32, (span, span), 1)
    same_block = (ri // hb) == (ci // hb)
    ti = lax.broadcasted_iota(jnp.int32, (half, 1), 0)
    rowi = lax.broadcasted_iota(jnp.int32, (hb, 1), 0)
    heads = range(HGRN_HEADS)
    sls = [slice(hh * HGRN_HEAD, (hh + 1) * HGRN_HEAD) for hh in heads]
    cells = [(d, hh) for d in range(2) for hh in heads]

    for d, (q_ref, v_ref, f_ref, o_ref, reverse) in enumerate(dirs):
        tri = jnp.where(jnp.logical_and(same_block, (ci >= ri) if reverse else (ci <= ri)), 1.0, 0.0)
        for r0 in range(0, TM, span):
            b_s[d, r0:r0 + span] = _dot_sel(tri, jnp.log(f_ref[r0:r0 + span, :]), 3)

    def block(bi, par):
        pre = []
        for d, (q_ref, v_ref, f_ref, o_ref, reverse) in enumerate(dirs):
            blk = (nb - 1 - bi) if reverse else bi
            r0 = pl.multiple_of(blk * hb, hb)
            kx = 1.0 - f_ref[pl.ds(r0, hb), :]
            q = q_ref[pl.ds(r0, hb), :].astype(F32)
            v = v_ref[pl.ds(r0, hb), :].astype(F32)
            b = b_s[d, pl.ds(r0, hb), :]
            rb_s[d, par] = b
            rk_s[d, par] = kx
            rv_s[d, par] = v
            tot = b[0:1, :] if reverse else b[hb - 1:hb, :]
            first = (rowi >= half) if reverse else (rowi < half)
            beta = b[half:half + 1, :] if reverse else b[half - 1:half, :]
            pre.append(dict(
                r0=r0, q=q, v=v, b=b, qe=q * jnp.exp(b), kb=kx * jnp.exp(tot - b), e_tot=jnp.exp(tot),
                k_first=kx * jnp.exp(jnp.where(first, beta - b, -jnp.inf)),
                q_second=q * jnp.exp(jnp.where(first, -jnp.inf, b - beta)),
                causal=[(ti <= si) if reverse else (ti >= si) for si in range(half)]))
        for g0 in range(0, len(cells), HGRN_GROUP):
            grp = cells[g0:g0 + HGRN_GROUP]
            s = {c: s_ref[c[0], c[1]] for c in grp}
            m_first = {(d, hh): _dot_tn(pre[d]['k_first'][:, sls[hh]], pre[d]['v'][:, sls[hh]]) for d, hh in grp}
            o = {(d, hh): _dot_nt(pre[d]['qe'][:, sls[hh]], s[d, hh]) + _dot(pre[d]['q_second'][:, sls[hh]], m_first[d, hh])
                 for d, hh in grp}
            upd = {(d, hh): _dot_tn(pre[d]['v'][:, sls[hh]], pre[d]['kb'][:, sls[hh]]) for d, hh in grp}
            for d, hh in grp:
                sl = sls[hh]
                p = pre[d]
                parts = []
                for lo in (0, half):
                    bt = p['b'][lo:lo + half, sl]
                    qt = p['q'][lo:lo + half, sl]
                    acc = jnp.zeros((half, HGRN_HEAD), F32)
                    for si in range(half):
                        row = slice(lo + si, lo + si + 1)
                        dec = jnp.exp(jnp.where(p['causal'][si], bt - rb_s[d, par, row, sl], -jnp.inf))
                        sc = jnp.sum(qt * rk_s[d, par, row, sl] * dec, axis=-1, keepdims=True)
                        acc = acc + sc * rv_s[d, par, row, sl]
                    parts.append(acc)
                dirs[d][3][pl.ds(p['r0'], hb), sl] = o[d, hh] + jnp.concatenate(parts, axis=0)
                s_ref[d, hh] = s[d, hh] * p['e_tot'][:, sl] + upd[d, hh]

    def trip(bj, carry):
        for par in range(HGRN_UNROLL):
            block(HGRN_UNROLL * bj + par, par)
        return carry

    lax.fori_loop(0, nb // HGRN_UNROLL, trip, 0)


def _hgrn_out_kernel(of_ref, ob_ref, g_ref, ng_ref, w_ref, h_ref, mod_ref, lng_ref, lnb_ref, o_ref):
    parts = []
    for hh in range(HGRN_HEADS):
        sl = slice(hh * HGRN_HEAD, (hh + 1) * HGRN_HEAD)
        o = of_ref[:, sl] + ob_ref[:, sl]
        y = o * lax.rsqrt(jnp.mean(o * o, axis=-1, keepdims=True) + LN_EPS) * ng_ref[...]
        parts.append((y * g_ref[:, sl]).astype(MXU_DT))
    yo = _dot(jnp.concatenate(parts, axis=1), w_ref[...])
    z = DEEPNORM_ALPHA * h_ref[...] + mod_ref[0][2:3] * yo
    o_ref[...] = _ln_rows(z, lng_ref[...], lnb_ref[...])


def _hgrn_mixer(h, mods, nct, lb, w_in, b_f, norm_g, w_out, ln_g, ln_b):
    t, d = h.shape
    nt = t // TM
    q, v, sg, f0, f1 = pl.pallas_call(
        _hgrn_in_kernel, grid=(nt,),
        in_specs=[_row_spec(d), _mod_spec(nct), _full_spec(w_in.shape), _full_spec((1, d)), _full_spec((2, d))],
        out_specs=[_row_spec(d)] * 5,
        out_shape=[jax.ShapeDtypeStruct((t, d), dt) for dt in [ACT_DT] * 3 + [F32] * 2],
        compiler_params=_cparams("arbitrary"), name="hgrn_in",
    )(h, mods, w_in.astype(MXU_DT), lb[None], b_f)
    fwd = pl.BlockSpec((TM, d), lambda g_: (g_, 0))
    bwd = pl.BlockSpec((TM, d), lambda g_: (_tile_of(g_, nct, nt, True), 0))
    o_dir = pl.pallas_call(
        _hgrn_scan_kernel, grid=(nt,), in_specs=[fwd] * 3 + [bwd] * 3, out_specs=[fwd, bwd],
        out_shape=[jax.ShapeDtypeStruct((t, d), F32)] * 2,
        scratch_shapes=[pltpu.VMEM((2, HGRN_HEADS, HGRN_HEAD, HGRN_HEAD), F32)] + [pltpu.VMEM((2, TM, d), F32)]
        + [pltpu.VMEM((2, HGRN_UNROLL, HGRN_BLOCK, d), F32)] * 3,
        compiler_params=_cparams("arbitrary"), name="hgrn_scan",
    )(q, v, f0, q, v, f1)
    return pl.pallas_call(
        _hgrn_out_kernel, grid=(nt,),
        in_specs=[_row_spec(d)] * 3 + [_full_spec((1, HGRN_HEAD)), _full_spec((d, d)), _row_spec(d), _mod_spec(nct),
                                       _full_spec((1, d)), _full_spec((1, d))],
        out_specs=_row_spec(d), out_shape=jax.ShapeDtypeStruct((t, d), F32),
        compiler_params=_cparams("arbitrary"), name="hgrn_out",
    )(o_dir[0], o_dir[1], sg, norm_g[None], w_out.astype(MXU_DT), h, mods, ln_g[None], ln_b[None])


def _router_kernel(h_ref, mod_ref, rw_ref, rb_ref, u_o, gate_o, rank_o, x_o):
    u = _modulate(h_ref[...], mod_ref[0], 3)
    u_o[...] = u.astype(u_o.dtype)
    w_hi, w_lo = _split(rw_ref[...], 2)
    u_hi, u_lo = _split(u, 2)
    nt_dims = (((1,), (1,)), ((), ()))
    logits = (lax.dot_general(w_hi, u_hi, nt_dims, preferred_element_type=F32)
              + lax.dot_general(w_hi, u_lo, nt_dims, preferred_element_type=F32)
              + lax.dot_general(w_lo, u_hi, nt_dims, preferred_element_type=F32))
    ne, gs = N_EXPERTS, N_EXPERTS // N_GROUPS
    neg = -jnp.inf
    scores = jax.nn.sigmoid(logits[:ne])
    choice = scores + rb_ref[:ne]
    c3 = choice.reshape(N_GROUPS, gs, TM)
    mi = lax.broadcasted_iota(jnp.int32, c3.shape, 1).astype(F32)
    m1 = jnp.max(c3, axis=1, keepdims=True)
    i1 = jnp.min(jnp.where(c3 == m1, mi, float(gs)), axis=1, keepdims=True)
    m2 = jnp.max(jnp.where(mi == i1, neg, c3), axis=1, keepdims=True)
    gscore = m1 + m2
    gi = lax.broadcasted_iota(jnp.int32, gscore.shape, 0).astype(F32)
    gsel = jnp.zeros(gscore.shape, F32)
    for _ in range(TOPK_GROUPS):
        gm = jnp.max(gscore, axis=0, keepdims=True)
        pick = gi == jnp.min(jnp.where(gscore == gm, gi, float(N_GROUPS)), axis=0, keepdims=True)
        gsel = jnp.where(pick, 1.0, gsel)
        gscore = jnp.where(pick, neg, gscore)
    emask = jnp.broadcast_to(gsel, c3.shape).reshape(ne, TM)
    masked = jnp.where(emask > 0.5, choice, neg)
    ei = lax.broadcasted_iota(jnp.int32, masked.shape, 0).astype(F32)
    chosen = jnp.zeros(masked.shape, F32)
    for _ in range(TOP_K):
        em = jnp.max(masked, axis=0, keepdims=True)
        pick = ei == jnp.min(jnp.where(masked == em, ei, float(ne)), axis=0, keepdims=True)
        chosen = jnp.where(pick, 1.0, chosen)
        masked = jnp.where(pick, neg, masked)
    top_w = scores * chosen
    gates = ROUTED_SCALE * top_w / jnp.sum(top_w, axis=0, keepdims=True)
    ti = lax.broadcasted_iota(jnp.int32, (TM, TM), 0)
    tj = lax.broadcasted_iota(jnp.int32, (TM, TM), 1)
    before = jnp.where(ti < tj, 1.0, 0.0).astype(MXU_DT)
    prefix = jnp.dot(chosen.astype(MXU_DT), before, preferred_element_type=F32)
    rank = jnp.where(chosen > 0.5, prefix, -1.0)
    gate_o[0] = gates
    rank_o[0] = rank
    cap = MOE_CAP
    slot = lax.broadcasted_iota(jnp.int32, (cap, TM), 0).astype(F32)
    ub = u.astype(MXU_DT)
    for g0 in range(0, ne, MOE_EGROUP):
        onehot = jnp.concatenate([jnp.where(slot == rank[e:e + 1, :], 1.0, 0.0).astype(MXU_DT)
                                  for e in range(g0, g0 + MOE_EGROUP)], axis=0)
        xg = jnp.dot(onehot, ub, preferred_element_type=F32)
        x_o[0, g0:g0 + MOE_EGROUP] = xg.reshape(MOE_EGROUP, cap, D_MODEL).astype(x_o.dtype)


def _expert_kernel(x_ref, wgu_ref, wd_ref, y_ref, wgu_b, wd_b):
    @pl.when(pl.program_id(1) == 0)
    def _():
        wgu_b[0] = wgu_ref[0, 0].astype(wgu_b.dtype)
        wd_b[0] = wd_ref[0, 0].astype(wd_b.dtype)

    g = x_ref.shape[0]
    ch = max(c for c in range(1, MOE_CHUNK + 1) if g % c == 0)
    ed = EXPERT_DIM

    def chunk(ci, carry):
        t0 = ci * ch
        x = x_ref[pl.ds(t0, ch)].reshape(ch * MOE_CAP, D_MODEL)
        gu = _dot(x, wgu_b[0])
        y = _dot(_silu(gu[:, :ed]) * gu[:, ed:], wd_b[0])
        y_ref[pl.ds(t0, ch)] = y.reshape(ch, 1, MOE_CAP, D_MODEL).astype(y_ref.dtype)
        return carry

    lax.fori_loop(0, g // ch, chunk, 0)


def _combine_kernel(u_ref, gt_ref, rt_ref, y_ref, sgu_ref, sd_ref, h_ref, mod_ref, lng_ref, lnb_ref, *rest, extra):
    if extra:
        ex_ref, o_ref = rest
    else:
        (o_ref,) = rest
    ed = EXPERT_DIM
    cap = MOE_CAP
    gu = _dot(u_ref[...], sgu_ref[...])
    acc = _dot(_silu(gu[:, :ed]) * gu[:, ed:], sd_ref[...])
    if extra:
        acc = acc + ex_ref[...]
    slot = lax.broadcasted_iota(jnp.int32, (cap, TM), 0).astype(F32)
    for g0 in range(0, N_EXPERTS, MOE_EGROUP):
        pw = jnp.concatenate([jnp.where(slot == rt_ref[0, e:e + 1, :], gt_ref[0, e:e + 1, :], 0.0).astype(MXU_DT)
                              for e in range(g0, g0 + MOE_EGROUP)], axis=0)
        yg = y_ref[0, g0:g0 + MOE_EGROUP].reshape(MOE_EGROUP * cap, D_MODEL)
        acc = acc + _dot_tn(pw, yg)
    z = DEEPNORM_ALPHA * h_ref[...] + mod_ref[0][5:6] * acc
    o_ref[...] = _ln_rows(z, lng_ref[...], lnb_ref[...])


def _overflow_kernel(tile_ref, exp_ref, nr_ref, n_ref, u_ref, gate_ref, rank_ref, wgu_ref, wd_ref, zero_ref, o_ref):
    del zero_ref
    s = pl.program_id(0)
    tile = tile_ref[s]
    e = exp_ref[s]
    ed = EXPERT_DIM
    cap = MOE_CAP
    active = s < n_ref[0]
    first = jnp.logical_or(s == 0, tile_ref[jnp.maximum(s - 1, 0)] != tile)

    @pl.when(jnp.logical_and(active, first))
    def _():
        o_ref[...] = jnp.zeros_like(o_ref)

    @pl.when(active)
    def _():
        wgu = wgu_ref[0]
        wd = wd_ref[0]
        rank = rank_ref[0, pl.ds(e, 1), :]
        gate = gate_ref[0, pl.ds(e, 1), :]

        def one_round(r, carry):
            slot = lax.broadcasted_iota(jnp.int32, (cap, TM), 0).astype(F32) + (r * cap).astype(F32)
            hit = slot == rank
            x = _dot(jnp.where(hit, 1.0, 0.0), u_ref[...])
            gu = _dot(x, wgu)
            y = _dot(_silu(gu[:, :ed]) * gu[:, ed:], wd)
            o_ref[...] += _dot_tn(jnp.where(hit, gate, 0.0), y)
            return carry

        lax.fori_loop(1, nr_ref[s], one_round, 0)


def _moe_layer(h, mods, nct, layer, router_w, router_b, w_gu, w_down, sh_gu, sh_down, ln_g, ln_b, latent_only=False):
    t, d = h.shape
    nt = t // TM
    ne, cap = N_EXPERTS, MOE_CAP
    rw = jnp.concatenate([router_w.T, jnp.zeros((LANES - ne, d), F32)], axis=0)
    rb = jnp.concatenate([router_b, jnp.zeros((LANES - ne,), F32)])[:, None]
    per_tile = pl.BlockSpec((1, ne, TM), lambda i: (i, 0, 0))
    slots = pl.BlockSpec((1, ne, cap, d), lambda i: (i, 0, 0, 0))
    u, gates, ranks, xs = pl.pallas_call(
        _router_kernel, grid=(nt,),
        in_specs=[_row_spec(d), _mod_spec(nct), _full_spec((LANES, d)), _full_spec((LANES, 1))],
        out_specs=[_row_spec(d), per_tile, per_tile, slots],
        out_shape=[jax.ShapeDtypeStruct((t, d), MXU_DT), jax.ShapeDtypeStruct((nt, ne, TM), F32),
                   jax.ShapeDtypeStruct((nt, ne, TM), F32), jax.ShapeDtypeStruct((nt, ne, cap, d), MXU_DT)],
        compiler_params=_cparams("arbitrary"), name="moe_router",
    )(h, mods, rw, rb)

    run = max(g for g in range(1, MOE_RUN + 1) if nt % g == 0)
    ys, wgu_b, wd_b = pl.pallas_call(
        _expert_kernel, grid=(ne, nt // run),
        in_specs=[pl.BlockSpec((run, 1, cap, d), lambda e, c: (c, e, 0, 0)),
                  pl.BlockSpec((1, 1, d, 2 * EXPERT_DIM), lambda e, c: (layer, e, 0, 0)),
                  pl.BlockSpec((1, 1, EXPERT_DIM, d), lambda e, c: (layer, e, 0, 0))],
        out_specs=[pl.BlockSpec((run, 1, cap, d), lambda e, c: (c, e, 0, 0)),
                   pl.BlockSpec((1, d, 2 * EXPERT_DIM), lambda e, c: (e, 0, 0)),
                   pl.BlockSpec((1, EXPERT_DIM, d), lambda e, c: (e, 0, 0))],
        out_shape=[jax.ShapeDtypeStruct((nt, ne, cap, d), MXU_DT),
                   jax.ShapeDtypeStruct((ne, d, 2 * EXPERT_DIM), MXU_DT), jax.ShapeDtypeStruct((ne, EXPERT_DIM, d), MXU_DT)],
        compiler_params=_cparams("arbitrary", "arbitrary"), name="moe_experts",
    )(xs, w_gu, w_down)

    sgu, sd = sh_gu.astype(MXU_DT), sh_down.astype(MXU_DT)
    base_specs = [_row_spec(d), per_tile, per_tile, slots, _full_spec(sgu.shape), _full_spec(sd.shape),
                  _row_spec(d), _mod_spec(nct), _full_spec((1, d)), _full_spec((1, d))]
    base_args = (u, gates, ranks, ys, sgu, sd, h, mods, ln_g[None], ln_b[None])

    skip = nct if latent_only else 0
    out_spec = pl.BlockSpec((TM, d), lambda i: (jnp.maximum(i - skip, 0), 0))

    def combine(*extra):
        return pl.pallas_call(
            functools.partial(_combine_kernel, extra=bool(extra)), grid=(nt,),
            in_specs=base_specs + [_row_spec(d)] * len(extra), out_specs=out_spec,
            out_shape=jax.ShapeDtypeStruct((t - skip * TM, d), F32), compiler_params=_cparams("arbitrary"),
            name="moe_combine",
        )(*base_args, *extra)

    count = (jnp.max(ranks, axis=-1).astype(jnp.int32) + 1).reshape(-1)
    over = count > cap
    n_over = jnp.sum(over.astype(jnp.int32))

    def with_overflow(size):
        def run():
            idx = jnp.nonzero(over, size=size, fill_value=0)[0].astype(jnp.int32)
            idx = jnp.where(jnp.arange(size) < n_over, idx, idx[jnp.maximum(n_over - 1, 0)])
            tiles, exps = idx // ne, idx % ne
            rounds = (count[idx] + cap - 1) // cap
            grid_spec = pltpu.PrefetchScalarGridSpec(
                num_scalar_prefetch=4, grid=(size,),
                in_specs=[pl.BlockSpec((TM, d), lambda s, tl, ex, nr, n: (tl[s], 0)),
                          pl.BlockSpec((1, ne, TM), lambda s, tl, ex, nr, n: (tl[s], 0, 0)),
                          pl.BlockSpec((1, ne, TM), lambda s, tl, ex, nr, n: (tl[s], 0, 0)),
                          pl.BlockSpec((1, d, 2 * EXPERT_DIM), lambda s, tl, ex, nr, n: (ex[s], 0, 0)),
                          pl.BlockSpec((1, EXPERT_DIM, d), lambda s, tl, ex, nr, n: (ex[s], 0, 0)),
                          pl.BlockSpec(memory_space=pl.ANY)],
                out_specs=pl.BlockSpec((TM, d), lambda s, tl, ex, nr, n: (tl[s], 0)))
            extra = pl.pallas_call(
                _overflow_kernel, grid_spec=grid_spec, out_shape=jax.ShapeDtypeStruct((t, d), F32),
                input_output_aliases={9: 0}, compiler_params=_cparams("arbitrary"), name="moe_overflow",
            )(tiles, exps, rounds, n_over[None], u, gates, ranks, wgu_b, wd_b, jnp.zeros((t, d), F32))
            return combine(extra)
        return run

    sizes = sorted({min(sz, nt * ne) for sz in MOE_OVER_STEPS} | {nt * ne})
    branch = sum((n_over > sz).astype(jnp.int32) for sz in [0] + sizes[:-1])
    return lax.switch(branch, [combine] + [with_overflow(sz) for sz in sizes])


def kernel(x, c, ctx, c_ctx, ada_w, ada_b, post_ln_g, post_ln_b, lru_w_in, lru_conv_w, lru_conv_b, lru_gate_w, lru_gate_b, lru_lambda, lru_w_out, rwkv_mu, rwkv_w_in, rwkv_w0, rwkv_w_l1, rwkv_w_l2, rwkv_a0, rwkv_a_l1, rwkv_a_l2, rwkv_g_l1, rwkv_g_l2, rwkv_k_k, rwkv_k_a, rwkv_r_k, rwkv_ln_g, rwkv_ln_b, rwkv_w_out, ret_w_in, ret_decay, ret_gn_g, ret_gn_b, ret_w_out, hgrn_w_in, hgrn_b_f, hgrn_lb, hgrn_norm_g, hgrn_w_out, moe_router, moe_bias, moe_w_gu, moe_w_down, moe_sh_gu, moe_sh_down):
    assert x.shape[0] == 1 and ctx.shape[0] == 1
    n_ctx, n_lat, d = ctx.shape[1], x.shape[1], x.shape[2]
    assert n_ctx % TM == 0 and n_lat % TM == 0 and d == D_MODEL
    nct = n_ctx // TM
    rows = n_lat // GRID_W
    n_freq = RET_QK // 4
    freqs = ROPE_BASE ** (-jnp.arange(n_freq, dtype=F32) / n_freq)
    ang_row = jnp.arange(rows, dtype=F32)[:, None] * freqs
    ang_col = jnp.arange(GRID_W, dtype=F32)[:, None] * freqs

    def table(fn, ctx_value):
        lat = jnp.concatenate([jnp.repeat(fn(ang_row), GRID_W, axis=0), jnp.tile(fn(ang_col), (rows, 1))], axis=-1)
        return jnp.concatenate([jnp.full((n_ctx, RET_QK), ctx_value, F32), jnp.concatenate([lat, lat], axis=-1)], axis=0)

    rope_cos = table(jnp.cos, 1.0)
    rope_sin = table(jnp.sin, 0.0)
    lb_cum = jnp.cumsum(jax.nn.softmax(hgrn_lb.astype(F32), axis=0), axis=0)

    cond = jnp.concatenate([c_ctx[None], c, jnp.zeros((6, d), F32)], axis=0)
    mods_all = _ada_mods(cond, ada_w, ada_b)
    h = jnp.concatenate([ctx[0], x[0]], axis=0)
    for i in range(DEPTH):
        kind, j = i % N_MIXERS, i // N_MIXERS
        mods = mods_all[i]
        lng, lnb = post_ln_g[i, 0], post_ln_b[i, 0]
        if kind == 0:
            h = _lru_mixer(h, mods, nct, lru_w_in[j], lru_conv_w[j], lru_conv_b[j], lru_gate_w[j], lru_gate_b[j],
                           lru_lambda[j], lru_w_out[j], lng, lnb)
        elif kind == 1:
            h = _rwkv_mixer(h, mods, nct, rwkv_mu[j], rwkv_w_in[j], rwkv_w0[j], rwkv_w_l1[j], rwkv_w_l2[j], rwkv_a0[j],
                            rwkv_a_l1[j], rwkv_a_l2[j], rwkv_g_l1[j], rwkv_g_l2[j], rwkv_k_k[j], rwkv_k_a[j],
                            rwkv_r_k[j], rwkv_ln_g[j], rwkv_ln_b[j], rwkv_w_out[j], lng, lnb)
        elif kind == 2:
            h = _ret_mixer(h, mods, nct, rope_cos, rope_sin, ret_w_in[j], ret_decay[j], ret_gn_g[j], ret_gn_b[j],
                           ret_w_out[j], lng, lnb)
        else:
            h = _hgrn_mixer(h, mods, nct, lb_cum[i] - lb_cum[0], hgrn_w_in[j], hgrn_b_f[j], hgrn_norm_g[j],
                            hgrn_w_out[j], lng, lnb)
        h = _moe_layer(h, mods, nct, i, moe_router[i], moe_bias[i], moe_w_gu, moe_w_down, moe_sh_gu[i],
                       moe_sh_down[i], post_ln_g[i, 1], post_ln_b[i, 1], latent_only=i == DEPTH - 1)
    return h[None]
```

```python
import math
import functools
import jax
import jax.numpy as jnp
from jax import lax
from jax.experimental import pallas as pl
from jax.experimental.pallas import tpu as pltpu

F32 = jnp.float32
MXU_DT = jnp.bfloat16
ACT_DT = jnp.bfloat16
LANES = 128
TM = 256
VMEM_LIMIT = 56 * 2 ** 20

D_MODEL = 1024
DEPTH = 4
GRID_W = 64
N_MIXERS = 4
DEEPNORM_ALPHA = (2.0 * DEPTH) ** 0.25
LN_EPS = 1e-5
LRU_WIDTH = D_MODEL
LRU_BLOCKS = 16
LRU_BLOCK = LRU_WIDTH // LRU_BLOCKS
LRU_C = 8.0
RWKV_HEAD = 64
RWKV_HEADS = D_MODEL // RWKV_HEAD
RWKV_DECAY_SCALE = math.exp(-0.5)
RWKV_GN_EPS = 64e-5
RWKV_CHUNK = 64
RET_HEADS = 4
RET_QK = D_MODEL // RET_HEADS
RET_V = 2 * RET_QK
RET_CHUNK = 128
ROPE_BASE = 10000.0
HGRN_HEADS = 8
HGRN_HEAD = D_MODEL // HGRN_HEADS
HGRN_BLOCK = 16
HGRN_GROUP = 16
HGRN_UNROLL = 4
N_EXPERTS = 64
TOP_K = 8
N_GROUPS = 8
TOPK_GROUPS = 4
EXPERT_DIM = 256
ROUTED_SCALE = 2.5
MOE_OVER_STEPS = (128, 256, 512)
MOE_CAP = 64
MOE_EGROUP = 8
MOE_RUN = 65
MOE_CHUNK = 13


def _cparams(*sem):
    return pltpu.CompilerParams(dimension_semantics=sem, vmem_limit_bytes=VMEM_LIMIT)


def _dot(a, b):
    return jnp.dot(a.astype(MXU_DT), b.astype(MXU_DT), preferred_element_type=F32)


def _dot_nt(a, b):
    return lax.dot_general(a.astype(MXU_DT), b.astype(MXU_DT), (((1,), (1,)), ((), ())), preferred_element_type=F32)


def _dot_tn(a, b):
    return lax.dot_general(a.astype(MXU_DT), b.astype(MXU_DT), (((0,), (0,)), ((), ())), preferred_element_type=F32)


def _split(x, n):
    parts = []
    for _ in range(n):
        p = x.astype(MXU_DT)
        parts.append(p)
        x = x - p.astype(F32)
    return parts


def _dot_sel(sel, x, n):
    return sum(jnp.dot(sel.astype(MXU_DT), p, preferred_element_type=F32) for p in _split(x, n))


def _dot_xsel(x, sel, n):
    return sum(jnp.dot(p, sel.astype(MXU_DT), preferred_element_type=F32) for p in _split(x, n))


def _modulate(h, m, shift_idx):
    return h * (1.0 + m[shift_idx + 1:shift_idx + 2]) + m[shift_idx:shift_idx + 1]


def _ln_rows(z, g, b):
    mu = jnp.mean(z, axis=-1, keepdims=True)
    zc = z - mu
    var = jnp.mean(zc * zc, axis=-1, keepdims=True)
    return zc * lax.rsqrt(var + LN_EPS) * g + b


def _silu(x):
    return x * jax.nn.sigmoid(x)


def _shift_down(x, first_row):
    rows = lax.broadcasted_iota(jnp.int32, (x.shape[0], 1), 0)
    return jnp.where(rows == 0, first_row, pltpu.roll(x, 1, 0))


def _shift_up(x, last_row):
    n = x.shape[0]
    rows = lax.broadcasted_iota(jnp.int32, (n, 1), 0)
    return jnp.where(rows == n - 1, last_row, pltpu.roll(x, n - 1, 0))


def _tile_of(g, nct, nt, reverse):
    if not reverse:
        return g
    return jnp.where(g < nct, nct - 1 - g, nt - 1 - (g - nct))


def _halo_flags(t, nct, nt):
    prev_ok = jnp.logical_and(t != 0, t != nct).astype(F32)
    next_ok = jnp.logical_and(t != nct - 1, t != nt - 1).astype(F32)
    return prev_ok, next_ok


def _ada_kernel(s_ref, w_ref, b_ref, o_ref):
    o_ref[0] = _dot(_silu(s_ref[...]), w_ref[0]) + b_ref[0]


def _ada_mods(cond, ada_w, ada_b):
    nl, d, n6 = ada_w.shape
    out = pl.pallas_call(
        _ada_kernel, grid=(nl, n6 // d),
        in_specs=[pl.BlockSpec((8, d), lambda l, j: (0, 0)),
                  pl.BlockSpec((1, d, d), lambda l, j: (l, 0, j)),
                  pl.BlockSpec((1, 1, d), lambda l, j: (l, 0, j))],
        out_specs=pl.BlockSpec((1, 8, d), lambda l, j: (l, 0, j)),
        out_shape=jax.ShapeDtypeStruct((nl, 8, n6), F32),
        compiler_params=_cparams("arbitrary", "arbitrary"), name="ada_mods",
    )(cond, ada_w, ada_b.reshape(nl, 1, n6))
    return out[:, :2].reshape(nl, 2, 6, d)


def _row_spec(width, tm=TM):
    return pl.BlockSpec((tm, width), lambda i: (i, 0))


def _full_spec(shape):
    nd = len(shape)
    return pl.BlockSpec(tuple(shape), lambda *_: (0,) * nd)


def _mod_spec(nct):
    return pl.BlockSpec((1, 6, D_MODEL), lambda i: (jnp.minimum(i // nct, 1), 0, 0))


def _lru_out_kernel(g_ref, hf_ref, hb_ref, w_ref, h_ref, mod_ref, lng_ref, lnb_ref, o_ref):
    y = _dot(g_ref[...] * (hf_ref[...] + hb_ref[...]), w_ref[...])
    z = DEEPNORM_ALPHA * h_ref[...] + mod_ref[0][2:3] * y
    o_ref[...] = _ln_rows(z, lng_ref[...], lnb_ref[...])

def _lru_in_kernel(h_ref, mod_ref, w_ref, g_ref, x_ref):
    u = _modulate(h_ref[...], mod_ref[0], 0)
    z = _dot(u, w_ref[...])
    g_ref[...] = jax.nn.gelu(z[:, :LRU_WIDTH], approximate=True).astype(g_ref.dtype)
    x_ref[...] = z[:, LRU_WIDTH:]


def _lru_scan_kernel(xf_ref, xfp_ref, xfn_ref, xb_ref, xbp_ref, xbn_ref, cw_ref, cb_ref, gw_ref, gb_ref, lam_ref,
                     hf_o, hb_o, a_s, b_s, st_s, *, nct, nt):
    g = pl.program_id(0)

    @pl.when(g == 0)
    def _():
        st_s[...] = jnp.zeros_like(st_s)

    cw = cw_ref[...]
    for d, (x_ref, xp_ref, xn_ref) in enumerate(((xf_ref, xfp_ref, xfn_ref), (xb_ref, xbp_ref, xbn_ref))):
        prev_ok, next_ok = _halo_flags(_tile_of(g, nct, nt, d == 1), nct, nt)
        x = x_ref[...]
        xm1 = _shift_down(x, xp_ref[7:8, :] * prev_ok)
        xp1 = _shift_up(x, xn_ref[0:1, :] * next_ok)
        xp2 = _shift_up(xp1, xn_ref[1:2, :] * next_ok)
        xc = cw[0:1] * xm1 + cw[1:2] * x + cw[2:3] * xp1 + cw[3:4] * xp2 + cb_ref[...]
        gates = jax.nn.sigmoid(_dot(xc, gw_ref[d]) + gb_ref[d])
        lam = lam_ref[d:d + 1, :]
        softplus = jnp.maximum(-lam, 0.0) + jnp.log(1.0 + jnp.exp(-jnp.abs(lam)))
        log_a = -LRU_C * gates[:, :LRU_WIDTH] * softplus
        a_s[d] = jnp.exp(log_a)
        b_s[d] = jnp.sqrt(1.0 - jnp.exp(2.0 * log_a)) * (gates[:, LRU_WIDTH:] * xc)

    def row(r, carry):
        hf, hb = carry
        rb = TM - 1 - r
        hf = a_s[0, pl.ds(r, 1), :] * hf + b_s[0, pl.ds(r, 1), :]
        hb = a_s[1, pl.ds(rb, 1), :] * hb + b_s[1, pl.ds(rb, 1), :]
        hf_o[pl.ds(r, 1), :] = hf
        hb_o[pl.ds(rb, 1), :] = hb
        return hf, hb

    hf, hb = lax.fori_loop(0, TM, row, (st_s[0], st_s[1]), unroll=8)
    st_s[0] = hf
    st_s[1] = hb


def _lru_mixer(h, mods, nct, w_in, conv_w, conv_b, gate_w, gate_b, lam, w_out, ln_g, ln_b):
    t, d = h.shape
    nt = t // TM
    w = LRU_WIDTH
    gelu, rnn = pl.pallas_call(
        _lru_in_kernel, grid=(nt,),
        in_specs=[_row_spec(d), _mod_spec(nct), _full_spec((d, 2 * w))],
        out_specs=[_row_spec(w), _row_spec(w)],
        out_shape=[jax.ShapeDtypeStruct((t, w), ACT_DT), jax.ShapeDtypeStruct((t, w), F32)],
        compiler_params=_cparams("arbitrary"), name="lru_in",
    )(h, mods, w_in.astype(MXU_DT))
    eye = jnp.eye(LRU_BLOCKS, dtype=F32)
    gw = jnp.einsum('dgnij,nm->dgnimj', gate_w, eye).reshape(2, 2, w, w)
    gw = jnp.concatenate([gw[:, 0], gw[:, 1]], axis=-1).astype(MXU_DT)
    gb = gate_b.reshape(2, 1, 2 * w)
    def tile_specs(reverse):
        tile = lambda g: _tile_of(g, nct, nt, reverse)
        return [pl.BlockSpec((TM, w), lambda g: (tile(g), 0)),
                pl.BlockSpec((8, w), lambda g: (jnp.maximum(tile(g) * (TM // 8) - 1, 0), 0)),
                pl.BlockSpec((8, w), lambda g: (jnp.minimum((tile(g) + 1) * (TM // 8), t // 8 - 1), 0))]

    fwd, bwd = tile_specs(False), tile_specs(True)
    hf, hb = pl.pallas_call(
        functools.partial(_lru_scan_kernel, nct=nct, nt=nt), grid=(nt,),
        in_specs=fwd + bwd + [_full_spec((4, w)), _full_spec((1, w)), _full_spec((2, w, 2 * w)),
                              _full_spec((2, 1, 2 * w)), _full_spec((2, w))],
        out_specs=[fwd[0], bwd[0]], out_shape=[jax.ShapeDtypeStruct((t, w), F32)] * 2,
        scratch_shapes=[pltpu.VMEM((2, TM, w), F32)] * 2 + [pltpu.VMEM((2, 1, w), F32)],
        compiler_params=_cparams("arbitrary"), name="lru_scan",
    )(rnn, rnn, rnn, rnn, rnn, rnn, conv_w, conv_b[None], gw, gb, lam)
    return pl.pallas_call(
        _lru_out_kernel, grid=(nt,),
        in_specs=[_row_spec(w)] * 3 + [_full_spec((w, d)), _row_spec(d), _mod_spec(nct), _full_spec((1, d)),
                                       _full_spec((1, d))],
        out_specs=_row_spec(d), out_shape=jax.ShapeDtypeStruct((t, d), F32),
        compiler_params=_cparams("arbitrary"), name="lru_out",
    )(gelu, hf, hb, w_out.astype(MXU_DT), h, mods, ln_g[None], ln_b[None])


def _seg_sum(x, e_ref, et_ref):
    s = _dot_xsel(x, e_ref[...], 2)
    return _dot_xsel(s, et_ref[...], 2)


def _rwkv_prep_kernel(h_ref, hp_ref, hn_ref, mod_ref, mu_ref, win_ref, wl1_ref, wl2_ref, w0_ref, al1_ref, al2_ref,
                      a0_ref, gl1_ref, gl2_ref, kk_ref, ka_ref, rk_ref, e_ref, et_ref,
                      r_o, v_o, kk_o, g_o, bv_o, lw0_o, lw1_o, kt0_o, kt1_o, ab0_o, ab1_o, *, nct, nt):
    i = pl.program_id(0)
    prev_ok, next_ok = _halo_flags(i, nct, nt)
    m = mod_ref[0]
    u = _modulate(h_ref[...], m, 0)
    up = _modulate(hp_ref[7:8, :], m, 0) * prev_ok
    un = _modulate(hn_ref[0:1, :], m, 0) * next_ok
    lane = lax.broadcasted_iota(jnp.int32, (1, D_MODEL), 1)
    sh = jnp.where(lane < D_MODEL // 2, _shift_down(u, up), _shift_up(u, un))
    dx = sh - u
    mu = mu_ref[...]
    xm = [u + dx * mu[c:c + 1] for c in range(6)]
    r = _dot(xm[0], win_ref[0])
    k = _dot(xm[1], win_ref[1])
    v = _dot(xm[2], win_ref[2])
    t1 = jnp.tanh(_dot(xm[3], wl1_ref[...]))
    t2 = _dot(xm[4], al1_ref[...])
    g = _dot(jax.nn.sigmoid(_dot(xm[5], gl1_ref[...])), gl2_ref[...])
    kk = k * kk_ref[...]
    kk = kk * lax.rsqrt(_seg_sum(kk * kk, e_ref, et_ref) + 1e-12)
    ktsum = None
    for z, (lw_o, kt_o, ab_o) in enumerate(((lw0_o, kt0_o, ab0_o), (lw1_o, kt1_o, ab1_o))):
        d_w = w0_ref[z:z + 1, :] + _dot(t1, wl2_ref[z])
        lw_o[...] = -RWKV_DECAY_SCALE * jax.nn.sigmoid(d_w)
        a = jax.nn.sigmoid(a0_ref[z:z + 1, :] + _dot(t2, al2_ref[z]))
        kt = k * (1.0 + (a - 1.0) * ka_ref[...])
        kt_o[...] = kt.astype(kt_o.dtype)
        ab_o[...] = (kk * a).astype(ab_o.dtype)
        ktsum = kt if ktsum is None else ktsum + kt
    r_o[...] = r.astype(r_o.dtype)
    v_o[...] = v.astype(v_o.dtype)
    kk_o[...] = kk.astype(kk_o.dtype)
    g_o[...] = g.astype(g_o.dtype)
    bv_o[...] = (_seg_sum(r * ktsum * rk_ref[...], e_ref, et_ref) * v).astype(bv_o.dtype)


def _rwkv_scan_kernel(rf_ref, vf_ref, kkf_ref, lwf_ref, ktf_ref, abf_ref, rb_ref, vb_ref, kkb_ref, lwb_ref, ktb_ref,
                      abb_ref, of_ref, ob_ref, s_ref):
    c = RWKV_CHUNK
    dirs = ((rf_ref, vf_ref, kkf_ref, lwf_ref, ktf_ref, abf_ref, of_ref),
            (rb_ref, vb_ref, kkb_ref, lwb_ref, ktb_ref, abb_ref, ob_ref))

    @pl.when(pl.program_id(0) == 0)
    def _():
        s_ref[...] = jnp.zeros_like(s_ref)

    ri = lax.broadcasted_iota(jnp.int32, (c, c), 0)
    ci = lax.broadcasted_iota(jnp.int32, (c, c), 1)
    incl = (ci <= ri, ci >= ri)
    ri2 = lax.broadcasted_iota(jnp.int32, (c, 2 * c), 0)
    ci2 = jnp.bitwise_and(lax.broadcasted_iota(jnp.int32, (c, 2 * c), 1), c - 1)
    incl2 = (ci2 <= ri2, ci2 >= ri2)
    strict2 = (ci2 < ri2, ci2 > ri2)
    lane_a = lax.broadcasted_iota(jnp.int32, (1, LANES), 1) < RWKV_HEAD
    bi = lax.broadcasted_iota(jnp.int32, (LANES, LANES), 0) < RWKV_HEAD
    bj = lax.broadcasted_iota(jnp.int32, (LANES, LANES), 1) < RWKV_HEAD
    blockdiag = bi == bj

    def stack2(x):
        return jnp.concatenate([jnp.where(lane_a, x, 0.0), jnp.where(lane_a, 0.0, x)], axis=0)

    nch = rf_ref.shape[0] // c
    sls = [slice(p * LANES, (p + 1) * LANES) for p in range(D_MODEL // LANES)]
    cells = [(d, p) for d in range(2) for p in range(len(sls))]

    def one_chunk(ci, carry):
        pre = []
        for d, (r_ref, v_ref, kk_ref, lw_ref, kt_ref, ab_ref, o_ref) in enumerate(dirs):
            reverse = d == 1
            rows = pl.ds(pl.multiple_of(((nch - 1 - ci) if reverse else ci) * c, c), c)
            lw = lw_ref[rows, :]
            cl = _dot_sel(jnp.where(incl[d], 1.0, 0.0), lw, 3)
            tot = cl[0:1, :] if reverse else cl[c - 1:c, :]
            e_out = jnp.exp(-cl)
            e_end = jnp.exp(tot - cl)
            kt = kt_ref[rows, :].astype(F32)
            ab = ab_ref[rows, :].astype(F32)
            pre.append(dict(rows=rows, kap=kk_ref[rows, :].astype(F32) * jnp.exp(cl - lw),
                            rh=r_ref[rows, :].astype(F32) * jnp.exp(cl), kh=kt * e_out, bh=ab * e_out,
                            kb=kt * e_end, bb=ab * e_end, e_tot=jnp.exp(tot), vv=v_ref[rows, :].astype(F32)))
        s = {(d, p): s_ref[d, p] for d, p in cells}
        xq = {(d, p): jnp.concatenate([pre[d]['kap'][:, sls[p]], pre[d]['rh'][:, sls[p]]], axis=0) for d, p in cells}
        yk = {(d, p): jnp.concatenate([stack2(pre[d]['kh'][:, sls[p]]), stack2(pre[d]['bh'][:, sls[p]])], axis=0)
              for d, p in cells}
        gm = {k: _dot_nt(xq[k], yk[k]) for k in cells}
        xs = {k: _dot_nt(xq[k], s[k]) for k in cells}
        l_kk = {k: jnp.where(strict2[k[0]], gm[k][:c, :2 * c], 0.0) for k in cells}
        l_bk = {k: jnp.where(strict2[k[0]], gm[k][:c, 2 * c:], 0.0) for k in cells}
        a_rk = {k: jnp.where(incl2[k[0]], gm[k][c:, :2 * c], 0.0) for k in cells}
        a_rb = {k: jnp.where(incl2[k[0]], gm[k][c:, 2 * c:], 0.0) for k in cells}
        v2 = {(d, p): stack2(pre[d]['vv'][:, sls[p]]) for d, p in cells}
        x = {k: xs[k][:c] + _dot(l_kk[k], v2[k]) for k in cells}
        lp = {k: _dot(l_bk[k], stack2(l_bk[k])) for k in cells}
        x = {k: x[k] - _dot(l_bk[k], stack2(x[k])) for k in cells}
        for it in range(5):
            x = {k: x[k] + _dot(lp[k], stack2(x[k])) for k in cells}
            if it < 4:
                lp = {k: _dot(lp[k], stack2(lp[k])) for k in cells}
        o = {k: xs[k][c:] + _dot(jnp.concatenate([a_rk[k], -a_rb[k]], axis=1),
                                 jnp.concatenate([v2[k], stack2(x[k])], axis=0)) for k in cells}
        upd = {(d, p): _dot_tn(jnp.concatenate([pre[d]['vv'][:, sls[p]], -x[d, p]], axis=0),
                               jnp.concatenate([pre[d]['kb'][:, sls[p]], pre[d]['bb'][:, sls[p]]], axis=0))
               for d, p in cells}
        for d, p in cells:
            dirs[d][6][pre[d]['rows'], sls[p]] = o[d, p]
            s_ref[d, p] = s[d, p] * pre[d]['e_tot'][:, sls[p]] + jnp.where(blockdiag, upd[d, p], 0.0)
        return carry

    def two_chunks(cj, carry):
        one_chunk(2 * cj, carry)
        return one_chunk(2 * cj + 1, carry)

    lax.fori_loop(0, nch // 2, two_chunks, 0)


def _rwkv_out_kernel(of_ref, ob_ref, bv_ref, g_ref, lg_ref, lb_ref, e_ref, et_ref, w_ref, h_ref, mod_ref, lng_ref,
                     lnb_ref, o_ref):
    o = of_ref[...] + ob_ref[...]
    inv = 1.0 / RWKV_HEAD
    oc = o - _seg_sum(o, e_ref, et_ref) * inv
    var = _seg_sum(oc * oc, e_ref, et_ref) * inv
    y = oc * lax.rsqrt(var + RWKV_GN_EPS) * lg_ref[...] + lb_ref[...] + bv_ref[...]
    yo = _dot(y * g_ref[...], w_ref[...])
    z = DEEPNORM_ALPHA * h_ref[...] + mod_ref[0][2:3] * yo
    o_ref[...] = _ln_rows(z, lng_ref[...], lnb_ref[...])


def _rwkv_mixer(h, mods, nct, mu, w_in, w0, w_l1, w_l2, a0, a_l1, a_l2, g_l1, g_l2, k_k, k_a, r_k, gn_g, gn_b, w_out,
                ln_g, ln_b):
    t, d = h.shape
    nt = t // TM
    bf = MXU_DT
    lw_ = w_l1.shape[-1]
    la_ = a_l1.shape[-1]
    zw = jnp.zeros((lw_, d), F32)
    za = jnp.zeros((la_, d), F32)
    wl1 = jnp.concatenate([w_l1[0], w_l1[1]], axis=1).astype(bf)
    wl2 = jnp.stack([jnp.concatenate([w_l2[0], zw], 0), jnp.concatenate([zw, w_l2[1]], 0)]).astype(bf)
    al1 = jnp.concatenate([a_l1[0], a_l1[1]], axis=1).astype(bf)
    al2 = jnp.stack([jnp.concatenate([a_l2[0], za], 0), jnp.concatenate([za, a_l2[1]], 0)]).astype(bf)
    head_of = jnp.arange(d) // RWKV_HEAD
    e = (head_of[:, None] == jnp.arange(LANES)[None, :]).astype(bf)
    et = e.T
    halo_p = pl.BlockSpec((8, d), lambda i: (jnp.maximum(i * (TM // 8) - 1, 0), 0))
    halo_n = pl.BlockSpec((8, d), lambda i: (jnp.minimum((i + 1) * (TM // 8), t // 8 - 1), 0))
    args = [h, h, h, mods, mu, w_in.astype(bf), wl1, wl2, w0, al1, al2, a0, g_l1.astype(bf), g_l2.astype(bf),
            k_k[None], k_a[None], r_k.reshape(1, d), e, et]
    ins = [_row_spec(d), halo_p, halo_n, _mod_spec(nct)] + [_full_spec(a.shape) for a in args[4:]]
    outs = pl.pallas_call(
        functools.partial(_rwkv_prep_kernel, nct=nct, nt=nt), grid=(nt,), in_specs=ins,
        out_specs=[_row_spec(d)] * 11,
        out_shape=[jax.ShapeDtypeStruct((t, d), dt) for dt in [ACT_DT] * 5 + [F32] * 2 + [ACT_DT] * 4],
        compiler_params=_cparams("arbitrary"), name="rwkv_prep",
    )(*args)
    r, v, kk, g, bv, lw0, lw1, kt0, kt1, ab0, ab1 = outs
    fwd = pl.BlockSpec((TM, d), lambda g_: (g_, 0))
    bwd = pl.BlockSpec((TM, d), lambda g_: (_tile_of(g_, nct, nt, True), 0))
    o_dir = pl.pallas_call(
        _rwkv_scan_kernel, grid=(nt,), in_specs=[fwd] * 6 + [bwd] * 6, out_specs=[fwd, bwd],
        out_shape=[jax.ShapeDtypeStruct((t, d), F32)] * 2,
        scratch_shapes=[pltpu.VMEM((2, d // LANES, LANES, LANES), F32)],
        compiler_params=_cparams("arbitrary"), name="rwkv_scan",
    )(r, v, kk, lw0, kt0, ab0, r, v, kk, lw1, kt1, ab1)
    args = [o_dir[0], o_dir[1], bv, g, gn_g[None], gn_b[None], e, et, w_out.astype(bf), h, mods, ln_g[None], ln_b[None]]
    ins = [_row_spec(d)] * 4 + [_full_spec(a.shape) for a in args[4:9]] + [_row_spec(d), _mod_spec(nct),
                                                                          _full_spec((1, d)), _full_spec((1, d))]
    return pl.pallas_call(
        _rwkv_out_kernel, grid=(nt,), in_specs=ins, out_specs=_row_spec(d),
        out_shape=jax.ShapeDtypeStruct((t, d), F32), compiler_params=_cparams("arbitrary"), name="rwkv_out",
    )(*args)


def _ret_in_kernel(h_ref, mod_ref, w_ref, cos_ref, sin_ref, q_o, k_o, v_o, g_o):
    d = D_MODEL
    u = _modulate(h_ref[...], mod_ref[0], 0).astype(MXU_DT)
    q = _dot(u, w_ref[:, 0:d])
    k = _dot(u, w_ref[:, d:2 * d]) * (RET_QK ** -0.5)
    v_o[...] = _dot(u, w_ref[:, 2 * d:4 * d]).astype(v_o.dtype)
    g_o[...] = _silu(_dot(u, w_ref[:, 4 * d:6 * d])).astype(g_o.dtype)
    cos = cos_ref[...]
    sin = sin_ref[...]
    half = RET_QK // 2
    for z, z_o in ((q, q_o), (k, k_o)):
        for hh in range(RET_HEADS):
            lo = z[:, hh * RET_QK:hh * RET_QK + half]
            hi = z[:, hh * RET_QK + half:(hh + 1) * RET_QK]
            zh = jnp.concatenate([lo, hi], axis=1)
            rot = jnp.concatenate([-hi, lo], axis=1)
            z_o[:, hh * RET_QK:(hh + 1) * RET_QK] = (zh * cos + rot * sin).astype(z_o.dtype)


def _ret_scan_kernel(qf_ref, kf_ref, vf_ref, qb_ref, kb_ref, vb_ref, inner_ref, qd_ref, kd_ref, bd_ref, of_ref, ob_ref,
                     r_ref):
    @pl.when(pl.program_id(0) == 0)
    def _():
        r_ref[...] = jnp.zeros_like(r_ref)

    refs = ((qf_ref, kf_ref, vf_ref, of_ref), (qb_ref, kb_ref, vb_ref, ob_ref))
    cells = [(d, hh) for d in range(2) for hh in range(RET_HEADS)]
    qs = lambda hh: slice(hh * RET_QK, (hh + 1) * RET_QK)
    vs = lambda hh: slice(hh * RET_V, (hh + 1) * RET_V)
    q = {(d, hh): refs[d][0][:, qs(hh)] for d, hh in cells}
    k = {(d, hh): refs[d][1][:, qs(hh)] for d, hh in cells}
    v = {(d, hh): refs[d][2][:, vs(hh)] for d, hh in cells}
    state = {c: r_ref[c[0], c[1]] for c in cells}
    scores = {c: _dot_nt(q[c], k[c]) * inner_ref[c[0], c[1]] for c in cells}
    carry_in = {c: _dot(q[c], state[c]) * qd_ref[c[0], c[1]] for c in cells}
    upd = {c: _dot_tn(k[c] * kd_ref[c[0], c[1]], v[c]) for c in cells}
    for c in cells:
        refs[c[0]][3][:, vs(c[1])] = _dot(scores[c], v[c]) + carry_in[c]
        r_ref[c[0], c[1]] = state[c] * bd_ref[c[0], c[1]] + upd[c]


def _ret_out_kernel(of_ref, ob_ref, g_ref, gg_ref, gb_ref, w_ref, h_ref, mod_ref, lng_ref, lnb_ref, o_ref):
    parts = []
    for hh in range(RET_HEADS):
        sl = slice(hh * RET_V, (hh + 1) * RET_V)
        o = of_ref[:, sl] + ob_ref[:, sl]
        mu = jnp.mean(o, axis=-1, keepdims=True)
        oc = o - mu
        var = jnp.mean(oc * oc, axis=-1, keepdims=True)
        y = oc * lax.rsqrt(var + LN_EPS) * gg_ref[:, sl] + gb_ref[:, sl]
        parts.append((g_ref[:, sl] * y).astype(MXU_DT))
    yo = _dot(jnp.concatenate(parts, axis=1), w_ref[...])
    z = DEEPNORM_ALPHA * h_ref[...] + mod_ref[0][2:3] * yo
    o_ref[...] = _ln_rows(z, lng_ref[...], lnb_ref[...])


def _ret_mixer(h, mods, nct, rope_cos, rope_sin, w_in, decay_logit, gn_g, gn_b, w_out, ln_g, ln_b):
    t, d = h.shape
    nt = t // TM
    hv = RET_HEADS * RET_V
    q, k, v, sg = pl.pallas_call(
        _ret_in_kernel, grid=(nt,),
        in_specs=[_row_spec(d), _mod_spec(nct), _full_spec(w_in.shape), _row_spec(RET_QK), _row_spec(RET_QK)],
        out_specs=[_row_spec(d), _row_spec(d), _row_spec(hv), _row_spec(hv)],
        out_shape=[jax.ShapeDtypeStruct((t, w), ACT_DT) for w in (d, d, hv, hv)],
        compiler_params=_cparams("arbitrary"), name="ret_in",
    )(h, mods, w_in.astype(MXU_DT), rope_cos, rope_sin)
    c = RET_CHUNK
    ncc, nc = nct * (TM // c), t // c
    log_gamma = jax.nn.log_sigmoid(decay_logit.astype(F32))
    pos = jnp.arange(c, dtype=F32)
    tabs = []
    for d_ in range(2):
        lg = log_gamma[d_][:, None, None]
        p = (c - 1.0 - pos) if d_ == 1 else pos
        rel = p[:, None] - p[None, :]
        tabs.append((jnp.where(rel >= 0, jnp.exp(jnp.maximum(rel, 0.0) * lg), 0.0),
                     jnp.exp((p + 1.0) * log_gamma[d_][:, None])[:, :, None],
                     jnp.exp((c - 1.0 - p) * log_gamma[d_][:, None])[:, :, None],
                     jnp.exp(c * log_gamma[d_])[:, None, None]))
    inner, q_dec, k_dec, blk_dec = (jnp.stack(z) for z in zip(*tabs))
    cs = lambda w, reverse: pl.BlockSpec((c, w), lambda g_: (_tile_of(g_, ncc, nc, reverse), 0))
    o_dir = pl.pallas_call(
        _ret_scan_kernel, grid=(nc,),
        in_specs=[cs(d, False), cs(d, False), cs(hv, False), cs(d, True), cs(d, True), cs(hv, True),
                  _full_spec(inner.shape), _full_spec(q_dec.shape), _full_spec(k_dec.shape), _full_spec(blk_dec.shape)],
        out_specs=[cs(hv, False), cs(hv, True)], out_shape=[jax.ShapeDtypeStruct((t, hv), F32)] * 2,
        scratch_shapes=[pltpu.VMEM((2, RET_HEADS, RET_QK, RET_V), F32)],
        compiler_params=_cparams("arbitrary"), name="ret_scan",
    )(q, k, v, q, k, v, inner, q_dec, k_dec, blk_dec)
    return pl.pallas_call(
        _ret_out_kernel, grid=(nt,),
        in_specs=[_row_spec(hv)] * 3 + [_full_spec((1, hv)), _full_spec((1, hv)), _full_spec((hv, d)), _row_spec(d),
                                        _mod_spec(nct), _full_spec((1, d)), _full_spec((1, d))],
        out_specs=_row_spec(d), out_shape=jax.ShapeDtypeStruct((t, d), F32),
        compiler_params=_cparams("arbitrary"), name="ret_out",
    )(o_dir[0], o_dir[1], sg, gn_g[None], gn_b[None], w_out.astype(MXU_DT), h, mods, ln_g[None], ln_b[None])


def _hgrn_in_kernel(h_ref, mod_ref, w_ref, lb_ref, bf_ref, q_o, v_o, g_o, f0_o, f1_o):
    d = D_MODEL
    u = _modulate(h_ref[...], mod_ref[0], 0).astype(MXU_DT)
    lb = lb_ref[...]
    q_o[...] = _silu(_dot(u, w_ref[:, 0:d])).astype(q_o.dtype)
    f0_o[...] = lb + (1.0 - lb) * jax.nn.sigmoid(_dot(u, w_ref[:, d:2 * d]) + bf_ref[0:1, :])
    f1_o[...] = lb + (1.0 - lb) * jax.nn.sigmoid(_dot(u, w_ref[:, 2 * d:3 * d]) + bf_ref[1:2, :])
    v_o[...] = _dot(u, w_ref[:, 3 * d:4 * d]).astype(v_o.dtype)
    g_o[...] = _silu(_dot(u, w_ref[:, 4 * d:5 * d])).astype(g_o.dtype)


def _hgrn_scan_kernel(qf_ref, vf_ref, ff_ref, qb_ref, vb_ref, fb_ref, of_ref, ob_ref, s_ref, b_s, rb_s, rk_s, rv_s):
    hb = HGRN_BLOCK
    nb = TM // hb
    half = hb // 2
    dirs = ((qf_ref, vf_ref, ff_ref, of_ref, False), (qb_ref, vb_ref, fb_ref, ob_ref, True))

    @pl.when(pl.program_id(0) == 0)
    def _():
        s_ref[...] = jnp.zeros_like(s_ref)

    span = 4 * hb
    ri = lax.broadcasted_iota(jnp.int32, (span, span), 0)
    ci = lax.broadcasted_iota(jnp.int32, (span, span), 1)
    same_block = (ri // hb) == (ci // hb)
    ti = lax.broadcasted_iota(jnp.int32, (half, 1), 0)
    rowi = lax.broadcasted_iota(jnp.int32, (hb, 1), 0)
    heads = range(HGRN_HEADS)
    sls = [slice(hh * HGRN_HEAD, (hh + 1) * HGRN_HEAD) for hh in heads]
    cells = [(d, hh) for d in range(2) for hh in heads]

    for d, (q_ref, v_ref, f_ref, o_ref, reverse) in enumerate(dirs):
        tri = jnp.where(jnp.logical_and(same_block, (ci >= ri) if reverse else (ci <= ri)), 1.0, 0.0)
        for r0 in range(0, TM, span):
            b_s[d, r0:r0 + span] = _dot_sel(tri, jnp.log(f_ref[r0:r0 + span, :]), 3)

    def block(bi, par):
        pre = []
        for d, (q_ref, v_ref, f_ref, o_ref, reverse) in enumerate(dirs):
            blk = (nb - 1 - bi) if reverse else bi
            r0 = pl.multiple_of(blk * hb, hb)
            kx = 1.0 - f_ref[pl.ds(r0, hb), :]
            q = q_ref[pl.ds(r0, hb), :].astype(F32)
            v = v_ref[pl.ds(r0, hb), :].astype(F32)
            b = b_s[d, pl.ds(r0, hb), :]
            rb_s[d, par] = b
            rk_s[d, par] = kx
            rv_s[d, par] = v
            tot = b[0:1, :] if reverse else b[hb - 1:hb, :]
            first = (rowi >= half) if reverse else (rowi < half)
            beta = b[half:half + 1, :] if reverse else b[half - 1:half, :]
            pre.append(dict(
                r0=r0, q=q, v=v, b=b, qe=q * jnp.exp(b), kb=kx * jnp.exp(tot - b), e_tot=jnp.exp(tot),
                k_first=kx * jnp.exp(jnp.where(first, beta - b, -jnp.inf)),
                q_second=q * jnp.exp(jnp.where(first, -jnp.inf, b - beta)),
                causal=[(ti <= si) if reverse else (ti >= si) for si in range(half)]))
        for g0 in range(0, len(cells), HGRN_GROUP):
            grp = cells[g0:g0 + HGRN_GROUP]
            s = {c: s_ref[c[0], c[1]] for c in grp}
            m_first = {(d, hh): _dot_tn(pre[d]['k_first'][:, sls[hh]], pre[d]['v'][:, sls[hh]]) for d, hh in grp}
            o = {(d, hh): _dot_nt(pre[d]['qe'][:, sls[hh]], s[d, hh]) + _dot(pre[d]['q_second'][:, sls[hh]], m_first[d, hh])
                 for d, hh in grp}
            upd = {(d, hh): _dot_tn(pre[d]['v'][:, sls[hh]], pre[d]['kb'][:, sls[hh]]) for d, hh in grp}
            for d, hh in grp:
                sl = sls[hh]
                p = pre[d]
                parts = []
                for lo in (0, half):
                    bt = p['b'][lo:lo + half, sl]
                    qt = p['q'][lo:lo + half, sl]
                    acc = jnp.zeros((half, HGRN_HEAD), F32)
                    for si in range(half):
                        row = slice(lo + si, lo + si + 1)
                        dec = jnp.exp(jnp.where(p['causal'][si], bt - rb_s[d, par, row, sl], -jnp.inf))
                        sc = jnp.sum(qt * rk_s[d, par, row, sl] * dec, axis=-1, keepdims=True)
                        acc = acc + sc * rv_s[d, par, row, sl]
                    parts.append(acc)
                dirs[d][3][pl.ds(p['r0'], hb), sl] = o[d, hh] + jnp.concatenate(parts, axis=0)
                s_ref[d, hh] = s[d, hh] * p['e_tot'][:, sl] + upd[d, hh]

    def trip(bj, carry):
        for par in range(HGRN_UNROLL):
            block(HGRN_UNROLL * bj + par, par)
        return carry

    lax.fori_loop(0, nb // HGRN_UNROLL, trip, 0)


def _hgrn_out_kernel(of_ref, ob_ref, g_ref, ng_ref, w_ref, h_ref, mod_ref, lng_ref, lnb_ref, o_ref):
    parts = []
    for hh in range(HGRN_HEADS):
        sl = slice(hh * HGRN_HEAD, (hh + 1) * HGRN_HEAD)
        o = of_ref[:, sl] + ob_ref[:, sl]
        y = o * lax.rsqrt(jnp.mean(o * o, axis=-1, keepdims=True) + LN_EPS) * ng_ref[...]
        parts.append((y * g_ref[:, sl]).astype(MXU_DT))
    yo = _dot(jnp.concatenate(parts, axis=1), w_ref[...])
    z = DEEPNORM_ALPHA * h_ref[...] + mod_ref[0][2:3] * yo
    o_ref[...] = _ln_rows(z, lng_ref[...], lnb_ref[...])


def _hgrn_mixer(h, mods, nct, lb, w_in, b_f, norm_g, w_out, ln_g, ln_b):
    t, d = h.shape
    nt = t // TM
    q, v, sg, f0, f1 = pl.pallas_call(
        _hgrn_in_kernel, grid=(nt,),
        in_specs=[_row_spec(d), _mod_spec(nct), _full_spec(w_in.shape), _full_spec((1, d)), _full_spec((2, d))],
        out_specs=[_row_spec(d)] * 5,
        out_shape=[jax.ShapeDtypeStruct((t, d), dt) for dt in [ACT_DT] * 3 + [F32] * 2],
        compiler_params=_cparams("arbitrary"), name="hgrn_in",
    )(h, mods, w_in.astype(MXU_DT), lb[None], b_f)
    fwd = pl.BlockSpec((TM, d), lambda g_: (g_, 0))
    bwd = pl.BlockSpec((TM, d), lambda g_: (_tile_of(g_, nct, nt, True), 0))
    o_dir = pl.pallas_call(
        _hgrn_scan_kernel, grid=(nt,), in_specs=[fwd] * 3 + [bwd] * 3, out_specs=[fwd, bwd],
        out_shape=[jax.ShapeDtypeStruct((t, d), F32)] * 2,
        scratch_shapes=[pltpu.VMEM((2, HGRN_HEADS, HGRN_HEAD, HGRN_HEAD), F32)] + [pltpu.VMEM((2, TM, d), F32)]
        + [pltpu.VMEM((2, HGRN_UNROLL, HGRN_BLOCK, d), F32)] * 3,
        compiler_params=_cparams("arbitrary"), name="hgrn_scan",
    )(q, v, f0, q, v, f1)
    return pl.pallas_call(
        _hgrn_out_kernel, grid=(nt,),
        in_specs=[_row_spec(d)] * 3 + [_full_spec((1, HGRN_HEAD)), _full_spec((d, d)), _row_spec(d), _mod_spec(nct),
                                       _full_spec((1, d)), _full_spec((1, d))],
        out_specs=_row_spec(d), out_shape=jax.ShapeDtypeStruct((t, d), F32),
        compiler_params=_cparams("arbitrary"), name="hgrn_out",
    )(o_dir[0], o_dir[1], sg, norm_g[None], w_out.astype(MXU_DT), h, mods, ln_g[None], ln_b[None])


def _router_kernel(h_ref, mod_ref, rw_ref, rb_ref, u_o, gate_o, rank_o, x_o):
    u = _modulate(h_ref[...], mod_ref[0], 3)
    u_o[...] = u.astype(u_o.dtype)
    w_hi, w_lo = _split(rw_ref[...], 2)
    u_hi, u_lo = _split(u, 2)
    nt_dims = (((1,), (1,)), ((), ()))
    logits = (lax.dot_general(w_hi, u_hi, nt_dims, preferred_element_type=F32)
              + lax.dot_general(w_hi, u_lo, nt_dims, preferred_element_type=F32)
              + lax.dot_general(w_lo, u_hi, nt_dims, preferred_element_type=F32))
    ne, gs = N_EXPERTS, N_EXPERTS // N_GROUPS
    neg = -jnp.inf
    scores = jax.nn.sigmoid(logits[:ne])
    choice = scores + rb_ref[:ne]
    c3 = choice.reshape(N_GROUPS, gs, TM)
    mi = lax.broadcasted_iota(jnp.int32, c3.shape, 1).astype(F32)
    m1 = jnp.max(c3, axis=1, keepdims=True)
    i1 = jnp.min(jnp.where(c3 == m1, mi, float(gs)), axis=1, keepdims=True)
    m2 = jnp.max(jnp.where(mi == i1, neg, c3), axis=1, keepdims=True)
    gscore = m1 + m2
    gi = lax.broadcasted_iota(jnp.int32, gscore.shape, 0).astype(F32)
    gsel = jnp.zeros(gscore.shape, F32)
    for _ in range(TOPK_GROUPS):
        gm = jnp.max(gscore, axis=0, keepdims=True)
        pick = gi == jnp.min(jnp.where(gscore == gm, gi, float(N_GROUPS)), axis=0, keepdims=True)
        gsel = jnp.where(pick, 1.0, gsel)
        gscore = jnp.where(pick, neg, gscore)
    emask = jnp.broadcast_to(gsel, c3.shape).reshape(ne, TM)
    masked = jnp.where(emask > 0.5, choice, neg)
    ei = lax.broadcasted_iota(jnp.int32, masked.shape, 0).astype(F32)
    chosen = jnp.zeros(masked.shape, F32)
    for _ in range(TOP_K):
        em = jnp.max(masked, axis=0, keepdims=True)
        pick = ei == jnp.min(jnp.where(masked == em, ei, float(ne)), axis=0, keepdims=True)
        chosen = jnp.where(pick, 1.0, chosen)
        masked = jnp.where(pick, neg, masked)
    top_w = scores * chosen
    gates = ROUTED_SCALE * top_w / jnp.sum(top_w, axis=0, keepdims=True)
    ti = lax.broadcasted_iota(jnp.int32, (TM, TM), 0)
    tj = lax.broadcasted_iota(jnp.int32, (TM, TM), 1)
    before = jnp.where(ti < tj, 1.0, 0.0).astype(MXU_DT)
    prefix = jnp.dot(chosen.astype(MXU_DT), before, preferred_element_type=F32)
    rank = jnp.where(chosen > 0.5, prefix, -1.0)
    gate_o[0] = gates
    rank_o[0] = rank
    cap = MOE_CAP
    slot = lax.broadcasted_iota(jnp.int32, (cap, TM), 0).astype(F32)
    ub = u.astype(MXU_DT)
    for g0 in range(0, ne, MOE_EGROUP):
        onehot = jnp.concatenate([jnp.where(slot == rank[e:e + 1, :], 1.0, 0.0).astype(MXU_DT)
                                  for e in range(g0, g0 + MOE_EGROUP)], axis=0)
        xg = jnp.dot(onehot, ub, preferred_element_type=F32)
        x_o[0, g0:g0 + MOE_EGROUP] = xg.reshape(MOE_EGROUP, cap, D_MODEL).astype(x_o.dtype)


def _expert_kernel(x_ref, wgu_ref, wd_ref, y_ref, wgu_b, wd_b):
    @pl.when(pl.program_id(1) == 0)
    def _():
        wgu_b[0] = wgu_ref[0, 0].astype(wgu_b.dtype)
        wd_b[0] = wd_ref[0, 0].astype(wd_b.dtype)

    g = x_ref.shape[0]
    ch = max(c for c in range(1, MOE_CHUNK + 1) if g % c == 0)
    ed = EXPERT_DIM

    def chunk(ci, carry):
        t0 = ci * ch
        x = x_ref[pl.ds(t0, ch)].reshape(ch * MOE_CAP, D_MODEL)
        gu = _dot(x, wgu_b[0])
        y = _dot(_silu(gu[:, :ed]) * gu[:, ed:], wd_b[0])
        y_ref[pl.ds(t0, ch)] = y.reshape(ch, 1, MOE_CAP, D_MODEL).astype(y_ref.dtype)
        return carry

    lax.fori_loop(0, g // ch, chunk, 0)


def _combine_kernel(u_ref, gt_ref, rt_ref, y_ref, sgu_ref, sd_ref, h_ref, mod_ref, lng_ref, lnb_ref, *rest, extra):
    if extra:
        ex_ref, o_ref = rest
    else:
        (o_ref,) = rest
    ed = EXPERT_DIM
    cap = MOE_CAP
    gu = _dot(u_ref[...], sgu_ref[...])
    acc = _dot(_silu(gu[:, :ed]) * gu[:, ed:], sd_ref[...])
    if extra:
        acc = acc + ex_ref[...]
    slot = lax.broadcasted_iota(jnp.int32, (cap, TM), 0).astype(F32)
    for g0 in range(0, N_EXPERTS, MOE_EGROUP):
        pw = jnp.concatenate([jnp.where(slot == rt_ref[0, e:e + 1, :], gt_ref[0, e:e + 1, :], 0.0).astype(MXU_DT)
                              for e in range(g0, g0 + MOE_EGROUP)], axis=0)
        yg = y_ref[0, g0:g0 + MOE_EGROUP].reshape(MOE_EGROUP * cap, D_MODEL)
        acc = acc + _dot_tn(pw, yg)
    z = DEEPNORM_ALPHA * h_ref[...] + mod_ref[0][5:6] * acc
    o_ref[...] = _ln_rows(z, lng_ref[...], lnb_ref[...])


def _overflow_kernel(tile_ref, exp_ref, nr_ref, n_ref, u_ref, gate_ref, rank_ref, wgu_ref, wd_ref, zero_ref, o_ref):
    del zero_ref
    s = pl.program_id(0)
    tile = tile_ref[s]
    e = exp_ref[s]
    ed = EXPERT_DIM
    cap = MOE_CAP
    active = s < n_ref[0]
    first = jnp.logical_or(s == 0, tile_ref[jnp.maximum(s - 1, 0)] != tile)

    @pl.when(jnp.logical_and(active, first))
    def _():
        o_ref[...] = jnp.zeros_like(o_ref)

    @pl.when(active)
    def _():
        wgu = wgu_ref[0]
        wd = wd_ref[0]
        rank = rank_ref[0, pl.ds(e, 1), :]
        gate = gate_ref[0, pl.ds(e, 1), :]

        def one_round(r, carry):
            slot = lax.broadcasted_iota(jnp.int32, (cap, TM), 0).astype(F32) + (r * cap).astype(F32)
            hit = slot == rank
            x = _dot(jnp.where(hit, 1.0, 0.0), u_ref[...])
            gu = _dot(x, wgu)
            y = _dot(_silu(gu[:, :ed]) * gu[:, ed:], wd)
            o_ref[...] += _dot_tn(jnp.where(hit, gate, 0.0), y)
            return carry

        lax.fori_loop(1, nr_ref[s], one_round, 0)


def _moe_layer(h, mods, nct, layer, router_w, router_b, w_gu, w_down, sh_gu, sh_down, ln_g, ln_b, latent_only=False):
    t, d = h.shape
    nt = t // TM
    ne, cap = N_EXPERTS, MOE_CAP
    rw = jnp.concatenate([router_w.T, jnp.zeros((LANES - ne, d), F32)], axis=0)
    rb = jnp.concatenate([router_b, jnp.zeros((LANES - ne,), F32)])[:, None]
    per_tile = pl.BlockSpec((1, ne, TM), lambda i: (i, 0, 0))
    slots = pl.BlockSpec((1, ne, cap, d), lambda i: (i, 0, 0, 0))
    u, gates, ranks, xs = pl.pallas_call(
        _router_kernel, grid=(nt,),
        in_specs=[_row_spec(d), _mod_spec(nct), _full_spec((LANES, d)), _full_spec((LANES, 1))],
        out_specs=[_row_spec(d), per_tile, per_tile, slots],
        out_shape=[jax.ShapeDtypeStruct((t, d), MXU_DT), jax.ShapeDtypeStruct((nt, ne, TM), F32),
                   jax.ShapeDtypeStruct((nt, ne, TM), F32), jax.ShapeDtypeStruct((nt, ne, cap, d), MXU_DT)],
        compiler_params=_cparams("arbitrary"), name="moe_router",
    )(h, mods, rw, rb)

    run = max(g for g in range(1, MOE_RUN + 1) if nt % g == 0)
    ys, wgu_b, wd_b = pl.pallas_call(
        _expert_kernel, grid=(ne, nt // run),
        in_specs=[pl.BlockSpec((run, 1, cap, d), lambda e, c: (c, e, 0, 0)),
                  pl.BlockSpec((1, 1, d, 2 * EXPERT_DIM), lambda e, c: (layer, e, 0, 0)),
                  pl.BlockSpec((1, 1, EXPERT_DIM, d), lambda e, c: (layer, e, 0, 0))],
        out_specs=[pl.BlockSpec((run, 1, cap, d), lambda e, c: (c, e, 0, 0)),
                   pl.BlockSpec((1, d, 2 * EXPERT_DIM), lambda e, c: (e, 0, 0)),
                   pl.BlockSpec((1, EXPERT_DIM, d), lambda e, c: (e, 0, 0))],
        out_shape=[jax.ShapeDtypeStruct((nt, ne, cap, d), MXU_DT),
                   jax.ShapeDtypeStruct((ne, d, 2 * EXPERT_DIM), MXU_DT), jax.ShapeDtypeStruct((ne, EXPERT_DIM, d), MXU_DT)],
        compiler_params=_cparams("arbitrary", "arbitrary"), name="moe_experts",
    )(xs, w_gu, w_down)

    sgu, sd = sh_gu.astype(MXU_DT), sh_down.astype(MXU_DT)
    base_specs = [_row_spec(d), per_tile, per_tile, slots, _full_spec(sgu.shape), _full_spec(sd.shape),
                  _row_spec(d), _mod_spec(nct), _full_spec((1, d)), _full_spec((1, d))]
    base_args = (u, gates, ranks, ys, sgu, sd, h, mods, ln_g[None], ln_b[None])

    skip = nct if latent_only else 0
    out_spec = pl.BlockSpec((TM, d), lambda i: (jnp.maximum(i - skip, 0), 0))

    def combine(*extra):
        return pl.pallas_call(
            functools.partial(_combine_kernel, extra=bool(extra)), grid=(nt,),
            in_specs=base_specs + [_row_spec(d)] * len(extra), out_specs=out_spec,
            out_shape=jax.ShapeDtypeStruct((t - skip * TM, d), F32), compiler_params=_cparams("arbitrary"),
            name="moe_combine",
        )(*base_args, *extra)

    count = (jnp.max(ranks, axis=-1).astype(jnp.int32) + 1).reshape(-1)
    over = count > cap
    n_over = jnp.sum(over.astype(jnp.int32))

    def with_overflow(size):
        def run():
            idx = jnp.nonzero(over, size=size, fill_value=0)[0].astype(jnp.int32)
            idx = jnp.where(jnp.arange(size) < n_over, idx, idx[jnp.maximum(n_over - 1, 0)])
            tiles, exps = idx // ne, idx % ne
            rounds = (count[idx] + cap - 1) // cap
            grid_spec = pltpu.PrefetchScalarGridSpec(
                num_scalar_prefetch=4, grid=(size,),
                in_specs=[pl.BlockSpec((TM, d), lambda s, tl, ex, nr, n: (tl[s], 0)),
                          pl.BlockSpec((1, ne, TM), lambda s, tl, ex, nr, n: (tl[s], 0, 0)),
                          pl.BlockSpec((1, ne, TM), lambda s, tl, ex, nr, n: (tl[s], 0, 0)),
                          pl.BlockSpec((1, d, 2 * EXPERT_DIM), lambda s, tl, ex, nr, n: (ex[s], 0, 0)),
                          pl.BlockSpec((1, EXPERT_DIM, d), lambda s, tl, ex, nr, n: (ex[s], 0, 0)),
                          pl.BlockSpec(memory_space=pl.ANY)],
                out_specs=pl.BlockSpec((TM, d), lambda s, tl, ex, nr, n: (tl[s], 0)))
            extra = pl.pallas_call(
                _overflow_kernel, grid_spec=grid_spec, out_shape=jax.ShapeDtypeStruct((t, d), F32),
                input_output_aliases={9: 0}, compiler_params=_cparams("arbitrary"), name="moe_overflow",
            )(tiles, exps, rounds, n_over[None], u, gates, ranks, wgu_b, wd_b, jnp.zeros((t, d), F32))
            return combine(extra)
        return run

    sizes = sorted({min(sz, nt * ne) for sz in MOE_OVER_STEPS} | {nt * ne})
    branch = sum((n_over > sz).astype(jnp.int32) for sz in [0] + sizes[:-1])
    return lax.switch(branch, [combine] + [with_overflow(sz) for sz in sizes])


def kernel(x, c, ctx, c_ctx, ada_w, ada_b, post_ln_g, post_ln_b, lru_w_in, lru_conv_w, lru_conv_b, lru_gate_w, lru_gate_b, lru_lambda, lru_w_out, rwkv_mu, rwkv_w_in, rwkv_w0, rwkv_w_l1, rwkv_w_l2, rwkv_a0, rwkv_a_l1, rwkv_a_l2, rwkv_g_l1, rwkv_g_l2, rwkv_k_k, rwkv_k_a, rwkv_r_k, rwkv_ln_g, rwkv_ln_b, rwkv_w_out, ret_w_in, ret_decay, ret_gn_g, ret_gn_b, ret_w_out, hgrn_w_in, hgrn_b_f, hgrn_lb, hgrn_norm_g, hgrn_w_out, moe_router, moe_bias, moe_w_gu, moe_w_down, moe_sh_gu, moe_sh_down):
    assert x.shape[0] == 1 and ctx.shape[0] == 1
    n_ctx, n_lat, d = ctx.shape[1], x.shape[1], x.shape[2]
    assert n_ctx % TM == 0 and n_lat % TM == 0 and d == D_MODEL
    nct = n_ctx // TM
    rows = n_lat // GRID_W
    n_freq = RET_QK // 4
    freqs = ROPE_BASE ** (-jnp.arange(n_freq, dtype=F32) / n_freq)
    ang_row = jnp.arange(rows, dtype=F32)[:, None] * freqs
    ang_col = jnp.arange(GRID_W, dtype=F32)[:, None] * freqs

    def table(fn, ctx_value):
        lat = jnp.concatenate([jnp.repeat(fn(ang_row), GRID_W, axis=0), jnp.tile(fn(ang_col), (rows, 1))], axis=-1)
        return jnp.concatenate([jnp.full((n_ctx, RET_QK), ctx_value, F32), jnp.concatenate([lat, lat], axis=-1)], axis=0)

    rope_cos = table(jnp.cos, 1.0)
    rope_sin = table(jnp.sin, 0.0)
    lb_cum = jnp.cumsum(jax.nn.softmax(hgrn_lb.astype(F32), axis=0), axis=0)

    cond = jnp.concatenate([c_ctx[None], c, jnp.zeros((6, d), F32)], axis=0)
    mods_all = _ada_mods(cond, ada_w, ada_b)
    h = jnp.concatenate([ctx[0], x[0]], axis=0)
    for i in range(DEPTH):
        kind, j = i % N_MIXERS, i // N_MIXERS
        mods = mods_all[i]
        lng, lnb = post_ln_g[i, 0], post_ln_b[i, 0]
        if kind == 0:
            h = _lru_mixer(h, mods, nct, lru_w_in[j], lru_conv_w[j], lru_conv_b[j], lru_gate_w[j], lru_gate_b[j],
                           lru_lambda[j], lru_w_out[j], lng, lnb)
        elif kind == 1:
            h = _rwkv_mixer(h, mods, nct, rwkv_mu[j], rwkv_w_in[j], rwkv_w0[j], rwkv_w_l1[j], rwkv_w_l2[j], rwkv_a0[j],
                            rwkv_a_l1[j], rwkv_a_l2[j], rwkv_g_l1[j], rwkv_g_l2[j], rwkv_k_k[j], rwkv_k_a[j],
                            rwkv_r_k[j], rwkv_ln_g[j], rwkv_ln_b[j], rwkv_w_out[j], lng, lnb)
        elif kind == 2:
            h = _ret_mixer(h, mods, nct, rope_cos, rope_sin, ret_w_in[j], ret_decay[j], ret_gn_g[j], ret_gn_b[j],
                           ret_w_out[j], lng, lnb)
        else:
            h = _hgrn_mixer(h, mods, nct, lb_cum[i] - lb_cum[0], hgrn_w_in[j], hgrn_b_f[j], hgrn_norm_g[j],
                            hgrn_w_out[j], lng, lnb)
        h = _moe_layer(h, mods, nct, i, moe_router[i], moe_bias[i], moe_w_gu, moe_w_down, moe_sh_gu[i],
                       moe_sh_down[i], post_ln_g[i, 1], post_ln_b[i, 1], latent_only=i == DEPTH - 1)
    return h[None]
```
